```python
import math
import jax, jax.numpy as jnp
from jax import lax
import numpy as np

D_MODEL = 2048
BATCH = 4
SEQ = 2048
DEPTH = 1
DEC_BATCH = 32
DEC_SEQ = 4
PAST_LEN = 16384
PAGE_SIZE = 128

D_MIX = D_MODEL
D_ATTN = D_MIX // 2
D_SSM = D_MIX - D_ATTN
HEAD_DIM = 64
N_HEADS = D_ATTN // HEAD_DIM
N_KV_HEADS = N_HEADS // 4
REP = N_HEADS // N_KV_HEADS
D_KV = N_KV_HEADS * HEAD_DIM
ROT_DIM = HEAD_DIM // 4
ROPE_THETA = 500000.0
WINDOW = 128
Q_BLOCK = 128
SSM_CH = 16
N_SSM_GROUPS = D_SSM // SSM_CH
SSM_STATE = 64
PROJ_COLS = D_ATTN + 2 * D_KV + D_SSM
N_EXPERT_GROUPS = 4
EXPERTS_PER_GROUP = 8
N_EXPERTS = N_EXPERT_GROUPS * EXPERTS_PER_GROUP
D_EXPERT = D_MODEL // 4
TOP_K_IN_GROUP = 2
DEEPNORM_ALPHA = (2.0 * DEPTH) ** 0.25
DEEPNORM_BETA = (8.0 * DEPTH) ** -0.25
LN_EPS = 1e-5

kernel_name = 'hymba_s5_swa_sink_hiermoe_deepnorm_adaln_step'


def _layernorm(x):
    xf = x.astype(jnp.float32)
    mu = jnp.mean(xf, -1, keepdims=True)
    var = jnp.mean(jnp.square(xf - mu), -1, keepdims=True)
    return ((xf - mu) * lax.rsqrt(var + LN_EPS)).astype(x.dtype)


def _rmsnorm(x):
    xf = x.astype(jnp.float32)
    return (xf * lax.rsqrt(jnp.mean(jnp.square(xf), -1, keepdims=True) + LN_EPS)).astype(x.dtype)


def _rotary(x, pos):
    half = ROT_DIM // 2
    inv_freq = ROPE_THETA ** (-jnp.arange(half, dtype=jnp.float32) * 2.0 / ROT_DIM)
    ang = pos.astype(jnp.float32)[:, None] * inv_freq[None, :]
    cos = jnp.cos(ang)[None, :, None, :]
    sin = jnp.sin(ang)[None, :, None, :]
    xr = x[..., :ROT_DIM].astype(jnp.float32)
    x1, x2 = xr[..., :half], xr[..., half:]
    rot = jnp.concatenate([x1 * cos - x2 * sin, x2 * cos + x1 * sin], -1).astype(x.dtype)
    return jnp.concatenate([rot, x[..., ROT_DIM:]], -1)


def _window_attention(q, k_new, v_new, k_prev, v_prev, start, sinks):
    B, L = q.shape[0], q.shape[1]
    qb = Q_BLOCK if L % Q_BLOCK == 0 else L
    nb = L // qb
    k_all = jnp.concatenate([k_prev, k_new], 1)
    v_all = jnp.concatenate([v_prev, v_new], 1)
    k_pos = start - WINDOW + jnp.arange(WINDOW + L)
    q_pos = (start + jnp.arange(L)).reshape(nb, qb)
    kidx = jnp.arange(nb)[:, None] * qb + jnp.arange(qb + WINDOW)[None, :]
    kb = k_all[:, kidx]
    vb = v_all[:, kidx]
    kpb = k_pos[kidx]
    qg = q.reshape(B, nb, qb, N_KV_HEADS, REP, HEAD_DIM)
    s = jnp.einsum('bnqgrd,bnkgd->bgrnqk', qg, kb, preferred_element_type=jnp.float32) * (HEAD_DIM ** -0.5)
    rel = q_pos[:, :, None] - kpb[:, None, :]
    mask = (rel >= 0) & (rel < WINDOW) & (kpb[:, None, :] >= 0)
    s = jnp.where(mask, s, -jnp.inf)
    sink = sinks.astype(jnp.float32).reshape(1, N_KV_HEADS, REP, 1, 1, 1)
    m = jnp.maximum(jnp.max(s, -1, keepdims=True), sink)
    p = jnp.exp(s - m)
    p = p / (jnp.sum(p, -1, keepdims=True) + jnp.exp(sink - m))
    o = jnp.einsum('bgrnqk,bnkgd->bnqgrd', p.astype(vb.dtype), vb)
    return o.reshape(B, L, D_ATTN)


def _complex_affine_combine(e1, e2):
    a1r, a1i, b1r, b1i = e1
    a2r, a2i, b2r, b2i = e2
    return (a2r * a1r - a2i * a1i,
            a2r * a1i + a2i * a1r,
            a2r * b1r - a2i * b1i + b2r,
            a2r * b1i + a2i * b1r + b2i)


def _s5(u, h0_re, h0_im, lam_re, lam_im, log_step, b_re, b_im, c_re, c_im, d_skip):
    B, L = u.shape[0], u.shape[1]
    f32 = jnp.float32
    ug = u.reshape(B, L, N_SSM_GROUPS, SSM_CH).astype(f32)
    dt = jnp.exp(log_step.astype(f32))[:, None]
    lr, li = lam_re.astype(f32), lam_im.astype(f32)
    mag = jnp.exp(lr * dt)
    ar, ai = mag * jnp.cos(li * dt), mag * jnp.sin(li * dt)
    den = lr * lr + li * li
    cr = ((ar - 1.0) * lr + ai * li) / den
    ci = (ai * lr - (ar - 1.0) * li) / den
    br, bi = b_re.astype(f32), b_im.astype(f32)
    bbr = cr[..., None] * br - ci[..., None] * bi
    bbi = cr[..., None] * bi + ci[..., None] * br
    xr = jnp.einsum('gpc,blgc->blgp', bbr, ug)
    xi = jnp.einsum('gpc,blgc->blgp', bbi, ug)
    h0r, h0i = h0_re.astype(f32), h0_im.astype(f32)
    xr = xr.at[:, 0].add(ar * h0r - ai * h0i)
    xi = xi.at[:, 0].add(ar * h0i + ai * h0r)
    a_r = jnp.broadcast_to(ar, xr.shape)
    a_i = jnp.broadcast_to(ai, xi.shape)
    _, _, hr, hi = lax.associative_scan(_complex_affine_combine, (a_r, a_i, xr, xi), axis=1)
    y = (jnp.einsum('gcp,blgp->blgc', c_re.astype(f32), hr)
         - jnp.einsum('gcp,blgp->blgc', c_im.astype(f32), hi)
         + d_skip.astype(f32).reshape(N_SSM_GROUPS, SSM_CH) * ug)
    return y.reshape(B, L, D_SSM).astype(u.dtype), hr[:, -1].astype(h0_re.dtype), hi[:, -1].astype(h0_im.dtype)


def _hier_moe(h, rg_w, rg_b, re_w, re_b, w_gate, w_up, w_down):
    B, L, D = h.shape
    t = h.reshape(B * L, D)
    T = t.shape[0]
    g_logits = jnp.dot(t, rg_w).astype(jnp.float32) + rg_b.astype(jnp.float32)
    g_prob = jax.nn.softmax(g_logits, -1)
    g_val, g_idx = lax.top_k(g_prob, 1)
    e_logits = (jnp.dot(t, re_w).astype(jnp.float32) + re_b.astype(jnp.float32)).reshape(T, N_EXPERT_GROUPS, EXPERTS_PER_GROUP)
    e_in = jnp.take_along_axis(e_logits, g_idx[:, :, None], axis=1)[:, 0]
    e_prob = jax.nn.softmax(e_in, -1)
    e_val, e_idx = lax.top_k(e_prob, TOP_K_IN_GROUP)
    e_val = e_val / jnp.sum(e_val, -1, keepdims=True)
    within = jnp.einsum('tk,tke->te', e_val, jax.nn.one_hot(e_idx, EXPERTS_PER_GROUP, dtype=jnp.float32))
    comb = jax.nn.one_hot(g_idx[:, 0], N_EXPERT_GROUPS, dtype=jnp.float32)[:, :, None] * (g_val * within)[:, None, :]
    comb = comb.reshape(T, N_EXPERTS).astype(h.dtype)
    hg = jnp.einsum('td,edf->tef', t, w_gate)
    hu = jnp.einsum('td,edf->tef', t, w_up)
    act = jax.nn.silu(hg) * hu * comb[:, :, None]
    out = jnp.einsum('tef,efd->td', act, w_down)
    return out.reshape(B, L, D)


def _layer(x, c, start, k_prev, v_prev, h0_re, h0_im, p):
    B, L, _ = x.shape
    mod = jnp.dot(jax.nn.silu(c), p['ada_w']) + p['ada_b']
    sh1, sc1, g1, sh2, sc2, g2 = [m[:, None, :] for m in jnp.split(mod, 6, -1)]
    h = _layernorm(x) * (1.0 + sc1) + sh1
    proj = jnp.dot(h, p['w_in'])
    q, k, v, u = jnp.split(proj, [D_ATTN, D_ATTN + D_KV, D_ATTN + 2 * D_KV], -1)
    pos = start + jnp.arange(L)
    q = _rotary(q.reshape(B, L, N_HEADS, HEAD_DIM), pos)
    k = _rotary(k.reshape(B, L, N_KV_HEADS, HEAD_DIM), pos)
    v = v.reshape(B, L, N_KV_HEADS, HEAD_DIM)
    attn = _window_attention(q, k, v, k_prev, v_prev, start, p['attn_sinks'])
    y_ssm, hr, hi = _s5(u, h0_re, h0_im, p['ssm_lambda_re'], p['ssm_lambda_im'], p['ssm_log_step'],
                        p['ssm_b_re'], p['ssm_b_im'], p['ssm_c_re'], p['ssm_c_im'], p['ssm_d'])
    z = jax.nn.gelu(y_ssm)
    ssm = z * jax.nn.sigmoid(jnp.dot(z, p['ssm_glu_w']) + p['ssm_glu_b'])
    mixed = jnp.concatenate([_rmsnorm(attn) * p['attn_norm_g'], _rmsnorm(ssm) * p['ssm_norm_g']], -1)
    o = jnp.dot(mixed, p['w_out'])
    x = _layernorm(DEEPNORM_ALPHA * x + g1 * o) * p['ln1_g'] + p['ln1_b']
    h2 = _layernorm(x) * (1.0 + sc2) + sh2
    f = _hier_moe(h2, p['router_group_w'], p['router_group_b'], p['router_expert_w'], p['router_expert_b'],
                  p['exp_w_gate'], p['exp_w_up'], p['exp_w_down'])
    x = _layernorm(DEEPNORM_ALPHA * x + g2 * f) * p['ln2_g'] + p['ln2_b']
    k_buf = jnp.concatenate([k_prev, k], 1)[:, -WINDOW:]
    v_buf = jnp.concatenate([v_prev, v], 1)[:, -WINDOW:]
    return x, k_buf, v_buf, hr, hi


def setup_inputs(seed: int = 0) -> dict:
    key = jax.random.key(seed)
    ks = iter(jax.random.split(key, 48))
    f32 = jnp.float32

    def nrm(shape, scale=1.0):
        return jax.random.normal(next(ks), shape, f32) * scale

    L = DEPTH
    n_idx = jnp.arange(SSM_STATE, dtype=f32)
    inp = {}
    inp['x_prompt'] = nrm((BATCH, SEQ, D_MODEL))
    inp['x_sample'] = nrm((DEC_BATCH, DEC_SEQ, D_MODEL))
    inp['cache_k_win'] = nrm((L, DEC_BATCH, WINDOW, N_KV_HEADS, HEAD_DIM))
    inp['cache_v_win'] = nrm((L, DEC_BATCH, WINDOW, N_KV_HEADS, HEAD_DIM))
    inp['state_ssm_re'] = nrm((L, DEC_BATCH, N_SSM_GROUPS, SSM_STATE), 0.1)
    inp['state_ssm_im'] = nrm((L, DEC_BATCH, N_SSM_GROUPS, SSM_STATE), 0.1)
    inp['c_prompt'] = nrm((BATCH, D_MODEL))
    inp['c_sample'] = nrm((DEC_BATCH, D_MODEL))
    inp['ada_w'] = nrm((L, D_MODEL, 6 * D_MODEL), 0.5 * D_MODEL ** -0.5)
    inp['ada_b'] = nrm((L, 6 * D_MODEL), 0.01)
    inp['w_in'] = nrm((L, D_MODEL, PROJ_COLS), D_MODEL ** -0.5)
    inp['attn_sinks'] = nrm((L, N_HEADS), 0.5)
    inp['ssm_lambda_re'] = -0.5 + nrm((L, N_SSM_GROUPS, SSM_STATE), 0.01)
    inp['ssm_lambda_im'] = math.pi * n_idx + nrm((L, N_SSM_GROUPS, SSM_STATE), 0.01)
    inp['ssm_log_step'] = jax.random.uniform(next(ks), (L, N_SSM_GROUPS), f32, math.log(1e-3), math.log(1e-1))
    inp['ssm_b_re'] = nrm((L, N_SSM_GROUPS, SSM_STATE, SSM_CH), (2 * SSM_CH) ** -0.5)
    inp['ssm_b_im'] = nrm((L, N_SSM_GROUPS, SSM_STATE, SSM_CH), (2 * SSM_CH) ** -0.5)
    inp['ssm_c_re'] = nrm((L, N_SSM_GROUPS, SSM_CH, SSM_STATE), SSM_STATE ** -0.5)
    inp['ssm_c_im'] = nrm((L, N_SSM_GROUPS, SSM_CH, SSM_STATE), SSM_STATE ** -0.5)
    inp['ssm_d'] = nrm((L, D_SSM))
    inp['ssm_glu_w'] = nrm((L, D_SSM, D_SSM), D_SSM ** -0.5)
    inp['ssm_glu_b'] = nrm((L, D_SSM), 0.01)
    inp['attn_norm_g'] = 1.0 + nrm((L, D_ATTN), 0.01)
    inp['ssm_norm_g'] = 1.0 + nrm((L, D_SSM), 0.01)
    inp['w_out'] = nrm((L, D_MIX, D_MODEL), D_MIX ** -0.5 * DEEPNORM_BETA)
    inp['ln1_g'] = 1.0 + nrm((L, D_MODEL), 0.01)
    inp['ln1_b'] = nrm((L, D_MODEL), 0.01)
    inp['router_group_w'] = nrm((L, D_MODEL, N_EXPERT_GROUPS), D_MODEL ** -0.5)
    inp['router_group_b'] = nrm((L, N_EXPERT_GROUPS), 0.01)
    inp['router_expert_w'] = nrm((L, D_MODEL, N_EXPERTS), D_MODEL ** -0.5)
    inp['router_expert_b'] = nrm((L, N_EXPERTS), 0.01)
    inp['exp_w_gate'] = nrm((L, N_EXPERTS, D_MODEL, D_EXPERT), D_MODEL ** -0.5)
    inp['exp_w_up'] = nrm((L, N_EXPERTS, D_MODEL, D_EXPERT), D_MODEL ** -0.5)
    inp['exp_w_down'] = nrm((L, N_EXPERTS, D_EXPERT, D_MODEL), D_EXPERT ** -0.5 * DEEPNORM_BETA)
    inp['ln2_g'] = 1.0 + nrm((L, D_MODEL), 0.01)
    inp['ln2_b'] = nrm((L, D_MODEL), 0.01)
    return inp


def reference(x_prompt, x_sample, cache_k_win, cache_v_win, state_ssm_re, state_ssm_im, c_prompt, c_sample,
              ada_w, ada_b, w_in, attn_sinks, ssm_lambda_re, ssm_lambda_im, ssm_log_step, ssm_b_re, ssm_b_im,
              ssm_c_re, ssm_c_im, ssm_d, ssm_glu_w, ssm_glu_b, attn_norm_g, ssm_norm_g, w_out, ln1_g, ln1_b,
              router_group_w, router_group_b, router_expert_w, router_expert_b, exp_w_gate, exp_w_up, exp_w_down,
              ln2_g, ln2_b):
    y_prompt, y_sample = x_prompt, x_sample
    bp = x_prompt.shape[0]
    kp_l, vp_l, hrp_l, hip_l = [], [], [], []
    ks_l, vs_l, hrs_l, his_l = [], [], [], []
    for l in range(DEPTH):
        p = dict(ada_w=ada_w[l], ada_b=ada_b[l], w_in=w_in[l], attn_sinks=attn_sinks[l],
                 ssm_lambda_re=ssm_lambda_re[l], ssm_lambda_im=ssm_lambda_im[l], ssm_log_step=ssm_log_step[l],
                 ssm_b_re=ssm_b_re[l], ssm_b_im=ssm_b_im[l], ssm_c_re=ssm_c_re[l], ssm_c_im=ssm_c_im[l],
                 ssm_d=ssm_d[l], ssm_glu_w=ssm_glu_w[l], ssm_glu_b=ssm_glu_b[l],
                 attn_norm_g=attn_norm_g[l], ssm_norm_g=ssm_norm_g[l], w_out=w_out[l],
                 ln1_g=ln1_g[l], ln1_b=ln1_b[l], router_group_w=router_group_w[l], router_group_b=router_group_b[l],
                 router_expert_w=router_expert_w[l], router_expert_b=router_expert_b[l],
                 exp_w_gate=exp_w_gate[l], exp_w_up=exp_w_up[l], exp_w_down=exp_w_down[l],
                 ln2_g=ln2_g[l], ln2_b=ln2_b[l])
        k0 = jnp.zeros((bp, WINDOW, N_KV_HEADS, HEAD_DIM), x_prompt.dtype)
        h0 = jnp.zeros((bp, N_SSM_GROUPS, SSM_STATE), state_ssm_re.dtype)
        y_prompt, kp, vp, hrp, hip = _layer(y_prompt, c_prompt, 0, k0, k0, h0, h0, p)
        y_sample, ksm, vsm, hrs, his = _layer(y_sample, c_sample, PAST_LEN, cache_k_win[l], cache_v_win[l],
                                              state_ssm_re[l], state_ssm_im[l], p)
        kp_l.append(kp); vp_l.append(vp); hrp_l.append(hrp); hip_l.append(hip)
        ks_l.append(ksm); vs_l.append(vsm); hrs_l.append(hrs); his_l.append(his)
    k_win_prompt = jnp.stack(kp_l)
    v_win_prompt = jnp.stack(vp_l)
    ssm_re_prompt = jnp.stack(hrp_l)
    ssm_im_prompt = jnp.stack(hip_l)
    k_win_sample = jnp.stack(ks_l)
    v_win_sample = jnp.stack(vs_l)
    ssm_re_sample = jnp.stack(hrs_l)
    ssm_im_sample = jnp.stack(his_l)
    return (y_prompt, y_sample, k_win_prompt, v_win_prompt, ssm_re_prompt, ssm_im_prompt,
            k_win_sample, v_win_sample, ssm_re_sample, ssm_im_sample)
```

```python
import functools
import math

import jax
import jax.numpy as jnp
from jax import lax
from jax.experimental import pallas as pl
from jax.experimental.pallas import tpu as pltpu

F32 = jnp.float32
BF16 = jnp.bfloat16

D_MODEL = 2048
D_ATTN = 1024
D_SSM = 1024
HEAD_DIM = 64
N_HEADS = 16
N_KV_HEADS = 4
REP = 4
D_KV = 256
ROT_DIM = 16
ROPE_THETA = 500000.0
WINDOW = 128
SSM_CH = 16
N_SSM_GROUPS = 64
SSM_STATE = 64
N_STATE = N_SSM_GROUPS * SSM_STATE
PROJ_COLS = D_ATTN + 2 * D_KV + D_SSM
N_EXPERT_GROUPS = 4
EXPERTS_PER_GROUP = 8
N_EXPERTS = 32
D_EXPERT = 512
DEPTH = 1
DEEPNORM_ALPHA = (2.0 * DEPTH) ** 0.25
LN_EPS = 1e-5
PAST_LEN = 16384

LANES = 128
SUBLANES = 8
MXU_DIM = 256
TM = 256
TM_MOE = 256
S5_TT = 64
SAMPLE_PAD = 8
ROUTE_LANE0 = N_EXPERT_GROUPS
VMEM_LIMIT = 56 * 1024 * 1024


def _cparams(sem):
    return pltpu.CompilerParams(dimension_semantics=sem, vmem_limit_bytes=VMEM_LIMIT)


def _ln(x):
    mu = jnp.mean(x, axis=-1, keepdims=True)
    xc = x - mu
    var = jnp.mean(xc * xc, axis=-1, keepdims=True)
    return xc * lax.rsqrt(var + LN_EPS)


def _rms(x):
    return x * lax.rsqrt(jnp.mean(x * x, axis=-1, keepdims=True) + LN_EPS)


def _ada_kernel(c_ref, w_ref, b_ref, o_ref):
    c = c_ref[...]
    s = c * jax.nn.sigmoid(c)
    o_ref[...] = jnp.dot(s.astype(BF16), w_ref[...].astype(BF16), preferred_element_type=F32) + b_ref[...]


def _ada(c_all, ada_w, ada_b):
    n, tn = c_all.shape[0], 1024
    return pl.pallas_call(
        _ada_kernel,
        out_shape=jax.ShapeDtypeStruct((n, 6 * D_MODEL), F32),
        grid=(6 * D_MODEL // tn,),
        in_specs=[pl.BlockSpec((n, D_MODEL), lambda j: (0, 0)),
                  pl.BlockSpec((D_MODEL, tn), lambda j: (0, j)),
                  pl.BlockSpec((1, tn), lambda j: (0, j))],
        out_specs=pl.BlockSpec((n, tn), lambda j: (0, j)),
        compiler_params=_cparams(("arbitrary",)),
        name="ada",
    )(c_all, ada_w, ada_b)


def _mod_spec(per_row, tm, chunk):
    if per_row:
        return pl.BlockSpec((None, tm, D_MODEL), lambda b, i: (b, i, chunk))
    return pl.BlockSpec((None, 1, D_MODEL), lambda b, i: (b, 0, chunk))


def _inproj_kernel(x_ref, sh_ref, sc_ref, w_ref, rc_ref, ra_ref, rb_ref, q_ref, k_ref, v_ref, u_ref):
    h = _ln(x_ref[...]) * (1.0 + sc_ref[...]) + sh_ref[...]
    proj = jnp.dot(h.astype(BF16), w_ref[...], preferred_element_type=F32)
    rc, ra, rb = rc_ref[...], ra_ref[...], rb_ref[...]

    def rope(t):
        return t * rc + pltpu.roll(t, LANES - ROT_DIM // 2, 1) * ra + pltpu.roll(t, ROT_DIM // 2, 1) * rb

    for j in range(D_ATTN // LANES):
        q_ref[:, j * LANES:(j + 1) * LANES] = rope(proj[:, j * LANES:(j + 1) * LANES]).astype(BF16)
    for j in range(D_KV // LANES):
        c0 = D_ATTN + j * LANES
        k_ref[:, j * LANES:(j + 1) * LANES] = rope(proj[:, c0:c0 + LANES])
    v_ref[...] = proj[:, D_ATTN + D_KV:D_ATTN + 2 * D_KV]
    u_ref[...] = proj[:, D_ATTN + 2 * D_KV:]


def _inproj(x, mod, w_in_bf, rope_tabs, per_row):
    nb, l, _ = x.shape
    tm = min(TM, l)
    row = lambda w: pl.BlockSpec((None, tm, w), lambda b, i: (b, i, 0))
    tab = pl.BlockSpec((tm, LANES), lambda b, i: (i, 0))
    return pl.pallas_call(
        _inproj_kernel,
        out_shape=(jax.ShapeDtypeStruct((nb, l, D_ATTN), BF16), jax.ShapeDtypeStruct((nb, l, D_KV), F32),
                   jax.ShapeDtypeStruct((nb, l, D_KV), F32), jax.ShapeDtypeStruct((nb, l, D_SSM), F32)),
        grid=(nb, l // tm),
        in_specs=[row(D_MODEL), _mod_spec(per_row, tm, 0), _mod_spec(per_row, tm, 1),
                  pl.BlockSpec((D_MODEL, PROJ_COLS), lambda b, i: (0, 0)), tab, tab, tab],
        out_specs=(row(D_ATTN), row(D_KV), row(D_KV), row(D_SSM)),
        compiler_params=_cparams(("arbitrary", "arbitrary")),
        name="inproj",
    )(x, mod, mod, w_in_bf, *rope_tabs)


def _attn_kernel(sink_ref, q_ref, kp_ref, kc_ref, vp_ref, vc_ref, o_ref, *, lq, prev_from_block):
    m_rows = REP * lq
    ii = lax.broadcasted_iota(jnp.int32, (m_rows, WINDOW), 0) & (lq - 1)
    jj = lax.broadcasted_iota(jnp.int32, (m_rows, WINDOW), 1)
    if prev_from_block:
        off = jnp.where(pl.program_id(1) > 0, 0, WINDOW)
        mask_p = jj > ii + off
    else:
        mask_p = jj > ii
    mask_c = jj <= ii
    for g in range(N_KV_HEADS):
        qg = q_ref[:, g * REP * HEAD_DIM:(g + 1) * REP * HEAD_DIM].astype(F32)
        qs = jnp.concatenate([qg[:, r * HEAD_DIM:(r + 1) * HEAD_DIM] for r in range(REP)], axis=0).astype(BF16)
        ks = slice(g * HEAD_DIM, (g + 1) * HEAD_DIM)
        dn = (((1,), (1,)), ((), ()))
        s_p = lax.dot_general(qs, kp_ref[:, ks].astype(BF16), dn, preferred_element_type=F32) * (HEAD_DIM ** -0.5)
        s_c = lax.dot_general(qs, kc_ref[:, ks].astype(BF16), dn, preferred_element_type=F32) * (HEAD_DIM ** -0.5)
        s_p = jnp.where(mask_p, s_p, -jnp.inf)
        s_c = jnp.where(mask_c, s_c, -jnp.inf)
        rr = lax.broadcasted_iota(jnp.int32, (m_rows, 1), 0)
        sink = jnp.zeros((m_rows, 1), F32)
        for r in range(REP):
            sink = jnp.where((rr >= r * lq) & (rr < (r + 1) * lq), sink_ref[g * REP + r], sink)
        m = jnp.maximum(jnp.maximum(jnp.max(s_p, axis=-1, keepdims=True), jnp.max(s_c, axis=-1, keepdims=True)), sink)
        p_p = jnp.exp(s_p - m)
        p_c = jnp.exp(s_c - m)
        den = jnp.sum(p_p, axis=-1, keepdims=True) + jnp.sum(p_c, axis=-1, keepdims=True) + jnp.exp(sink - m)
        o = (jnp.dot((p_p / den).astype(BF16), vp_ref[:, ks].astype(BF16), preferred_element_type=F32)
             + jnp.dot((p_c / den).astype(BF16), vc_ref[:, ks].astype(BF16), preferred_element_type=F32))
        for r in range(REP):
            h = g * REP + r
            o_ref[:, h * HEAD_DIM:(h + 1) * HEAD_DIM] = o[r * lq:(r + 1) * lq]


def _attn_prompt(sinks, q, k, v):
    nb, l, _ = q.shape
    nblk = l // WINDOW
    cur = lambda w: pl.BlockSpec((None, WINDOW, w), lambda b, n: (b, n, 0))
    prev = lambda w: pl.BlockSpec((None, WINDOW, w), lambda b, n: (b, jnp.maximum(n - 1, 0), 0))
    return pl.pallas_call(
        functools.partial(_attn_kernel, lq=WINDOW, prev_from_block=True),
        out_shape=jax.ShapeDtypeStruct((nb, l, D_ATTN), F32),
        grid=(nb, nblk),
        in_specs=[pl.BlockSpec(memory_space=pltpu.SMEM), cur(D_ATTN), prev(D_KV), cur(D_KV), prev(D_KV), cur(D_KV)],
        out_specs=cur(D_ATTN),
        compiler_params=_cparams(("arbitrary", "arbitrary")),
        name="attn_prompt",
    )(sinks, q, k, k, v, v)


def _attn_sample(sinks, q, kcat, vcat):
    nb, lq, _ = q.shape
    kblk = lambda n: pl.BlockSpec((None, WINDOW, D_KV), lambda b: (b, n, 0))
    return pl.pallas_call(
        functools.partial(_attn_kernel, lq=lq, prev_from_block=False),
        out_shape=jax.ShapeDtypeStruct((nb, lq, D_ATTN), F32),
        grid=(nb,),
        in_specs=[pl.BlockSpec(memory_space=pltpu.SMEM), pl.BlockSpec((None, lq, D_ATTN), lambda b: (b, 0, 0)),
                  kblk(0), kblk(1), kblk(0), kblk(1)],
        out_specs=pl.BlockSpec((None, lq, D_ATTN), lambda b: (b, 0, 0)),
        compiler_params=_cparams(("arbitrary",)),
        name="attn_sample",
    )(sinks, q, kcat, kcat, vcat, vcat)


N_KT = D_SSM // MXU_DIM
ST_PER_KT = N_STATE // N_KT
N_SLAB = D_SSM // LANES


def _s5_prompt_kernel(u_ref, bre_ref, bim_ref, cre_ref, cimn_ref, a1r_ref, a1i_ref, par_ref, pai_ref, d_ref,
                      h0r_ref, h0i_ref, y_ref, cr_ref, ci_ref, il_ref, xr_ref, xi_ref, *, nb, tt):
    rows = nb * tt

    @pl.when(pl.program_id(0) == 0)
    def _():
        cr_ref[...] = h0r_ref[...]
        ci_ref[...] = h0i_ref[...]

    for b in range(nb):
        for j in range(N_SLAB):
            il_ref[j, pl.ds(b, tt, stride=nb), :] = u_ref[b, :, j * LANES:(j + 1) * LANES]
    for kt in range(N_KT):
        ub = jnp.concatenate([il_ref[2 * kt], il_ref[2 * kt + 1]], axis=1).astype(BF16)
        cs = slice(kt * ST_PER_KT, (kt + 1) * ST_PER_KT)
        xr_ref[:, cs] = jnp.dot(ub, bre_ref[kt], preferred_element_type=F32)
        xi_ref[:, cs] = jnp.dot(ub, bim_ref[kt], preferred_element_type=F32)

    half = lax.broadcasted_iota(jnp.int32, (SUBLANES, LANES), 0) < nb

    def step(i, carry):
        r0 = pl.multiple_of(i * SUBLANES, SUBLANES)
        for j in range(N_STATE // LANES):
            ls = slice(j * LANES, (j + 1) * LANES)
            x_r = xr_ref[pl.ds(r0, SUBLANES), ls]
            x_i = xi_ref[pl.ds(r0, SUBLANES), ls]
            s_r = pltpu.roll(x_r, nb, 0)
            s_i = pltpu.roll(x_i, nb, 0)
            a1r, a1i = a1r_ref[:, ls], a1i_ref[:, ls]
            c_r, c_i = cr_ref[:, ls], ci_ref[:, ls]
            par, pai = par_ref[:, ls], pai_ref[:, ls]
            h_r = x_r + (a1r * s_r - a1i * s_i) + (par * c_r - pai * c_i)
            h_i = x_i + (a1r * s_i + a1i * s_r) + (par * c_i + pai * c_r)
            xr_ref[pl.ds(r0, SUBLANES), ls] = h_r
            xi_ref[pl.ds(r0, SUBLANES), ls] = h_i
            cr_ref[:, ls] = jnp.where(half, pltpu.roll(h_r, nb, 0), h_r)
            ci_ref[:, ls] = jnp.where(half, pltpu.roll(h_i, nb, 0), h_i)
        return carry

    lax.fori_loop(0, rows // SUBLANES, step, 0)

    for kt in range(N_KT):
        cs = slice(kt * ST_PER_KT, (kt + 1) * ST_PER_KT)
        y = (jnp.dot(xr_ref[:, cs].astype(BF16), cre_ref[kt], preferred_element_type=F32)
             + jnp.dot(xi_ref[:, cs].astype(BF16), cimn_ref[kt], preferred_element_type=F32))
        il_ref[2 * kt] = y[:, :LANES]
        il_ref[2 * kt + 1] = y[:, LANES:]
    for b in range(nb):
        for j in range(N_SLAB):
            ls = slice(j * LANES, (j + 1) * LANES)
            y_ref[b, :, ls] = il_ref[j, pl.ds(b, tt, stride=nb), :] + d_ref[:, ls] * u_ref[b, :, ls]


def _s5_prompt(u, wts, tabs, d_skip, h0r, h0i):
    nb, l, _ = u.shape
    tt = S5_TT
    rows = nb * tt
    full = lambda a: pl.BlockSpec(a.shape, lambda i: (0,) * a.ndim)
    blk = pl.BlockSpec((nb, tt, D_SSM), lambda i: (0, i, 0))
    carry = jax.ShapeDtypeStruct((SUBLANES, N_STATE), F32)
    return pl.pallas_call(
        functools.partial(_s5_prompt_kernel, nb=nb, tt=tt),
        out_shape=(jax.ShapeDtypeStruct((nb, l, D_SSM), F32), carry, carry),
        grid=(l // tt,),
        in_specs=[blk] + [full(a) for a in (*wts, *tabs, d_skip, h0r, h0i)],
        out_specs=(blk, pl.BlockSpec((SUBLANES, N_STATE), lambda i: (0, 0)),
                   pl.BlockSpec((SUBLANES, N_STATE), lambda i: (0, 0))),
        scratch_shapes=[pltpu.VMEM((N_SLAB, rows, LANES), F32), pltpu.VMEM((rows, N_STATE), F32),
                        pltpu.VMEM((rows, N_STATE), F32)],
        compiler_params=_cparams(("arbitrary",)),
        name="s5_prompt",
    )(u, *wts, *tabs, d_skip, h0r, h0i)


def _s5_sample_kernel(u_ref, bre_ref, bim_ref, cre_ref, cimn_ref, ar_ref, ai_ref, d_ref, h0r_ref, h0i_ref,
                      y_ref, sr_ref, si_ref, xr_ref, xi_ref, *, nt):
    sr_ref[...] = h0r_ref[...]
    si_ref[...] = h0i_ref[...]
    for t in range(nt):
        for kt in range(N_KT):
            ub = u_ref[t, :, kt * MXU_DIM:(kt + 1) * MXU_DIM].astype(BF16)
            cs = slice(kt * ST_PER_KT, (kt + 1) * ST_PER_KT)
            xr_ref[:, cs] = jnp.dot(ub, bre_ref[kt], preferred_element_type=F32)
            xi_ref[:, cs] = jnp.dot(ub, bim_ref[kt], preferred_element_type=F32)
        ar, ai = ar_ref[...], ai_ref[...]
        s_r, s_i = sr_ref[...], si_ref[...]
        sr_ref[...] = xr_ref[...] + (ar * s_r - ai * s_i)
        si_ref[...] = xi_ref[...] + (ar * s_i + ai * s_r)
        for kt in range(N_KT):
            cs = slice(kt * ST_PER_KT, (kt + 1) * ST_PER_KT)
            ys = slice(kt * MXU_DIM, (kt + 1) * MXU_DIM)
            y = (jnp.dot(sr_ref[:, cs].astype(BF16), cre_ref[kt], preferred_element_type=F32)
                 + jnp.dot(si_ref[:, cs].astype(BF16), cimn_ref[kt], preferred_element_type=F32))
            y_ref[t, :, ys] = y + d_ref[:, ys] * u_ref[t, :, ys]


def _s5_sample(u_tb, wts, ar, ai, d_skip, h0r, h0i):
    nt, nb, _ = u_tb.shape
    st = jax.ShapeDtypeStruct((nb, N_STATE), F32)
    args = (u_tb, *wts, ar, ai, d_skip, h0r, h0i)
    full = lambda a: pl.BlockSpec(a.shape, lambda i: (0,) * a.ndim)
    return pl.pallas_call(
        functools.partial(_s5_sample_kernel, nt=nt),
        out_shape=(jax.ShapeDtypeStruct((nt, nb, D_SSM), F32), st, st),
        grid=(1,),
        in_specs=[full(a) for a in args],
        out_specs=(pl.BlockSpec((nt, nb, D_SSM), lambda i: (0, 0, 0)), pl.BlockSpec((nb, N_STATE), lambda i: (0, 0)),
                   pl.BlockSpec((nb, N_STATE), lambda i: (0, 0))),
        scratch_shapes=[pltpu.VMEM((nb, N_STATE), F32), pltpu.VMEM((nb, N_STATE), F32)],
        compiler_params=_cparams(("arbitrary",)),
        name="s5_sample",
    )(*args)


def _post_kernel(attn_ref, yssm_ref, x_ref, g1_ref, sh2_ref, sc2_ref, gluw_ref, glub_ref, ga_ref, gs_ref, wout_ref,
                 l1g_ref, l1b_ref, wr_ref, br_ref, cnt0_ref, x1_ref, h2_ref, route_ref, cnt_ref):
    tm = x_ref.shape[0]
    first = (pl.program_id(0) == 0) & (pl.program_id(1) == 0)

    @pl.when(first)
    def _():
        cnt_ref[...] = cnt0_ref[...]

    z = jax.nn.gelu(yssm_ref[...])
    ssm = z * jax.nn.sigmoid(jnp.dot(z.astype(BF16), gluw_ref[...], preferred_element_type=F32) + glub_ref[...])
    mixed_a = (_rms(attn_ref[...]) * ga_ref[...]).astype(BF16)
    mixed_s = (_rms(ssm) * gs_ref[...]).astype(BF16)
    o = (jnp.dot(mixed_a, wout_ref[:D_ATTN, :], preferred_element_type=F32)
         + jnp.dot(mixed_s, wout_ref[D_ATTN:, :], preferred_element_type=F32))
    x1 = _ln(DEEPNORM_ALPHA * x_ref[...] + g1_ref[...] * o) * l1g_ref[...] + l1b_ref[...]
    x1_ref[...] = x1
    h2 = _ln(x1) * (1.0 + sc2_ref[...]) + sh2_ref[...]
    h2_ref[...] = h2
    logits = jnp.dot(h2.astype(BF16), wr_ref[...], preferred_element_type=F32) + br_ref[...]

    lane = lax.broadcasted_iota(jnp.int32, (tm, LANES), 1).astype(F32)
    big = float(4 * LANES)
    neg = -jnp.inf
    gl = jnp.where(lane < N_EXPERT_GROUPS, logits, neg)
    gp = jnp.exp(gl - jnp.max(gl, axis=-1, keepdims=True))
    gp = gp / jnp.sum(gp, axis=-1, keepdims=True)
    g_val = jnp.max(gp, axis=-1, keepdims=True)
    g_idx = jnp.min(jnp.where(gp == g_val, lane, big), axis=-1, keepdims=True)
    lo = ROUTE_LANE0 + EXPERTS_PER_GROUP * g_idx
    emask = (lane >= lo) & (lane < lo + EXPERTS_PER_GROUP)
    el = jnp.where(emask, logits, neg)
    ep = jnp.exp(el - jnp.max(el, axis=-1, keepdims=True))
    ep = jnp.where(emask, ep / jnp.sum(ep, axis=-1, keepdims=True), -1.0)
    v1 = jnp.max(ep, axis=-1, keepdims=True)
    i1 = jnp.min(jnp.where(ep == v1, lane, big), axis=-1, keepdims=True)
    ep2 = jnp.where(lane == i1, -1.0, ep)
    v2 = jnp.max(ep2, axis=-1, keepdims=True)
    i2 = jnp.min(jnp.where(ep2 == v2, lane, big), axis=-1, keepdims=True)
    vs = v1 + v2
    w1 = g_val * (v1 / vs)
    w2 = g_val * (v2 / vs)
    hit = jnp.where((lane == i1) | (lane == i2), 1.0, 0.0)
    tri = (lax.broadcasted_iota(jnp.int32, (tm, tm), 0) > lax.broadcasted_iota(jnp.int32, (tm, tm), 1))
    before = jnp.dot(jnp.where(tri, 1.0, 0.0).astype(BF16), hit.astype(BF16), preferred_element_type=F32) + cnt_ref[0:1, :]
    r1 = jnp.sum(jnp.where(lane == i1, before, 0.0), axis=-1, keepdims=True)
    r2 = jnp.sum(jnp.where(lane == i2, before, 0.0), axis=-1, keepdims=True)
    cnt_ref[...] = cnt_ref[...] + jnp.sum(hit, axis=0, keepdims=True)
    e1 = i1 - ROUTE_LANE0
    e2 = i2 - ROUTE_LANE0
    route = jnp.zeros((tm, LANES), F32)
    for n, val in enumerate((e1, e2, w1, w2, r1, r2)):
        route = jnp.where(lane == n, val, route)
    route_ref[...] = route


def _post(attn, yssm, x, mod, per_row, w, cnt0):
    nb, l, _ = x.shape
    tm = min(TM, l)
    row = lambda wd: pl.BlockSpec((None, tm, wd), lambda b, i: (b, i, 0))
    full = lambda a: pl.BlockSpec(a.shape, lambda b, i: (0,) * a.ndim)
    consts = (w["glu_w"], w["glu_b"], w["attn_g"], w["ssm_g"], w["w_out"], w["ln1_g"], w["ln1_b"], w["wr"], w["br"],
              cnt0)
    return pl.pallas_call(
        _post_kernel,
        out_shape=(jax.ShapeDtypeStruct((nb, l, D_MODEL), F32), jax.ShapeDtypeStruct((nb, l, D_MODEL), F32),
                   jax.ShapeDtypeStruct((nb, l, LANES), F32), jax.ShapeDtypeStruct((SUBLANES, LANES), F32)),
        grid=(nb, l // tm),
        in_specs=[row(D_ATTN), row(D_SSM), row(D_MODEL), _mod_spec(per_row, tm, 2), _mod_spec(per_row, tm, 3),
                  _mod_spec(per_row, tm, 4)] + [full(a) for a in consts],
        out_specs=(row(D_MODEL), row(D_MODEL), row(LANES), pl.BlockSpec((SUBLANES, LANES), lambda b, i: (0, 0))),
        compiler_params=_cparams(("arbitrary", "arbitrary")),
        name="post",
    )(attn, yssm, x, mod, mod, mod, *consts)


def _dispatch_kernel(pos_ref, h2_ref, xs_in_ref, xs_ref, sem, *, tm, row0):
    del xs_in_ref
    base = (row0 + pl.program_id(0) * tm) * 2

    def row_copy(t, p):
        return pltpu.make_async_copy(h2_ref.at[pl.ds(t, 1)], xs_ref.at[pl.ds(p, 1)], sem)

    def issue(t, c):
        row_copy(t, pos_ref[base + 2 * t]).start()
        row_copy(t, pos_ref[base + 2 * t + 1]).start()
        return c

    def drain(t, c):
        row_copy(0, 0).wait()
        row_copy(0, 0).wait()
        return c

    lax.fori_loop(0, tm, issue, 0)
    lax.fori_loop(0, tm, drain, 0)


def _dispatch(pos, h2_rows, xs, row0):
    n = h2_rows.shape[0]
    tm = min(TM, n)
    return pl.pallas_call(
        functools.partial(_dispatch_kernel, tm=tm, row0=row0),
        out_shape=jax.ShapeDtypeStruct(xs.shape, xs.dtype),
        grid_spec=pltpu.PrefetchScalarGridSpec(
            num_scalar_prefetch=1, grid=(n // tm,),
            in_specs=[pl.BlockSpec((tm, D_MODEL), lambda i, pos: (i, 0)), pl.BlockSpec(memory_space=pl.ANY)],
            out_specs=pl.BlockSpec(memory_space=pl.ANY),
            scratch_shapes=[pltpu.SemaphoreType.DMA]),
        input_output_aliases={2: 0},
        compiler_params=_cparams(("arbitrary",)),
        name="dispatch",
    )(pos, h2_rows, xs)


def _moe_kernel(te_ref, nv_ref, x_ref, wg_ref, wu_ref, wd_ref, y_ref, wgb_ref, wub_ref, wdb_ref):
    i = pl.program_id(0)
    valid = i < nv_ref[0]
    changed = (i == 0) | (te_ref[i] != te_ref[jnp.maximum(i - 1, 0)])

    @pl.when(valid & changed)
    def _():
        wgb_ref[...] = wg_ref[...].astype(BF16)
        wub_ref[...] = wu_ref[...].astype(BF16)
        wdb_ref[...] = wd_ref[...].astype(BF16)

    @pl.when(valid)
    def _():
        xb = x_ref[...].astype(BF16)
        hg = jnp.dot(xb, wgb_ref[...], preferred_element_type=F32)
        hu = jnp.dot(xb, wub_ref[...], preferred_element_type=F32)
        act = (hg * jax.nn.sigmoid(hg)) * hu
        y_ref[...] = jnp.dot(act.astype(BF16), wdb_ref[...], preferred_element_type=F32)

    @pl.when(jnp.logical_not(valid))
    def _():
        y_ref[...] = jnp.zeros(y_ref.shape, y_ref.dtype)


def _moe(tile_expert, n_valid, xs, w_gate, w_up, w_down):
    n_tiles = xs.shape[0] // TM_MOE
    return pl.pallas_call(
        _moe_kernel,
        out_shape=jax.ShapeDtypeStruct((xs.shape[0], D_MODEL), F32),
        grid_spec=pltpu.PrefetchScalarGridSpec(
            num_scalar_prefetch=2, grid=(n_tiles,),
            in_specs=[pl.BlockSpec((TM_MOE, D_MODEL), lambda i, te, nv: (i, 0)),
                      pl.BlockSpec((None, D_MODEL, D_EXPERT), lambda i, te, nv: (te[i], 0, 0)),
                      pl.BlockSpec((None, D_MODEL, D_EXPERT), lambda i, te, nv: (te[i], 0, 0)),
                      pl.BlockSpec((None, D_EXPERT, D_MODEL), lambda i, te, nv: (te[i], 0, 0))],
            out_specs=pl.BlockSpec((TM_MOE, D_MODEL), lambda i, te, nv: (i, 0)),
            scratch_shapes=[pltpu.VMEM((D_MODEL, D_EXPERT), BF16), pltpu.VMEM((D_MODEL, D_EXPERT), BF16),
                            pltpu.VMEM((D_EXPERT, D_MODEL), BF16)]),
        compiler_params=_cparams(("arbitrary",)),
        name="moe",
    )(tile_expert, n_valid, xs, w_gate, w_up, w_down)


def _final_kernel(pos_ref, x1_ref, g2_ref, route_ref, l2g_ref, l2b_ref, ys_ref, o_ref, buf_ref, sem, *, tm, row0):
    base = (row0 + (pl.program_id(0) * pl.num_programs(1) + pl.program_id(1)) * tm) * 2

    def row_copy(t, k, p):
        return pltpu.make_async_copy(ys_ref.at[pl.ds(p, 1)], buf_ref.at[k, pl.ds(t, 1)], sem)

    def issue(t, c):
        row_copy(t, 0, pos_ref[base + 2 * t]).start()
        row_copy(t, 1, pos_ref[base + 2 * t + 1]).start()
        return c

    def drain(t, c):
        row_copy(0, 0, 0).wait()
        row_copy(0, 1, 0).wait()
        return c

    lax.fori_loop(0, tm, issue, 0)
    lax.fori_loop(0, tm, drain, 0)
    route = route_ref[...]
    f = route[:, 2:3] * buf_ref[0] + route[:, 3:4] * buf_ref[1]
    o_ref[...] = _ln(DEEPNORM_ALPHA * x1_ref[...] + g2_ref[...] * f) * l2g_ref[...] + l2b_ref[...]


def _final(pos, x1, mod, per_row, route, ln2_g, ln2_b, ys, row0):
    nb, l, _ = x1.shape
    tm = min(TM, l)
    row = lambda wd: pl.BlockSpec((None, tm, wd), lambda b, i, pos: (b, i, 0))
    if per_row:
        g2 = pl.BlockSpec((None, tm, D_MODEL), lambda b, i, pos: (b, i, 5))
    else:
        g2 = pl.BlockSpec((None, 1, D_MODEL), lambda b, i, pos: (b, 0, 5))
    vec = pl.BlockSpec((1, D_MODEL), lambda b, i, pos: (0, 0))
    return pl.pallas_call(
        functools.partial(_final_kernel, tm=tm, row0=row0),
        out_shape=jax.ShapeDtypeStruct((nb, l, D_MODEL), F32),
        grid_spec=pltpu.PrefetchScalarGridSpec(
            num_scalar_prefetch=1, grid=(nb, l // tm),
            in_specs=[row(D_MODEL), g2, row(LANES), vec, vec, pl.BlockSpec(memory_space=pl.ANY)],
            out_specs=row(D_MODEL),
            scratch_shapes=[pltpu.VMEM((2, tm, D_MODEL), F32), pltpu.SemaphoreType.DMA]),
        compiler_params=_cparams(("arbitrary", "arbitrary")),
        name="final",
    )(pos, x1, mod, route, ln2_g, ln2_b, ys)


def _rope_tables(pos):
    half = ROT_DIM // 2
    inv_freq = ROPE_THETA ** (-jnp.arange(half, dtype=jnp.float32) * 2.0 / ROT_DIM)
    ang = pos.astype(jnp.float32)[:, None] * inv_freq[None, :]
    cos, sin = jnp.cos(ang), jnp.sin(ang)
    n = pos.shape[0]
    one = jnp.ones((n, HEAD_DIM - ROT_DIM), F32)
    zero = jnp.zeros((n, HEAD_DIM - half), F32)
    c = jnp.concatenate([cos, cos, one], -1)
    a = jnp.concatenate([-sin, zero], -1)
    b = jnp.concatenate([jnp.zeros((n, half), F32), sin, jnp.zeros((n, HEAD_DIM - ROT_DIM), F32)], -1)
    return tuple(jnp.tile(t, (1, LANES // HEAD_DIM)) for t in (c, a, b))


def _s5_params(lam_re, lam_im, log_step, b_re, b_im, c_re, c_im):
    f32 = jnp.float32
    dt = jnp.exp(log_step.astype(f32))[:, None]
    lr, li = lam_re.astype(f32), lam_im.astype(f32)
    mag = jnp.exp(lr * dt)
    ar, ai = mag * jnp.cos(li * dt), mag * jnp.sin(li * dt)
    den = lr * lr + li * li
    cr = ((ar - 1.0) * lr + ai * li) / den
    ci = (ai * lr - (ar - 1.0) * li) / den
    br, bi = b_re.astype(f32), b_im.astype(f32)
    bbr = cr[..., None] * br - ci[..., None] * bi
    bbi = cr[..., None] * bi + ci[..., None] * br
    gpt = MXU_DIM // SSM_CH
    eye = jnp.eye(gpt, dtype=f32)

    def pack_b(m):
        m = m.reshape(N_KT, gpt, SSM_STATE, SSM_CH)
        return jnp.einsum("kgpc,gh->kgchp", m, eye).reshape(N_KT, MXU_DIM, ST_PER_KT).astype(BF16)

    def pack_c(m):
        m = m.astype(f32).reshape(N_KT, gpt, SSM_CH, SSM_STATE)
        return jnp.einsum("kgcp,gh->khpgc", m, eye).reshape(N_KT, ST_PER_KT, MXU_DIM).astype(BF16)

    wts = (pack_b(bbr), pack_b(bbi), pack_c(c_re), pack_c(-c_im.astype(f32)))
    return wts, ar.reshape(1, N_STATE), ai.reshape(1, N_STATE)


def _scan_tables(ar, ai, nb):
    assert SUBLANES // nb == 2
    a2r, a2i = ar * ar - ai * ai, 2.0 * ar * ai
    z = jnp.zeros_like(ar)
    rep = lambda first, second: jnp.concatenate([jnp.tile(first, (nb, 1)), jnp.tile(second, (nb, 1))], 0)
    return rep(z, ar), rep(z, ai), rep(ar, a2r), rep(ar, a2i)


def kernel(x_prompt, x_sample, cache_k_win, cache_v_win, state_ssm_re, state_ssm_im, c_prompt, c_sample, ada_w, ada_b,
           w_in, attn_sinks, ssm_lambda_re, ssm_lambda_im, ssm_log_step, ssm_b_re, ssm_b_im, ssm_c_re, ssm_c_im, ssm_d,
           ssm_glu_w, ssm_glu_b, attn_norm_g, ssm_norm_g, w_out, ln1_g, ln1_b, router_group_w, router_group_b,
           router_expert_w, router_expert_b, exp_w_gate, exp_w_up, exp_w_down, ln2_g, ln2_b):
    assert ada_w.shape[0] == DEPTH
    bp, lp, _ = x_prompt.shape
    bs, ls, _ = x_sample.shape
    lsp = SAMPLE_PAD

    w_in_bf = w_in[0].astype(BF16)
    wr = jnp.concatenate([router_group_w[0], router_expert_w[0]], -1)
    wr = jnp.pad(wr, ((0, 0), (0, LANES - wr.shape[1]))).astype(BF16)
    br = jnp.pad(jnp.concatenate([router_group_b[0], router_expert_b[0]], -1), (0, LANES - N_EXPERT_GROUPS - N_EXPERTS))
    wpost = dict(glu_w=ssm_glu_w[0].astype(BF16), glu_b=ssm_glu_b[0][None], attn_g=attn_norm_g[0][None],
                 ssm_g=ssm_norm_g[0][None], w_out=w_out[0].astype(BF16), ln1_g=ln1_g[0][None], ln1_b=ln1_b[0][None],
                 wr=wr, br=br[None])
    s5w, ar, ai = _s5_params(ssm_lambda_re[0], ssm_lambda_im[0], ssm_log_step[0], ssm_b_re[0], ssm_b_im[0],
                             ssm_c_re[0], ssm_c_im[0])
    d_skip = ssm_d[0][None]
    sinks = attn_sinks[0]

    n_c = bp + bs
    n_cp = -(-n_c // SUBLANES) * SUBLANES
    c_all = jnp.pad(jnp.concatenate([c_prompt, c_sample], 0), ((0, n_cp - n_c), (0, 0)))
    mod = _ada(c_all, ada_w[0], ada_b[0][None])
    mod_p = mod[:bp][:, None, :]
    mod_s = jnp.repeat(mod[bp:n_c], lsp, axis=0)[None]

    xs_pad = jnp.pad(x_sample, ((0, 0), (0, lsp - ls), (0, 0))).reshape(1, bs * lsp, D_MODEL)

    rope_p = _rope_tables(jnp.arange(lp))
    pos_s = PAST_LEN + jnp.minimum(jnp.arange(lsp), ls - 1)
    rope_s = tuple(jnp.tile(t, (bs, 1)) for t in _rope_tables(pos_s))
    q_p, k_p, v_p, u_p = _inproj(x_prompt, mod_p, w_in_bf, rope_p, False)
    q_s, k_s, v_s, u_s = _inproj(xs_pad, mod_s, w_in_bf, rope_s, True)

    attn_p = _attn_prompt(sinks, q_p, k_p, v_p)
    k_s3 = k_s.reshape(bs, lsp, D_KV)
    v_s3 = v_s.reshape(bs, lsp, D_KV)
    ck = cache_k_win[0].reshape(bs, WINDOW, D_KV)
    cv = cache_v_win[0].reshape(bs, WINDOW, D_KV)
    padk = lambda new: jnp.pad(new, ((0, 0), (0, WINDOW - lsp), (0, 0)))
    attn_s = _attn_sample(sinks, q_s.reshape(bs, lsp, D_ATTN), jnp.concatenate([ck, padk(k_s3)], 1),
                          jnp.concatenate([cv, padk(v_s3)], 1))

    tabs = _scan_tables(ar, ai, bp)
    zero_carry = jnp.zeros((SUBLANES, N_STATE), F32)
    y_p, hr_p, hi_p = _s5_prompt(u_p, s5w, tabs, d_skip, zero_carry, zero_carry)
    u_tb = jnp.transpose(u_s.reshape(bs, lsp, D_SSM)[:, :ls], (1, 0, 2))
    y_tb, hr_s, hi_s = _s5_sample(u_tb, s5w, ar, ai, d_skip, state_ssm_re[0].reshape(bs, N_STATE),
                                  state_ssm_im[0].reshape(bs, N_STATE))
    y_s = jnp.pad(jnp.transpose(y_tb, (1, 0, 2)), ((0, 0), (0, lsp - ls), (0, 0))).reshape(1, bs * lsp, D_SSM)

    cnt0 = jnp.zeros((SUBLANES, LANES), F32)
    x1_p, h2_p, route_p, cnt_p = _post(attn_p, y_p, x_prompt, mod_p, False, wpost, cnt0)
    x1_s, h2_s, route_s, cnt = _post(attn_s.reshape(1, bs * lsp, D_ATTN), y_s, xs_pad, mod_s, True, wpost, cnt_p)

    n_p, n_s = bp * lp, bs * lsp
    n_tok = n_p + n_s
    n_tiles = -(-(2 * n_tok + N_EXPERTS * (TM_MOE - 1)) // TM_MOE)
    route = jnp.concatenate([route_p.reshape(n_p, LANES), route_s.reshape(n_s, LANES)], 0)
    counts = cnt[0, ROUTE_LANE0:ROUTE_LANE0 + N_EXPERTS].astype(jnp.int32)
    padded = ((counts + TM_MOE - 1) // TM_MOE) * TM_MOE
    ends = jnp.cumsum(padded)
    offs = ends - padded
    eid = route[:, 0:2].astype(jnp.int32)
    pos = (offs[eid] + route[:, 4:6].astype(jnp.int32)).reshape(-1)
    n_valid = ends[-1] // TM_MOE
    tile_row = jnp.minimum(jnp.arange(n_tiles), n_valid - 1) * TM_MOE
    tile_expert = jnp.searchsorted(ends, tile_row, side="right").astype(jnp.int32)

    xs = jnp.zeros((n_tiles * TM_MOE, D_MODEL), F32)
    xs = _dispatch(pos, h2_p.reshape(n_p, D_MODEL), xs, 0)
    xs = _dispatch(pos, h2_s.reshape(n_s, D_MODEL), xs, n_p)
    ys = _moe(tile_expert, n_valid.reshape(1).astype(jnp.int32), xs, exp_w_gate[0], exp_w_up[0], exp_w_down[0])
    y_prompt = _final(pos, x1_p, mod_p, False, route_p, ln2_g[0][None], ln2_b[0][None], ys, 0)
    y_samp = _final(pos, x1_s, mod_s, True, route_s, ln2_g[0][None], ln2_b[0][None], ys, n_p)
    y_sample = y_samp.reshape(bs, lsp, D_MODEL)[:, :ls]

    kv5 = lambda a, n: a.reshape(1, a.shape[0], n, N_KV_HEADS, HEAD_DIM)
    k_win_p = kv5(k_p[:, lp - WINDOW:], WINDOW)
    v_win_p = kv5(v_p[:, lp - WINDOW:], WINDOW)
    k_win_s = kv5(jnp.concatenate([ck[:, ls:], k_s3[:, :ls]], 1), WINDOW)
    v_win_s = kv5(jnp.concatenate([cv[:, ls:], v_s3[:, :ls]], 1), WINDOW)
    st = lambda a: a.reshape(1, a.shape[0], N_SSM_GROUPS, SSM_STATE)
    return (y_prompt, y_sample, k_win_p, v_win_p, st(hr_p[bp:2 * bp]), st(hi_p[bp:2 * bp]),
            k_win_s, v_win_s, st(hr_s), st(hi_s))
```

```python
import functools
import math

import jax
import jax.numpy as jnp
from jax import lax
from jax.experimental import pallas as pl
from jax.experimental.pallas import tpu as pltpu

F32 = jnp.float32
BF16 = jnp.bfloat16

D_MODEL = 2048
D_ATTN = 1024
D_SSM = 1024
HEAD_DIM = 64
N_HEADS = 16
N_KV_HEADS = 4
REP = 4
D_KV = 256
ROT_DIM = 16
ROPE_THETA = 500000.0
WINDOW = 128
SSM_CH = 16
N_SSM_GROUPS = 64
SSM_STATE = 64
N_STATE = N_SSM_GROUPS * SSM_STATE
PROJ_COLS = D_ATTN + 2 * D_KV + D_SSM
N_EXPERT_GROUPS = 4
EXPERTS_PER_GROUP = 8
N_EXPERTS = 32
D_EXPERT = 512
DEPTH = 1
DEEPNORM_ALPHA = (2.0 * DEPTH) ** 0.25
LN_EPS = 1e-5
PAST_LEN = 16384

LANES = 128
SUBLANES = 8
MXU_DIM = 256
TM = 256
TM_MOE = 256
S5_TT = 64
SAMPLE_PAD = 8
ROUTE_LANE0 = N_EXPERT_GROUPS
VMEM_LIMIT = 56 * 1024 * 1024


def _cparams(sem):
    return pltpu.CompilerParams(dimension_semantics=sem, vmem_limit_bytes=VMEM_LIMIT)


def _ln(x):
    mu = jnp.mean(x, axis=-1, keepdims=True)
    xc = x - mu
    var = jnp.mean(xc * xc, axis=-1, keepdims=True)
    return xc * lax.rsqrt(var + LN_EPS)


def _rms(x):
    return x * lax.rsqrt(jnp.mean(x * x, axis=-1, keepdims=True) + LN_EPS)


def _ada_kernel(c_ref, w_ref, b_ref, o_ref):
    c = c_ref[...]
    s = c * jax.nn.sigmoid(c)
    o_ref[...] = jnp.dot(s.astype(BF16), w_ref[...].astype(BF16), preferred_element_type=F32) + b_ref[...]


def _ada(c_all, ada_w, ada_b):
    n, tn = c_all.shape[0], 1024
    return pl.pallas_call(
        _ada_kernel,
        out_shape=jax.ShapeDtypeStruct((n, 6 * D_MODEL), F32),
        grid=(6 * D_MODEL // tn,),
        in_specs=[pl.BlockSpec((n, D_MODEL), lambda j: (0, 0)),
                  pl.BlockSpec((D_MODEL, tn), lambda j: (0, j)),
                  pl.BlockSpec((1, tn), lambda j: (0, j))],
        out_specs=pl.BlockSpec((n, tn), lambda j: (0, j)),
        compiler_params=_cparams(("arbitrary",)),
        name="ada",
    )(c_all, ada_w, ada_b)


def _mod_spec(per_row, tm, chunk):
    if per_row:
        return pl.BlockSpec((None, tm, D_MODEL), lambda b, i: (b, i, chunk))
    return pl.BlockSpec((None, 1, D_MODEL), lambda b, i: (b, 0, chunk))


def _inproj_kernel(x_ref, sh_ref, sc_ref, w_ref, rc_ref, ra_ref, rb_ref, q_ref, k_ref, v_ref, u_ref):
    h = _ln(x_ref[...]) * (1.0 + sc_ref[...]) + sh_ref[...]
    proj = jnp.dot(h.astype(BF16), w_ref[...], preferred_element_type=F32)
    rc, ra, rb = rc_ref[...], ra_ref[...], rb_ref[...]

    def rope(t):
        return t * rc + pltpu.roll(t, LANES - ROT_DIM // 2, 1) * ra + pltpu.roll(t, ROT_DIM // 2, 1) * rb

    for j in range(D_ATTN // LANES):
        q_ref[:, j * LANES:(j + 1) * LANES] = (rope(proj[:, j * LANES:(j + 1) * LANES]) * HEAD_DIM ** -0.5).astype(BF16)
    for j in range(D_KV // LANES):
        c0 = D_ATTN + j * LANES
        k_ref[:, j * LANES:(j + 1) * LANES] = rope(proj[:, c0:c0 + LANES])
    v_ref[...] = proj[:, D_ATTN + D_KV:D_ATTN + 2 * D_KV]
    u_ref[...] = proj[:, D_ATTN + 2 * D_KV:]


def _inproj(x, mod, w_in_bf, rope_tabs, per_row):
    nb, l, _ = x.shape
    tm = min(TM, l)
    row = lambda w: pl.BlockSpec((None, tm, w), lambda b, i: (b, i, 0))
    tab = pl.BlockSpec((tm, LANES), lambda b, i: (i, 0))
    return pl.pallas_call(
        _inproj_kernel,
        out_shape=(jax.ShapeDtypeStruct((nb, l, D_ATTN), BF16), jax.ShapeDtypeStruct((nb, l, D_KV), F32),
                   jax.ShapeDtypeStruct((nb, l, D_KV), F32), jax.ShapeDtypeStruct((nb, l, D_SSM), F32)),
        grid=(nb, l // tm),
        in_specs=[row(D_MODEL), _mod_spec(per_row, tm, 0), _mod_spec(per_row, tm, 1),
                  pl.BlockSpec((D_MODEL, PROJ_COLS), lambda b, i: (0, 0)), tab, tab, tab],
        out_specs=(row(D_ATTN), row(D_KV), row(D_KV), row(D_SSM)),
        compiler_params=_cparams(("arbitrary", "arbitrary")),
        name="inproj",
    )(x, mod, mod, w_in_bf, *rope_tabs)


ATTN_SAMPLE_BATCH = 8


def _attn_kernel(sink_ref, q_ref, kp_ref, kc_ref, vp_ref, vc_ref, o_ref, *, lq, prev_from_block):
    m_rows = REP * lq
    ii = lax.broadcasted_iota(jnp.int32, (m_rows, WINDOW), 0) & (lq - 1)
    jj = lax.broadcasted_iota(jnp.int32, (m_rows, WINDOW), 1)
    from_prev = jj > ii
    if prev_from_block:
        dead = jj > ii + jnp.where(pl.program_id(1) > 0, WINDOW, 0)
    rr = lax.broadcasted_iota(jnp.int32, (m_rows, 1), 0)
    dn = (((1,), (1,)), ((), ()))
    for bi in range(q_ref.shape[0]):
        for g in range(N_KV_HEADS):
            qg = q_ref[bi, :, g * REP * HEAD_DIM:(g + 1) * REP * HEAD_DIM].astype(F32)
            qs = jnp.concatenate([qg[:, r * HEAD_DIM:(r + 1) * HEAD_DIM] for r in range(REP)], axis=0).astype(BF16)
            ks = slice(g * HEAD_DIM, (g + 1) * HEAD_DIM)
            s_p = lax.dot_general(qs, kp_ref[bi, :, ks].astype(BF16), dn, preferred_element_type=F32)
            s_c = lax.dot_general(qs, kc_ref[bi, :, ks].astype(BF16), dn, preferred_element_type=F32)
            s = jnp.where(from_prev, s_p, s_c)
            if prev_from_block:
                s = jnp.where(dead, -jnp.inf, s)
            sink = jnp.zeros((m_rows, 1), F32)
            for r in range(REP):
                sink = jnp.where((rr >= r * lq) & (rr < (r + 1) * lq), sink_ref[g * REP + r], sink)
            m = jnp.maximum(jnp.max(s, axis=-1, keepdims=True), sink)
            p = jnp.exp(s - m)
            p = p / (jnp.sum(p, axis=-1, keepdims=True) + jnp.exp(sink - m))
            o = (jnp.dot(jnp.where(from_prev, p, 0.0).astype(BF16), vp_ref[bi, :, ks].astype(BF16),
                         preferred_element_type=F32)
                 + jnp.dot(jnp.where(from_prev, 0.0, p).astype(BF16), vc_ref[bi, :, ks].astype(BF16),
                           preferred_element_type=F32))
            for r in range(REP):
                h = g * REP + r
                o_ref[bi, :, h * HEAD_DIM:(h + 1) * HEAD_DIM] = o[r * lq:(r + 1) * lq]


def _attn_prompt(sinks, q, k, v):
    nb, l, _ = q.shape
    nblk = l // WINDOW
    cur = lambda w: pl.BlockSpec((1, WINDOW, w), lambda b, n: (b, n, 0))
    prev = lambda w: pl.BlockSpec((1, WINDOW, w), lambda b, n: (b, jnp.maximum(n - 1, 0), 0))
    return pl.pallas_call(
        functools.partial(_attn_kernel, lq=WINDOW, prev_from_block=True),
        out_shape=jax.ShapeDtypeStruct((nb, l, D_ATTN), F32),
        grid=(nb, nblk),
        in_specs=[pl.BlockSpec(memory_space=pltpu.SMEM), cur(D_ATTN), prev(D_KV), cur(D_KV), prev(D_KV), cur(D_KV)],
        out_specs=cur(D_ATTN),
        compiler_params=_cparams(("arbitrary", "arbitrary")),
        name="attn_prompt",
    )(sinks, q, k, k, v, v)


def _attn_sample(sinks, q, kcat, vcat):
    nb, lq, _ = q.shape
    nbb = ATTN_SAMPLE_BATCH
    kblk = lambda n: pl.BlockSpec((nbb, WINDOW, D_KV), lambda b: (b, n, 0))
    return pl.pallas_call(
        functools.partial(_attn_kernel, lq=lq, prev_from_block=False),
        out_shape=jax.ShapeDtypeStruct((nb, lq, D_ATTN), F32),
        grid=(nb // nbb,),
        in_specs=[pl.BlockSpec(memory_space=pltpu.SMEM), pl.BlockSpec((nbb, lq, D_ATTN), lambda b: (b, 0, 0)),
                  kblk(0), kblk(1), kblk(0), kblk(1)],
        out_specs=pl.BlockSpec((nbb, lq, D_ATTN), lambda b: (b, 0, 0)),
        compiler_params=_cparams(("arbitrary",)),
        name="attn_sample",
    )(sinks, q, kcat, kcat, vcat, vcat)


N_KT = D_SSM // MXU_DIM
ST_PER_KT = N_STATE // N_KT
N_SLAB = D_SSM // LANES


def _cproj(hr_ref, hi_ref, cre_ref, cimn_ref, kt):
    cs = slice(kt * ST_PER_KT, (kt + 1) * ST_PER_KT)
    return (jnp.dot(hr_ref[:, cs].astype(BF16), cre_ref[kt], preferred_element_type=F32)
            + jnp.dot(hi_ref[:, cs].astype(BF16), cimn_ref[kt], preferred_element_type=F32))


def _s5_prompt_kernel(u_ref, bre_ref, bim_ref, cre_ref, cimn_ref, a1r_ref, a1i_ref, par_ref, pai_ref, d_ref,
                      h0r_ref, h0i_ref, y_ref, cr_ref, ci_ref, il_ref, xr_ref, xi_ref, *, nb, tt):
    rows = nb * tt

    @pl.when(pl.program_id(0) == 0)
    def _():
        cr_ref[...] = h0r_ref[...]
        ci_ref[...] = h0i_ref[...]

    for b in range(nb):
        for j in range(N_SLAB):
            il_ref[j, pl.ds(b, tt, stride=nb), :] = u_ref[b, :, j * LANES:(j + 1) * LANES]
    for kt in range(N_KT):
        ub = jnp.concatenate([il_ref[2 * kt], il_ref[2 * kt + 1]], axis=1).astype(BF16)
        cs = slice(kt * ST_PER_KT, (kt + 1) * ST_PER_KT)
        xr_ref[:, cs] = jnp.dot(ub, bre_ref[kt], preferred_element_type=F32)
        xi_ref[:, cs] = jnp.dot(ub, bim_ref[kt], preferred_element_type=F32)

    half = lax.broadcasted_iota(jnp.int32, (SUBLANES, LANES), 0) < nb

    def step(i, carry):
        r0 = pl.multiple_of(i * SUBLANES, SUBLANES)
        for j in range(N_STATE // LANES):
            ls = slice(j * LANES, (j + 1) * LANES)
            x_r = xr_ref[pl.ds(r0, SUBLANES), ls]
            x_i = xi_ref[pl.ds(r0, SUBLANES), ls]
            s_r = pltpu.roll(x_r, nb, 0)
            s_i = pltpu.roll(x_i, nb, 0)
            a1r, a1i = a1r_ref[:, ls], a1i_ref[:, ls]
            c_r, c_i = cr_ref[:, ls], ci_ref[:, ls]
            par, pai = par_ref[:, ls], pai_ref[:, ls]
            h_r = x_r + (a1r * s_r - a1i * s_i) + (par * c_r - pai * c_i)
            h_i = x_i + (a1r * s_i + a1i * s_r) + (par * c_i + pai * c_r)
            xr_ref[pl.ds(r0, SUBLANES), ls] = h_r
            xi_ref[pl.ds(r0, SUBLANES), ls] = h_i
            cr_ref[:, ls] = jnp.where(half, pltpu.roll(h_r, nb, 0), h_r)
            ci_ref[:, ls] = jnp.where(half, pltpu.roll(h_i, nb, 0), h_i)
        return carry

    lax.fori_loop(0, rows // SUBLANES, step, 0)

    for kt in range(N_KT):
        y = _cproj(xr_ref, xi_ref, cre_ref, cimn_ref, kt)
        il_ref[2 * kt] = y[:, :LANES]
        il_ref[2 * kt + 1] = y[:, LANES:]
    for b in range(nb):
        for j in range(N_SLAB):
            ls = slice(j * LANES, (j + 1) * LANES)
            y_ref[b, :, ls] = il_ref[j, pl.ds(b, tt, stride=nb), :] + d_ref[:, ls] * u_ref[b, :, ls]


def _s5_prompt(u, wts, tabs, d_skip, h0r, h0i):
    nb, l, _ = u.shape
    tt = S5_TT
    rows = nb * tt
    full = lambda a: pl.BlockSpec(a.shape, lambda i: (0,) * a.ndim)
    blk = pl.BlockSpec((nb, tt, D_SSM), lambda i: (0, i, 0))
    carry = jax.ShapeDtypeStruct((SUBLANES, N_STATE), F32)
    return pl.pallas_call(
        functools.partial(_s5_prompt_kernel, nb=nb, tt=tt),
        out_shape=(jax.ShapeDtypeStruct((nb, l, D_SSM), F32), carry, carry),
        grid=(l // tt,),
        in_specs=[blk] + [full(a) for a in (*wts, *tabs, d_skip, h0r, h0i)],
        out_specs=(blk, pl.BlockSpec((SUBLANES, N_STATE), lambda i: (0, 0)),
                   pl.BlockSpec((SUBLANES, N_STATE), lambda i: (0, 0))),
        scratch_shapes=[pltpu.VMEM((N_SLAB, rows, LANES), F32), pltpu.VMEM((rows, N_STATE), F32),
                        pltpu.VMEM((rows, N_STATE), F32)],
        compiler_params=_cparams(("arbitrary",)),
        name="s5_prompt",
    )(u, *wts, *tabs, d_skip, h0r, h0i)


def _s5_sample_kernel(u_ref, bre_ref, bim_ref, cre_ref, cimn_ref, ar_ref, ai_ref, d_ref, h0r_ref, h0i_ref,
                      y_ref, sr_ref, si_ref, xr_ref, xi_ref, *, nt):
    sr_ref[...] = h0r_ref[...]
    si_ref[...] = h0i_ref[...]
    for t in range(nt):
        for kt in range(N_KT):
            ub = u_ref[t, :, kt * MXU_DIM:(kt + 1) * MXU_DIM].astype(BF16)
            cs = slice(kt * ST_PER_KT, (kt + 1) * ST_PER_KT)
            xr_ref[:, cs] = jnp.dot(ub, bre_ref[kt], preferred_element_type=F32)
            xi_ref[:, cs] = jnp.dot(ub, bim_ref[kt], preferred_element_type=F32)
        ar, ai = ar_ref[...], ai_ref[...]
        s_r, s_i = sr_ref[...], si_ref[...]
        sr_ref[...] = xr_ref[...] + (ar * s_r - ai * s_i)
        si_ref[...] = xi_ref[...] + (ar * s_i + ai * s_r)
        for kt in range(N_KT):
            ys = slice(kt * MXU_DIM, (kt + 1) * MXU_DIM)
            y_ref[t, :, ys] = _cproj(sr_ref, si_ref, cre_ref, cimn_ref, kt) + d_ref[:, ys] * u_ref[t, :, ys]


def _s5_sample(u_tb, wts, ar, ai, d_skip, h0r, h0i):
    nt, nb, _ = u_tb.shape
    st = jax.ShapeDtypeStruct((nb, N_STATE), F32)
    args = (u_tb, *wts, ar, ai, d_skip, h0r, h0i)
    full = lambda a: pl.BlockSpec(a.shape, lambda i: (0,) * a.ndim)
    return pl.pallas_call(
        functools.partial(_s5_sample_kernel, nt=nt),
        out_shape=(jax.ShapeDtypeStruct((nt, nb, D_SSM), F32), st, st),
        grid=(1,),
        in_specs=[full(a) for a in args],
        out_specs=(pl.BlockSpec((nt, nb, D_SSM), lambda i: (0, 0, 0)), pl.BlockSpec((nb, N_STATE), lambda i: (0, 0)),
                   pl.BlockSpec((nb, N_STATE), lambda i: (0, 0))),
        scratch_shapes=[pltpu.VMEM((nb, N_STATE), F32), pltpu.VMEM((nb, N_STATE), F32)],
        compiler_params=_cparams(("arbitrary",)),
        name="s5_sample",
    )(*args)


def _post_kernel(attn_ref, yssm_ref, x_ref, g1_ref, sh2_ref, sc2_ref, gluw_ref, glub_ref, ga_ref, gs_ref, wout_ref,
                 l1g_ref, l1b_ref, wr_ref, br_ref, tri_ref, cnt0_ref, x1_ref, h2_ref, route_ref, cnt_ref):
    tm = x_ref.shape[0]
    first = (pl.program_id(0) == 0) & (pl.program_id(1) == 0)

    @pl.when(first)
    def _():
        cnt_ref[...] = cnt0_ref[...]

    z = jax.nn.gelu(yssm_ref[...])
    ssm = z * jax.nn.sigmoid(jnp.dot(z.astype(BF16), gluw_ref[...], preferred_element_type=F32) + glub_ref[...])
    mixed_a = (_rms(attn_ref[...]) * ga_ref[...]).astype(BF16)
    mixed_s = (_rms(ssm) * gs_ref[...]).astype(BF16)
    o = (jnp.dot(mixed_a, wout_ref[:D_ATTN, :], preferred_element_type=F32)
         + jnp.dot(mixed_s, wout_ref[D_ATTN:, :], preferred_element_type=F32))
    x1 = _ln(DEEPNORM_ALPHA * x_ref[...] + g1_ref[...] * o) * l1g_ref[...] + l1b_ref[...]
    x1_ref[...] = x1
    h2 = _ln(x1) * (1.0 + sc2_ref[...]) + sh2_ref[...]
    h2_ref[...] = h2
    logits = jnp.dot(h2.astype(BF16), wr_ref[...], preferred_element_type=F32) + br_ref[...]

    lane = lax.broadcasted_iota(jnp.int32, (tm, LANES), 1).astype(F32)
    big = float(4 * LANES)
    neg = -jnp.inf
    gl = jnp.where(lane < N_EXPERT_GROUPS, logits, neg)
    gp = jnp.exp(gl - jnp.max(gl, axis=-1, keepdims=True))
    gp = gp / jnp.sum(gp, axis=-1, keepdims=True)
    g_val = jnp.max(gp, axis=-1, keepdims=True)
    g_idx = jnp.min(jnp.where(gp == g_val, lane, big), axis=-1, keepdims=True)
    lo = ROUTE_LANE0 + EXPERTS_PER_GROUP * g_idx
    emask = (lane >= lo) & (lane < lo + EXPERTS_PER_GROUP)
    el = jnp.where(emask, logits, neg)
    ep = jnp.exp(el - jnp.max(el, axis=-1, keepdims=True))
    ep = jnp.where(emask, ep / jnp.sum(ep, axis=-1, keepdims=True), -1.0)
    v1 = jnp.max(ep, axis=-1, keepdims=True)
    i1 = jnp.min(jnp.where(ep == v1, lane, big), axis=-1, keepdims=True)
    ep2 = jnp.where(lane == i1, -1.0, ep)
    v2 = jnp.max(ep2, axis=-1, keepdims=True)
    i2 = jnp.min(jnp.where(ep2 == v2, lane, big), axis=-1, keepdims=True)
    vs = v1 + v2
    w1 = g_val * (v1 / vs)
    w2 = g_val * (v2 / vs)
    hit = jnp.where((lane == i1) | (lane == i2), 1.0, 0.0)
    before = jnp.dot(tri_ref[...], hit.astype(BF16), preferred_element_type=F32) + cnt_ref[0:1, :]
    r1 = jnp.sum(jnp.where(lane == i1, before, 0.0), axis=-1, keepdims=True)
    r2 = jnp.sum(jnp.where(lane == i2, before, 0.0), axis=-1, keepdims=True)
    cnt_ref[...] = cnt_ref[...] + jnp.sum(hit, axis=0, keepdims=True)
    e1 = i1 - ROUTE_LANE0
    e2 = i2 - ROUTE_LANE0
    route = jnp.zeros((tm, LANES), F32)
    for n, val in enumerate((e1, e2, w1, w2, r1, r2)):
        route = jnp.where(lane == n, val, route)
    route_ref[...] = route


def _post(attn, yssm, x, mod, per_row, w, cnt0):
    nb, l, _ = x.shape
    tm = min(TM, l)
    row = lambda wd: pl.BlockSpec((None, tm, wd), lambda b, i: (b, i, 0))
    full = lambda a: pl.BlockSpec(a.shape, lambda b, i: (0,) * a.ndim)
    tri = jnp.tril(jnp.ones((tm, tm), F32), -1).astype(BF16)
    consts = (w["glu_w"], w["glu_b"], w["attn_g"], w["ssm_g"], w["w_out"], w["ln1_g"], w["ln1_b"], w["wr"], w["br"],
              tri, cnt0)
    return pl.pallas_call(
        _post_kernel,
        out_shape=(jax.ShapeDtypeStruct((nb, l, D_MODEL), F32), jax.ShapeDtypeStruct((nb, l, D_MODEL), F32),
                   jax.ShapeDtypeStruct((nb, l, LANES), F32), jax.ShapeDtypeStruct((SUBLANES, LANES), F32)),
        grid=(nb, l // tm),
        in_specs=[row(D_ATTN), row(D_SSM), row(D_MODEL), _mod_spec(per_row, tm, 2), _mod_spec(per_row, tm, 3),
                  _mod_spec(per_row, tm, 4)] + [full(a) for a in consts],
        out_specs=(row(D_MODEL), row(D_MODEL), row(LANES), pl.BlockSpec((SUBLANES, LANES), lambda b, i: (0, 0))),
        compiler_params=_cparams(("arbitrary", "arbitrary")),
        name="post",
    )(attn, yssm, x, mod, mod, mod, *consts)


def _dispatch_kernel(pos_ref, h2_ref, xs_in_ref, xs_ref, sem, *, tm, row0):
    del xs_in_ref
    base = (row0 + pl.program_id(0) * tm) * 2

    def row_copy(t, p):
        return pltpu.make_async_copy(h2_ref.at[pl.ds(t, 1)], xs_ref.at[pl.ds(p, 1)], sem)

    def issue(t, c):
        row_copy(t, pos_ref[base + 2 * t]).start()
        row_copy(t, pos_ref[base + 2 * t + 1]).start()
        return c

    def drain(t, c):
        row_copy(0, 0).wait()
        row_copy(0, 0).wait()
        return c

    lax.fori_loop(0, tm, issue, 0)
    lax.fori_loop(0, tm, drain, 0)


def _dispatch(pos, h2_rows, xs, row0):
    n = h2_rows.shape[0]
    tm = min(TM, n)
    return pl.pallas_call(
        functools.partial(_dispatch_kernel, tm=tm, row0=row0),
        out_shape=jax.ShapeDtypeStruct(xs.shape, xs.dtype),
        grid_spec=pltpu.PrefetchScalarGridSpec(
            num_scalar_prefetch=1, grid=(n // tm,),
            in_specs=[pl.BlockSpec((tm, D_MODEL), lambda i, pos: (i, 0)), pl.BlockSpec(memory_space=pl.ANY)],
            out_specs=pl.BlockSpec(memory_space=pl.ANY),
            scratch_shapes=[pltpu.SemaphoreType.DMA]),
        input_output_aliases={2: 0},
        compiler_params=_cparams(("arbitrary",)),
        name="dispatch",
    )(pos, h2_rows, xs)


def _moe_kernel(te_ref, nv_ref, x_ref, wg_ref, wu_ref, wd_ref, y_ref, wgb_ref, wub_ref, wdb_ref):
    i = pl.program_id(0)
    valid = i < nv_ref[0]
    changed = (i == 0) | (te_ref[i] != te_ref[jnp.maximum(i - 1, 0)])

    @pl.when(valid & changed)
    def _():
        wgb_ref[...] = wg_ref[...].astype(BF16)
        wub_ref[...] = wu_ref[...].astype(BF16)
        wdb_ref[...] = wd_ref[...].astype(BF16)

    @pl.when(valid)
    def _():
        xb = x_ref[...].astype(BF16)
        hg = jnp.dot(xb, wgb_ref[...], preferred_element_type=F32)
        hu = jnp.dot(xb, wub_ref[...], preferred_element_type=F32)
        act = (hg * jax.nn.sigmoid(hg)) * hu
        y_ref[...] = jnp.dot(act.astype(BF16), wdb_ref[...], preferred_element_type=F32)

    @pl.when(jnp.logical_not(valid))
    def _():
        y_ref[...] = jnp.zeros(y_ref.shape, y_ref.dtype)


def _moe(tile_expert, n_valid, xs, w_gate, w_up, w_down):
    n_tiles = xs.shape[0] // TM_MOE
    return pl.pallas_call(
        _moe_kernel,
        out_shape=jax.ShapeDtypeStruct((xs.shape[0], D_MODEL), F32),
        grid_spec=pltpu.PrefetchScalarGridSpec(
            num_scalar_prefetch=2, grid=(n_tiles,),
            in_specs=[pl.BlockSpec((TM_MOE, D_MODEL), lambda i, te, nv: (i, 0)),
                      pl.BlockSpec((None, D_MODEL, D_EXPERT), lambda i, te, nv: (te[i], 0, 0)),
                      pl.BlockSpec((None, D_MODEL, D_EXPERT), lambda i, te, nv: (te[i], 0, 0)),
                      pl.BlockSpec((None, D_EXPERT, D_MODEL), lambda i, te, nv: (te[i], 0, 0))],
            out_specs=pl.BlockSpec((TM_MOE, D_MODEL), lambda i, te, nv: (i, 0)),
            scratch_shapes=[pltpu.VMEM((D_MODEL, D_EXPERT), BF16), pltpu.VMEM((D_MODEL, D_EXPERT), BF16),
                            pltpu.VMEM((D_EXPERT, D_MODEL), BF16)]),
        compiler_params=_cparams(("arbitrary",)),
        name="moe",
    )(tile_expert, n_valid, xs, w_gate, w_up, w_down)


def _final_kernel(pos_ref, x1_ref, g2_ref, route_ref, l2g_ref, l2b_ref, ys_ref, o_ref, buf_ref, sem, *, tm, row0):
    base = (row0 + (pl.program_id(0) * pl.num_programs(1) + pl.program_id(1)) * tm) * 2

    def row_copy(t, k, p):
        return pltpu.make_async_copy(ys_ref.at[pl.ds(p, 1)], buf_ref.at[k, pl.ds(t, 1)], sem)

    def issue(t, c):
        row_copy(t, 0, pos_ref[base + 2 * t]).start()
        row_copy(t, 1, pos_ref[base + 2 * t + 1]).start()
        return c

    def drain(t, c):
        row_copy(0, 0, 0).wait()
        row_copy(0, 1, 0).wait()
        return c

    lax.fori_loop(0, tm, issue, 0)
    lax.fori_loop(0, tm, drain, 0)
    route = route_ref[...]
    f = route[:, 2:3] * buf_ref[0] + route[:, 3:4] * buf_ref[1]
    o_ref[...] = _ln(DEEPNORM_ALPHA * x1_ref[...] + g2_ref[...] * f) * l2g_ref[...] + l2b_ref[...]


def _final(pos, x1, mod, per_row, route, ln2_g, ln2_b, ys, row0):
    nb, l, _ = x1.shape
    tm = min(TM, l)
    row = lambda wd: pl.BlockSpec((None, tm, wd), lambda b, i, pos: (b, i, 0))
    if per_row:
        g2 = pl.BlockSpec((None, tm, D_MODEL), lambda b, i, pos: (b, i, 5))
    else:
        g2 = pl.BlockSpec((None, 1, D_MODEL), lambda b, i, pos: (b, 0, 5))
    vec = pl.BlockSpec((1, D_MODEL), lambda b, i, pos: (0, 0))
    return pl.pallas_call(
        functools.partial(_final_kernel, tm=tm, row0=row0),
        out_shape=jax.ShapeDtypeStruct((nb, l, D_MODEL), F32),
        grid_spec=pltpu.PrefetchScalarGridSpec(
            num_scalar_prefetch=1, grid=(nb, l // tm),
            in_specs=[row(D_MODEL), g2, row(LANES), vec, vec, pl.BlockSpec(memory_space=pl.ANY)],
            out_specs=row(D_MODEL),
            scratch_shapes=[pltpu.VMEM((2, tm, D_MODEL), F32), pltpu.SemaphoreType.DMA]),
        compiler_params=_cparams(("arbitrary", "arbitrary")),
        name="final",
    )(pos, x1, mod, route, ln2_g, ln2_b, ys)


def _rope_tables(pos):
    half = ROT_DIM // 2
    inv_freq = ROPE_THETA ** (-jnp.arange(half, dtype=jnp.float32) * 2.0 / ROT_DIM)
    ang = pos.astype(jnp.float32)[:, None] * inv_freq[None, :]
    cos, sin = jnp.cos(ang), jnp.sin(ang)
    n = pos.shape[0]
    one = jnp.ones((n, HEAD_DIM - ROT_DIM), F32)
    zero = jnp.zeros((n, HEAD_DIM - half), F32)
    c = jnp.concatenate([cos, cos, one], -1)
    a = jnp.concatenate([-sin, zero], -1)
    b = jnp.concatenate([jnp.zeros((n, half), F32), sin, jnp.zeros((n, HEAD_DIM - ROT_DIM), F32)], -1)
    return tuple(jnp.tile(t, (1, LANES // HEAD_DIM)) for t in (c, a, b))


def _s5_params(lam_re, lam_im, log_step, b_re, b_im, c_re, c_im):
    f32 = jnp.float32
    dt = jnp.exp(log_step.astype(f32))[:, None]
    lr, li = lam_re.astype(f32), lam_im.astype(f32)
    mag = jnp.exp(lr * dt)
    ar, ai = mag * jnp.cos(li * dt), mag * jnp.sin(li * dt)
    den = lr * lr + li * li
    cr = ((ar - 1.0) * lr + ai * li) / den
    ci = (ai * lr - (ar - 1.0) * li) / den
    br, bi = b_re.astype(f32), b_im.astype(f32)
    bbr = cr[..., None] * br - ci[..., None] * bi
    bbi = cr[..., None] * bi + ci[..., None] * br
    gpt = MXU_DIM // SSM_CH
    eye = jnp.eye(gpt, dtype=f32)

    def pack_b(m):
        m = m.reshape(N_KT, gpt, SSM_STATE, SSM_CH)
        return jnp.einsum("kgpc,gh->kgchp", m, eye).reshape(N_KT, MXU_DIM, ST_PER_KT).astype(BF16)

    def pack_c(m):
        m = m.astype(f32).reshape(N_KT, gpt, SSM_CH, SSM_STATE)
        return jnp.einsum("kgcp,gh->khpgc", m, eye).reshape(N_KT, ST_PER_KT, MXU_DIM).astype(BF16)

    wts = (pack_b(bbr), pack_b(bbi), pack_c(c_re), pack_c(-c_im.astype(f32)))
    return wts, ar.reshape(1, N_STATE), ai.reshape(1, N_STATE)


def _scan_tables(ar, ai, nb):
    assert SUBLANES // nb == 2
    a2r, a2i = ar * ar - ai * ai, 2.0 * ar * ai
    z = jnp.zeros_like(ar)
    rep = lambda first, second: jnp.concatenate([jnp.tile(first, (nb, 1)), jnp.tile(second, (nb, 1))], 0)
    return rep(z, ar), rep(z, ai), rep(ar, a2r), rep(ai, a2i)


def kernel(x_prompt, x_sample, cache_k_win, cache_v_win, state_ssm_re, state_ssm_im, c_prompt, c_sample, ada_w, ada_b,
           w_in, attn_sinks, ssm_lambda_re, ssm_lambda_im, ssm_log_step, ssm_b_re, ssm_b_im, ssm_c_re, ssm_c_im, ssm_d,
           ssm_glu_w, ssm_glu_b, attn_norm_g, ssm_norm_g, w_out, ln1_g, ln1_b, router_group_w, router_group_b,
           router_expert_w, router_expert_b, exp_w_gate, exp_w_up, exp_w_down, ln2_g, ln2_b):
    assert ada_w.shape[0] == DEPTH
    bp, lp, _ = x_prompt.shape
    bs, ls, _ = x_sample.shape
    lsp = SAMPLE_PAD

    w_in_bf = w_in[0].astype(BF16)
    wr = jnp.concatenate([router_group_w[0], router_expert_w[0]], -1)
    wr = jnp.pad(wr, ((0, 0), (0, LANES - wr.shape[1]))).astype(BF16)
    br = jnp.pad(jnp.concatenate([router_group_b[0], router_expert_b[0]], -1), (0, LANES - N_EXPERT_GROUPS - N_EXPERTS))
    wpost = dict(glu_w=ssm_glu_w[0].astype(BF16), glu_b=ssm_glu_b[0][None], attn_g=attn_norm_g[0][None],
                 ssm_g=ssm_norm_g[0][None], w_out=w_out[0].astype(BF16), ln1_g=ln1_g[0][None], ln1_b=ln1_b[0][None],
                 wr=wr, br=br[None])
    s5w, ar, ai = _s5_params(ssm_lambda_re[0], ssm_lambda_im[0], ssm_log_step[0], ssm_b_re[0], ssm_b_im[0],
                             ssm_c_re[0], ssm_c_im[0])
    d_skip = ssm_d[0][None]
    sinks = attn_sinks[0]

    n_c = bp + bs
    n_cp = -(-n_c // SUBLANES) * SUBLANES
    c_all = jnp.pad(jnp.concatenate([c_prompt, c_sample], 0), ((0, n_cp - n_c), (0, 0)))
    mod = _ada(c_all, ada_w[0], ada_b[0][None])
    mod_p = mod[:bp][:, None, :]
    mod_s = jnp.repeat(mod[bp:n_c], lsp, axis=0)[None]

    xs_pad = jnp.pad(x_sample, ((0, 0), (0, lsp - ls), (0, 0))).reshape(1, bs * lsp, D_MODEL)

    rope_p = _rope_tables(jnp.arange(lp))
    pos_s = PAST_LEN + jnp.minimum(jnp.arange(lsp), ls - 1)
    rope_s = tuple(jnp.tile(t, (bs, 1)) for t in _rope_tables(pos_s))
    q_p, k_p, v_p, u_p = _inproj(x_prompt, mod_p, w_in_bf, rope_p, False)
    q_s, k_s, v_s, u_s = _inproj(xs_pad, mod_s, w_in_bf, rope_s, True)

    attn_p = _attn_prompt(sinks, q_p, k_p, v_p)
    k_s3 = k_s.reshape(bs, lsp, D_KV)
    v_s3 = v_s.reshape(bs, lsp, D_KV)
    ck = cache_k_win[0].reshape(bs, WINDOW, D_KV)
    cv = cache_v_win[0].reshape(bs, WINDOW, D_KV)
    padk = lambda new: jnp.pad(new, ((0, 0), (0, WINDOW - lsp), (0, 0)))
    attn_s = _attn_sample(sinks, q_s.reshape(bs, lsp, D_ATTN), jnp.concatenate([ck, padk(k_s3)], 1),
                          jnp.concatenate([cv, padk(v_s3)], 1))

    tabs = _scan_tables(ar, ai, bp)
    zero_carry = jnp.zeros((SUBLANES, N_STATE), F32)
    y_p, hr_p, hi_p = _s5_prompt(u_p, s5w, tabs, d_skip, zero_carry, zero_carry)
    u_tb = jnp.transpose(u_s.reshape(bs, lsp, D_SSM)[:, :ls], (1, 0, 2))
    y_tb, hr_s, hi_s = _s5_sample(u_tb, s5w, ar, ai, d_skip, state_ssm_re[0].reshape(bs, N_STATE),
                                  state_ssm_im[0].reshape(bs, N_STATE))
    y_s = jnp.pad(jnp.transpose(y_tb, (1, 0, 2)), ((0, 0), (0, lsp - ls), (0, 0))).reshape(1, bs * lsp, D_SSM)

    cnt0 = jnp.zeros((SUBLANES, LANES), F32)
    x1_p, h2_p, route_p, cnt_p = _post(attn_p, y_p, x_prompt, mod_p, False, wpost, cnt0)
    x1_s, h2_s, route_s, cnt = _post(attn_s.reshape(1, bs * lsp, D_ATTN), y_s, xs_pad, mod_s, True, wpost, cnt_p)

    n_p, n_s = bp * lp, bs * lsp
    n_tok = n_p + n_s
    n_tiles = -(-(2 * n_tok + N_EXPERTS * (TM_MOE - 1)) // TM_MOE)
    route = jnp.concatenate([route_p.reshape(n_p, LANES), route_s.reshape(n_s, LANES)], 0)
    counts = cnt[0, ROUTE_LANE0:ROUTE_LANE0 + N_EXPERTS].astype(jnp.int32)
    padded = ((counts + TM_MOE - 1) // TM_MOE) * TM_MOE
    ends = jnp.cumsum(padded)
    offs = ends - padded
    eid = route[:, 0:2].astype(jnp.int32)
    pos = (offs[eid] + route[:, 4:6].astype(jnp.int32)).reshape(-1)
    n_valid = ends[-1] // TM_MOE
    tile_row = jnp.minimum(jnp.arange(n_tiles), n_valid - 1) * TM_MOE
    tile_expert = jnp.sum((ends[None, :] <= tile_row[:, None]).astype(jnp.int32), axis=1)

    xs = jnp.zeros((n_tiles * TM_MOE, D_MODEL), F32)
    xs = _dispatch(pos, h2_p.reshape(n_p, D_MODEL), xs, 0)
    xs = _dispatch(pos, h2_s.reshape(n_s, D_MODEL), xs, n_p)
    ys = _moe(tile_expert, n_valid.reshape(1).astype(jnp.int32), xs, exp_w_gate[0], exp_w_up[0], exp_w_down[0])
    y_prompt = _final(pos, x1_p, mod_p, False, route_p, ln2_g[0][None], ln2_b[0][None], ys, 0)
    y_samp = _final(pos, x1_s, mod_s, True, route_s, ln2_g[0][None], ln2_b[0][None], ys, n_p)
    y_sample = y_samp.reshape(bs, lsp, D_MODEL)[:, :ls]

    kv5 = lambda a, n: a.reshape(1, a.shape[0], n, N_KV_HEADS, HEAD_DIM)
    k_win_p = kv5(k_p[:, lp - WINDOW:], WINDOW)
    v_win_p = kv5(v_p[:, lp - WINDOW:], WINDOW)
    k_win_s = kv5(jnp.concatenate([ck[:, ls:], k_s3[:, :ls]], 1), WINDOW)
    v_win_s = kv5(jnp.concatenate([cv[:, ls:], v_s3[:, :ls]], 1), WINDOW)
    st = lambda a: a.reshape(1, a.shape[0], N_SSM_GROUPS, SSM_STATE)
    return (y_prompt, y_sample, k_win_p, v_win_p, st(hr_p[bp:2 * bp]), st(hi_p[bp:2 * bp]),
            k_win_s, v_win_s, st(hr_s), st(hi_s))
```

```python
import functools
import math

import jax
import jax.numpy as jnp
from jax import lax
from jax.experimental import pallas as pl
from jax.experimental.pallas import tpu as pltpu

F32 = jnp.float32
BF16 = jnp.bfloat16

D_MODEL = 2048
D_ATTN = 1024
D_SSM = 1024
HEAD_DIM = 64
N_HEADS = 16
N_KV_HEADS = 4
REP = 4
D_KV = 256
ROT_DIM = 16
ROPE_THETA = 500000.0
WINDOW = 128
SSM_CH = 16
N_SSM_GROUPS = 64
SSM_STATE = 64
N_STATE = N_SSM_GROUPS * SSM_STATE
PROJ_COLS = D_ATTN + 2 * D_KV + D_SSM
N_EXPERT_GROUPS = 4
EXPERTS_PER_GROUP = 8
N_EXPERTS = 32
D_EXPERT = 512
DEPTH = 1
DEEPNORM_ALPHA = (2.0 * DEPTH) ** 0.25
LN_EPS = 1e-5
PAST_LEN = 16384

LANES = 128
SUBLANES = 8
MXU_DIM = 256
TM = 256
TM_MOE = 256
S5_TT = 64
SAMPLE_PAD = 8
ROUTE_LANE0 = N_EXPERT_GROUPS
VMEM_LIMIT = 56 * 1024 * 1024


def _cparams(sem):
    return pltpu.CompilerParams(dimension_semantics=sem, vmem_limit_bytes=VMEM_LIMIT)


def _ln(x):
    mu = jnp.mean(x, axis=-1, keepdims=True)
    xc = x - mu
    var = jnp.mean(xc * xc, axis=-1, keepdims=True)
    return xc * lax.rsqrt(var + LN_EPS)


def _rms(x):
    return x * lax.rsqrt(jnp.mean(x * x, axis=-1, keepdims=True) + LN_EPS)


def _ada_kernel(c_ref, w_ref, b_ref, o_ref):
    c = c_ref[...]
    s = c * jax.nn.sigmoid(c)
    o_ref[...] = jnp.dot(s.astype(BF16), w_ref[...].astype(BF16), preferred_element_type=F32) + b_ref[...]


def _ada(c_all, ada_w, ada_b):
    n, tn = c_all.shape[0], 1024
    return pl.pallas_call(
        _ada_kernel,
        out_shape=jax.ShapeDtypeStruct((n, 6 * D_MODEL), F32),
        grid=(6 * D_MODEL // tn,),
        in_specs=[pl.BlockSpec((n, D_MODEL), lambda j: (0, 0)),
                  pl.BlockSpec((D_MODEL, tn), lambda j: (0, j)),
                  pl.BlockSpec((1, tn), lambda j: (0, j))],
        out_specs=pl.BlockSpec((n, tn), lambda j: (0, j)),
        compiler_params=_cparams(("arbitrary",)),
        name="ada",
    )(c_all, ada_w, ada_b)


def _mod_spec(per_row, tm, chunk):
    if per_row:
        return pl.BlockSpec((None, tm, D_MODEL), lambda b, i: (b, i, chunk))
    return pl.BlockSpec((None, 1, D_MODEL), lambda b, i: (b, 0, chunk))


def _inproj_kernel(x_ref, sh_ref, sc_ref, w_ref, rc_ref, ra_ref, rb_ref, q_ref, k_ref, v_ref, u_ref):
    h = _ln(x_ref[...]) * (1.0 + sc_ref[...]) + sh_ref[...]
    proj = jnp.dot(h.astype(BF16), w_ref[...], preferred_element_type=F32)
    rc, ra, rb = rc_ref[...], ra_ref[...], rb_ref[...]

    def rope(t):
        return t * rc + pltpu.roll(t, LANES - ROT_DIM // 2, 1) * ra + pltpu.roll(t, ROT_DIM // 2, 1) * rb

    for j in range(D_ATTN // LANES):
        q_ref[:, j * LANES:(j + 1) * LANES] = (rope(proj[:, j * LANES:(j + 1) * LANES]) * HEAD_DIM ** -0.5).astype(BF16)
    for j in range(D_KV // LANES):
        c0 = D_ATTN + j * LANES
        k_ref[:, j * LANES:(j + 1) * LANES] = rope(proj[:, c0:c0 + LANES])
    v_ref[...] = proj[:, D_ATTN + D_KV:D_ATTN + 2 * D_KV]
    u_ref[...] = proj[:, D_ATTN + 2 * D_KV:]


def _inproj(x, mod, w_in_bf, rope_tabs, per_row):
    nb, l, _ = x.shape
    tm = min(TM, l)
    row = lambda w: pl.BlockSpec((None, tm, w), lambda b, i: (b, i, 0))
    tab = pl.BlockSpec((tm, LANES), lambda b, i: (i, 0))
    return pl.pallas_call(
        _inproj_kernel,
        out_shape=(jax.ShapeDtypeStruct((nb, l, D_ATTN), BF16), jax.ShapeDtypeStruct((nb, l, D_KV), F32),
                   jax.ShapeDtypeStruct((nb, l, D_KV), F32), jax.ShapeDtypeStruct((nb, l, D_SSM), F32)),
        grid=(nb, l // tm),
        in_specs=[row(D_MODEL), _mod_spec(per_row, tm, 0), _mod_spec(per_row, tm, 1),
                  pl.BlockSpec((D_MODEL, PROJ_COLS), lambda b, i: (0, 0)), tab, tab, tab],
        out_specs=(row(D_ATTN), row(D_KV), row(D_KV), row(D_SSM)),
        compiler_params=_cparams(("arbitrary", "arbitrary")),
        name="inproj",
    )(x, mod, mod, w_in_bf, *rope_tabs)


ATTN_SAMPLE_BATCH = 8


def _attn_kernel(sink_ref, q_ref, kp_ref, kc_ref, vp_ref, vc_ref, o_ref, *, lq, prev_from_block):
    m_rows = REP * lq
    ii = lax.broadcasted_iota(jnp.int32, (m_rows, WINDOW), 0) & (lq - 1)
    jj = lax.broadcasted_iota(jnp.int32, (m_rows, WINDOW), 1)
    from_prev = jj > ii
    if prev_from_block:
        dead = jj > ii + jnp.where(pl.program_id(1) > 0, WINDOW, 0)
    rr = lax.broadcasted_iota(jnp.int32, (m_rows, 1), 0)
    dn = (((1,), (1,)), ((), ()))
    for bi in range(q_ref.shape[0]):
        for g in range(N_KV_HEADS):
            qg = q_ref[bi, :, g * REP * HEAD_DIM:(g + 1) * REP * HEAD_DIM].astype(F32)
            qs = jnp.concatenate([qg[:, r * HEAD_DIM:(r + 1) * HEAD_DIM] for r in range(REP)], axis=0).astype(BF16)
            ks = slice(g * HEAD_DIM, (g + 1) * HEAD_DIM)
            s_p = lax.dot_general(qs, kp_ref[bi, :, ks].astype(BF16), dn, preferred_element_type=F32)
            s_c = lax.dot_general(qs, kc_ref[bi, :, ks].astype(BF16), dn, preferred_element_type=F32)
            s = jnp.where(from_prev, s_p, s_c)
            if prev_from_block:
                s = jnp.where(dead, -jnp.inf, s)
            sink = jnp.zeros((m_rows, 1), F32)
            for r in range(REP):
                sink = jnp.where((rr >= r * lq) & (rr < (r + 1) * lq), sink_ref[g * REP + r], sink)
            m = jnp.maximum(jnp.max(s, axis=-1, keepdims=True), sink)
            p = jnp.exp(s - m)
            p = p / (jnp.sum(p, axis=-1, keepdims=True) + jnp.exp(sink - m))
            o = (jnp.dot(jnp.where(from_prev, p, 0.0).astype(BF16), vp_ref[bi, :, ks].astype(BF16),
                         preferred_element_type=F32)
                 + jnp.dot(jnp.where(from_prev, 0.0, p).astype(BF16), vc_ref[bi, :, ks].astype(BF16),
                           preferred_element_type=F32))
            for r in range(REP):
                h = g * REP + r
                o_ref[bi, :, h * HEAD_DIM:(h + 1) * HEAD_DIM] = o[r * lq:(r + 1) * lq]


def _attn_prompt(sinks, q, k, v):
    nb, l, _ = q.shape
    nblk = l // WINDOW
    cur = lambda w: pl.BlockSpec((1, WINDOW, w), lambda b, n: (b, n, 0))
    prev = lambda w: pl.BlockSpec((1, WINDOW, w), lambda b, n: (b, jnp.maximum(n - 1, 0), 0))
    return pl.pallas_call(
        functools.partial(_attn_kernel, lq=WINDOW, prev_from_block=True),
        out_shape=jax.ShapeDtypeStruct((nb, l, D_ATTN), F32),
        grid=(nb, nblk),
        in_specs=[pl.BlockSpec(memory_space=pltpu.SMEM), cur(D_ATTN), prev(D_KV), cur(D_KV), prev(D_KV), cur(D_KV)],
        out_specs=cur(D_ATTN),
        compiler_params=_cparams(("arbitrary", "arbitrary")),
        name="attn_prompt",
    )(sinks, q, k, k, v, v)


def _attn_sample(sinks, q, kcat, vcat):
    nb, lq, _ = q.shape
    nbb = ATTN_SAMPLE_BATCH
    kblk = lambda n: pl.BlockSpec((nbb, WINDOW, D_KV), lambda b: (b, n, 0))
    return pl.pallas_call(
        functools.partial(_attn_kernel, lq=lq, prev_from_block=False),
        out_shape=jax.ShapeDtypeStruct((nb, lq, D_ATTN), F32),
        grid=(nb // nbb,),
        in_specs=[pl.BlockSpec(memory_space=pltpu.SMEM), pl.BlockSpec((nbb, lq, D_ATTN), lambda b: (b, 0, 0)),
                  kblk(0), kblk(1), kblk(0), kblk(1)],
        out_specs=pl.BlockSpec((nbb, lq, D_ATTN), lambda b: (b, 0, 0)),
        compiler_params=_cparams(("arbitrary",)),
        name="attn_sample",
    )(sinks, q, kcat, kcat, vcat, vcat)


N_KT = D_SSM // MXU_DIM
ST_PER_KT = N_STATE // N_KT
N_SLAB = D_SSM // LANES


def _cproj(hr_ref, hi_ref, cre_ref, cimn_ref, kt):
    cs = slice(kt * ST_PER_KT, (kt + 1) * ST_PER_KT)
    return (jnp.dot(hr_ref[:, cs].astype(BF16), cre_ref[kt], preferred_element_type=F32)
            + jnp.dot(hi_ref[:, cs].astype(BF16), cimn_ref[kt], preferred_element_type=F32))


def _s5_prompt_kernel(u_ref, bre_ref, bim_ref, cre_ref, cimn_ref, a1r_ref, a1i_ref, par_ref, pai_ref, d_ref,
                      h0r_ref, h0i_ref, y_ref, cr_ref, ci_ref, il_ref, xr_ref, xi_ref, *, nb, tt):
    rows = nb * tt

    @pl.when(pl.program_id(0) == 0)
    def _():
        cr_ref[...] = h0r_ref[...]
        ci_ref[...] = h0i_ref[...]

    for b in range(nb):
        for j in range(N_SLAB):
            il_ref[j, pl.ds(b, tt, stride=nb), :] = u_ref[b, :, j * LANES:(j + 1) * LANES]
    for kt in range(N_KT):
        ub = jnp.concatenate([il_ref[2 * kt], il_ref[2 * kt + 1]], axis=1).astype(BF16)
        cs = slice(kt * ST_PER_KT, (kt + 1) * ST_PER_KT)
        xr_ref[:, cs] = jnp.dot(ub, bre_ref[kt], preferred_element_type=F32)
        xi_ref[:, cs] = jnp.dot(ub, bim_ref[kt], preferred_element_type=F32)

    half = lax.broadcasted_iota(jnp.int32, (SUBLANES, LANES), 0) < nb

    def step(i, carry):
        r0 = pl.multiple_of(i * SUBLANES, SUBLANES)
        for j in range(N_STATE // LANES):
            ls = slice(j * LANES, (j + 1) * LANES)
            x_r = xr_ref[pl.ds(r0, SUBLANES), ls]
            x_i = xi_ref[pl.ds(r0, SUBLANES), ls]
            s_r = pltpu.roll(x_r, nb, 0)
            s_i = pltpu.roll(x_i, nb, 0)
            a1r, a1i = a1r_ref[:, ls], a1i_ref[:, ls]
            c_r, c_i = cr_ref[:, ls], ci_ref[:, ls]
            par, pai = par_ref[:, ls], pai_ref[:, ls]
            h_r = x_r + (a1r * s_r - a1i * s_i) + (par * c_r - pai * c_i)
            h_i = x_i + (a1r * s_i + a1i * s_r) + (par * c_i + pai * c_r)
            xr_ref[pl.ds(r0, SUBLANES), ls] = h_r
            xi_ref[pl.ds(r0, SUBLANES), ls] = h_i
            cr_ref[:, ls] = jnp.where(half, pltpu.roll(h_r, nb, 0), h_r)
            ci_ref[:, ls] = jnp.where(half, pltpu.roll(h_i, nb, 0), h_i)
        return carry

    lax.fori_loop(0, rows // SUBLANES, step, 0)

    for kt in range(N_KT):
        y = _cproj(xr_ref, xi_ref, cre_ref, cimn_ref, kt)
        il_ref[2 * kt] = y[:, :LANES]
        il_ref[2 * kt + 1] = y[:, LANES:]
    for b in range(nb):
        for j in range(N_SLAB):
            ls = slice(j * LANES, (j + 1) * LANES)
            y_ref[b, :, ls] = il_ref[j, pl.ds(b, tt, stride=nb), :] + d_ref[:, ls] * u_ref[b, :, ls]


def _s5_prompt(u, wts, tabs, d_skip, h0r, h0i):
    nb, l, _ = u.shape
    tt = S5_TT
    rows = nb * tt
    full = lambda a: pl.BlockSpec(a.shape, lambda i: (0,) * a.ndim)
    blk = pl.BlockSpec((nb, tt, D_SSM), lambda i: (0, i, 0))
    carry = jax.ShapeDtypeStruct((SUBLANES, N_STATE), F32)
    return pl.pallas_call(
        functools.partial(_s5_prompt_kernel, nb=nb, tt=tt),
        out_shape=(jax.ShapeDtypeStruct((nb, l, D_SSM), F32), carry, carry),
        grid=(l // tt,),
        in_specs=[blk] + [full(a) for a in (*wts, *tabs, d_skip, h0r, h0i)],
        out_specs=(blk, pl.BlockSpec((SUBLANES, N_STATE), lambda i: (0, 0)),
                   pl.BlockSpec((SUBLANES, N_STATE), lambda i: (0, 0))),
        scratch_shapes=[pltpu.VMEM((N_SLAB, rows, LANES), F32), pltpu.VMEM((rows, N_STATE), F32),
                        pltpu.VMEM((rows, N_STATE), F32)],
        compiler_params=_cparams(("arbitrary",)),
        name="s5_prompt",
    )(u, *wts, *tabs, d_skip, h0r, h0i)


def _s5_sample_kernel(u_ref, bre_ref, bim_ref, cre_ref, cimn_ref, ar_ref, ai_ref, d_ref, h0r_ref, h0i_ref,
                      y_ref, sr_ref, si_ref, xr_ref, xi_ref, *, nt):
    sr_ref[...] = h0r_ref[...]
    si_ref[...] = h0i_ref[...]
    for t in range(nt):
        for kt in range(N_KT):
            ub = u_ref[t, :, kt * MXU_DIM:(kt + 1) * MXU_DIM].astype(BF16)
            cs = slice(kt * ST_PER_KT, (kt + 1) * ST_PER_KT)
            xr_ref[:, cs] = jnp.dot(ub, bre_ref[kt], preferred_element_type=F32)
            xi_ref[:, cs] = jnp.dot(ub, bim_ref[kt], preferred_element_type=F32)
        ar, ai = ar_ref[...], ai_ref[...]
        s_r, s_i = sr_ref[...], si_ref[...]
        sr_ref[...] = xr_ref[...] + (ar * s_r - ai * s_i)
        si_ref[...] = xi_ref[...] + (ar * s_i + ai * s_r)
        for kt in range(N_KT):
            ys = slice(kt * MXU_DIM, (kt + 1) * MXU_DIM)
            y_ref[t, :, ys] = _cproj(sr_ref, si_ref, cre_ref, cimn_ref, kt) + d_ref[:, ys] * u_ref[t, :, ys]


def _s5_sample(u_tb, wts, ar, ai, d_skip, h0r, h0i):
    nt, nb, _ = u_tb.shape
    st = jax.ShapeDtypeStruct((nb, N_STATE), F32)
    args = (u_tb, *wts, ar, ai, d_skip, h0r, h0i)
    full = lambda a: pl.BlockSpec(a.shape, lambda i: (0,) * a.ndim)
    return pl.pallas_call(
        functools.partial(_s5_sample_kernel, nt=nt),
        out_shape=(jax.ShapeDtypeStruct((nt, nb, D_SSM), F32), st, st),
        grid=(1,),
        in_specs=[full(a) for a in args],
        out_specs=(pl.BlockSpec((nt, nb, D_SSM), lambda i: (0, 0, 0)), pl.BlockSpec((nb, N_STATE), lambda i: (0, 0)),
                   pl.BlockSpec((nb, N_STATE), lambda i: (0, 0))),
        scratch_shapes=[pltpu.VMEM((nb, N_STATE), F32), pltpu.VMEM((nb, N_STATE), F32)],
        compiler_params=_cparams(("arbitrary",)),
        name="s5_sample",
    )(*args)


TOKEN_TILE_ROWS = SUBLANES
PACK_HALF = D_MODEL // 2
HI_MASK = 0xFFFF0000


def _bits(v):
    return lax.bitcast_convert_type(v.astype(BF16).astype(F32), jnp.uint32)


def _store_token_tiles(ref, v):
    n = v.shape[0]
    for s in range(TOKEN_TILE_ROWS):
        lo = _bits(v[:, s * LANES:(s + 1) * LANES]) >> 16
        hi = _bits(v[:, PACK_HALF + s * LANES:PACK_HALF + (s + 1) * LANES]) & jnp.uint32(HI_MASK)
        ref[pl.ds(s, n, stride=TOKEN_TILE_ROWS), :] = hi | lo


def _load_token_chunk(ref, s, n):
    w = ref[pl.ds(s, n, stride=TOKEN_TILE_ROWS), :]
    lo = lax.bitcast_convert_type(w << 16, F32)
    hi = lax.bitcast_convert_type(w & jnp.uint32(HI_MASK), F32)
    return lo, hi


N_POST_INPUTS = 17


def _post_kernel(*refs, has_tail):
    ins, outs = refs[:N_POST_INPUTS], refs[N_POST_INPUTS + int(has_tail):]
    if not has_tail:
        _post_body(*ins, *outs)
        return
    tail_ref, h2p_ref = refs[N_POST_INPUTS], outs[1]
    last = pl.num_programs(0) - 1

    @pl.when(pl.program_id(0) == last)
    def _():
        h2p_ref[...] = tail_ref[...]

    @pl.when(pl.program_id(0) < last)
    def _():
        _post_body(*ins, *outs)


def _post_body(attn_ref, yssm_ref, x_ref, g1_ref, sh2_ref, sc2_ref, gluw_ref, glub_ref, ga_ref, gs_ref, wout_ref,
               l1g_ref, l1b_ref, wr_ref, br_ref, tri_ref, cnt0_ref, x1_ref, h2p_ref, route_ref, cnt_ref):
    tm = x_ref.shape[0]

    @pl.when(pl.program_id(0) == 0)
    def _():
        cnt_ref[...] = cnt0_ref[...]

    z = jax.nn.gelu(yssm_ref[...])
    ssm = z * jax.nn.sigmoid(jnp.dot(z.astype(BF16), gluw_ref[...], preferred_element_type=F32) + glub_ref[...])
    mixed_a = (_rms(attn_ref[...]) * ga_ref[...]).astype(BF16)
    mixed_s = (_rms(ssm) * gs_ref[...]).astype(BF16)
    o = (jnp.dot(mixed_a, wout_ref[:D_ATTN, :], preferred_element_type=F32)
         + jnp.dot(mixed_s, wout_ref[D_ATTN:, :], preferred_element_type=F32))
    x1 = _ln(DEEPNORM_ALPHA * x_ref[...] + g1_ref[...] * o) * l1g_ref[...] + l1b_ref[...]
    x1_ref[...] = x1
    h2 = (_ln(x1) * (1.0 + sc2_ref[...]) + sh2_ref[...]).astype(BF16)
    _store_token_tiles(h2p_ref, h2)
    logits = jnp.dot(h2, wr_ref[...], preferred_element_type=F32) + br_ref[...]

    lane = lax.broadcasted_iota(jnp.int32, (tm, LANES), 1).astype(F32)
    big = float(4 * LANES)
    neg = -jnp.inf
    gl = jnp.where(lane < N_EXPERT_GROUPS, logits, neg)
    gp = jnp.exp(gl - jnp.max(gl, axis=-1, keepdims=True))
    gp = gp / jnp.sum(gp, axis=-1, keepdims=True)
    g_val = jnp.max(gp, axis=-1, keepdims=True)
    g_idx = jnp.min(jnp.where(gp == g_val, lane, big), axis=-1, keepdims=True)
    lo = ROUTE_LANE0 + EXPERTS_PER_GROUP * g_idx
    emask = (lane >= lo) & (lane < lo + EXPERTS_PER_GROUP)
    el = jnp.where(emask, logits, neg)
    ep = jnp.exp(el - jnp.max(el, axis=-1, keepdims=True))
    ep = jnp.where(emask, ep / jnp.sum(ep, axis=-1, keepdims=True), -1.0)
    v1 = jnp.max(ep, axis=-1, keepdims=True)
    i1 = jnp.min(jnp.where(ep == v1, lane, big), axis=-1, keepdims=True)
    ep2 = jnp.where(lane == i1, -1.0, ep)
    v2 = jnp.max(ep2, axis=-1, keepdims=True)
    i2 = jnp.min(jnp.where(ep2 == v2, lane, big), axis=-1, keepdims=True)
    vs = v1 + v2
    w1 = g_val * (v1 / vs)
    w2 = g_val * (v2 / vs)
    hit = jnp.where((lane == i1) | (lane == i2), 1.0, 0.0)
    before = jnp.dot(tri_ref[...], hit.astype(BF16), preferred_element_type=F32) + cnt_ref[0:1, :]
    r1 = jnp.sum(jnp.where(lane == i1, before, 0.0), axis=-1, keepdims=True)
    r2 = jnp.sum(jnp.where(lane == i2, before, 0.0), axis=-1, keepdims=True)
    cnt_ref[...] = cnt_ref[...] + jnp.sum(hit, axis=0, keepdims=True)
    e1 = i1 - ROUTE_LANE0
    e2 = i2 - ROUTE_LANE0
    route = jnp.zeros((tm, LANES), F32)
    for n, val in enumerate((e1, e2, w1, w2, r1, r2)):
        route = jnp.where(lane == n, val, route)
    route_ref[...] = route


def _post(attn, yssm, x, mod, per_row, w, cnt0, tail=None):
    nb, l, _ = x.shape
    tm = min(TM, l)
    nt = l // tm
    n_body = nb * nt
    has_tail = tail is not None
    if has_tail:
        assert tail.shape == (tm * TOKEN_TILE_ROWS, LANES)

    def bi(s):
        s = jnp.minimum(s, n_body - 1)
        return s // nt, s % nt

    row = lambda wd: pl.BlockSpec((None, tm, wd), lambda s: (*bi(s), 0))
    if per_row:
        mspec = lambda chunk: pl.BlockSpec((None, tm, D_MODEL), lambda s: (*bi(s), chunk))
    else:
        mspec = lambda chunk: pl.BlockSpec((None, 1, D_MODEL), lambda s: (bi(s)[0], 0, chunk))
    full = lambda a: pl.BlockSpec(a.shape, lambda s: (0,) * a.ndim)
    tri = jnp.tril(jnp.ones((tm, tm), F32), -1).astype(BF16)
    consts = (w["glu_w"], w["glu_b"], w["attn_g"], w["ssm_g"], w["w_out"], w["ln1_g"], w["ln1_b"], w["wr"], w["br"],
              tri, cnt0) + ((tail,) if has_tail else ())
    assert 6 + len(consts) == N_POST_INPUTS + int(has_tail)
    n_steps = n_body + int(has_tail)
    return pl.pallas_call(
        functools.partial(_post_kernel, has_tail=has_tail),
        out_shape=(jax.ShapeDtypeStruct((nb, l, D_MODEL), F32),
                   jax.ShapeDtypeStruct((n_steps * tm * TOKEN_TILE_ROWS, LANES), jnp.uint32),
                   jax.ShapeDtypeStruct((nb, l, LANES), F32), jax.ShapeDtypeStruct((SUBLANES, LANES), F32)),
        grid=(n_steps,),
        in_specs=[row(D_ATTN), row(D_SSM), row(D_MODEL), mspec(2), mspec(3), mspec(4)] + [full(a) for a in consts],
        out_specs=(row(D_MODEL), pl.BlockSpec((tm * TOKEN_TILE_ROWS, LANES), lambda s: (s, 0)),
                   row(LANES), pl.BlockSpec((SUBLANES, LANES), lambda s: (0, 0))),
        compiler_params=_cparams(("arbitrary",)),
        name="post",
    )(attn, yssm, x, mod, mod, mod, *consts)


def _token_tile_copy(src_ref, t, dst_ref, r, sem):
    return pltpu.make_async_copy(src_ref.at[pl.ds(t * TOKEN_TILE_ROWS, TOKEN_TILE_ROWS)],
                                 dst_ref.at[pl.ds(r * TOKEN_TILE_ROWS, TOKEN_TILE_ROWS)], sem)


def _moe_kernel(te_ref, nv_ref, rt_ref, h2p_ref, wg_ref, wu_ref, wd_ref, y_ref, buf_ref, wgb_ref, wub_ref, wdb_ref,
                sem):
    i = pl.program_id(0)
    nv = nv_ref[0]
    slot = i & 1
    rows = TM_MOE * TOKEN_TILE_ROWS

    def gather(tile, dst_slot):
        def body(r, c):
            for k in range(2):
                rr = 2 * r + k
                _token_tile_copy(h2p_ref, rt_ref[tile * TM_MOE + rr], buf_ref.at[dst_slot], rr,
                                 sem.at[dst_slot]).start(priority=k)
            return c
        lax.fori_loop(0, TM_MOE // 2, body, 0, unroll=4)

    @pl.when(i == 0)
    def _():
        gather(0, 0)

    @pl.when(i <= nv)
    def _():
        pltpu.make_async_copy(h2p_ref.at[pl.ds(0, rows)], buf_ref.at[slot], sem.at[slot]).wait()

    changed = (i == 0) | (te_ref[i] != te_ref[jnp.maximum(i - 1, 0)])

    @pl.when((i < nv) & changed)
    def _():
        wgb_ref[...] = wg_ref[...].astype(BF16)
        wub_ref[...] = wu_ref[...].astype(BF16)
        wdb_ref[...] = wd_ref[...].astype(BF16)

    @pl.when(i < nv)
    def _():
        gather(i + 1, 1 - slot)
        lo, hi = [], []
        for s in range(TOKEN_TILE_ROWS):
            l_s, h_s = _load_token_chunk(buf_ref.at[slot], s, TM_MOE)
            lo.append(l_s.astype(BF16))
            hi.append(h_s.astype(BF16))
        x_lo = jnp.concatenate(lo, axis=1)
        x_hi = jnp.concatenate(hi, axis=1)
        hg = (jnp.dot(x_lo, wgb_ref[:PACK_HALF, :], preferred_element_type=F32)
              + jnp.dot(x_hi, wgb_ref[PACK_HALF:, :], preferred_element_type=F32))
        hu = (jnp.dot(x_lo, wub_ref[:PACK_HALF, :], preferred_element_type=F32)
              + jnp.dot(x_hi, wub_ref[PACK_HALF:, :], preferred_element_type=F32))
        act = (hg * jax.nn.sigmoid(hg)) * hu
        _store_token_tiles(y_ref, jnp.dot(act.astype(BF16), wdb_ref[...], preferred_element_type=F32))

    @pl.when(i >= nv)
    def _():
        y_ref[...] = jnp.zeros(y_ref.shape, y_ref.dtype)


def _moe(tile_expert, n_valid, row_token, h2p, w_gate, w_up, w_down):
    n_steps = row_token.shape[0] // TM_MOE
    rows = TM_MOE * TOKEN_TILE_ROWS
    wspec = lambda a, b: pl.BlockSpec((None, a, b), lambda i, te, nv, rt: (te[i], 0, 0))
    return pl.pallas_call(
        _moe_kernel,
        out_shape=jax.ShapeDtypeStruct((n_steps * rows, LANES), jnp.uint32),
        grid_spec=pltpu.PrefetchScalarGridSpec(
            num_scalar_prefetch=3, grid=(n_steps,),
            in_specs=[pl.BlockSpec(memory_space=pl.ANY), wspec(D_MODEL, D_EXPERT), wspec(D_MODEL, D_EXPERT),
                      wspec(D_EXPERT, D_MODEL)],
            out_specs=pl.BlockSpec((rows, LANES), lambda i, te, nv, rt: (i, 0)),
            scratch_shapes=[pltpu.VMEM((2, rows, LANES), jnp.uint32), pltpu.VMEM((D_MODEL, D_EXPERT), BF16),
                            pltpu.VMEM((D_MODEL, D_EXPERT), BF16), pltpu.VMEM((D_EXPERT, D_MODEL), BF16),
                            pltpu.SemaphoreType.DMA((2,))]),
        compiler_params=_cparams(("arbitrary",)),
        name="moe",
    )(tile_expert, n_valid, row_token, h2p, w_gate, w_up, w_down)


def _final_kernel(pos_ref, x1_ref, g2_ref, route_ref, l2g_ref, l2b_ref, ys_ref, o_ref, buf_ref, f_ref, sem, *, tm,
                  row0):
    step = pl.program_id(0) * pl.num_programs(1) + pl.program_id(1)
    n_steps = pl.num_programs(0) * pl.num_programs(1)
    slot = step & 1
    rows = tm * TOKEN_TILE_ROWS

    def gather(st, dst_slot):
        base = (row0 + st * tm) * 2

        def body(t, c):
            for k in range(2):
                _token_tile_copy(ys_ref, pos_ref[base + 2 * t + k], buf_ref.at[dst_slot, k], t,
                                 sem.at[dst_slot]).start(priority=k)
            return c
        lax.fori_loop(0, tm, body, 0, unroll=4)

    @pl.when(step == 0)
    def _():
        gather(0, 0)

    for k in range(2):
        pltpu.make_async_copy(ys_ref.at[pl.ds(0, rows)], buf_ref.at[slot, k], sem.at[slot]).wait()

    @pl.when(step + 1 < n_steps)
    def _():
        gather(step + 1, 1 - slot)

    route = route_ref[...]
    w1, w2 = route[:, 2:3], route[:, 3:4]
    for s in range(TOKEN_TILE_ROWS):
        a_lo, a_hi = _load_token_chunk(buf_ref.at[slot, 0], s, tm)
        b_lo, b_hi = _load_token_chunk(buf_ref.at[slot, 1], s, tm)
        f_ref[:, s * LANES:(s + 1) * LANES] = w1 * a_lo + w2 * b_lo
        f_ref[:, PACK_HALF + s * LANES:PACK_HALF + (s + 1) * LANES] = w1 * a_hi + w2 * b_hi
    o_ref[...] = _ln(DEEPNORM_ALPHA * x1_ref[...] + g2_ref[...] * f_ref[...]) * l2g_ref[...] + l2b_ref[...]


def _final(pos, x1, mod, per_row, route, ln2_g, ln2_b, ys, row0):
    nb, l, _ = x1.shape
    tm = min(TM, l)
    row = lambda wd: pl.BlockSpec((None, tm, wd), lambda b, i, pos: (b, i, 0))
    if per_row:
        g2 = pl.BlockSpec((None, tm, D_MODEL), lambda b, i, pos: (b, i, 5))
    else:
        g2 = pl.BlockSpec((None, 1, D_MODEL), lambda b, i, pos: (b, 0, 5))
    vec = pl.BlockSpec((1, D_MODEL), lambda b, i, pos: (0, 0))
    return pl.pallas_call(
        functools.partial(_final_kernel, tm=tm, row0=row0),
        out_shape=jax.ShapeDtypeStruct((nb, l, D_MODEL), F32),
        grid_spec=pltpu.PrefetchScalarGridSpec(
            num_scalar_prefetch=1, grid=(nb, l // tm),
            in_specs=[row(D_MODEL), g2, row(LANES), vec, vec, pl.BlockSpec(memory_space=pl.ANY)],
            out_specs=row(D_MODEL),
            scratch_shapes=[pltpu.VMEM((2, 2, tm * TOKEN_TILE_ROWS, LANES), jnp.uint32),
                            pltpu.VMEM((tm, D_MODEL), F32), pltpu.SemaphoreType.DMA((2,))]),
        compiler_params=_cparams(("arbitrary", "arbitrary")),
        name="final",
    )(pos, x1, mod, route, ln2_g, ln2_b, ys)


def _rope_tables(pos):
    half = ROT_DIM // 2
    inv_freq = ROPE_THETA ** (-jnp.arange(half, dtype=jnp.float32) * 2.0 / ROT_DIM)
    ang = pos.astype(jnp.float32)[:, None] * inv_freq[None, :]
    cos, sin = jnp.cos(ang), jnp.sin(ang)
    n = pos.shape[0]
    one = jnp.ones((n, HEAD_DIM - ROT_DIM), F32)
    zero = jnp.zeros((n, HEAD_DIM - half), F32)
    c = jnp.concatenate([cos, cos, one], -1)
    a = jnp.concatenate([-sin, zero], -1)
    b = jnp.concatenate([jnp.zeros((n, half), F32), sin, jnp.zeros((n, HEAD_DIM - ROT_DIM), F32)], -1)
    return tuple(jnp.tile(t, (1, LANES // HEAD_DIM)) for t in (c, a, b))


def _s5_params(lam_re, lam_im, log_step, b_re, b_im, c_re, c_im):
    f32 = jnp.float32
    dt = jnp.exp(log_step.astype(f32))[:, None]
    lr, li = lam_re.astype(f32), lam_im.astype(f32)
    mag = jnp.exp(lr * dt)
    ar, ai = mag * jnp.cos(li * dt), mag * jnp.sin(li * dt)
    den = lr * lr + li * li
    cr = ((ar - 1.0) * lr + ai * li) / den
    ci = (ai * lr - (ar - 1.0) * li) / den
    br, bi = b_re.astype(f32), b_im.astype(f32)
    bbr = cr[..., None] * br - ci[..., None] * bi
    bbi = cr[..., None] * bi + ci[..., None] * br
    gpt = MXU_DIM // SSM_CH
    eye = jnp.eye(gpt, dtype=f32)

    def pack_b(m):
        m = m.reshape(N_KT, gpt, SSM_STATE, SSM_CH)
        return jnp.einsum("kgpc,gh->kgchp", m, eye).reshape(N_KT, MXU_DIM, ST_PER_KT).astype(BF16)

    def pack_c(m):
        m = m.astype(f32).reshape(N_KT, gpt, SSM_CH, SSM_STATE)
        return jnp.einsum("kgcp,gh->khpgc", m, eye).reshape(N_KT, ST_PER_KT, MXU_DIM).astype(BF16)

    wts = (pack_b(bbr), pack_b(bbi), pack_c(c_re), pack_c(-c_im.astype(f32)))
    return wts, ar.reshape(1, N_STATE), ai.reshape(1, N_STATE)


def _scan_tables(ar, ai, nb):
    assert SUBLANES // nb == 2
    a2r, a2i = ar * ar - ai * ai, 2.0 * ar * ai
    z = jnp.zeros_like(ar)
    rep = lambda first, second: jnp.concatenate([jnp.tile(first, (nb, 1)), jnp.tile(second, (nb, 1))], 0)
    return rep(z, ar), rep(z, ai), rep(ar, a2r), rep(ai, a2i)


def kernel(x_prompt, x_sample, cache_k_win, cache_v_win, state_ssm_re, state_ssm_im, c_prompt, c_sample, ada_w, ada_b,
           w_in, attn_sinks, ssm_lambda_re, ssm_lambda_im, ssm_log_step, ssm_b_re, ssm_b_im, ssm_c_re, ssm_c_im, ssm_d,
           ssm_glu_w, ssm_glu_b, attn_norm_g, ssm_norm_g, w_out, ln1_g, ln1_b, router_group_w, router_group_b,
           router_expert_w, router_expert_b, exp_w_gate, exp_w_up, exp_w_down, ln2_g, ln2_b):
    assert ada_w.shape[0] == DEPTH
    bp, lp, _ = x_prompt.shape
    bs, ls, _ = x_sample.shape
    lsp = SAMPLE_PAD

    w_in_bf = w_in[0].astype(BF16)
    wr = jnp.concatenate([router_group_w[0], router_expert_w[0]], -1)
    wr = jnp.pad(wr, ((0, 0), (0, LANES - wr.shape[1]))).astype(BF16)
    br = jnp.pad(jnp.concatenate([router_group_b[0], router_expert_b[0]], -1), (0, LANES - N_EXPERT_GROUPS - N_EXPERTS))
    wpost = dict(glu_w=ssm_glu_w[0].astype(BF16), glu_b=ssm_glu_b[0][None], attn_g=attn_norm_g[0][None],
                 ssm_g=ssm_norm_g[0][None], w_out=w_out[0].astype(BF16), ln1_g=ln1_g[0][None], ln1_b=ln1_b[0][None],
                 wr=wr, br=br[None])
    s5w, ar, ai = _s5_params(ssm_lambda_re[0], ssm_lambda_im[0], ssm_log_step[0], ssm_b_re[0], ssm_b_im[0],
                             ssm_c_re[0], ssm_c_im[0])
    d_skip = ssm_d[0][None]
    sinks = attn_sinks[0]

    n_c = bp + bs
    n_cp = -(-n_c // SUBLANES) * SUBLANES
    c_all = jnp.pad(jnp.concatenate([c_prompt, c_sample], 0), ((0, n_cp - n_c), (0, 0)))
    mod = _ada(c_all, ada_w[0], ada_b[0][None])
    mod_p = mod[:bp][:, None, :]
    mod_s = jnp.repeat(mod[bp:n_c], lsp, axis=0)[None]

    xs_pad = jnp.pad(x_sample, ((0, 0), (0, lsp - ls), (0, 0))).reshape(1, bs * lsp, D_MODEL)

    rope_p = _rope_tables(jnp.arange(lp))
    pos_s = PAST_LEN + jnp.minimum(jnp.arange(lsp), ls - 1)
    rope_s = tuple(jnp.tile(t, (bs, 1)) for t in _rope_tables(pos_s))
    q_p, k_p, v_p, u_p = _inproj(x_prompt, mod_p, w_in_bf, rope_p, False)
    q_s, k_s, v_s, u_s = _inproj(xs_pad, mod_s, w_in_bf, rope_s, True)

    attn_p = _attn_prompt(sinks, q_p, k_p, v_p)
    k_s3 = k_s.reshape(bs, lsp, D_KV)
    v_s3 = v_s.reshape(bs, lsp, D_KV)
    ck = cache_k_win[0].reshape(bs, WINDOW, D_KV)
    cv = cache_v_win[0].reshape(bs, WINDOW, D_KV)
    padk = lambda new: jnp.pad(new, ((0, 0), (0, WINDOW - lsp), (0, 0)))
    attn_s = _attn_sample(sinks, q_s.reshape(bs, lsp, D_ATTN), jnp.concatenate([ck, padk(k_s3)], 1),
                          jnp.concatenate([cv, padk(v_s3)], 1))

    tabs = _scan_tables(ar, ai, bp)
    zero_carry = jnp.zeros((SUBLANES, N_STATE), F32)
    y_p, hr_p, hi_p = _s5_prompt(u_p, s5w, tabs, d_skip, zero_carry, zero_carry)
    u_tb = jnp.transpose(u_s.reshape(bs, lsp, D_SSM)[:, :ls], (1, 0, 2))
    y_tb, hr_s, hi_s = _s5_sample(u_tb, s5w, ar, ai, d_skip, state_ssm_re[0].reshape(bs, N_STATE),
                                  state_ssm_im[0].reshape(bs, N_STATE))
    y_s = jnp.pad(jnp.transpose(y_tb, (1, 0, 2)), ((0, 0), (0, lsp - ls), (0, 0))).reshape(1, bs * lsp, D_SSM)

    cnt0 = jnp.zeros((SUBLANES, LANES), F32)
    n_p, n_s = bp * lp, bs * lsp
    n_tok = n_p + n_s
    x1_s, h2p_s, route_s, cnt_s = _post(attn_s.reshape(1, bs * lsp, D_ATTN), y_s, xs_pad, mod_s, True, wpost, cnt0)
    x1_p, h2p, route_p, cnt = _post(attn_p, y_p, x_prompt, mod_p, False, wpost, cnt_s, tail=h2p_s)

    n_tiles = -(-(2 * n_tok + N_EXPERTS * (TM_MOE - 1)) // TM_MOE) + 1
    route = jnp.concatenate([route_p.reshape(n_p, LANES), route_s.reshape(n_s, LANES)], 0)
    counts = cnt[0, ROUTE_LANE0:ROUTE_LANE0 + N_EXPERTS].astype(jnp.int32)
    padded = ((counts + TM_MOE - 1) // TM_MOE) * TM_MOE
    ends = jnp.cumsum(padded)
    offs = ends - padded
    eid = route[:, 0:2].astype(jnp.int32)
    pos = (offs[eid] + route[:, 4:6].astype(jnp.int32)).reshape(-1)
    n_valid = ends[-1] // TM_MOE
    tile_row = jnp.minimum(jnp.arange(n_tiles), n_valid - 1) * TM_MOE
    tile_expert = jnp.sum((ends[None, :] <= tile_row[:, None]).astype(jnp.int32), axis=1)
    token_of = jnp.repeat(jnp.arange(n_tok, dtype=jnp.int32), 2)
    row_token = jnp.zeros((n_tiles * TM_MOE,), jnp.int32).at[pos].set(token_of)

    ys = _moe(tile_expert, n_valid.reshape(1).astype(jnp.int32), row_token, h2p, exp_w_gate[0], exp_w_up[0],
              exp_w_down[0])
    y_prompt = _final(pos, x1_p, mod_p, False, route_p, ln2_g[0][None], ln2_b[0][None], ys, 0)
    y_samp = _final(pos, x1_s, mod_s, True, route_s, ln2_g[0][None], ln2_b[0][None], ys, n_p)
    y_sample = y_samp.reshape(bs, lsp, D_MODEL)[:, :ls]

    kv5 = lambda a, n: a.reshape(1, a.shape[0], n, N_KV_HEADS, HEAD_DIM)
    k_win_p = kv5(k_p[:, lp - WINDOW:], WINDOW)
    v_win_p = kv5(v_p[:, lp - WINDOW:], WINDOW)
    k_win_s = kv5(jnp.concatenate([ck[:, ls:], k_s3[:, :ls]], 1), WINDOW)
    v_win_s = kv5(jnp.concatenate([cv[:, ls:], v_s3[:, :ls]], 1), WINDOW)
    st = lambda a: a.reshape(1, a.shape[0], N_SSM_GROUPS, SSM_STATE)
    return (y_prompt, y_sample, k_win_p, v_win_p, st(hr_p[bp:2 * bp]), st(hi_p[bp:2 * bp]),
            k_win_s, v_win_s, st(hr_s), st(hi_s))
```

```python
import functools
import math

import jax
import jax.numpy as jnp
from jax import lax
from jax.experimental import pallas as pl
from jax.experimental.pallas import tpu as pltpu

F32 = jnp.float32
BF16 = jnp.bfloat16

D_MODEL = 2048
D_ATTN = 1024
D_SSM = 1024
HEAD_DIM = 64
N_HEADS = 16
N_KV_HEADS = 4
REP = 4
D_KV = 256
ROT_DIM = 16
ROPE_THETA = 500000.0
WINDOW = 128
SSM_CH = 16
N_SSM_GROUPS = 64
SSM_STATE = 64
N_STATE = N_SSM_GROUPS * SSM_STATE
PROJ_COLS = D_ATTN + 2 * D_KV + D_SSM
N_EXPERT_GROUPS = 4
EXPERTS_PER_GROUP = 8
N_EXPERTS = 32
D_EXPERT = 512
DEPTH = 1
DEEPNORM_ALPHA = (2.0 * DEPTH) ** 0.25
LN_EPS = 1e-5
PAST_LEN = 16384

LANES = 128
SUBLANES = 8
MXU_DIM = 256
TM = 256
TM_MOE = 256
S5_TT = 64
SAMPLE_PAD = 8
ROUTE_LANE0 = N_EXPERT_GROUPS
VMEM_LIMIT = 56 * 1024 * 1024


def _cparams(sem):
    return pltpu.CompilerParams(dimension_semantics=sem, vmem_limit_bytes=VMEM_LIMIT)


def _ln(x):
    mu = jnp.mean(x, axis=-1, keepdims=True)
    xc = x - mu
    var = jnp.mean(xc * xc, axis=-1, keepdims=True)
    return xc * lax.rsqrt(var + LN_EPS)


def _rms(x):
    return x * lax.rsqrt(jnp.mean(x * x, axis=-1, keepdims=True) + LN_EPS)


def _ada_kernel(c_ref, w_ref, b_ref, o_ref):
    c = c_ref[...]
    s = c * jax.nn.sigmoid(c)
    o_ref[...] = jnp.dot(s.astype(BF16), w_ref[...].astype(BF16), preferred_element_type=F32) + b_ref[...]


def _ada(c_all, ada_w, ada_b):
    n, tn = c_all.shape[0], 1024
    return pl.pallas_call(
        _ada_kernel,
        out_shape=jax.ShapeDtypeStruct((n, 6 * D_MODEL), F32),
        grid=(6 * D_MODEL // tn,),
        in_specs=[pl.BlockSpec((n, D_MODEL), lambda j: (0, 0)),
                  pl.BlockSpec((D_MODEL, tn), lambda j: (0, j)),
                  pl.BlockSpec((1, tn), lambda j: (0, j))],
        out_specs=pl.BlockSpec((n, tn), lambda j: (0, j)),
        compiler_params=_cparams(("arbitrary",)),
        name="ada",
    )(c_all, ada_w, ada_b)


def _mod_spec(per_row, tm, chunk):
    if per_row:
        return pl.BlockSpec((None, tm, D_MODEL), lambda b, i: (b, i, chunk))
    return pl.BlockSpec((None, 1, D_MODEL), lambda b, i: (b, 0, chunk))


def _inproj_kernel(x_ref, sh_ref, sc_ref, w_ref, rc_ref, ra_ref, rb_ref, q_ref, k_ref, v_ref, u_ref):
    h = _ln(x_ref[...]) * (1.0 + sc_ref[...]) + sh_ref[...]
    proj = jnp.dot(h.astype(BF16), w_ref[...], preferred_element_type=F32)
    rc, ra, rb = rc_ref[...], ra_ref[...], rb_ref[...]

    def rope(t):
        return t * rc + pltpu.roll(t, LANES - ROT_DIM // 2, 1) * ra + pltpu.roll(t, ROT_DIM // 2, 1) * rb

    for j in range(D_ATTN // LANES):
        q_ref[:, j * LANES:(j + 1) * LANES] = (rope(proj[:, j * LANES:(j + 1) * LANES]) * HEAD_DIM ** -0.5).astype(BF16)
    for j in range(D_KV // LANES):
        c0 = D_ATTN + j * LANES
        k_ref[:, j * LANES:(j + 1) * LANES] = rope(proj[:, c0:c0 + LANES])
    v_ref[...] = proj[:, D_ATTN + D_KV:D_ATTN + 2 * D_KV]
    u_ref[...] = proj[:, D_ATTN + 2 * D_KV:]


def _inproj(x, mod, w_in_bf, rope_tabs, per_row):
    nb, l, _ = x.shape
    tm = min(TM, l)
    row = lambda w: pl.BlockSpec((None, tm, w), lambda b, i: (b, i, 0))
    tab = pl.BlockSpec((tm, LANES), lambda b, i: (i, 0))
    return pl.pallas_call(
        _inproj_kernel,
        out_shape=(jax.ShapeDtypeStruct((nb, l, D_ATTN), BF16), jax.ShapeDtypeStruct((nb, l, D_KV), F32),
                   jax.ShapeDtypeStruct((nb, l, D_KV), F32), jax.ShapeDtypeStruct((nb, l, D_SSM), F32)),
        grid=(nb, l // tm),
        in_specs=[row(D_MODEL), _mod_spec(per_row, tm, 0), _mod_spec(per_row, tm, 1),
                  pl.BlockSpec((D_MODEL, PROJ_COLS), lambda b, i: (0, 0)), tab, tab, tab],
        out_specs=(row(D_ATTN), row(D_KV), row(D_KV), row(D_SSM)),
        compiler_params=_cparams(("arbitrary", "arbitrary")),
        name="inproj",
    )(x, mod, mod, w_in_bf, *rope_tabs)


ATTN_SAMPLE_BATCH = 8


def _attn_kernel(sink_ref, q_ref, kp_ref, kc_ref, vp_ref, vc_ref, o_ref, *, lq, prev_from_block):
    m_rows = REP * lq
    ii = lax.broadcasted_iota(jnp.int32, (m_rows, WINDOW), 0) & (lq - 1)
    jj = lax.broadcasted_iota(jnp.int32, (m_rows, WINDOW), 1)
    from_prev = jj > ii
    if prev_from_block:
        dead = jj > ii + jnp.where(pl.program_id(1) > 0, WINDOW, 0)
    rr = lax.broadcasted_iota(jnp.int32, (m_rows, 1), 0)
    dn = (((1,), (1,)), ((), ()))
    for bi in range(q_ref.shape[0]):
        for g in range(N_KV_HEADS):
            qg = q_ref[bi, :, g * REP * HEAD_DIM:(g + 1) * REP * HEAD_DIM].astype(F32)
            qs = jnp.concatenate([qg[:, r * HEAD_DIM:(r + 1) * HEAD_DIM] for r in range(REP)], axis=0).astype(BF16)
            ks = slice(g * HEAD_DIM, (g + 1) * HEAD_DIM)
            s_p = lax.dot_general(qs, kp_ref[bi, :, ks].astype(BF16), dn, preferred_element_type=F32)
            s_c = lax.dot_general(qs, kc_ref[bi, :, ks].astype(BF16), dn, preferred_element_type=F32)
            s = jnp.where(from_prev, s_p, s_c)
            if prev_from_block:
                s = jnp.where(dead, -jnp.inf, s)
            sink = jnp.zeros((m_rows, 1), F32)
            for r in range(REP):
                sink = jnp.where((rr >= r * lq) & (rr < (r + 1) * lq), sink_ref[g * REP + r], sink)
            m = jnp.maximum(jnp.max(s, axis=-1, keepdims=True), sink)
            p = jnp.exp(s - m)
            p = p / (jnp.sum(p, axis=-1, keepdims=True) + jnp.exp(sink - m))
            o = (jnp.dot(jnp.where(from_prev, p, 0.0).astype(BF16), vp_ref[bi, :, ks].astype(BF16),
                         preferred_element_type=F32)
                 + jnp.dot(jnp.where(from_prev, 0.0, p).astype(BF16), vc_ref[bi, :, ks].astype(BF16),
                           preferred_element_type=F32))
            for r in range(REP):
                h = g * REP + r
                o_ref[bi, :, h * HEAD_DIM:(h + 1) * HEAD_DIM] = o[r * lq:(r + 1) * lq]


def _attn_prompt(sinks, q, k, v):
    nb, l, _ = q.shape
    nblk = l // WINDOW
    cur = lambda w: pl.BlockSpec((1, WINDOW, w), lambda b, n: (b, n, 0))
    prev = lambda w: pl.BlockSpec((1, WINDOW, w), lambda b, n: (b, jnp.maximum(n - 1, 0), 0))
    return pl.pallas_call(
        functools.partial(_attn_kernel, lq=WINDOW, prev_from_block=True),
        out_shape=jax.ShapeDtypeStruct((nb, l, D_ATTN), F32),
        grid=(nb, nblk),
        in_specs=[pl.BlockSpec(memory_space=pltpu.SMEM), cur(D_ATTN), prev(D_KV), cur(D_KV), prev(D_KV), cur(D_KV)],
        out_specs=cur(D_ATTN),
        compiler_params=_cparams(("arbitrary", "arbitrary")),
        name="attn_prompt",
    )(sinks, q, k, k, v, v)


def _attn_sample(sinks, q, kcat, vcat):
    nb, lq, _ = q.shape
    nbb = ATTN_SAMPLE_BATCH
    kblk = lambda n: pl.BlockSpec((nbb, WINDOW, D_KV), lambda b: (b, n, 0))
    return pl.pallas_call(
        functools.partial(_attn_kernel, lq=lq, prev_from_block=False),
        out_shape=jax.ShapeDtypeStruct((nb, lq, D_ATTN), F32),
        grid=(nb // nbb,),
        in_specs=[pl.BlockSpec(memory_space=pltpu.SMEM), pl.BlockSpec((nbb, lq, D_ATTN), lambda b: (b, 0, 0)),
                  kblk(0), kblk(1), kblk(0), kblk(1)],
        out_specs=pl.BlockSpec((nbb, lq, D_ATTN), lambda b: (b, 0, 0)),
        compiler_params=_cparams(("arbitrary",)),
        name="attn_sample",
    )(sinks, q, kcat, kcat, vcat, vcat)


N_KT = D_SSM // MXU_DIM
ST_PER_KT = N_STATE // N_KT
N_SLAB = D_SSM // LANES


def _cproj(hr_ref, hi_ref, cre_ref, cimn_ref, kt):
    cs = slice(kt * ST_PER_KT, (kt + 1) * ST_PER_KT)
    return (jnp.dot(hr_ref[:, cs].astype(BF16), cre_ref[kt], preferred_element_type=F32)
            + jnp.dot(hi_ref[:, cs].astype(BF16), cimn_ref[kt], preferred_element_type=F32))


def _s5_prompt_kernel(u_ref, bre_ref, bim_ref, cre_ref, cimn_ref, a1r_ref, a1i_ref, par_ref, pai_ref, d_ref,
                      h0r_ref, h0i_ref, y_ref, cr_ref, ci_ref, il_ref, xr_ref, xi_ref, *, nb, tt):
    rows = nb * tt

    @pl.when(pl.program_id(0) == 0)
    def _():
        cr_ref[...] = h0r_ref[...]
        ci_ref[...] = h0i_ref[...]

    for b in range(nb):
        for j in range(N_SLAB):
            il_ref[j, pl.ds(b, tt, stride=nb), :] = u_ref[b, :, j * LANES:(j + 1) * LANES]
    for kt in range(N_KT):
        ub = jnp.concatenate([il_ref[2 * kt], il_ref[2 * kt + 1]], axis=1).astype(BF16)
        cs = slice(kt * ST_PER_KT, (kt + 1) * ST_PER_KT)
        xr_ref[:, cs] = jnp.dot(ub, bre_ref[kt], preferred_element_type=F32)
        xi_ref[:, cs] = jnp.dot(ub, bim_ref[kt], preferred_element_type=F32)

    half = lax.broadcasted_iota(jnp.int32, (SUBLANES, LANES), 0) < nb

    def step(i, carry):
        r0 = pl.multiple_of(i * SUBLANES, SUBLANES)
        for j in range(N_STATE // LANES):
            ls = slice(j * LANES, (j + 1) * LANES)
            x_r = xr_ref[pl.ds(r0, SUBLANES), ls]
            x_i = xi_ref[pl.ds(r0, SUBLANES), ls]
            s_r = pltpu.roll(x_r, nb, 0)
            s_i = pltpu.roll(x_i, nb, 0)
            a1r, a1i = a1r_ref[:, ls], a1i_ref[:, ls]
            c_r, c_i = cr_ref[:, ls], ci_ref[:, ls]
            par, pai = par_ref[:, ls], pai_ref[:, ls]
            h_r = x_r + (a1r * s_r - a1i * s_i) + (par * c_r - pai * c_i)
            h_i = x_i + (a1r * s_i + a1i * s_r) + (par * c_i + pai * c_r)
            xr_ref[pl.ds(r0, SUBLANES), ls] = h_r
            xi_ref[pl.ds(r0, SUBLANES), ls] = h_i
            cr_ref[:, ls] = jnp.where(half, pltpu.roll(h_r, nb, 0), h_r)
            ci_ref[:, ls] = jnp.where(half, pltpu.roll(h_i, nb, 0), h_i)
        return carry

    lax.fori_loop(0, rows // SUBLANES, step, 0)

    for kt in range(N_KT):
        y = _cproj(xr_ref, xi_ref, cre_ref, cimn_ref, kt)
        il_ref[2 * kt] = y[:, :LANES]
        il_ref[2 * kt + 1] = y[:, LANES:]
    for b in range(nb):
        for j in range(N_SLAB):
            ls = slice(j * LANES, (j + 1) * LANES)
            y_ref[b, :, ls] = il_ref[j, pl.ds(b, tt, stride=nb), :] + d_ref[:, ls] * u_ref[b, :, ls]


def _s5_prompt(u, wts, tabs, d_skip, h0r, h0i):
    nb, l, _ = u.shape
    tt = S5_TT
    rows = nb * tt
    full = lambda a: pl.BlockSpec(a.shape, lambda i: (0,) * a.ndim)
    blk = pl.BlockSpec((nb, tt, D_SSM), lambda i: (0, i, 0))
    carry = jax.ShapeDtypeStruct((SUBLANES, N_STATE), F32)
    return pl.pallas_call(
        functools.partial(_s5_prompt_kernel, nb=nb, tt=tt),
        out_shape=(jax.ShapeDtypeStruct((nb, l, D_SSM), F32), carry, carry),
        grid=(l // tt,),
        in_specs=[blk] + [full(a) for a in (*wts, *tabs, d_skip, h0r, h0i)],
        out_specs=(blk, pl.BlockSpec((SUBLANES, N_STATE), lambda i: (0, 0)),
                   pl.BlockSpec((SUBLANES, N_STATE), lambda i: (0, 0))),
        scratch_shapes=[pltpu.VMEM((N_SLAB, rows, LANES), F32), pltpu.VMEM((rows, N_STATE), F32),
                        pltpu.VMEM((rows, N_STATE), F32)],
        compiler_params=_cparams(("arbitrary",)),
        name="s5_prompt",
    )(u, *wts, *tabs, d_skip, h0r, h0i)


def _s5_sample_kernel(u_ref, bre_ref, bim_ref, cre_ref, cimn_ref, ar_ref, ai_ref, d_ref, h0r_ref, h0i_ref,
                      y_ref, sr_ref, si_ref, xr_ref, xi_ref, *, nt):
    sr_ref[...] = h0r_ref[...]
    si_ref[...] = h0i_ref[...]
    for t in range(nt):
        for kt in range(N_KT):
            ub = u_ref[t, :, kt * MXU_DIM:(kt + 1) * MXU_DIM].astype(BF16)
            cs = slice(kt * ST_PER_KT, (kt + 1) * ST_PER_KT)
            xr_ref[:, cs] = jnp.dot(ub, bre_ref[kt], preferred_element_type=F32)
            xi_ref[:, cs] = jnp.dot(ub, bim_ref[kt], preferred_element_type=F32)
        ar, ai = ar_ref[...], ai_ref[...]
        s_r, s_i = sr_ref[...], si_ref[...]
        sr_ref[...] = xr_ref[...] + (ar * s_r - ai * s_i)
        si_ref[...] = xi_ref[...] + (ar * s_i + ai * s_r)
        for kt in range(N_KT):
            ys = slice(kt * MXU_DIM, (kt + 1) * MXU_DIM)
            y_ref[t, :, ys] = _cproj(sr_ref, si_ref, cre_ref, cimn_ref, kt) + d_ref[:, ys] * u_ref[t, :, ys]


def _s5_sample(u_tb, wts, ar, ai, d_skip, h0r, h0i):
    nt, nb, _ = u_tb.shape
    st = jax.ShapeDtypeStruct((nb, N_STATE), F32)
    args = (u_tb, *wts, ar, ai, d_skip, h0r, h0i)
    full = lambda a: pl.BlockSpec(a.shape, lambda i: (0,) * a.ndim)
    return pl.pallas_call(
        functools.partial(_s5_sample_kernel, nt=nt),
        out_shape=(jax.ShapeDtypeStruct((nt, nb, D_SSM), F32), st, st),
        grid=(1,),
        in_specs=[full(a) for a in args],
        out_specs=(pl.BlockSpec((nt, nb, D_SSM), lambda i: (0, 0, 0)), pl.BlockSpec((nb, N_STATE), lambda i: (0, 0)),
                   pl.BlockSpec((nb, N_STATE), lambda i: (0, 0))),
        scratch_shapes=[pltpu.VMEM((nb, N_STATE), F32), pltpu.VMEM((nb, N_STATE), F32)],
        compiler_params=_cparams(("arbitrary",)),
        name="s5_sample",
    )(*args)


def _gather_rows(src_ref, ids_ref, id_stride, id_offset, dst_ref, sem):
    def body(g, c):
        for j in range(SUBLANES):
            t = ids_ref[0, id_stride * (g * SUBLANES + j) + id_offset]
            pltpu.make_async_copy(src_ref.at[t >> 3, pl.ds(t & (SUBLANES - 1), 1)], dst_ref.at[g, pl.ds(j, 1)],
                                  sem).start(priority=j % 2)
        return c
    lax.fori_loop(0, dst_ref.shape[0], body, 0)


def _index_specs(n_cols, first_tile, n_steps):
    def spec(fn):
        return pl.BlockSpec((None, 1, n_cols), fn, memory_space=pltpu.SMEM)

    return (spec(lambda s, *_: (first_tile, 0, 0)),
            spec(lambda s, *_: (first_tile + jnp.minimum(s + 1, n_steps - 1), 0, 0)))


N_POST_INPUTS = 17


def _post_kernel(*refs, has_tail):
    ins, outs = refs[:N_POST_INPUTS], refs[N_POST_INPUTS + int(has_tail):]
    if not has_tail:
        _post_body(*ins, *outs)
        return
    tail_ref, h2_ref = refs[N_POST_INPUTS], outs[1]
    last = pl.num_programs(0) - 1

    @pl.when(pl.program_id(0) == last)
    def _():
        h2_ref[...] = tail_ref[...]

    @pl.when(pl.program_id(0) < last)
    def _():
        _post_body(*ins, *outs)


def _post_body(attn_ref, yssm_ref, x_ref, g1_ref, sh2_ref, sc2_ref, gluw_ref, glub_ref, ga_ref, gs_ref, wout_ref,
               l1g_ref, l1b_ref, wr_ref, br_ref, tri_ref, cnt0_ref, x1_ref, h2_ref, route_ref, cnt_ref):
    tm = x_ref.shape[0]

    @pl.when(pl.program_id(0) == 0)
    def _():
        cnt_ref[...] = cnt0_ref[...]

    z = jax.nn.gelu(yssm_ref[...])
    ssm = z * jax.nn.sigmoid(jnp.dot(z.astype(BF16), gluw_ref[...], preferred_element_type=F32) + glub_ref[...])
    mixed_a = (_rms(attn_ref[...]) * ga_ref[...]).astype(BF16)
    mixed_s = (_rms(ssm) * gs_ref[...]).astype(BF16)
    o = (jnp.dot(mixed_a, wout_ref[:D_ATTN, :], preferred_element_type=F32)
         + jnp.dot(mixed_s, wout_ref[D_ATTN:, :], preferred_element_type=F32))
    x1 = _ln(DEEPNORM_ALPHA * x_ref[...] + g1_ref[...] * o) * l1g_ref[...] + l1b_ref[...]
    x1_ref[...] = x1
    h2 = _ln(x1) * (1.0 + sc2_ref[...]) + sh2_ref[...]
    h2_ref[...] = h2
    logits = jnp.dot(h2.astype(BF16), wr_ref[...], preferred_element_type=F32) + br_ref[...]

    lane = lax.broadcasted_iota(jnp.int32, (tm, LANES), 1).astype(F32)
    big = float(4 * LANES)
    neg = -jnp.inf
    gl = jnp.where(lane < N_EXPERT_GROUPS, logits, neg)
    gp = jnp.exp(gl - jnp.max(gl, axis=-1, keepdims=True))
    gp = gp / jnp.sum(gp, axis=-1, keepdims=True)
    g_val = jnp.max(gp, axis=-1, keepdims=True)
    g_idx = jnp.min(jnp.where(gp == g_val, lane, big), axis=-1, keepdims=True)
    lo = ROUTE_LANE0 + EXPERTS_PER_GROUP * g_idx
    emask = (lane >= lo) & (lane < lo + EXPERTS_PER_GROUP)
    el = jnp.where(emask, logits, neg)
    ep = jnp.exp(el - jnp.max(el, axis=-1, keepdims=True))
    ep = jnp.where(emask, ep / jnp.sum(ep, axis=-1, keepdims=True), -1.0)
    v1 = jnp.max(ep, axis=-1, keepdims=True)
    i1 = jnp.min(jnp.where(ep == v1, lane, big), axis=-1, keepdims=True)
    ep2 = jnp.where(lane == i1, -1.0, ep)
    v2 = jnp.max(ep2, axis=-1, keepdims=True)
    i2 = jnp.min(jnp.where(ep2 == v2, lane, big), axis=-1, keepdims=True)
    vs = v1 + v2
    w1 = g_val * (v1 / vs)
    w2 = g_val * (v2 / vs)
    hit = jnp.where((lane == i1) | (lane == i2), 1.0, 0.0)
    before = jnp.dot(tri_ref[...], hit.astype(BF16), preferred_element_type=F32) + cnt_ref[0:1, :]
    r1 = jnp.sum(jnp.where(lane == i1, before, 0.0), axis=-1, keepdims=True)
    r2 = jnp.sum(jnp.where(lane == i2, before, 0.0), axis=-1, keepdims=True)
    cnt_ref[...] = cnt_ref[...] + jnp.sum(hit, axis=0, keepdims=True)
    e1 = i1 - ROUTE_LANE0
    e2 = i2 - ROUTE_LANE0
    route = jnp.zeros((tm, LANES), F32)
    for n, val in enumerate((e1, e2, w1, w2, r1, r2)):
        route = jnp.where(lane == n, val, route)
    route_ref[...] = route


def _post(attn, yssm, x, mod, per_row, w, cnt0, tail=None):
    nb, l, _ = x.shape
    tm = min(TM, l)
    nt = l // tm
    n_body = nb * nt
    has_tail = tail is not None
    if has_tail:
        assert tail.shape == (tm, D_MODEL)

    def bi(s):
        s = jnp.minimum(s, n_body - 1)
        return s // nt, s % nt

    row = lambda wd: pl.BlockSpec((None, tm, wd), lambda s: (*bi(s), 0))
    if per_row:
        mspec = lambda chunk: pl.BlockSpec((None, tm, D_MODEL), lambda s: (*bi(s), chunk))
    else:
        mspec = lambda chunk: pl.BlockSpec((None, 1, D_MODEL), lambda s: (bi(s)[0], 0, chunk))
    full = lambda a: pl.BlockSpec(a.shape, lambda s: (0,) * a.ndim)
    tri = jnp.tril(jnp.ones((tm, tm), F32), -1).astype(BF16)
    consts = (w["glu_w"], w["glu_b"], w["attn_g"], w["ssm_g"], w["w_out"], w["ln1_g"], w["ln1_b"], w["wr"], w["br"],
              tri, cnt0) + ((tail,) if has_tail else ())
    assert 6 + len(consts) == N_POST_INPUTS + int(has_tail)
    n_steps = n_body + int(has_tail)
    return pl.pallas_call(
        functools.partial(_post_kernel, has_tail=has_tail),
        out_shape=(jax.ShapeDtypeStruct((nb, l, D_MODEL), F32),
                   jax.ShapeDtypeStruct((n_steps * tm, D_MODEL), F32),
                   jax.ShapeDtypeStruct((nb, l, LANES), F32), jax.ShapeDtypeStruct((SUBLANES, LANES), F32)),
        grid=(n_steps,),
        in_specs=[row(D_ATTN), row(D_SSM), row(D_MODEL), mspec(2), mspec(3), mspec(4)] + [full(a) for a in consts],
        out_specs=(row(D_MODEL), pl.BlockSpec((tm, D_MODEL), lambda s: (s, 0)),
                   row(LANES), pl.BlockSpec((SUBLANES, LANES), lambda s: (0, 0))),
        compiler_params=_cparams(("arbitrary",)),
        name="post",
    )(attn, yssm, x, mod, mod, mod, *consts)


def _moe_kernel(te_ref, nv_ref, rt0_ref, rtn_ref, h2_ref, wg_ref, wu_ref, wd_ref, y_ref, buf_ref, wgb_ref, wub_ref,
                wdb_ref, sem):
    i = pl.program_id(0)
    nv = nv_ref[0]
    slot = i & 1

    def gather(ids_ref, dst_slot):
        _gather_rows(h2_ref, ids_ref, 1, 0, buf_ref.at[dst_slot], sem.at[dst_slot])

    @pl.when(i == 0)
    def _():
        gather(rt0_ref, 0)

    @pl.when(i <= nv)
    def _():
        pltpu.make_async_copy(h2_ref.at[pl.ds(0, TM_MOE // SUBLANES)], buf_ref.at[slot], sem.at[slot]).wait()

    changed = (i == 0) | (te_ref[i] != te_ref[jnp.maximum(i - 1, 0)])

    @pl.when((i < nv) & changed)
    def _():
        wgb_ref[...] = wg_ref[...].astype(BF16)
        wub_ref[...] = wu_ref[...].astype(BF16)
        wdb_ref[...] = wd_ref[...].astype(BF16)

    @pl.when(i < nv)
    def _():
        gather(rtn_ref, 1 - slot)
        xb = buf_ref[slot].reshape(TM_MOE, D_MODEL).astype(BF16)
        hg = jnp.dot(xb, wgb_ref[...], preferred_element_type=F32)
        hu = jnp.dot(xb, wub_ref[...], preferred_element_type=F32)
        act = (hg * jax.nn.sigmoid(hg)) * hu
        y_ref[...] = jnp.dot(act.astype(BF16), wdb_ref[...], preferred_element_type=F32)

    @pl.when(i >= nv)
    def _():
        y_ref[...] = jnp.zeros(y_ref.shape, y_ref.dtype)


def _moe(tile_expert, n_valid, row_token, h2, w_gate, w_up, w_down):
    n_steps = row_token.shape[0] // TM_MOE
    rt = row_token.reshape(n_steps, 1, TM_MOE)
    wspec = lambda a, b: pl.BlockSpec((None, a, b), lambda i, te, nv: (te[i], 0, 0))
    return pl.pallas_call(
        _moe_kernel,
        out_shape=jax.ShapeDtypeStruct((n_steps * TM_MOE, D_MODEL), F32),
        grid_spec=pltpu.PrefetchScalarGridSpec(
            num_scalar_prefetch=2, grid=(n_steps,),
            in_specs=[*_index_specs(TM_MOE, 0, n_steps), pl.BlockSpec(memory_space=pl.ANY),
                      wspec(D_MODEL, D_EXPERT), wspec(D_MODEL, D_EXPERT), wspec(D_EXPERT, D_MODEL)],
            out_specs=pl.BlockSpec((TM_MOE, D_MODEL), lambda i, te, nv: (i, 0)),
            scratch_shapes=[pltpu.VMEM((2, TM_MOE // SUBLANES, SUBLANES, D_MODEL), F32),
                            pltpu.VMEM((D_MODEL, D_EXPERT), BF16),
                            pltpu.VMEM((D_MODEL, D_EXPERT), BF16), pltpu.VMEM((D_EXPERT, D_MODEL), BF16),
                            pltpu.SemaphoreType.DMA((2,))]),
        compiler_params=_cparams(("arbitrary",)),
        name="moe",
    )(tile_expert, n_valid, rt, rt, h2.reshape(-1, SUBLANES, D_MODEL), w_gate, w_up, w_down)


def _final_kernel(pos0_ref, posn_ref, x1_ref, g2_ref, route_ref, l2g_ref, l2b_ref, ys_ref, o_ref, buf_ref, sem, *, tm):
    step = pl.program_id(0)
    slot = step & 1

    def gather(ids_ref, dst_slot):
        for k in range(2):
            _gather_rows(ys_ref, ids_ref, 2, k, buf_ref.at[dst_slot, k], sem.at[dst_slot])

    @pl.when(step == 0)
    def _():
        gather(pos0_ref, 0)

    for k in range(2):
        pltpu.make_async_copy(ys_ref.at[pl.ds(0, tm // SUBLANES)], buf_ref.at[slot, k], sem.at[slot]).wait()

    @pl.when(step + 1 < pl.num_programs(0))
    def _():
        gather(posn_ref, 1 - slot)

    route = route_ref[...]
    f = (route[:, 2:3] * buf_ref[slot, 0].reshape(tm, D_MODEL)
         + route[:, 3:4] * buf_ref[slot, 1].reshape(tm, D_MODEL))
    o_ref[...] = _ln(DEEPNORM_ALPHA * x1_ref[...] + g2_ref[...] * f) * l2g_ref[...] + l2b_ref[...]


def _final(pos_tiles, first_tile, x1, mod, per_row, route, ln2_g, ln2_b, ys):
    nb, l, _ = x1.shape
    tm = min(TM, l)
    nt = l // tm
    n_steps = nb * nt
    row = lambda wd: pl.BlockSpec((None, tm, wd), lambda s: (s // nt, s % nt, 0))
    if per_row:
        g2 = pl.BlockSpec((None, tm, D_MODEL), lambda s: (s // nt, s % nt, 5))
    else:
        g2 = pl.BlockSpec((None, 1, D_MODEL), lambda s: (s // nt, 0, 5))
    vec = pl.BlockSpec((1, D_MODEL), lambda s: (0, 0))
    return pl.pallas_call(
        functools.partial(_final_kernel, tm=tm),
        out_shape=jax.ShapeDtypeStruct((nb, l, D_MODEL), F32),
        grid=(n_steps,),
        in_specs=[*_index_specs(2 * tm, first_tile, n_steps), row(D_MODEL), g2, row(LANES), vec, vec,
                  pl.BlockSpec(memory_space=pl.ANY)],
        out_specs=row(D_MODEL),
        scratch_shapes=[pltpu.VMEM((2, 2, tm // SUBLANES, SUBLANES, D_MODEL), F32), pltpu.SemaphoreType.DMA((2,))],
        compiler_params=_cparams(("arbitrary",)),
        name="final",
    )(pos_tiles, pos_tiles, x1, mod, route, ln2_g, ln2_b, ys.reshape(-1, SUBLANES, D_MODEL))


def _rope_tables(pos):
    half = ROT_DIM // 2
    inv_freq = ROPE_THETA ** (-jnp.arange(half, dtype=jnp.float32) * 2.0 / ROT_DIM)
    ang = pos.astype(jnp.float32)[:, None] * inv_freq[None, :]
    cos, sin = jnp.cos(ang), jnp.sin(ang)
    n = pos.shape[0]
    one = jnp.ones((n, HEAD_DIM - ROT_DIM), F32)
    zero = jnp.zeros((n, HEAD_DIM - half), F32)
    c = jnp.concatenate([cos, cos, one], -1)
    a = jnp.concatenate([-sin, zero], -1)
    b = jnp.concatenate([jnp.zeros((n, half), F32), sin, jnp.zeros((n, HEAD_DIM - ROT_DIM), F32)], -1)
    return tuple(jnp.tile(t, (1, LANES // HEAD_DIM)) for t in (c, a, b))


def _s5_params(lam_re, lam_im, log_step, b_re, b_im, c_re, c_im):
    f32 = jnp.float32
    dt = jnp.exp(log_step.astype(f32))[:, None]
    lr, li = lam_re.astype(f32), lam_im.astype(f32)
    mag = jnp.exp(lr * dt)
    ar, ai = mag * jnp.cos(li * dt), mag * jnp.sin(li * dt)
    den = lr * lr + li * li
    cr = ((ar - 1.0) * lr + ai * li) / den
    ci = (ai * lr - (ar - 1.0) * li) / den
    br, bi = b_re.astype(f32), b_im.astype(f32)
    bbr = cr[..., None] * br - ci[..., None] * bi
    bbi = cr[..., None] * bi + ci[..., None] * br
    gpt = MXU_DIM // SSM_CH
    eye = jnp.eye(gpt, dtype=f32)

    def pack_b(m):
        m = m.reshape(N_KT, gpt, SSM_STATE, SSM_CH)
        return jnp.einsum("kgpc,gh->kgchp", m, eye).reshape(N_KT, MXU_DIM, ST_PER_KT).astype(BF16)

    def pack_c(m):
        m = m.astype(f32).reshape(N_KT, gpt, SSM_CH, SSM_STATE)
        return jnp.einsum("kgcp,gh->khpgc", m, eye).reshape(N_KT, ST_PER_KT, MXU_DIM).astype(BF16)

    wts = (pack_b(bbr), pack_b(bbi), pack_c(c_re), pack_c(-c_im.astype(f32)))
    return wts, ar.reshape(1, N_STATE), ai.reshape(1, N_STATE)


def _scan_tables(ar, ai, nb):
    assert SUBLANES // nb == 2
    a2r, a2i = ar * ar - ai * ai, 2.0 * ar * ai
    z = jnp.zeros_like(ar)
    rep = lambda first, second: jnp.concatenate([jnp.tile(first, (nb, 1)), jnp.tile(second, (nb, 1))], 0)
    return rep(z, ar), rep(z, ai), rep(ar, a2r), rep(ai, a2i)


def kernel(x_prompt, x_sample, cache_k_win, cache_v_win, state_ssm_re, state_ssm_im, c_prompt, c_sample, ada_w, ada_b,
           w_in, attn_sinks, ssm_lambda_re, ssm_lambda_im, ssm_log_step, ssm_b_re, ssm_b_im, ssm_c_re, ssm_c_im, ssm_d,
           ssm_glu_w, ssm_glu_b, attn_norm_g, ssm_norm_g, w_out, ln1_g, ln1_b, router_group_w, router_group_b,
           router_expert_w, router_expert_b, exp_w_gate, exp_w_up, exp_w_down, ln2_g, ln2_b):
    assert ada_w.shape[0] == DEPTH
    bp, lp, _ = x_prompt.shape
    bs, ls, _ = x_sample.shape
    lsp = SAMPLE_PAD

    w_in_bf = w_in[0].astype(BF16)
    wr = jnp.concatenate([router_group_w[0], router_expert_w[0]], -1)
    wr = jnp.pad(wr, ((0, 0), (0, LANES - wr.shape[1]))).astype(BF16)
    br = jnp.pad(jnp.concatenate([router_group_b[0], router_expert_b[0]], -1), (0, LANES - N_EXPERT_GROUPS - N_EXPERTS))
    wpost = dict(glu_w=ssm_glu_w[0].astype(BF16), glu_b=ssm_glu_b[0][None], attn_g=attn_norm_g[0][None],
                 ssm_g=ssm_norm_g[0][None], w_out=w_out[0].astype(BF16), ln1_g=ln1_g[0][None], ln1_b=ln1_b[0][None],
                 wr=wr, br=br[None])
    s5w, ar, ai = _s5_params(ssm_lambda_re[0], ssm_lambda_im[0], ssm_log_step[0], ssm_b_re[0], ssm_b_im[0],
                             ssm_c_re[0], ssm_c_im[0])
    d_skip = ssm_d[0][None]
    sinks = attn_sinks[0]

    n_c = bp + bs
    n_cp = -(-n_c // SUBLANES) * SUBLANES
    c_all = jnp.pad(jnp.concatenate([c_prompt, c_sample], 0), ((0, n_cp - n_c), (0, 0)))
    mod = _ada(c_all, ada_w[0], ada_b[0][None])
    mod_p = mod[:bp][:, None, :]
    mod_s = jnp.repeat(mod[bp:n_c], lsp, axis=0)[None]

    xs_pad = jnp.pad(x_sample, ((0, 0), (0, lsp - ls), (0, 0))).reshape(1, bs * lsp, D_MODEL)

    rope_p = _rope_tables(jnp.arange(lp))
    pos_s = PAST_LEN + jnp.minimum(jnp.arange(lsp), ls - 1)
    rope_s = tuple(jnp.tile(t, (bs, 1)) for t in _rope_tables(pos_s))
    q_p, k_p, v_p, u_p = _inproj(x_prompt, mod_p, w_in_bf, rope_p, False)
    q_s, k_s, v_s, u_s = _inproj(xs_pad, mod_s, w_in_bf, rope_s, True)

    attn_p = _attn_prompt(sinks, q_p, k_p, v_p)
    k_s3 = k_s.reshape(bs, lsp, D_KV)
    v_s3 = v_s.reshape(bs, lsp, D_KV)
    ck = cache_k_win[0].reshape(bs, WINDOW, D_KV)
    cv = cache_v_win[0].reshape(bs, WINDOW, D_KV)
    padk = lambda new: jnp.pad(new, ((0, 0), (0, WINDOW - lsp), (0, 0)))
    attn_s = _attn_sample(sinks, q_s.reshape(bs, lsp, D_ATTN), jnp.concatenate([ck, padk(k_s3)], 1),
                          jnp.concatenate([cv, padk(v_s3)], 1))

    tabs = _scan_tables(ar, ai, bp)
    zero_carry = jnp.zeros((SUBLANES, N_STATE), F32)
    y_p, hr_p, hi_p = _s5_prompt(u_p, s5w, tabs, d_skip, zero_carry, zero_carry)
    u_tb = jnp.transpose(u_s.reshape(bs, lsp, D_SSM)[:, :ls], (1, 0, 2))
    y_tb, hr_s, hi_s = _s5_sample(u_tb, s5w, ar, ai, d_skip, state_ssm_re[0].reshape(bs, N_STATE),
                                  state_ssm_im[0].reshape(bs, N_STATE))
    y_s = jnp.pad(jnp.transpose(y_tb, (1, 0, 2)), ((0, 0), (0, lsp - ls), (0, 0))).reshape(1, bs * lsp, D_SSM)

    cnt0 = jnp.zeros((SUBLANES, LANES), F32)
    n_p, n_s = bp * lp, bs * lsp
    n_tok = n_p + n_s
    x1_s, h2_s, route_s, cnt_s = _post(attn_s.reshape(1, bs * lsp, D_ATTN), y_s, xs_pad, mod_s, True, wpost, cnt0)
    x1_p, h2, route_p, cnt = _post(attn_p, y_p, x_prompt, mod_p, False, wpost, cnt_s, tail=h2_s)

    n_tiles = -(-(2 * n_tok + N_EXPERTS * (TM_MOE - 1)) // TM_MOE) + 1
    route = jnp.concatenate([route_p.reshape(n_p, LANES), route_s.reshape(n_s, LANES)], 0)
    counts = cnt[0, ROUTE_LANE0:ROUTE_LANE0 + N_EXPERTS].astype(jnp.int32)
    padded = ((counts + TM_MOE - 1) // TM_MOE) * TM_MOE
    ends = jnp.cumsum(padded)
    offs = ends - padded
    eid = route[:, 0:2].astype(jnp.int32)
    pos = (offs[eid] + route[:, 4:6].astype(jnp.int32)).reshape(-1)
    n_valid = ends[-1] // TM_MOE
    tile_row = jnp.minimum(jnp.arange(n_tiles), n_valid - 1) * TM_MOE
    tile_expert = jnp.sum((ends[None, :] <= tile_row[:, None]).astype(jnp.int32), axis=1)
    token_of = jnp.repeat(jnp.arange(n_tok, dtype=jnp.int32), 2)
    row_token = jnp.zeros((n_tiles * TM_MOE,), jnp.int32).at[pos].set(token_of)

    ys = _moe(tile_expert, n_valid.reshape(1).astype(jnp.int32), row_token, h2, exp_w_gate[0], exp_w_up[0],
              exp_w_down[0])
    pos_tiles = pos.reshape(n_tok // TM, 1, 2 * TM)
    y_prompt = _final(pos_tiles, 0, x1_p, mod_p, False, route_p, ln2_g[0][None], ln2_b[0][None], ys)
    y_samp = _final(pos_tiles, n_p // TM, x1_s, mod_s, True, route_s, ln2_g[0][None], ln2_b[0][None], ys)
    y_sample = y_samp.reshape(bs, lsp, D_MODEL)[:, :ls]

    kv5 = lambda a, n: a.reshape(1, a.shape[0], n, N_KV_HEADS, HEAD_DIM)
    k_win_p = kv5(k_p[:, lp - WINDOW:], WINDOW)
    v_win_p = kv5(v_p[:, lp - WINDOW:], WINDOW)
    k_win_s = kv5(jnp.concatenate([ck[:, ls:], k_s3[:, :ls]], 1), WINDOW)
    v_win_s = kv5(jnp.concatenate([cv[:, ls:], v_s3[:, :ls]], 1), WINDOW)
    st = lambda a: a.reshape(1, a.shape[0], N_SSM_GROUPS, SSM_STATE)
    return (y_prompt, y_sample, k_win_p, v_win_p, st(hr_p[bp:2 * bp]), st(hi_p[bp:2 * bp]),
            k_win_s, v_win_s, st(hr_s), st(hi_s))
```

```python
import functools
import math

import jax
import jax.numpy as jnp
from jax import lax
from jax.experimental import pallas as pl
from jax.experimental.pallas import tpu as pltpu

F32 = jnp.float32
BF16 = jnp.bfloat16

D_MODEL = 2048
D_ATTN = 1024
D_SSM = 1024
HEAD_DIM = 64
N_HEADS = 16
N_KV_HEADS = 4
REP = 4
D_KV = 256
ROT_DIM = 16
ROPE_THETA = 500000.0
WINDOW = 128
SSM_CH = 16
N_SSM_GROUPS = 64
SSM_STATE = 64
N_STATE = N_SSM_GROUPS * SSM_STATE
PROJ_COLS = D_ATTN + 2 * D_KV + D_SSM
N_EXPERT_GROUPS = 4
EXPERTS_PER_GROUP = 8
N_EXPERTS = 32
D_EXPERT = 512
DEPTH = 1
DEEPNORM_ALPHA = (2.0 * DEPTH) ** 0.25
LN_EPS = 1e-5
PAST_LEN = 16384

LANES = 128
SUBLANES = 8
MXU_DIM = 256
TM = 256
TM_MOE = 256
S5_TT = 64
SAMPLE_PAD = 8
ROUTE_LANE0 = N_EXPERT_GROUPS
VMEM_LIMIT = 56 * 1024 * 1024


def _cparams(sem):
    return pltpu.CompilerParams(dimension_semantics=sem, vmem_limit_bytes=VMEM_LIMIT)


def _ln(x):
    mu = jnp.mean(x, axis=-1, keepdims=True)
    xc = x - mu
    var = jnp.mean(xc * xc, axis=-1, keepdims=True)
    return xc * lax.rsqrt(var + LN_EPS)


def _rms(x):
    return x * lax.rsqrt(jnp.mean(x * x, axis=-1, keepdims=True) + LN_EPS)


def _ada_kernel(c_ref, w_ref, b_ref, o_ref):
    c = c_ref[...]
    s = c * jax.nn.sigmoid(c)
    o_ref[...] = jnp.dot(s.astype(BF16), w_ref[...].astype(BF16), preferred_element_type=F32) + b_ref[...]


def _ada(c_all, ada_w, ada_b):
    n, tn = c_all.shape[0], 1024
    return pl.pallas_call(
        _ada_kernel,
        out_shape=jax.ShapeDtypeStruct((n, 6 * D_MODEL), F32),
        grid=(6 * D_MODEL // tn,),
        in_specs=[pl.BlockSpec((n, D_MODEL), lambda j: (0, 0)),
                  pl.BlockSpec((D_MODEL, tn), lambda j: (0, j)),
                  pl.BlockSpec((1, tn), lambda j: (0, j))],
        out_specs=pl.BlockSpec((n, tn), lambda j: (0, j)),
        compiler_params=_cparams(("arbitrary",)),
        name="ada",
    )(c_all, ada_w, ada_b)


def _mod_spec(per_row, tm, chunk):
    if per_row:
        return pl.BlockSpec((None, tm, D_MODEL), lambda b, i: (b, i, chunk))
    return pl.BlockSpec((None, 1, D_MODEL), lambda b, i: (b, 0, chunk))


def _inproj_kernel(x_ref, sh_ref, sc_ref, w_ref, rc_ref, ra_ref, rb_ref, q_ref, k_ref, v_ref, u_ref):
    h = _ln(x_ref[...]) * (1.0 + sc_ref[...]) + sh_ref[...]
    proj = jnp.dot(h.astype(BF16), w_ref[...], preferred_element_type=F32)
    rc, ra, rb = rc_ref[...], ra_ref[...], rb_ref[...]

    def rope(t):
        return t * rc + pltpu.roll(t, LANES - ROT_DIM // 2, 1) * ra + pltpu.roll(t, ROT_DIM // 2, 1) * rb

    for j in range(D_ATTN // LANES):
        q_ref[:, j * LANES:(j + 1) * LANES] = (rope(proj[:, j * LANES:(j + 1) * LANES]) * HEAD_DIM ** -0.5).astype(BF16)
    for j in range(D_KV // LANES):
        c0 = D_ATTN + j * LANES
        k_ref[:, j * LANES:(j + 1) * LANES] = rope(proj[:, c0:c0 + LANES])
    v_ref[...] = proj[:, D_ATTN + D_KV:D_ATTN + 2 * D_KV]
    u_ref[...] = proj[:, D_ATTN + 2 * D_KV:]


def _inproj(x, mod, w_in_bf, rope_tabs, per_row):
    nb, l, _ = x.shape
    tm = min(TM, l)
    row = lambda w: pl.BlockSpec((None, tm, w), lambda b, i: (b, i, 0))
    tab = pl.BlockSpec((tm, LANES), lambda b, i: (i, 0))
    return pl.pallas_call(
        _inproj_kernel,
        out_shape=(jax.ShapeDtypeStruct((nb, l, D_ATTN), BF16), jax.ShapeDtypeStruct((nb, l, D_KV), F32),
                   jax.ShapeDtypeStruct((nb, l, D_KV), F32), jax.ShapeDtypeStruct((nb, l, D_SSM), F32)),
        grid=(nb, l // tm),
        in_specs=[row(D_MODEL), _mod_spec(per_row, tm, 0), _mod_spec(per_row, tm, 1),
                  pl.BlockSpec((D_MODEL, PROJ_COLS), lambda b, i: (0, 0)), tab, tab, tab],
        out_specs=(row(D_ATTN), row(D_KV), row(D_KV), row(D_SSM)),
        compiler_params=_cparams(("arbitrary", "arbitrary")),
        name="inproj",
    )(x, mod, mod, w_in_bf, *rope_tabs)


ATTN_SAMPLE_BATCH = 8


def _attn_kernel(sink_ref, q_ref, kp_ref, kc_ref, vp_ref, vc_ref, o_ref, *, lq, prev_from_block):
    m_rows = REP * lq
    ii = lax.broadcasted_iota(jnp.int32, (m_rows, WINDOW), 0) & (lq - 1)
    jj = lax.broadcasted_iota(jnp.int32, (m_rows, WINDOW), 1)
    from_prev = jj > ii
    if prev_from_block:
        dead = jj > ii + jnp.where(pl.program_id(1) > 0, WINDOW, 0)
    rr = lax.broadcasted_iota(jnp.int32, (m_rows, 1), 0)
    dn = (((1,), (1,)), ((), ()))
    for bi in range(q_ref.shape[0]):
        for g in range(N_KV_HEADS):
            qg = q_ref[bi, :, g * REP * HEAD_DIM:(g + 1) * REP * HEAD_DIM].astype(F32)
            qs = jnp.concatenate([qg[:, r * HEAD_DIM:(r + 1) * HEAD_DIM] for r in range(REP)], axis=0).astype(BF16)
            ks = slice(g * HEAD_DIM, (g + 1) * HEAD_DIM)
            s_p = lax.dot_general(qs, kp_ref[bi, :, ks].astype(BF16), dn, preferred_element_type=F32)
            s_c = lax.dot_general(qs, kc_ref[bi, :, ks].astype(BF16), dn, preferred_element_type=F32)
            s = jnp.where(from_prev, s_p, s_c)
            if prev_from_block:
                s = jnp.where(dead, -jnp.inf, s)
            sink = jnp.zeros((m_rows, 1), F32)
            for r in range(REP):
                sink = jnp.where((rr >= r * lq) & (rr < (r + 1) * lq), sink_ref[g * REP + r], sink)
            m = jnp.maximum(jnp.max(s, axis=-1, keepdims=True), sink)
            p = jnp.exp(s - m)
            p = p / (jnp.sum(p, axis=-1, keepdims=True) + jnp.exp(sink - m))
            o = (jnp.dot(jnp.where(from_prev, p, 0.0).astype(BF16), vp_ref[bi, :, ks].astype(BF16),
                         preferred_element_type=F32)
                 + jnp.dot(jnp.where(from_prev, 0.0, p).astype(BF16), vc_ref[bi, :, ks].astype(BF16),
                           preferred_element_type=F32))
            for r in range(REP):
                h = g * REP + r
                o_ref[bi, :, h * HEAD_DIM:(h + 1) * HEAD_DIM] = o[r * lq:(r + 1) * lq]


def _attn_prompt(sinks, q, k, v):
    nb, l, _ = q.shape
    nblk = l // WINDOW
    cur = lambda w: pl.BlockSpec((1, WINDOW, w), lambda b, n: (b, n, 0))
    prev = lambda w: pl.BlockSpec((1, WINDOW, w), lambda b, n: (b, jnp.maximum(n - 1, 0), 0))
    return pl.pallas_call(
        functools.partial(_attn_kernel, lq=WINDOW, prev_from_block=True),
        out_shape=jax.ShapeDtypeStruct((nb, l, D_ATTN), F32),
        grid=(nb, nblk),
        in_specs=[pl.BlockSpec(memory_space=pltpu.SMEM), cur(D_ATTN), prev(D_KV), cur(D_KV), prev(D_KV), cur(D_KV)],
        out_specs=cur(D_ATTN),
        compiler_params=_cparams(("arbitrary", "arbitrary")),
        name="attn_prompt",
    )(sinks, q, k, k, v, v)


def _attn_sample(sinks, q, kcat, vcat):
    nb, lq, _ = q.shape
    nbb = ATTN_SAMPLE_BATCH
    kblk = lambda n: pl.BlockSpec((nbb, WINDOW, D_KV), lambda b: (b, n, 0))
    return pl.pallas_call(
        functools.partial(_attn_kernel, lq=lq, prev_from_block=False),
        out_shape=jax.ShapeDtypeStruct((nb, lq, D_ATTN), F32),
        grid=(nb // nbb,),
        in_specs=[pl.BlockSpec(memory_space=pltpu.SMEM), pl.BlockSpec((nbb, lq, D_ATTN), lambda b: (b, 0, 0)),
                  kblk(0), kblk(1), kblk(0), kblk(1)],
        out_specs=pl.BlockSpec((nbb, lq, D_ATTN), lambda b: (b, 0, 0)),
        compiler_params=_cparams(("arbitrary",)),
        name="attn_sample",
    )(sinks, q, kcat, kcat, vcat, vcat)


N_KT = D_SSM // MXU_DIM
ST_PER_KT = N_STATE // N_KT
N_SLAB = D_SSM // LANES


def _cproj(hr_ref, hi_ref, cre_ref, cimn_ref, kt):
    cs = slice(kt * ST_PER_KT, (kt + 1) * ST_PER_KT)
    return (jnp.dot(hr_ref[:, cs].astype(BF16), cre_ref[kt], preferred_element_type=F32)
            + jnp.dot(hi_ref[:, cs].astype(BF16), cimn_ref[kt], preferred_element_type=F32))


def _s5_prompt_kernel(u_ref, bre_ref, bim_ref, cre_ref, cimn_ref, a1r_ref, a1i_ref, par_ref, pai_ref, d_ref,
                      h0r_ref, h0i_ref, y_ref, cr_ref, ci_ref, il_ref, xr_ref, xi_ref, *, nb, tt):
    rows = nb * tt

    @pl.when(pl.program_id(0) == 0)
    def _():
        cr_ref[...] = h0r_ref[...]
        ci_ref[...] = h0i_ref[...]

    for b in range(nb):
        for j in range(N_SLAB):
            il_ref[j, pl.ds(b, tt, stride=nb), :] = u_ref[b, :, j * LANES:(j + 1) * LANES]
    for kt in range(N_KT):
        ub = jnp.concatenate([il_ref[2 * kt], il_ref[2 * kt + 1]], axis=1).astype(BF16)
        cs = slice(kt * ST_PER_KT, (kt + 1) * ST_PER_KT)
        xr_ref[:, cs] = jnp.dot(ub, bre_ref[kt], preferred_element_type=F32)
        xi_ref[:, cs] = jnp.dot(ub, bim_ref[kt], preferred_element_type=F32)

    half = lax.broadcasted_iota(jnp.int32, (SUBLANES, LANES), 0) < nb

    def step(i, carry):
        r0 = pl.multiple_of(i * SUBLANES, SUBLANES)
        for j in range(N_STATE // LANES):
            ls = slice(j * LANES, (j + 1) * LANES)
            x_r = xr_ref[pl.ds(r0, SUBLANES), ls]
            x_i = xi_ref[pl.ds(r0, SUBLANES), ls]
            s_r = pltpu.roll(x_r, nb, 0)
            s_i = pltpu.roll(x_i, nb, 0)
            a1r, a1i = a1r_ref[:, ls], a1i_ref[:, ls]
            c_r, c_i = cr_ref[:, ls], ci_ref[:, ls]
            par, pai = par_ref[:, ls], pai_ref[:, ls]
            h_r = x_r + (a1r * s_r - a1i * s_i) + (par * c_r - pai * c_i)
            h_i = x_i + (a1r * s_i + a1i * s_r) + (par * c_i + pai * c_r)
            xr_ref[pl.ds(r0, SUBLANES), ls] = h_r
            xi_ref[pl.ds(r0, SUBLANES), ls] = h_i
            cr_ref[:, ls] = jnp.where(half, pltpu.roll(h_r, nb, 0), h_r)
            ci_ref[:, ls] = jnp.where(half, pltpu.roll(h_i, nb, 0), h_i)
        return carry

    lax.fori_loop(0, rows // SUBLANES, step, 0)

    for kt in range(N_KT):
        y = _cproj(xr_ref, xi_ref, cre_ref, cimn_ref, kt)
        il_ref[2 * kt] = y[:, :LANES]
        il_ref[2 * kt + 1] = y[:, LANES:]
    for b in range(nb):
        for j in range(N_SLAB):
            ls = slice(j * LANES, (j + 1) * LANES)
            y_ref[b, :, ls] = il_ref[j, pl.ds(b, tt, stride=nb), :] + d_ref[:, ls] * u_ref[b, :, ls]


def _s5_prompt(u, wts, tabs, d_skip, h0r, h0i):
    nb, l, _ = u.shape
    tt = S5_TT
    rows = nb * tt
    full = lambda a: pl.BlockSpec(a.shape, lambda i: (0,) * a.ndim)
    blk = pl.BlockSpec((nb, tt, D_SSM), lambda i: (0, i, 0))
    carry = jax.ShapeDtypeStruct((SUBLANES, N_STATE), F32)
    return pl.pallas_call(
        functools.partial(_s5_prompt_kernel, nb=nb, tt=tt),
        out_shape=(jax.ShapeDtypeStruct((nb, l, D_SSM), F32), carry, carry),
        grid=(l // tt,),
        in_specs=[blk] + [full(a) for a in (*wts, *tabs, d_skip, h0r, h0i)],
        out_specs=(blk, pl.BlockSpec((SUBLANES, N_STATE), lambda i: (0, 0)),
                   pl.BlockSpec((SUBLANES, N_STATE), lambda i: (0, 0))),
        scratch_shapes=[pltpu.VMEM((N_SLAB, rows, LANES), F32), pltpu.VMEM((rows, N_STATE), F32),
                        pltpu.VMEM((rows, N_STATE), F32)],
        compiler_params=_cparams(("arbitrary",)),
        name="s5_prompt",
    )(u, *wts, *tabs, d_skip, h0r, h0i)


def _s5_sample_kernel(u_ref, bre_ref, bim_ref, cre_ref, cimn_ref, ar_ref, ai_ref, d_ref, h0r_ref, h0i_ref,
                      y_ref, sr_ref, si_ref, xr_ref, xi_ref, *, nt):
    sr_ref[...] = h0r_ref[...]
    si_ref[...] = h0i_ref[...]
    for t in range(nt):
        for kt in range(N_KT):
            ub = u_ref[t, :, kt * MXU_DIM:(kt + 1) * MXU_DIM].astype(BF16)
            cs = slice(kt * ST_PER_KT, (kt + 1) * ST_PER_KT)
            xr_ref[:, cs] = jnp.dot(ub, bre_ref[kt], preferred_element_type=F32)
            xi_ref[:, cs] = jnp.dot(ub, bim_ref[kt], preferred_element_type=F32)
        ar, ai = ar_ref[...], ai_ref[...]
        s_r, s_i = sr_ref[...], si_ref[...]
        sr_ref[...] = xr_ref[...] + (ar * s_r - ai * s_i)
        si_ref[...] = xi_ref[...] + (ar * s_i + ai * s_r)
        for kt in range(N_KT):
            ys = slice(kt * MXU_DIM, (kt + 1) * MXU_DIM)
            y_ref[t, :, ys] = _cproj(sr_ref, si_ref, cre_ref, cimn_ref, kt) + d_ref[:, ys] * u_ref[t, :, ys]


def _s5_sample(u_tb, wts, ar, ai, d_skip, h0r, h0i):
    nt, nb, _ = u_tb.shape
    st = jax.ShapeDtypeStruct((nb, N_STATE), F32)
    args = (u_tb, *wts, ar, ai, d_skip, h0r, h0i)
    full = lambda a: pl.BlockSpec(a.shape, lambda i: (0,) * a.ndim)
    return pl.pallas_call(
        functools.partial(_s5_sample_kernel, nt=nt),
        out_shape=(jax.ShapeDtypeStruct((nt, nb, D_SSM), F32), st, st),
        grid=(1,),
        in_specs=[full(a) for a in args],
        out_specs=(pl.BlockSpec((nt, nb, D_SSM), lambda i: (0, 0, 0)), pl.BlockSpec((nb, N_STATE), lambda i: (0, 0)),
                   pl.BlockSpec((nb, N_STATE), lambda i: (0, 0))),
        scratch_shapes=[pltpu.VMEM((nb, N_STATE), F32), pltpu.VMEM((nb, N_STATE), F32)],
        compiler_params=_cparams(("arbitrary",)),
        name="s5_sample",
    )(*args)


def _gather_rows(src_ref, ids_ref, id_stride, id_offset, dst_ref, sem, priorities):
    def body(g, c):
        for j in range(SUBLANES):
            t = ids_ref[0, id_stride * (g * SUBLANES + j) + id_offset]
            pltpu.make_async_copy(src_ref.at[t >> 3, pl.ds(t & (SUBLANES - 1), 1)], dst_ref.at[g, pl.ds(j, 1)],
                                  sem).start(priority=priorities[j % len(priorities)])
        return c
    lax.fori_loop(0, dst_ref.shape[0], body, 0)


def _index_specs(n_cols, first_tile, n_steps):
    def spec(fn):
        return pl.BlockSpec((None, 1, n_cols), fn, memory_space=pltpu.SMEM)

    return (spec(lambda s, *_: (first_tile, 0, 0)),
            spec(lambda s, *_: (first_tile + jnp.minimum(s + 1, n_steps - 1), 0, 0)))


N_POST_INPUTS = 17


def _post_kernel(*refs, has_tail):
    ins, outs = refs[:N_POST_INPUTS], refs[N_POST_INPUTS + int(has_tail):]
    if not has_tail:
        _post_body(*ins, *outs)
        return
    tail_ref, h2_ref = refs[N_POST_INPUTS], outs[1]
    last = pl.num_programs(0) - 1

    @pl.when(pl.program_id(0) == last)
    def _():
        h2_ref[...] = tail_ref[...]

    @pl.when(pl.program_id(0) < last)
    def _():
        _post_body(*ins, *outs)


def _post_body(attn_ref, yssm_ref, x_ref, g1_ref, sh2_ref, sc2_ref, gluw_ref, glub_ref, ga_ref, gs_ref, wout_ref,
               l1g_ref, l1b_ref, wr_ref, br_ref, tri_ref, cnt0_ref, x1_ref, h2_ref, route_ref, cnt_ref):
    tm = x_ref.shape[0]

    @pl.when(pl.program_id(0) == 0)
    def _():
        cnt_ref[...] = cnt0_ref[...]

    z = jax.nn.gelu(yssm_ref[...])
    ssm = z * jax.nn.sigmoid(jnp.dot(z.astype(BF16), gluw_ref[...], preferred_element_type=F32) + glub_ref[...])
    mixed_a = (_rms(attn_ref[...]) * ga_ref[...]).astype(BF16)
    mixed_s = (_rms(ssm) * gs_ref[...]).astype(BF16)
    o = (jnp.dot(mixed_a, wout_ref[:D_ATTN, :], preferred_element_type=F32)
         + jnp.dot(mixed_s, wout_ref[D_ATTN:, :], preferred_element_type=F32))
    x1 = _ln(DEEPNORM_ALPHA * x_ref[...] + g1_ref[...] * o) * l1g_ref[...] + l1b_ref[...]
    x1_ref[...] = x1
    h2 = _ln(x1) * (1.0 + sc2_ref[...]) + sh2_ref[...]
    h2_ref[...] = h2
    logits = jnp.dot(h2.astype(BF16), wr_ref[...], preferred_element_type=F32) + br_ref[...]

    lane = lax.broadcasted_iota(jnp.int32, (tm, LANES), 1).astype(F32)
    big = float(4 * LANES)
    neg = -jnp.inf
    gl = jnp.where(lane < N_EXPERT_GROUPS, logits, neg)
    gp = jnp.exp(gl - jnp.max(gl, axis=-1, keepdims=True))
    gp = gp / jnp.sum(gp, axis=-1, keepdims=True)
    g_val = jnp.max(gp, axis=-1, keepdims=True)
    g_idx = jnp.min(jnp.where(gp == g_val, lane, big), axis=-1, keepdims=True)
    lo = ROUTE_LANE0 + EXPERTS_PER_GROUP * g_idx
    emask = (lane >= lo) & (lane < lo + EXPERTS_PER_GROUP)
    el = jnp.where(emask, logits, neg)
    ep = jnp.exp(el - jnp.max(el, axis=-1, keepdims=True))
    ep = jnp.where(emask, ep / jnp.sum(ep, axis=-1, keepdims=True), -1.0)
    v1 = jnp.max(ep, axis=-1, keepdims=True)
    i1 = jnp.min(jnp.where(ep == v1, lane, big), axis=-1, keepdims=True)
    ep2 = jnp.where(lane == i1, -1.0, ep)
    v2 = jnp.max(ep2, axis=-1, keepdims=True)
    i2 = jnp.min(jnp.where(ep2 == v2, lane, big), axis=-1, keepdims=True)
    vs = v1 + v2
    w1 = g_val * (v1 / vs)
    w2 = g_val * (v2 / vs)
    hit = jnp.where((lane == i1) | (lane == i2), 1.0, 0.0)
    before = jnp.dot(tri_ref[...], hit.astype(BF16), preferred_element_type=F32) + cnt_ref[0:1, :]
    r1 = jnp.sum(jnp.where(lane == i1, before, 0.0), axis=-1, keepdims=True)
    r2 = jnp.sum(jnp.where(lane == i2, before, 0.0), axis=-1, keepdims=True)
    cnt_ref[...] = cnt_ref[...] + jnp.sum(hit, axis=0, keepdims=True)
    e1 = i1 - ROUTE_LANE0
    e2 = i2 - ROUTE_LANE0
    route = jnp.zeros((tm, LANES), F32)
    for n, val in enumerate((e1, e2, w1, w2, r1, r2)):
        route = jnp.where(lane == n, val, route)
    route_ref[...] = route


def _post(attn, yssm, x, mod, per_row, w, cnt0, tail=None):
    nb, l, _ = x.shape
    tm = min(TM, l)
    nt = l // tm
    n_body = nb * nt
    has_tail = tail is not None
    if has_tail:
        assert tail.shape == (tm, D_MODEL)

    def bi(s):
        s = jnp.minimum(s, n_body - 1)
        return s // nt, s % nt

    row = lambda wd: pl.BlockSpec((None, tm, wd), lambda s: (*bi(s), 0))
    if per_row:
        mspec = lambda chunk: pl.BlockSpec((None, tm, D_MODEL), lambda s: (*bi(s), chunk))
    else:
        mspec = lambda chunk: pl.BlockSpec((None, 1, D_MODEL), lambda s: (bi(s)[0], 0, chunk))
    full = lambda a: pl.BlockSpec(a.shape, lambda s: (0,) * a.ndim)
    tri = jnp.tril(jnp.ones((tm, tm), F32), -1).astype(BF16)
    consts = (w["glu_w"], w["glu_b"], w["attn_g"], w["ssm_g"], w["w_out"], w["ln1_g"], w["ln1_b"], w["wr"], w["br"],
              tri, cnt0) + ((tail,) if has_tail else ())
    assert 6 + len(consts) == N_POST_INPUTS + int(has_tail)
    n_steps = n_body + int(has_tail)
    return pl.pallas_call(
        functools.partial(_post_kernel, has_tail=has_tail),
        out_shape=(jax.ShapeDtypeStruct((nb, l, D_MODEL), F32),
                   jax.ShapeDtypeStruct((n_steps * tm, D_MODEL), F32),
                   jax.ShapeDtypeStruct((nb, l, LANES), F32), jax.ShapeDtypeStruct((SUBLANES, LANES), F32)),
        grid=(n_steps,),
        in_specs=[row(D_ATTN), row(D_SSM), row(D_MODEL), mspec(2), mspec(3), mspec(4)] + [full(a) for a in consts],
        out_specs=(row(D_MODEL), pl.BlockSpec((tm, D_MODEL), lambda s: (s, 0)),
                   row(LANES), pl.BlockSpec((SUBLANES, LANES), lambda s: (0, 0))),
        compiler_params=_cparams(("arbitrary",)),
        name="post",
    )(attn, yssm, x, mod, mod, mod, *consts)


def _moe_kernel(te_ref, nv_ref, first_ref, nxt_ref, par_ref, rt0_ref, rtn_ref, h2_ref, wg_hbm, wu_hbm, wd_hbm, y_ref,
                buf_ref, wgf_ref, wuf_ref, wdf_ref, wgb_ref, wub_ref, wdb_ref, sem, wsem):
    i = pl.program_id(0)
    nv = nv_ref[0]
    slot = i & 1
    w_pairs = ((wg_hbm, wgf_ref), (wu_hbm, wuf_ref), (wd_hbm, wdf_ref))

    def gather(ids_ref, dst_slot):
        _gather_rows(h2_ref, ids_ref, 1, 0, buf_ref.at[dst_slot], sem.at[dst_slot], (0,))

    def fetch_weights(e, ws):
        for src, dst in w_pairs:
            pltpu.make_async_copy(src.at[e], dst.at[ws], wsem.at[ws]).start(priority=1)

    @pl.when(i == 0)
    def _():
        gather(rt0_ref, 0)
        fetch_weights(te_ref[0], par_ref[0])

    @pl.when(i <= nv)
    def _():
        pltpu.make_async_copy(h2_ref.at[pl.ds(0, TM_MOE // SUBLANES)], buf_ref.at[slot], sem.at[slot]).wait()

    @pl.when((i < nv) & (first_ref[i] == 1))
    def _():
        ws = par_ref[i]
        for src, dst in w_pairs:
            pltpu.make_async_copy(src.at[0], dst.at[ws], wsem.at[ws]).wait()
        wgb_ref[...] = wgf_ref[ws].astype(BF16)
        wub_ref[...] = wuf_ref[ws].astype(BF16)
        wdb_ref[...] = wdf_ref[ws].astype(BF16)

        @pl.when(nxt_ref[i] >= 0)
        def _():
            fetch_weights(nxt_ref[i], 1 - ws)

    @pl.when(i < nv)
    def _():
        gather(rtn_ref, 1 - slot)
        xb = buf_ref[slot].reshape(TM_MOE, D_MODEL).astype(BF16)
        hg = jnp.dot(xb, wgb_ref[...], preferred_element_type=F32)
        hu = jnp.dot(xb, wub_ref[...], preferred_element_type=F32)
        act = (hg * jax.nn.sigmoid(hg)) * hu
        y_ref[...] = jnp.dot(act.astype(BF16), wdb_ref[...], preferred_element_type=F32)

    @pl.when(i >= nv)
    def _():
        y_ref[...] = jnp.zeros(y_ref.shape, y_ref.dtype)


def _moe(tile_meta, n_valid, row_token, h2, w_gate, w_up, w_down):
    n_steps = row_token.shape[0] // TM_MOE
    rt = row_token.reshape(n_steps, 1, TM_MOE)
    te, first, nxt, par = tile_meta
    any_spec = pl.BlockSpec(memory_space=pl.ANY)
    return pl.pallas_call(
        _moe_kernel,
        out_shape=jax.ShapeDtypeStruct((n_steps * TM_MOE, D_MODEL), F32),
        grid_spec=pltpu.PrefetchScalarGridSpec(
            num_scalar_prefetch=5, grid=(n_steps,),
            in_specs=[*_index_specs(TM_MOE, 0, n_steps), any_spec, any_spec, any_spec, any_spec],
            out_specs=pl.BlockSpec((TM_MOE, D_MODEL), lambda i, *_: (i, 0)),
            scratch_shapes=[pltpu.VMEM((2, TM_MOE // SUBLANES, SUBLANES, D_MODEL), F32),
                            pltpu.VMEM((2, D_MODEL, D_EXPERT), F32), pltpu.VMEM((2, D_MODEL, D_EXPERT), F32),
                            pltpu.VMEM((2, D_EXPERT, D_MODEL), F32),
                            pltpu.VMEM((D_MODEL, D_EXPERT), BF16), pltpu.VMEM((D_MODEL, D_EXPERT), BF16),
                            pltpu.VMEM((D_EXPERT, D_MODEL), BF16),
                            pltpu.SemaphoreType.DMA((2,)), pltpu.SemaphoreType.DMA((2,))]),
        compiler_params=_cparams(("arbitrary",)),
        name="moe",
    )(te, n_valid, first, nxt, par, rt, rt, h2.reshape(-1, SUBLANES, D_MODEL), w_gate, w_up, w_down)


def _final_kernel(pos0_ref, posn_ref, x1_ref, g2_ref, route_ref, l2g_ref, l2b_ref, ys_ref, o_ref, buf_ref, sem, *, tm):
    step = pl.program_id(0)
    slot = step & 1

    def gather(ids_ref, dst_slot):
        for k in range(2):
            _gather_rows(ys_ref, ids_ref, 2, k, buf_ref.at[dst_slot, k], sem.at[dst_slot], (0, 1))

    @pl.when(step == 0)
    def _():
        gather(pos0_ref, 0)

    for k in range(2):
        pltpu.make_async_copy(ys_ref.at[pl.ds(0, tm // SUBLANES)], buf_ref.at[slot, k], sem.at[slot]).wait()

    @pl.when(step + 1 < pl.num_programs(0))
    def _():
        gather(posn_ref, 1 - slot)

    route = route_ref[...]
    f = (route[:, 2:3] * buf_ref[slot, 0].reshape(tm, D_MODEL)
         + route[:, 3:4] * buf_ref[slot, 1].reshape(tm, D_MODEL))
    o_ref[...] = _ln(DEEPNORM_ALPHA * x1_ref[...] + g2_ref[...] * f) * l2g_ref[...] + l2b_ref[...]


def _final(pos_tiles, first_tile, x1, mod, per_row, route, ln2_g, ln2_b, ys):
    nb, l, _ = x1.shape
    tm = min(TM, l)
    nt = l // tm
    n_steps = nb * nt
    row = lambda wd: pl.BlockSpec((None, tm, wd), lambda s: (s // nt, s % nt, 0))
    if per_row:
        g2 = pl.BlockSpec((None, tm, D_MODEL), lambda s: (s // nt, s % nt, 5))
    else:
        g2 = pl.BlockSpec((None, 1, D_MODEL), lambda s: (s // nt, 0, 5))
    vec = pl.BlockSpec((1, D_MODEL), lambda s: (0, 0))
    return pl.pallas_call(
        functools.partial(_final_kernel, tm=tm),
        out_shape=jax.ShapeDtypeStruct((nb, l, D_MODEL), F32),
        grid=(n_steps,),
        in_specs=[*_index_specs(2 * tm, first_tile, n_steps), row(D_MODEL), g2, row(LANES), vec, vec,
                  pl.BlockSpec(memory_space=pl.ANY)],
        out_specs=row(D_MODEL),
        scratch_shapes=[pltpu.VMEM((2, 2, tm // SUBLANES, SUBLANES, D_MODEL), F32), pltpu.SemaphoreType.DMA((2,))],
        compiler_params=_cparams(("arbitrary",)),
        name="final",
    )(pos_tiles, pos_tiles, x1, mod, route, ln2_g, ln2_b, ys.reshape(-1, SUBLANES, D_MODEL))


def _rope_tables(pos):
    half = ROT_DIM // 2
    inv_freq = ROPE_THETA ** (-jnp.arange(half, dtype=jnp.float32) * 2.0 / ROT_DIM)
    ang = pos.astype(jnp.float32)[:, None] * inv_freq[None, :]
    cos, sin = jnp.cos(ang), jnp.sin(ang)
    n = pos.shape[0]
    one = jnp.ones((n, HEAD_DIM - ROT_DIM), F32)
    zero = jnp.zeros((n, HEAD_DIM - half), F32)
    c = jnp.concatenate([cos, cos, one], -1)
    a = jnp.concatenate([-sin, zero], -1)
    b = jnp.concatenate([jnp.zeros((n, half), F32), sin, jnp.zeros((n, HEAD_DIM - ROT_DIM), F32)], -1)
    return tuple(jnp.tile(t, (1, LANES // HEAD_DIM)) for t in (c, a, b))


def _s5_params(lam_re, lam_im, log_step, b_re, b_im, c_re, c_im):
    f32 = jnp.float32
    dt = jnp.exp(log_step.astype(f32))[:, None]
    lr, li = lam_re.astype(f32), lam_im.astype(f32)
    mag = jnp.exp(lr * dt)
    ar, ai = mag * jnp.cos(li * dt), mag * jnp.sin(li * dt)
    den = lr * lr + li * li
    cr = ((ar - 1.0) * lr + ai * li) / den
    ci = (ai * lr - (ar - 1.0) * li) / den
    br, bi = b_re.astype(f32), b_im.astype(f32)
    bbr = cr[..., None] * br - ci[..., None] * bi
    bbi = cr[..., None] * bi + ci[..., None] * br
    gpt = MXU_DIM // SSM_CH
    eye = jnp.eye(gpt, dtype=f32)

    def pack_b(m):
        m = m.reshape(N_KT, gpt, SSM_STATE, SSM_CH)
        return jnp.einsum("kgpc,gh->kgchp", m, eye).reshape(N_KT, MXU_DIM, ST_PER_KT).astype(BF16)

    def pack_c(m):
        m = m.astype(f32).reshape(N_KT, gpt, SSM_CH, SSM_STATE)
        return jnp.einsum("kgcp,gh->khpgc", m, eye).reshape(N_KT, ST_PER_KT, MXU_DIM).astype(BF16)

    wts = (pack_b(bbr), pack_b(bbi), pack_c(c_re), pack_c(-c_im.astype(f32)))
    return wts, ar.reshape(1, N_STATE), ai.reshape(1, N_STATE)


def _scan_tables(ar, ai, nb):
    assert SUBLANES // nb == 2
    a2r, a2i = ar * ar - ai * ai, 2.0 * ar * ai
    z = jnp.zeros_like(ar)
    rep = lambda first, second: jnp.concatenate([jnp.tile(first, (nb, 1)), jnp.tile(second, (nb, 1))], 0)
    return rep(z, ar), rep(z, ai), rep(ar, a2r), rep(ai, a2i)


def kernel(x_prompt, x_sample, cache_k_win, cache_v_win, state_ssm_re, state_ssm_im, c_prompt, c_sample, ada_w, ada_b,
           w_in, attn_sinks, ssm_lambda_re, ssm_lambda_im, ssm_log_step, ssm_b_re, ssm_b_im, ssm_c_re, ssm_c_im, ssm_d,
           ssm_glu_w, ssm_glu_b, attn_norm_g, ssm_norm_g, w_out, ln1_g, ln1_b, router_group_w, router_group_b,
           router_expert_w, router_expert_b, exp_w_gate, exp_w_up, exp_w_down, ln2_g, ln2_b):
    assert ada_w.shape[0] == DEPTH
    bp, lp, _ = x_prompt.shape
    bs, ls, _ = x_sample.shape
    lsp = SAMPLE_PAD

    w_in_bf = w_in[0].astype(BF16)
    wr = jnp.concatenate([router_group_w[0], router_expert_w[0]], -1)
    wr = jnp.pad(wr, ((0, 0), (0, LANES - wr.shape[1]))).astype(BF16)
    br = jnp.pad(jnp.concatenate([router_group_b[0], router_expert_b[0]], -1), (0, LANES - N_EXPERT_GROUPS - N_EXPERTS))
    wpost = dict(glu_w=ssm_glu_w[0].astype(BF16), glu_b=ssm_glu_b[0][None], attn_g=attn_norm_g[0][None],
                 ssm_g=ssm_norm_g[0][None], w_out=w_out[0].astype(BF16), ln1_g=ln1_g[0][None], ln1_b=ln1_b[0][None],
                 wr=wr, br=br[None])
    s5w, ar, ai = _s5_params(ssm_lambda_re[0], ssm_lambda_im[0], ssm_log_step[0], ssm_b_re[0], ssm_b_im[0],
                             ssm_c_re[0], ssm_c_im[0])
    d_skip = ssm_d[0][None]
    sinks = attn_sinks[0]

    n_c = bp + bs
    n_cp = -(-n_c // SUBLANES) * SUBLANES
    c_all = jnp.pad(jnp.concatenate([c_prompt, c_sample], 0), ((0, n_cp - n_c), (0, 0)))
    mod = _ada(c_all, ada_w[0], ada_b[0][None])
    mod_p = mod[:bp][:, None, :]
    mod_s = jnp.repeat(mod[bp:n_c], lsp, axis=0)[None]

    xs_pad = jnp.pad(x_sample, ((0, 0), (0, lsp - ls), (0, 0))).reshape(1, bs * lsp, D_MODEL)

    rope_p = _rope_tables(jnp.arange(lp))
    pos_s = PAST_LEN + jnp.minimum(jnp.arange(lsp), ls - 1)
    rope_s = tuple(jnp.tile(t, (bs, 1)) for t in _rope_tables(pos_s))
    q_p, k_p, v_p, u_p = _inproj(x_prompt, mod_p, w_in_bf, rope_p, False)
    q_s, k_s, v_s, u_s = _inproj(xs_pad, mod_s, w_in_bf, rope_s, True)

    attn_p = _attn_prompt(sinks, q_p, k_p, v_p)
    k_s3 = k_s.reshape(bs, lsp, D_KV)
    v_s3 = v_s.reshape(bs, lsp, D_KV)
    ck = cache_k_win[0].reshape(bs, WINDOW, D_KV)
    cv = cache_v_win[0].reshape(bs, WINDOW, D_KV)
    padk = lambda new: jnp.pad(new, ((0, 0), (0, WINDOW - lsp), (0, 0)))
    attn_s = _attn_sample(sinks, q_s.reshape(bs, lsp, D_ATTN), jnp.concatenate([ck, padk(k_s3)], 1),
                          jnp.concatenate([cv, padk(v_s3)], 1))

    tabs = _scan_tables(ar, ai, bp)
    zero_carry = jnp.zeros((SUBLANES, N_STATE), F32)
    y_p, hr_p, hi_p = _s5_prompt(u_p, s5w, tabs, d_skip, zero_carry, zero_carry)
    u_tb = jnp.transpose(u_s.reshape(bs, lsp, D_SSM)[:, :ls], (1, 0, 2))
    y_tb, hr_s, hi_s = _s5_sample(u_tb, s5w, ar, ai, d_skip, state_ssm_re[0].reshape(bs, N_STATE),
                                  state_ssm_im[0].reshape(bs, N_STATE))
    y_s = jnp.pad(jnp.transpose(y_tb, (1, 0, 2)), ((0, 0), (0, lsp - ls), (0, 0))).reshape(1, bs * lsp, D_SSM)

    cnt0 = jnp.zeros((SUBLANES, LANES), F32)
    n_p, n_s = bp * lp, bs * lsp
    n_tok = n_p + n_s
    x1_s, h2_s, route_s, cnt_s = _post(attn_s.reshape(1, bs * lsp, D_ATTN), y_s, xs_pad, mod_s, True, wpost, cnt0)
    x1_p, h2, route_p, cnt = _post(attn_p, y_p, x_prompt, mod_p, False, wpost, cnt_s, tail=h2_s)

    n_tiles = -(-(2 * n_tok + N_EXPERTS * (TM_MOE - 1)) // TM_MOE) + 1
    route = jnp.concatenate([route_p.reshape(n_p, LANES), route_s.reshape(n_s, LANES)], 0)
    counts = cnt[0, ROUTE_LANE0:ROUTE_LANE0 + N_EXPERTS].astype(jnp.int32)
    padded = ((counts + TM_MOE - 1) // TM_MOE) * TM_MOE
    ends = jnp.cumsum(padded)
    offs = ends - padded
    eid = route[:, 0:2].astype(jnp.int32)
    pos = (offs[eid] + route[:, 4:6].astype(jnp.int32)).reshape(-1)
    n_valid = ends[-1] // TM_MOE
    tile_row = jnp.minimum(jnp.arange(n_tiles), n_valid - 1) * TM_MOE
    tile_expert = jnp.sum((ends[None, :] <= tile_row[:, None]).astype(jnp.int32), axis=1)
    first_tile = jnp.concatenate([jnp.ones((1,), jnp.int32), (tile_expert[1:] != tile_expert[:-1]).astype(jnp.int32)])
    first_tile = first_tile * (jnp.arange(n_tiles) < n_valid)
    nonempty = padded > 0
    expert_ids = jnp.arange(N_EXPERTS, dtype=jnp.int32)
    later = jnp.where(nonempty[None, :] & (expert_ids[None, :] > expert_ids[:, None]), expert_ids[None, :], N_EXPERTS)
    next_expert = jnp.min(later, axis=1)
    next_expert = jnp.where(next_expert < N_EXPERTS, next_expert, -1).astype(jnp.int32)
    weight_slot = ((jnp.cumsum(nonempty.astype(jnp.int32)) - 1) & 1).astype(jnp.int32)
    tile_meta = (tile_expert, first_tile.astype(jnp.int32), next_expert[tile_expert], weight_slot[tile_expert])
    token_of = jnp.repeat(jnp.arange(n_tok, dtype=jnp.int32), 2)
    row_token = jnp.zeros((n_tiles * TM_MOE,), jnp.int32).at[pos].set(token_of)

    ys = _moe(tile_meta, n_valid.reshape(1).astype(jnp.int32), row_token, h2, exp_w_gate[0], exp_w_up[0],
              exp_w_down[0])
    pos_tiles = pos.reshape(n_tok // TM, 1, 2 * TM)
    y_prompt = _final(pos_tiles, 0, x1_p, mod_p, False, route_p, ln2_g[0][None], ln2_b[0][None], ys)
    y_samp = _final(pos_tiles, n_p // TM, x1_s, mod_s, True, route_s, ln2_g[0][None], ln2_b[0][None], ys)
    y_sample = y_samp.reshape(bs, lsp, D_MODEL)[:, :ls]

    kv5 = lambda a, n: a.reshape(1, a.shape[0], n, N_KV_HEADS, HEAD_DIM)
    k_win_p = kv5(k_p[:, lp - WINDOW:], WINDOW)
    v_win_p = kv5(v_p[:, lp - WINDOW:], WINDOW)
    k_win_s = kv5(jnp.concatenate([ck[:, ls:], k_s3[:, :ls]], 1), WINDOW)
    v_win_s = kv5(jnp.concatenate([cv[:, ls:], v_s3[:, :ls]], 1), WINDOW)
    st = lambda a: a.reshape(1, a.shape[0], N_SSM_GROUPS, SSM_STATE)
    return (y_prompt, y_sample, k_win_p, v_win_p, st(hr_p[bp:2 * bp]), st(hi_p[bp:2 * bp]),
            k_win_s, v_win_s, st(hr_s), st(hi_s))
```

```python
import functools
import math

import jax
import jax.numpy as jnp
from jax import lax
from jax.experimental import pallas as pl
from jax.experimental.pallas import tpu as pltpu

F32 = jnp.float32
BF16 = jnp.bfloat16

D_MODEL = 2048
D_ATTN = 1024
D_SSM = 1024
HEAD_DIM = 64
N_HEADS = 16
N_KV_HEADS = 4
REP = 4
D_KV = 256
ROT_DIM = 16
ROPE_THETA = 500000.0
WINDOW = 128
SSM_CH = 16
N_SSM_GROUPS = 64
SSM_STATE = 64
N_STATE = N_SSM_GROUPS * SSM_STATE
PROJ_COLS = D_ATTN + 2 * D_KV + D_SSM
N_EXPERT_GROUPS = 4
EXPERTS_PER_GROUP = 8
N_EXPERTS = 32
D_EXPERT = 512
DEPTH = 1
DEEPNORM_ALPHA = (2.0 * DEPTH) ** 0.25
LN_EPS = 1e-5
PAST_LEN = 16384

LANES = 128
SUBLANES = 8
MXU_DIM = 256
TM = 256
TM_MOE = 256
S5_TT = 64
SAMPLE_PAD = 8
ROUTE_LANE0 = N_EXPERT_GROUPS
VMEM_LIMIT = 56 * 1024 * 1024


def _cparams(sem):
    return pltpu.CompilerParams(dimension_semantics=sem, vmem_limit_bytes=VMEM_LIMIT)


def _ln(x):
    mu = jnp.mean(x, axis=-1, keepdims=True)
    xc = x - mu
    var = jnp.mean(xc * xc, axis=-1, keepdims=True)
    return xc * lax.rsqrt(var + LN_EPS)


def _rms(x):
    return x * lax.rsqrt(jnp.mean(x * x, axis=-1, keepdims=True) + LN_EPS)


def _ada_kernel(c_ref, w_ref, b_ref, o_ref):
    c = c_ref[...]
    s = c * jax.nn.sigmoid(c)
    o_ref[...] = jnp.dot(s.astype(BF16), w_ref[...].astype(BF16), preferred_element_type=F32) + b_ref[...]


def _ada(c_all, ada_w, ada_b):
    n, tn = c_all.shape[0], 1024
    return pl.pallas_call(
        _ada_kernel,
        out_shape=jax.ShapeDtypeStruct((n, 6 * D_MODEL), F32),
        grid=(6 * D_MODEL // tn,),
        in_specs=[pl.BlockSpec((n, D_MODEL), lambda j: (0, 0)),
                  pl.BlockSpec((D_MODEL, tn), lambda j: (0, j)),
                  pl.BlockSpec((1, tn), lambda j: (0, j))],
        out_specs=pl.BlockSpec((n, tn), lambda j: (0, j)),
        compiler_params=_cparams(("arbitrary",)),
        name="ada",
    )(c_all, ada_w, ada_b)


def _mod_spec(per_row, tm, chunk):
    if per_row:
        return pl.BlockSpec((None, tm, D_MODEL), lambda b, i: (b, i, chunk))
    return pl.BlockSpec((None, 1, D_MODEL), lambda b, i: (b, 0, chunk))


def _inproj_kernel(x_ref, sh_ref, sc_ref, w_ref, rc_ref, ra_ref, rb_ref, q_ref, k_ref, v_ref, u_ref):
    h = _ln(x_ref[...]) * (1.0 + sc_ref[...]) + sh_ref[...]
    proj = jnp.dot(h.astype(BF16), w_ref[...], preferred_element_type=F32)
    rc, ra, rb = rc_ref[...], ra_ref[...], rb_ref[...]

    def rope(t):
        return t * rc + pltpu.roll(t, LANES - ROT_DIM // 2, 1) * ra + pltpu.roll(t, ROT_DIM // 2, 1) * rb

    for j in range(D_ATTN // LANES):
        q_ref[:, j * LANES:(j + 1) * LANES] = (rope(proj[:, j * LANES:(j + 1) * LANES]) * HEAD_DIM ** -0.5).astype(BF16)
    for j in range(D_KV // LANES):
        c0 = D_ATTN + j * LANES
        k_ref[:, j * LANES:(j + 1) * LANES] = rope(proj[:, c0:c0 + LANES])
    v_ref[...] = proj[:, D_ATTN + D_KV:D_ATTN + 2 * D_KV]
    u_ref[...] = proj[:, D_ATTN + 2 * D_KV:]


def _inproj(x, mod, w_in_bf, rope_tabs, per_row):
    nb, l, _ = x.shape
    tm = min(TM, l)
    row = lambda w: pl.BlockSpec((None, tm, w), lambda b, i: (b, i, 0))
    tab = pl.BlockSpec((tm, LANES), lambda b, i: (i, 0))
    return pl.pallas_call(
        _inproj_kernel,
        out_shape=(jax.ShapeDtypeStruct((nb, l, D_ATTN), BF16), jax.ShapeDtypeStruct((nb, l, D_KV), F32),
                   jax.ShapeDtypeStruct((nb, l, D_KV), F32), jax.ShapeDtypeStruct((nb, l, D_SSM), F32)),
        grid=(nb, l // tm),
        in_specs=[row(D_MODEL), _mod_spec(per_row, tm, 0), _mod_spec(per_row, tm, 1),
                  pl.BlockSpec((D_MODEL, PROJ_COLS), lambda b, i: (0, 0)), tab, tab, tab],
        out_specs=(row(D_ATTN), row(D_KV), row(D_KV), row(D_SSM)),
        compiler_params=_cparams(("arbitrary", "arbitrary")),
        name="inproj",
    )(x, mod, mod, w_in_bf, *rope_tabs)


ATTN_SAMPLE_BATCH = 8


def _attn_kernel(sink_ref, q_ref, kp_ref, kc_ref, vp_ref, vc_ref, o_ref, *, lq, prev_from_block):
    m_rows = REP * lq
    ii = lax.broadcasted_iota(jnp.int32, (m_rows, WINDOW), 0) & (lq - 1)
    jj = lax.broadcasted_iota(jnp.int32, (m_rows, WINDOW), 1)
    from_prev = jj > ii
    if prev_from_block:
        dead = jj > ii + jnp.where(pl.program_id(1) > 0, WINDOW, 0)
    rr = lax.broadcasted_iota(jnp.int32, (m_rows, 1), 0)
    dn = (((1,), (1,)), ((), ()))
    for bi in range(q_ref.shape[0]):
        for g in range(N_KV_HEADS):
            qg = q_ref[bi, :, g * REP * HEAD_DIM:(g + 1) * REP * HEAD_DIM].astype(F32)
            qs = jnp.concatenate([qg[:, r * HEAD_DIM:(r + 1) * HEAD_DIM] for r in range(REP)], axis=0).astype(BF16)
            ks = slice(g * HEAD_DIM, (g + 1) * HEAD_DIM)
            s_p = lax.dot_general(qs, kp_ref[bi, :, ks].astype(BF16), dn, preferred_element_type=F32)
            s_c = lax.dot_general(qs, kc_ref[bi, :, ks].astype(BF16), dn, preferred_element_type=F32)
            s = jnp.where(from_prev, s_p, s_c)
            if prev_from_block:
                s = jnp.where(dead, -jnp.inf, s)
            sink = jnp.zeros((m_rows, 1), F32)
            for r in range(REP):
                sink = jnp.where((rr >= r * lq) & (rr < (r + 1) * lq), sink_ref[g * REP + r], sink)
            m = jnp.maximum(jnp.max(s, axis=-1, keepdims=True), sink)
            p = jnp.exp(s - m)
            p = p / (jnp.sum(p, axis=-1, keepdims=True) + jnp.exp(sink - m))
            o = (jnp.dot(jnp.where(from_prev, p, 0.0).astype(BF16), vp_ref[bi, :, ks].astype(BF16),
                         preferred_element_type=F32)
                 + jnp.dot(jnp.where(from_prev, 0.0, p).astype(BF16), vc_ref[bi, :, ks].astype(BF16),
                           preferred_element_type=F32))
            for r in range(REP):
                h = g * REP + r
                o_ref[bi, :, h * HEAD_DIM:(h + 1) * HEAD_DIM] = o[r * lq:(r + 1) * lq]


def _attn_prompt(sinks, q, k, v):
    nb, l, _ = q.shape
    nblk = l // WINDOW
    cur = lambda w: pl.BlockSpec((1, WINDOW, w), lambda b, n: (b, n, 0))
    prev = lambda w: pl.BlockSpec((1, WINDOW, w), lambda b, n: (b, jnp.maximum(n - 1, 0), 0))
    return pl.pallas_call(
        functools.partial(_attn_kernel, lq=WINDOW, prev_from_block=True),
        out_shape=jax.ShapeDtypeStruct((nb, l, D_ATTN), F32),
        grid=(nb, nblk),
        in_specs=[pl.BlockSpec(memory_space=pltpu.SMEM), cur(D_ATTN), prev(D_KV), cur(D_KV), prev(D_KV), cur(D_KV)],
        out_specs=cur(D_ATTN),
        compiler_params=_cparams(("arbitrary", "arbitrary")),
        name="attn_prompt",
    )(sinks, q, k, k, v, v)


def _attn_sample(sinks, q, kcat, vcat):
    nb, lq, _ = q.shape
    nbb = ATTN_SAMPLE_BATCH
    kblk = lambda n: pl.BlockSpec((nbb, WINDOW, D_KV), lambda b: (b, n, 0))
    return pl.pallas_call(
        functools.partial(_attn_kernel, lq=lq, prev_from_block=False),
        out_shape=jax.ShapeDtypeStruct((nb, lq, D_ATTN), F32),
        grid=(nb // nbb,),
        in_specs=[pl.BlockSpec(memory_space=pltpu.SMEM), pl.BlockSpec((nbb, lq, D_ATTN), lambda b: (b, 0, 0)),
                  kblk(0), kblk(1), kblk(0), kblk(1)],
        out_specs=pl.BlockSpec((nbb, lq, D_ATTN), lambda b: (b, 0, 0)),
        compiler_params=_cparams(("arbitrary",)),
        name="attn_sample",
    )(sinks, q, kcat, kcat, vcat, vcat)


N_KT = D_SSM // MXU_DIM
ST_PER_KT = N_STATE // N_KT
N_SLAB = D_SSM // LANES


def _cproj(hr_ref, hi_ref, cre_ref, cimn_ref, kt):
    cs = slice(kt * ST_PER_KT, (kt + 1) * ST_PER_KT)
    return (jnp.dot(hr_ref[:, cs].astype(BF16), cre_ref[kt], preferred_element_type=F32)
            + jnp.dot(hi_ref[:, cs].astype(BF16), cimn_ref[kt], preferred_element_type=F32))


def _s5_prompt_kernel(u_ref, bre_ref, bim_ref, cre_ref, cimn_ref, a1r_ref, a1i_ref, par_ref, pai_ref, d_ref,
                      h0r_ref, h0i_ref, y_ref, cr_ref, ci_ref, il_ref, xr_ref, xi_ref, *, nb, tt):
    rows = nb * tt

    @pl.when(pl.program_id(0) == 0)
    def _():
        cr_ref[...] = h0r_ref[...]
        ci_ref[...] = h0i_ref[...]

    for b in range(nb):
        for j in range(N_SLAB):
            il_ref[j, pl.ds(b, tt, stride=nb), :] = u_ref[b, :, j * LANES:(j + 1) * LANES]
    for kt in range(N_KT):
        ub = jnp.concatenate([il_ref[2 * kt], il_ref[2 * kt + 1]], axis=1).astype(BF16)
        cs = slice(kt * ST_PER_KT, (kt + 1) * ST_PER_KT)
        xr_ref[:, cs] = jnp.dot(ub, bre_ref[kt], preferred_element_type=F32)
        xi_ref[:, cs] = jnp.dot(ub, bim_ref[kt], preferred_element_type=F32)

    half = lax.broadcasted_iota(jnp.int32, (SUBLANES, LANES), 0) < nb

    def step(i, carry):
        r0 = pl.multiple_of(i * SUBLANES, SUBLANES)
        for j in range(N_STATE // LANES):
            ls = slice(j * LANES, (j + 1) * LANES)
            x_r = xr_ref[pl.ds(r0, SUBLANES), ls]
            x_i = xi_ref[pl.ds(r0, SUBLANES), ls]
            s_r = pltpu.roll(x_r, nb, 0)
            s_i = pltpu.roll(x_i, nb, 0)
            a1r, a1i = a1r_ref[:, ls], a1i_ref[:, ls]
            c_r, c_i = cr_ref[:, ls], ci_ref[:, ls]
            par, pai = par_ref[:, ls], pai_ref[:, ls]
            h_r = x_r + (a1r * s_r - a1i * s_i) + (par * c_r - pai * c_i)
            h_i = x_i + (a1r * s_i + a1i * s_r) + (par * c_i + pai * c_r)
            xr_ref[pl.ds(r0, SUBLANES), ls] = h_r
            xi_ref[pl.ds(r0, SUBLANES), ls] = h_i
            cr_ref[:, ls] = jnp.where(half, pltpu.roll(h_r, nb, 0), h_r)
            ci_ref[:, ls] = jnp.where(half, pltpu.roll(h_i, nb, 0), h_i)
        return carry

    lax.fori_loop(0, rows // SUBLANES, step, 0)

    for kt in range(N_KT):
        y = _cproj(xr_ref, xi_ref, cre_ref, cimn_ref, kt)
        il_ref[2 * kt] = y[:, :LANES]
        il_ref[2 * kt + 1] = y[:, LANES:]
    for b in range(nb):
        for j in range(N_SLAB):
            ls = slice(j * LANES, (j + 1) * LANES)
            y_ref[b, :, ls] = il_ref[j, pl.ds(b, tt, stride=nb), :] + d_ref[:, ls] * u_ref[b, :, ls]


def _s5_prompt(u, wts, tabs, d_skip, h0r, h0i):
    nb, l, _ = u.shape
    tt = S5_TT
    rows = nb * tt
    full = lambda a: pl.BlockSpec(a.shape, lambda i: (0,) * a.ndim)
    blk = pl.BlockSpec((nb, tt, D_SSM), lambda i: (0, i, 0))
    carry = jax.ShapeDtypeStruct((SUBLANES, N_STATE), F32)
    return pl.pallas_call(
        functools.partial(_s5_prompt_kernel, nb=nb, tt=tt),
        out_shape=(jax.ShapeDtypeStruct((nb, l, D_SSM), F32), carry, carry),
        grid=(l // tt,),
        in_specs=[blk] + [full(a) for a in (*wts, *tabs, d_skip, h0r, h0i)],
        out_specs=(blk, pl.BlockSpec((SUBLANES, N_STATE), lambda i: (0, 0)),
                   pl.BlockSpec((SUBLANES, N_STATE), lambda i: (0, 0))),
        scratch_shapes=[pltpu.VMEM((N_SLAB, rows, LANES), F32), pltpu.VMEM((rows, N_STATE), F32),
                        pltpu.VMEM((rows, N_STATE), F32)],
        compiler_params=_cparams(("arbitrary",)),
        name="s5_prompt",
    )(u, *wts, *tabs, d_skip, h0r, h0i)


def _s5_sample_kernel(u_ref, bre_ref, bim_ref, cre_ref, cimn_ref, ar_ref, ai_ref, d_ref, h0r_ref, h0i_ref,
                      y_ref, sr_ref, si_ref, xr_ref, xi_ref, *, nt):
    sr_ref[...] = h0r_ref[...]
    si_ref[...] = h0i_ref[...]
    for t in range(nt):
        for kt in range(N_KT):
            ub = u_ref[t, :, kt * MXU_DIM:(kt + 1) * MXU_DIM].astype(BF16)
            cs = slice(kt * ST_PER_KT, (kt + 1) * ST_PER_KT)
            xr_ref[:, cs] = jnp.dot(ub, bre_ref[kt], preferred_element_type=F32)
            xi_ref[:, cs] = jnp.dot(ub, bim_ref[kt], preferred_element_type=F32)
        ar, ai = ar_ref[...], ai_ref[...]
        s_r, s_i = sr_ref[...], si_ref[...]
        sr_ref[...] = xr_ref[...] + (ar * s_r - ai * s_i)
        si_ref[...] = xi_ref[...] + (ar * s_i + ai * s_r)
        for kt in range(N_KT):
            ys = slice(kt * MXU_DIM, (kt + 1) * MXU_DIM)
            y_ref[t, :, ys] = _cproj(sr_ref, si_ref, cre_ref, cimn_ref, kt) + d_ref[:, ys] * u_ref[t, :, ys]


def _s5_sample(u_tb, wts, ar, ai, d_skip, h0r, h0i):
    nt, nb, _ = u_tb.shape
    st = jax.ShapeDtypeStruct((nb, N_STATE), F32)
    args = (u_tb, *wts, ar, ai, d_skip, h0r, h0i)
    full = lambda a: pl.BlockSpec(a.shape, lambda i: (0,) * a.ndim)
    return pl.pallas_call(
        functools.partial(_s5_sample_kernel, nt=nt),
        out_shape=(jax.ShapeDtypeStruct((nt, nb, D_SSM), F32), st, st),
        grid=(1,),
        in_specs=[full(a) for a in args],
        out_specs=(pl.BlockSpec((nt, nb, D_SSM), lambda i: (0, 0, 0)), pl.BlockSpec((nb, N_STATE), lambda i: (0, 0)),
                   pl.BlockSpec((nb, N_STATE), lambda i: (0, 0))),
        scratch_shapes=[pltpu.VMEM((nb, N_STATE), F32), pltpu.VMEM((nb, N_STATE), F32)],
        compiler_params=_cparams(("arbitrary",)),
        name="s5_sample",
    )(*args)


def _gather_rows(src_ref, ids_ref, id_stride, id_offset, dst_ref, sem, priorities):
    def body(g, c):
        for j in range(SUBLANES):
            t = ids_ref[0, id_stride * (g * SUBLANES + j) + id_offset]
            pltpu.make_async_copy(src_ref.at[t >> 3, pl.ds(t & (SUBLANES - 1), 1)], dst_ref.at[g, pl.ds(j, 1)],
                                  sem).start(priority=priorities[j % len(priorities)])
        return c
    lax.fori_loop(0, dst_ref.shape[0], body, 0)


def _index_specs(n_cols, first_tile, n_steps, depth=1):
    def spec(fn):
        return pl.BlockSpec((None, 1, n_cols), fn, memory_space=pltpu.SMEM)

    head = [spec(lambda s, *_, k=k: (first_tile + min(k, n_steps - 1), 0, 0)) for k in range(depth)]
    return (*head, spec(lambda s, *_: (first_tile + jnp.minimum(s + depth, n_steps - 1), 0, 0)))


N_POST_INPUTS = 17


def _post_kernel(*refs, has_tail):
    ins, outs = refs[:N_POST_INPUTS], refs[N_POST_INPUTS + int(has_tail):]
    if not has_tail:
        _post_body(*ins, *outs)
        return
    tail_ref, h2_ref = refs[N_POST_INPUTS], outs[1]
    last = pl.num_programs(0) - 1

    @pl.when(pl.program_id(0) == last)
    def _():
        h2_ref[...] = tail_ref[...]

    @pl.when(pl.program_id(0) < last)
    def _():
        _post_body(*ins, *outs)


def _post_body(attn_ref, yssm_ref, x_ref, g1_ref, sh2_ref, sc2_ref, gluw_ref, glub_ref, ga_ref, gs_ref, wout_ref,
               l1g_ref, l1b_ref, wr_ref, br_ref, tri_ref, cnt0_ref, x1_ref, h2_ref, route_ref, cnt_ref):
    tm = x_ref.shape[0]

    @pl.when(pl.program_id(0) == 0)
    def _():
        cnt_ref[...] = cnt0_ref[...]

    z = jax.nn.gelu(yssm_ref[...])
    ssm = z * jax.nn.sigmoid(jnp.dot(z.astype(BF16), gluw_ref[...], preferred_element_type=F32) + glub_ref[...])
    mixed_a = (_rms(attn_ref[...]) * ga_ref[...]).astype(BF16)
    mixed_s = (_rms(ssm) * gs_ref[...]).astype(BF16)
    o = (jnp.dot(mixed_a, wout_ref[:D_ATTN, :], preferred_element_type=F32)
         + jnp.dot(mixed_s, wout_ref[D_ATTN:, :], preferred_element_type=F32))
    x1 = _ln(DEEPNORM_ALPHA * x_ref[...] + g1_ref[...] * o) * l1g_ref[...] + l1b_ref[...]
    x1_ref[...] = x1
    h2 = _ln(x1) * (1.0 + sc2_ref[...]) + sh2_ref[...]
    h2_ref[...] = h2
    logits = jnp.dot(h2.astype(BF16), wr_ref[...], preferred_element_type=F32) + br_ref[...]

    lane = lax.broadcasted_iota(jnp.int32, (tm, LANES), 1).astype(F32)
    big = float(4 * LANES)
    neg = -jnp.inf
    gl = jnp.where(lane < N_EXPERT_GROUPS, logits, neg)
    gp = jnp.exp(gl - jnp.max(gl, axis=-1, keepdims=True))
    gp = gp / jnp.sum(gp, axis=-1, keepdims=True)
    g_val = jnp.max(gp, axis=-1, keepdims=True)
    g_idx = jnp.min(jnp.where(gp == g_val, lane, big), axis=-1, keepdims=True)
    lo = ROUTE_LANE0 + EXPERTS_PER_GROUP * g_idx
    emask = (lane >= lo) & (lane < lo + EXPERTS_PER_GROUP)
    el = jnp.where(emask, logits, neg)
    ep = jnp.exp(el - jnp.max(el, axis=-1, keepdims=True))
    ep = jnp.where(emask, ep / jnp.sum(ep, axis=-1, keepdims=True), -1.0)
    v1 = jnp.max(ep, axis=-1, keepdims=True)
    i1 = jnp.min(jnp.where(ep == v1, lane, big), axis=-1, keepdims=True)
    ep2 = jnp.where(lane == i1, -1.0, ep)
    v2 = jnp.max(ep2, axis=-1, keepdims=True)
    i2 = jnp.min(jnp.where(ep2 == v2, lane, big), axis=-1, keepdims=True)
    vs = v1 + v2
    w1 = g_val * (v1 / vs)
    w2 = g_val * (v2 / vs)
    hit = jnp.where((lane == i1) | (lane == i2), 1.0, 0.0)
    before = jnp.dot(tri_ref[...], hit.astype(BF16), preferred_element_type=F32) + cnt_ref[0:1, :]
    r1 = jnp.sum(jnp.where(lane == i1, before, 0.0), axis=-1, keepdims=True)
    r2 = jnp.sum(jnp.where(lane == i2, before, 0.0), axis=-1, keepdims=True)
    cnt_ref[...] = cnt_ref[...] + jnp.sum(hit, axis=0, keepdims=True)
    e1 = i1 - ROUTE_LANE0
    e2 = i2 - ROUTE_LANE0
    route = jnp.zeros((tm, LANES), F32)
    for n, val in enumerate((e1, e2, w1, w2, r1, r2)):
        route = jnp.where(lane == n, val, route)
    route_ref[...] = route


def _post(attn, yssm, x, mod, per_row, w, cnt0, tail=None):
    nb, l, _ = x.shape
    tm = min(TM, l)
    nt = l // tm
    n_body = nb * nt
    has_tail = tail is not None
    if has_tail:
        assert tail.shape == (tm, D_MODEL)

    def bi(s):
        s = jnp.minimum(s, n_body - 1)
        return s // nt, s % nt

    row = lambda wd: pl.BlockSpec((None, tm, wd), lambda s: (*bi(s), 0))
    if per_row:
        mspec = lambda chunk: pl.BlockSpec((None, tm, D_MODEL), lambda s: (*bi(s), chunk))
    else:
        mspec = lambda chunk: pl.BlockSpec((None, 1, D_MODEL), lambda s: (bi(s)[0], 0, chunk))
    full = lambda a: pl.BlockSpec(a.shape, lambda s: (0,) * a.ndim)
    tri = jnp.tril(jnp.ones((tm, tm), F32), -1).astype(BF16)
    consts = (w["glu_w"], w["glu_b"], w["attn_g"], w["ssm_g"], w["w_out"], w["ln1_g"], w["ln1_b"], w["wr"], w["br"],
              tri, cnt0) + ((tail,) if has_tail else ())
    assert 6 + len(consts) == N_POST_INPUTS + int(has_tail)
    n_steps = n_body + int(has_tail)
    return pl.pallas_call(
        functools.partial(_post_kernel, has_tail=has_tail),
        out_shape=(jax.ShapeDtypeStruct((nb, l, D_MODEL), F32),
                   jax.ShapeDtypeStruct((n_steps * tm, D_MODEL), F32),
                   jax.ShapeDtypeStruct((nb, l, LANES), F32), jax.ShapeDtypeStruct((SUBLANES, LANES), F32)),
        grid=(n_steps,),
        in_specs=[row(D_ATTN), row(D_SSM), row(D_MODEL), mspec(2), mspec(3), mspec(4)] + [full(a) for a in consts],
        out_specs=(row(D_MODEL), pl.BlockSpec((tm, D_MODEL), lambda s: (s, 0)),
                   row(LANES), pl.BlockSpec((SUBLANES, LANES), lambda s: (0, 0))),
        compiler_params=_cparams(("arbitrary",)),
        name="post",
    )(attn, yssm, x, mod, mod, mod, *consts)


MOE_LOOKAHEAD = 2
MOE_SLOTS = MOE_LOOKAHEAD + 1


def _moe_kernel(te_ref, nv_ref, first_ref, nxt_ref, par_ref, rt0_ref, rt1_ref, rtn_ref, h2_ref, wg_hbm, wu_hbm, wd_hbm,
                y_ref, buf_ref, wgf_ref, wuf_ref, wdf_ref, wgb_ref, wub_ref, wdb_ref, sem, wsem):
    i = pl.program_id(0)
    nv = nv_ref[0]
    slot = lax.rem(i, MOE_SLOTS)
    w_pairs = ((wg_hbm, wgf_ref), (wu_hbm, wuf_ref), (wd_hbm, wdf_ref))

    def gather(ids_ref, dst_slot):
        _gather_rows(h2_ref, ids_ref, 1, 0, buf_ref.at[dst_slot], sem.at[dst_slot], (0,))

    def fetch_weights(e, ws):
        for src, dst in w_pairs:
            pltpu.make_async_copy(src.at[e], dst.at[ws], wsem.at[ws]).start(priority=1)

    @pl.when(i == 0)
    def _():
        gather(rt0_ref, 0)
        gather(rt1_ref, 1)
        fetch_weights(te_ref[0], par_ref[0])

    @pl.when(i < nv + MOE_LOOKAHEAD)
    def _():
        pltpu.make_async_copy(h2_ref.at[pl.ds(0, TM_MOE // SUBLANES)], buf_ref.at[slot], sem.at[slot]).wait()

    @pl.when((i < nv) & (first_ref[i] == 1))
    def _():
        ws = par_ref[i]
        for src, dst in w_pairs:
            pltpu.make_async_copy(src.at[0], dst.at[ws], wsem.at[ws]).wait()
        wgb_ref[...] = wgf_ref[ws].astype(BF16)
        wub_ref[...] = wuf_ref[ws].astype(BF16)
        wdb_ref[...] = wdf_ref[ws].astype(BF16)

        @pl.when(nxt_ref[i] >= 0)
        def _():
            fetch_weights(nxt_ref[i], 1 - ws)

    @pl.when(i < nv)
    def _():
        gather(rtn_ref, lax.rem(i + MOE_LOOKAHEAD, MOE_SLOTS))
        xb = buf_ref[slot].reshape(TM_MOE, D_MODEL).astype(BF16)
        hg = jnp.dot(xb, wgb_ref[...], preferred_element_type=F32)
        hu = jnp.dot(xb, wub_ref[...], preferred_element_type=F32)
        act = (hg * jax.nn.sigmoid(hg)) * hu
        y_ref[...] = jnp.dot(act.astype(BF16), wdb_ref[...], preferred_element_type=F32)

    @pl.when(i >= nv)
    def _():
        y_ref[...] = jnp.zeros(y_ref.shape, y_ref.dtype)


def _moe(tile_meta, n_valid, row_token, h2, w_gate, w_up, w_down):
    n_steps = row_token.shape[0] // TM_MOE
    rt = row_token.reshape(n_steps, 1, TM_MOE)
    te, first, nxt, par = tile_meta
    any_spec = pl.BlockSpec(memory_space=pl.ANY)
    return pl.pallas_call(
        _moe_kernel,
        out_shape=jax.ShapeDtypeStruct((n_steps * TM_MOE, D_MODEL), F32),
        grid_spec=pltpu.PrefetchScalarGridSpec(
            num_scalar_prefetch=5, grid=(n_steps,),
            in_specs=[*_index_specs(TM_MOE, 0, n_steps, MOE_LOOKAHEAD), any_spec, any_spec, any_spec, any_spec],
            out_specs=pl.BlockSpec((TM_MOE, D_MODEL), lambda i, *_: (i, 0)),
            scratch_shapes=[pltpu.VMEM((MOE_SLOTS, TM_MOE // SUBLANES, SUBLANES, D_MODEL), F32),
                            pltpu.VMEM((2, D_MODEL, D_EXPERT), F32), pltpu.VMEM((2, D_MODEL, D_EXPERT), F32),
                            pltpu.VMEM((2, D_EXPERT, D_MODEL), F32),
                            pltpu.VMEM((D_MODEL, D_EXPERT), BF16), pltpu.VMEM((D_MODEL, D_EXPERT), BF16),
                            pltpu.VMEM((D_EXPERT, D_MODEL), BF16),
                            pltpu.SemaphoreType.DMA((MOE_SLOTS,)), pltpu.SemaphoreType.DMA((2,))]),
        compiler_params=_cparams(("arbitrary",)),
        name="moe",
    )(te, n_valid, first, nxt, par, rt, rt, rt, h2.reshape(-1, SUBLANES, D_MODEL), w_gate, w_up, w_down)


def _final_kernel(pos0_ref, posn_ref, x1_ref, g2_ref, route_ref, l2g_ref, l2b_ref, ys_ref, o_ref, buf_ref, sem, *, tm):
    step = pl.program_id(0)
    slot = step & 1

    def gather(ids_ref, dst_slot):
        for k in range(2):
            _gather_rows(ys_ref, ids_ref, 2, k, buf_ref.at[dst_slot, k], sem.at[dst_slot], (0, 1))

    @pl.when(step == 0)
    def _():
        gather(pos0_ref, 0)

    for k in range(2):
        pltpu.make_async_copy(ys_ref.at[pl.ds(0, tm // SUBLANES)], buf_ref.at[slot, k], sem.at[slot]).wait()

    @pl.when(step + 1 < pl.num_programs(0))
    def _():
        gather(posn_ref, 1 - slot)

    route = route_ref[...]
    f = (route[:, 2:3] * buf_ref[slot, 0].reshape(tm, D_MODEL)
         + route[:, 3:4] * buf_ref[slot, 1].reshape(tm, D_MODEL))
    o_ref[...] = _ln(DEEPNORM_ALPHA * x1_ref[...] + g2_ref[...] * f) * l2g_ref[...] + l2b_ref[...]


def _final(pos_tiles, first_tile, x1, mod, per_row, route, ln2_g, ln2_b, ys):
    nb, l, _ = x1.shape
    tm = min(TM, l)
    nt = l // tm
    n_steps = nb * nt
    row = lambda wd: pl.BlockSpec((None, tm, wd), lambda s: (s // nt, s % nt, 0))
    if per_row:
        g2 = pl.BlockSpec((None, tm, D_MODEL), lambda s: (s // nt, s % nt, 5))
    else:
        g2 = pl.BlockSpec((None, 1, D_MODEL), lambda s: (s // nt, 0, 5))
    vec = pl.BlockSpec((1, D_MODEL), lambda s: (0, 0))
    return pl.pallas_call(
        functools.partial(_final_kernel, tm=tm),
        out_shape=jax.ShapeDtypeStruct((nb, l, D_MODEL), F32),
        grid=(n_steps,),
        in_specs=[*_index_specs(2 * tm, first_tile, n_steps), row(D_MODEL), g2, row(LANES), vec, vec,
                  pl.BlockSpec(memory_space=pl.ANY)],
        out_specs=row(D_MODEL),
        scratch_shapes=[pltpu.VMEM((2, 2, tm // SUBLANES, SUBLANES, D_MODEL), F32), pltpu.SemaphoreType.DMA((2,))],
        compiler_params=_cparams(("arbitrary",)),
        name="final",
    )(pos_tiles, pos_tiles, x1, mod, route, ln2_g, ln2_b, ys.reshape(-1, SUBLANES, D_MODEL))


def _rope_tables(pos):
    half = ROT_DIM // 2
    inv_freq = ROPE_THETA ** (-jnp.arange(half, dtype=jnp.float32) * 2.0 / ROT_DIM)
    ang = pos.astype(jnp.float32)[:, None] * inv_freq[None, :]
    cos, sin = jnp.cos(ang), jnp.sin(ang)
    n = pos.shape[0]
    one = jnp.ones((n, HEAD_DIM - ROT_DIM), F32)
    zero = jnp.zeros((n, HEAD_DIM - half), F32)
    c = jnp.concatenate([cos, cos, one], -1)
    a = jnp.concatenate([-sin, zero], -1)
    b = jnp.concatenate([jnp.zeros((n, half), F32), sin, jnp.zeros((n, HEAD_DIM - ROT_DIM), F32)], -1)
    return tuple(jnp.tile(t, (1, LANES // HEAD_DIM)) for t in (c, a, b))


def _s5_params(lam_re, lam_im, log_step, b_re, b_im, c_re, c_im):
    f32 = jnp.float32
    dt = jnp.exp(log_step.astype(f32))[:, None]
    lr, li = lam_re.astype(f32), lam_im.astype(f32)
    mag = jnp.exp(lr * dt)
    ar, ai = mag * jnp.cos(li * dt), mag * jnp.sin(li * dt)
    den = lr * lr + li * li
    cr = ((ar - 1.0) * lr + ai * li) / den
    ci = (ai * lr - (ar - 1.0) * li) / den
    br, bi = b_re.astype(f32), b_im.astype(f32)
    bbr = cr[..., None] * br - ci[..., None] * bi
    bbi = cr[..., None] * bi + ci[..., None] * br
    gpt = MXU_DIM // SSM_CH
    eye = jnp.eye(gpt, dtype=f32)

    def pack_b(m):
        m = m.reshape(N_KT, gpt, SSM_STATE, SSM_CH)
        return jnp.einsum("kgpc,gh->kgchp", m, eye).reshape(N_KT, MXU_DIM, ST_PER_KT).astype(BF16)

    def pack_c(m):
        m = m.astype(f32).reshape(N_KT, gpt, SSM_CH, SSM_STATE)
        return jnp.einsum("kgcp,gh->khpgc", m, eye).reshape(N_KT, ST_PER_KT, MXU_DIM).astype(BF16)

    wts = (pack_b(bbr), pack_b(bbi), pack_c(c_re), pack_c(-c_im.astype(f32)))
    return wts, ar.reshape(1, N_STATE), ai.reshape(1, N_STATE)


def _scan_tables(ar, ai, nb):
    assert SUBLANES // nb == 2
    a2r, a2i = ar * ar - ai * ai, 2.0 * ar * ai
    z = jnp.zeros_like(ar)
    rep = lambda first, second: jnp.concatenate([jnp.tile(first, (nb, 1)), jnp.tile(second, (nb, 1))], 0)
    return rep(z, ar), rep(z, ai), rep(ar, a2r), rep(ai, a2i)


def kernel(x_prompt, x_sample, cache_k_win, cache_v_win, state_ssm_re, state_ssm_im, c_prompt, c_sample, ada_w, ada_b,
           w_in, attn_sinks, ssm_lambda_re, ssm_lambda_im, ssm_log_step, ssm_b_re, ssm_b_im, ssm_c_re, ssm_c_im, ssm_d,
           ssm_glu_w, ssm_glu_b, attn_norm_g, ssm_norm_g, w_out, ln1_g, ln1_b, router_group_w, router_group_b,
           router_expert_w, router_expert_b, exp_w_gate, exp_w_up, exp_w_down, ln2_g, ln2_b):
    assert ada_w.shape[0] == DEPTH
    bp, lp, _ = x_prompt.shape
    bs, ls, _ = x_sample.shape
    lsp = SAMPLE_PAD

    w_in_bf = w_in[0].astype(BF16)
    wr = jnp.concatenate([router_group_w[0], router_expert_w[0]], -1)
    wr = jnp.pad(wr, ((0, 0), (0, LANES - wr.shape[1]))).astype(BF16)
    br = jnp.pad(jnp.concatenate([router_group_b[0], router_expert_b[0]], -1), (0, LANES - N_EXPERT_GROUPS - N_EXPERTS))
    wpost = dict(glu_w=ssm_glu_w[0].astype(BF16), glu_b=ssm_glu_b[0][None], attn_g=attn_norm_g[0][None],
                 ssm_g=ssm_norm_g[0][None], w_out=w_out[0].astype(BF16), ln1_g=ln1_g[0][None], ln1_b=ln1_b[0][None],
                 wr=wr, br=br[None])
    s5w, ar, ai = _s5_params(ssm_lambda_re[0], ssm_lambda_im[0], ssm_log_step[0], ssm_b_re[0], ssm_b_im[0],
                             ssm_c_re[0], ssm_c_im[0])
    d_skip = ssm_d[0][None]
    sinks = attn_sinks[0]

    n_c = bp + bs
    n_cp = -(-n_c // SUBLANES) * SUBLANES
    c_all = jnp.pad(jnp.concatenate([c_prompt, c_sample], 0), ((0, n_cp - n_c), (0, 0)))
    mod = _ada(c_all, ada_w[0], ada_b[0][None])
    mod_p = mod[:bp][:, None, :]
    mod_s = jnp.repeat(mod[bp:n_c], lsp, axis=0)[None]

    xs_pad = jnp.pad(x_sample, ((0, 0), (0, lsp - ls), (0, 0))).reshape(1, bs * lsp, D_MODEL)

    rope_p = _rope_tables(jnp.arange(lp))
    pos_s = PAST_LEN + jnp.minimum(jnp.arange(lsp), ls - 1)
    rope_s = tuple(jnp.tile(t, (bs, 1)) for t in _rope_tables(pos_s))
    q_p, k_p, v_p, u_p = _inproj(x_prompt, mod_p, w_in_bf, rope_p, False)
    q_s, k_s, v_s, u_s = _inproj(xs_pad, mod_s, w_in_bf, rope_s, True)

    attn_p = _attn_prompt(sinks, q_p, k_p, v_p)
    k_s3 = k_s.reshape(bs, lsp, D_KV)
    v_s3 = v_s.reshape(bs, lsp, D_KV)
    ck = cache_k_win[0].reshape(bs, WINDOW, D_KV)
    cv = cache_v_win[0].reshape(bs, WINDOW, D_KV)
    padk = lambda new: jnp.pad(new, ((0, 0), (0, WINDOW - lsp), (0, 0)))
    attn_s = _attn_sample(sinks, q_s.reshape(bs, lsp, D_ATTN), jnp.concatenate([ck, padk(k_s3)], 1),
                          jnp.concatenate([cv, padk(v_s3)], 1))

    tabs = _scan_tables(ar, ai, bp)
    zero_carry = jnp.zeros((SUBLANES, N_STATE), F32)
    y_p, hr_p, hi_p = _s5_prompt(u_p, s5w, tabs, d_skip, zero_carry, zero_carry)
    u_tb = jnp.transpose(u_s.reshape(bs, lsp, D_SSM)[:, :ls], (1, 0, 2))
    y_tb, hr_s, hi_s = _s5_sample(u_tb, s5w, ar, ai, d_skip, state_ssm_re[0].reshape(bs, N_STATE),
                                  state_ssm_im[0].reshape(bs, N_STATE))
    y_s = jnp.pad(jnp.transpose(y_tb, (1, 0, 2)), ((0, 0), (0, lsp - ls), (0, 0))).reshape(1, bs * lsp, D_SSM)

    cnt0 = jnp.zeros((SUBLANES, LANES), F32)
    n_p, n_s = bp * lp, bs * lsp
    n_tok = n_p + n_s
    x1_s, h2_s, route_s, cnt_s = _post(attn_s.reshape(1, bs * lsp, D_ATTN), y_s, xs_pad, mod_s, True, wpost, cnt0)
    x1_p, h2, route_p, cnt = _post(attn_p, y_p, x_prompt, mod_p, False, wpost, cnt_s, tail=h2_s)

    n_tiles = -(-(2 * n_tok + N_EXPERTS * (TM_MOE - 1)) // TM_MOE) + MOE_LOOKAHEAD
    route = jnp.concatenate([route_p.reshape(n_p, LANES), route_s.reshape(n_s, LANES)], 0)
    counts = cnt[0, ROUTE_LANE0:ROUTE_LANE0 + N_EXPERTS].astype(jnp.int32)
    padded = ((counts + TM_MOE - 1) // TM_MOE) * TM_MOE
    ends = jnp.cumsum(padded)
    offs = ends - padded
    eid = route[:, 0:2].astype(jnp.int32)
    pos = (offs[eid] + route[:, 4:6].astype(jnp.int32)).reshape(-1)
    n_valid = ends[-1] // TM_MOE
    tile_row = jnp.minimum(jnp.arange(n_tiles), n_valid - 1) * TM_MOE
    tile_expert = jnp.sum((ends[None, :] <= tile_row[:, None]).astype(jnp.int32), axis=1)
    first_tile = jnp.concatenate([jnp.ones((1,), jnp.int32), (tile_expert[1:] != tile_expert[:-1]).astype(jnp.int32)])
    first_tile = first_tile * (jnp.arange(n_tiles) < n_valid)
    nonempty = padded > 0
    expert_ids = jnp.arange(N_EXPERTS, dtype=jnp.int32)
    later = jnp.where(nonempty[None, :] & (expert_ids[None, :] > expert_ids[:, None]), expert_ids[None, :], N_EXPERTS)
    next_expert = jnp.min(later, axis=1)
    next_expert = jnp.where(next_expert < N_EXPERTS, next_expert, -1).astype(jnp.int32)
    weight_slot = ((jnp.cumsum(nonempty.astype(jnp.int32)) - 1) & 1).astype(jnp.int32)
    tile_meta = (tile_expert, first_tile.astype(jnp.int32), next_expert[tile_expert], weight_slot[tile_expert])
    token_of = jnp.repeat(jnp.arange(n_tok, dtype=jnp.int32), 2)
    row_token = jnp.zeros((n_tiles * TM_MOE,), jnp.int32).at[pos].set(token_of)

    ys = _moe(tile_meta, n_valid.reshape(1).astype(jnp.int32), row_token, h2, exp_w_gate[0], exp_w_up[0],
              exp_w_down[0])
    pos_tiles = pos.reshape(n_tok // TM, 1, 2 * TM)
    y_prompt = _final(pos_tiles, 0, x1_p, mod_p, False, route_p, ln2_g[0][None], ln2_b[0][None], ys)
    y_samp = _final(pos_tiles, n_p // TM, x1_s, mod_s, True, route_s, ln2_g[0][None], ln2_b[0][None], ys)
    y_sample = y_samp.reshape(bs, lsp, D_MODEL)[:, :ls]

    kv5 = lambda a, n: a.reshape(1, a.shape[0], n, N_KV_HEADS, HEAD_DIM)
    k_win_p = kv5(k_p[:, lp - WINDOW:], WINDOW)
    v_win_p = kv5(v_p[:, lp - WINDOW:], WINDOW)
    k_win_s = kv5(jnp.concatenate([ck[:, ls:], k_s3[:, :ls]], 1), WINDOW)
    v_win_s = kv5(jnp.concatenate([cv[:, ls:], v_s3[:, :ls]], 1), WINDOW)
    st = lambda a: a.reshape(1, a.shape[0], N_SSM_GROUPS, SSM_STATE)
    return (y_prompt, y_sample, k_win_p, v_win_p, st(hr_p[bp:2 * bp]), st(hi_p[bp:2 * bp]),
            k_win_s, v_win_s, st(hr_s), st(hi_s))
```

```python
import functools
import math

import jax
import jax.numpy as jnp
from jax import lax
from jax.experimental import pallas as pl
from jax.experimental.pallas import tpu as pltpu

F32 = jnp.float32
BF16 = jnp.bfloat16

D_MODEL = 2048
D_ATTN = 1024
D_SSM = 1024
HEAD_DIM = 64
N_HEADS = 16
N_KV_HEADS = 4
REP = 4
D_KV = 256
ROT_DIM = 16
ROPE_THETA = 500000.0
WINDOW = 128
SSM_CH = 16
N_SSM_GROUPS = 64
SSM_STATE = 64
N_STATE = N_SSM_GROUPS * SSM_STATE
PROJ_COLS = D_ATTN + 2 * D_KV + D_SSM
N_EXPERT_GROUPS = 4
EXPERTS_PER_GROUP = 8
N_EXPERTS = 32
D_EXPERT = 512
DEPTH = 1
DEEPNORM_ALPHA = (2.0 * DEPTH) ** 0.25
LN_EPS = 1e-5
PAST_LEN = 16384

LANES = 128
SUBLANES = 8
MXU_DIM = 256
TM = 256
TM_MOE = 256
S5_TT = 64
SAMPLE_PAD = 8
ROUTE_LANE0 = N_EXPERT_GROUPS
VMEM_LIMIT = 56 * 1024 * 1024


def _cparams(sem):
    return pltpu.CompilerParams(dimension_semantics=sem, vmem_limit_bytes=VMEM_LIMIT)


def _ln(x):
    mu = jnp.mean(x, axis=-1, keepdims=True)
    xc = x - mu
    var = jnp.mean(xc * xc, axis=-1, keepdims=True)
    return xc * lax.rsqrt(var + LN_EPS)


def _rms(x):
    return x * lax.rsqrt(jnp.mean(x * x, axis=-1, keepdims=True) + LN_EPS)


def _ada_kernel(c_ref, w_ref, b_ref, o_ref):
    c = c_ref[...]
    s = c * jax.nn.sigmoid(c)
    o_ref[...] = jnp.dot(s.astype(BF16), w_ref[...].astype(BF16), preferred_element_type=F32) + b_ref[...]


def _ada(c_all, ada_w, ada_b):
    n, tn = c_all.shape[0], 1024
    return pl.pallas_call(
        _ada_kernel,
        out_shape=jax.ShapeDtypeStruct((n, 6 * D_MODEL), F32),
        grid=(6 * D_MODEL // tn,),
        in_specs=[pl.BlockSpec((n, D_MODEL), lambda j: (0, 0)),
                  pl.BlockSpec((D_MODEL, tn), lambda j: (0, j)),
                  pl.BlockSpec((1, tn), lambda j: (0, j))],
        out_specs=pl.BlockSpec((n, tn), lambda j: (0, j)),
        compiler_params=_cparams(("arbitrary",)),
        name="ada",
    )(c_all, ada_w, ada_b)


def _mod_spec(per_row, tm, chunk):
    if per_row:
        return pl.BlockSpec((None, tm, D_MODEL), lambda b, i: (b, i, chunk))
    return pl.BlockSpec((None, 1, D_MODEL), lambda b, i: (b, 0, chunk))


def _inproj_kernel(x_ref, sh_ref, sc_ref, w_ref, rc_ref, ra_ref, rb_ref, q_ref, k_ref, v_ref, u_ref):
    h = _ln(x_ref[...]) * (1.0 + sc_ref[...]) + sh_ref[...]
    proj = jnp.dot(h.astype(BF16), w_ref[...], preferred_element_type=F32)
    rc, ra, rb = rc_ref[...], ra_ref[...], rb_ref[...]

    def rope(t):
        return t * rc + pltpu.roll(t, LANES - ROT_DIM // 2, 1) * ra + pltpu.roll(t, ROT_DIM // 2, 1) * rb

    for j in range(D_ATTN // LANES):
        q_ref[:, j * LANES:(j + 1) * LANES] = (rope(proj[:, j * LANES:(j + 1) * LANES]) * HEAD_DIM ** -0.5).astype(BF16)
    for j in range(D_KV // LANES):
        c0 = D_ATTN + j * LANES
        k_ref[:, j * LANES:(j + 1) * LANES] = rope(proj[:, c0:c0 + LANES])
    v_ref[...] = proj[:, D_ATTN + D_KV:D_ATTN + 2 * D_KV]
    u_ref[...] = proj[:, D_ATTN + 2 * D_KV:]


def _inproj(x, mod, w_in_bf, rope_tabs, per_row):
    nb, l, _ = x.shape
    tm = min(TM, l)
    row = lambda w: pl.BlockSpec((None, tm, w), lambda b, i: (b, i, 0))
    tab = pl.BlockSpec((tm, LANES), lambda b, i: (i, 0))
    return pl.pallas_call(
        _inproj_kernel,
        out_shape=(jax.ShapeDtypeStruct((nb, l, D_ATTN), BF16), jax.ShapeDtypeStruct((nb, l, D_KV), F32),
                   jax.ShapeDtypeStruct((nb, l, D_KV), F32), jax.ShapeDtypeStruct((nb, l, D_SSM), F32)),
        grid=(nb, l // tm),
        in_specs=[row(D_MODEL), _mod_spec(per_row, tm, 0), _mod_spec(per_row, tm, 1),
                  pl.BlockSpec((D_MODEL, PROJ_COLS), lambda b, i: (0, 0)), tab, tab, tab],
        out_specs=(row(D_ATTN), row(D_KV), row(D_KV), row(D_SSM)),
        compiler_params=_cparams(("arbitrary", "arbitrary")),
        name="inproj",
    )(x, mod, mod, w_in_bf, *rope_tabs)


ATTN_SAMPLE_BATCH = 8


def _attn_kernel(sink_ref, q_ref, kp_ref, kc_ref, vp_ref, vc_ref, o_ref, *, lq, prev_from_block):
    m_rows = REP * lq
    ii = lax.broadcasted_iota(jnp.int32, (m_rows, WINDOW), 0) & (lq - 1)
    jj = lax.broadcasted_iota(jnp.int32, (m_rows, WINDOW), 1)
    from_prev = jj > ii
    if prev_from_block:
        dead = jj > ii + jnp.where(pl.program_id(1) > 0, WINDOW, 0)
    rr = lax.broadcasted_iota(jnp.int32, (m_rows, 1), 0)
    dn = (((1,), (1,)), ((), ()))
    for bi in range(q_ref.shape[0]):
        for g in range(N_KV_HEADS):
            qg = q_ref[bi, :, g * REP * HEAD_DIM:(g + 1) * REP * HEAD_DIM].astype(F32)
            qs = jnp.concatenate([qg[:, r * HEAD_DIM:(r + 1) * HEAD_DIM] for r in range(REP)], axis=0).astype(BF16)
            ks = slice(g * HEAD_DIM, (g + 1) * HEAD_DIM)
            s_p = lax.dot_general(qs, kp_ref[bi, :, ks].astype(BF16), dn, preferred_element_type=F32)
            s_c = lax.dot_general(qs, kc_ref[bi, :, ks].astype(BF16), dn, preferred_element_type=F32)
            s = jnp.where(from_prev, s_p, s_c)
            if prev_from_block:
                s = jnp.where(dead, -jnp.inf, s)
            sink = jnp.zeros((m_rows, 1), F32)
            for r in range(REP):
                sink = jnp.where((rr >= r * lq) & (rr < (r + 1) * lq), sink_ref[g * REP + r], sink)
            m = jnp.maximum(jnp.max(s, axis=-1, keepdims=True), sink)
            p = jnp.exp(s - m)
            p = p / (jnp.sum(p, axis=-1, keepdims=True) + jnp.exp(sink - m))
            o = (jnp.dot(jnp.where(from_prev, p, 0.0).astype(BF16), vp_ref[bi, :, ks].astype(BF16),
                         preferred_element_type=F32)
                 + jnp.dot(jnp.where(from_prev, 0.0, p).astype(BF16), vc_ref[bi, :, ks].astype(BF16),
                           preferred_element_type=F32))
            for r in range(REP):
                h = g * REP + r
                o_ref[bi, :, h * HEAD_DIM:(h + 1) * HEAD_DIM] = o[r * lq:(r + 1) * lq]


def _attn_prompt(sinks, q, k, v):
    nb, l, _ = q.shape
    nblk = l // WINDOW
    cur = lambda w: pl.BlockSpec((1, WINDOW, w), lambda b, n: (b, n, 0))
    prev = lambda w: pl.BlockSpec((1, WINDOW, w), lambda b, n: (b, jnp.maximum(n - 1, 0), 0))
    return pl.pallas_call(
        functools.partial(_attn_kernel, lq=WINDOW, prev_from_block=True),
        out_shape=jax.ShapeDtypeStruct((nb, l, D_ATTN), F32),
        grid=(nb, nblk),
        in_specs=[pl.BlockSpec(memory_space=pltpu.SMEM), cur(D_ATTN), prev(D_KV), cur(D_KV), prev(D_KV), cur(D_KV)],
        out_specs=cur(D_ATTN),
        compiler_params=_cparams(("arbitrary", "arbitrary")),
        name="attn_prompt",
    )(sinks, q, k, k, v, v)


def _attn_sample(sinks, q, kcat, vcat):
    nb, lq, _ = q.shape
    nbb = ATTN_SAMPLE_BATCH
    kblk = lambda n: pl.BlockSpec((nbb, WINDOW, D_KV), lambda b: (b, n, 0))
    return pl.pallas_call(
        functools.partial(_attn_kernel, lq=lq, prev_from_block=False),
        out_shape=jax.ShapeDtypeStruct((nb, lq, D_ATTN), F32),
        grid=(nb // nbb,),
        in_specs=[pl.BlockSpec(memory_space=pltpu.SMEM), pl.BlockSpec((nbb, lq, D_ATTN), lambda b: (b, 0, 0)),
                  kblk(0), kblk(1), kblk(0), kblk(1)],
        out_specs=pl.BlockSpec((nbb, lq, D_ATTN), lambda b: (b, 0, 0)),
        compiler_params=_cparams(("arbitrary",)),
        name="attn_sample",
    )(sinks, q, kcat, kcat, vcat, vcat)


N_KT = D_SSM // MXU_DIM
ST_PER_KT = N_STATE // N_KT
N_SLAB = D_SSM // LANES


def _cproj(hr_ref, hi_ref, cre_ref, cimn_ref, kt):
    cs = slice(kt * ST_PER_KT, (kt + 1) * ST_PER_KT)
    return (jnp.dot(hr_ref[:, cs].astype(BF16), cre_ref[kt], preferred_element_type=F32)
            + jnp.dot(hi_ref[:, cs].astype(BF16), cimn_ref[kt], preferred_element_type=F32))


def _s5_prompt_kernel(u_ref, bre_ref, bim_ref, cre_ref, cimn_ref, a1r_ref, a1i_ref, par_ref, pai_ref, d_ref,
                      h0r_ref, h0i_ref, y_ref, cr_ref, ci_ref, il_ref, xr_ref, xi_ref, *, nb, tt):
    rows = nb * tt

    @pl.when(pl.program_id(0) == 0)
    def _():
        cr_ref[...] = h0r_ref[...]
        ci_ref[...] = h0i_ref[...]

    for b in range(nb):
        for j in range(N_SLAB):
            il_ref[j, pl.ds(b, tt, stride=nb), :] = u_ref[b, :, j * LANES:(j + 1) * LANES]
    for kt in range(N_KT):
        ub = jnp.concatenate([il_ref[2 * kt], il_ref[2 * kt + 1]], axis=1).astype(BF16)
        cs = slice(kt * ST_PER_KT, (kt + 1) * ST_PER_KT)
        xr_ref[:, cs] = jnp.dot(ub, bre_ref[kt], preferred_element_type=F32)
        xi_ref[:, cs] = jnp.dot(ub, bim_ref[kt], preferred_element_type=F32)

    half = lax.broadcasted_iota(jnp.int32, (SUBLANES, LANES), 0) < nb

    def step(i, carry):
        r0 = pl.multiple_of(i * SUBLANES, SUBLANES)
        for j in range(N_STATE // LANES):
            ls = slice(j * LANES, (j + 1) * LANES)
            x_r = xr_ref[pl.ds(r0, SUBLANES), ls]
            x_i = xi_ref[pl.ds(r0, SUBLANES), ls]
            s_r = pltpu.roll(x_r, nb, 0)
            s_i = pltpu.roll(x_i, nb, 0)
            a1r, a1i = a1r_ref[:, ls], a1i_ref[:, ls]
            c_r, c_i = cr_ref[:, ls], ci_ref[:, ls]
            par, pai = par_ref[:, ls], pai_ref[:, ls]
            h_r = x_r + (a1r * s_r - a1i * s_i) + (par * c_r - pai * c_i)
            h_i = x_i + (a1r * s_i + a1i * s_r) + (par * c_i + pai * c_r)
            xr_ref[pl.ds(r0, SUBLANES), ls] = h_r
            xi_ref[pl.ds(r0, SUBLANES), ls] = h_i
            cr_ref[:, ls] = jnp.where(half, pltpu.roll(h_r, nb, 0), h_r)
            ci_ref[:, ls] = jnp.where(half, pltpu.roll(h_i, nb, 0), h_i)
        return carry

    lax.fori_loop(0, rows // SUBLANES, step, 0)

    for kt in range(N_KT):
        y = _cproj(xr_ref, xi_ref, cre_ref, cimn_ref, kt)
        il_ref[2 * kt] = y[:, :LANES]
        il_ref[2 * kt + 1] = y[:, LANES:]
    for b in range(nb):
        for j in range(N_SLAB):
            ls = slice(j * LANES, (j + 1) * LANES)
            y_ref[b, :, ls] = il_ref[j, pl.ds(b, tt, stride=nb), :] + d_ref[:, ls] * u_ref[b, :, ls]


def _s5_prompt(u, wts, tabs, d_skip, h0r, h0i):
    nb, l, _ = u.shape
    tt = S5_TT
    rows = nb * tt
    full = lambda a: pl.BlockSpec(a.shape, lambda i: (0,) * a.ndim)
    blk = pl.BlockSpec((nb, tt, D_SSM), lambda i: (0, i, 0))
    carry = jax.ShapeDtypeStruct((SUBLANES, N_STATE), F32)
    return pl.pallas_call(
        functools.partial(_s5_prompt_kernel, nb=nb, tt=tt),
        out_shape=(jax.ShapeDtypeStruct((nb, l, D_SSM), F32), carry, carry),
        grid=(l // tt,),
        in_specs=[blk] + [full(a) for a in (*wts, *tabs, d_skip, h0r, h0i)],
        out_specs=(blk, pl.BlockSpec((SUBLANES, N_STATE), lambda i: (0, 0)),
                   pl.BlockSpec((SUBLANES, N_STATE), lambda i: (0, 0))),
        scratch_shapes=[pltpu.VMEM((N_SLAB, rows, LANES), F32), pltpu.VMEM((rows, N_STATE), F32),
                        pltpu.VMEM((rows, N_STATE), F32)],
        compiler_params=_cparams(("arbitrary",)),
        name="s5_prompt",
    )(u, *wts, *tabs, d_skip, h0r, h0i)


def _s5_sample_kernel(u_ref, bre_ref, bim_ref, cre_ref, cimn_ref, ar_ref, ai_ref, d_ref, h0r_ref, h0i_ref,
                      y_ref, sr_ref, si_ref, xr_ref, xi_ref, *, nt):
    sr_ref[...] = h0r_ref[...]
    si_ref[...] = h0i_ref[...]
    for t in range(nt):
        for kt in range(N_KT):
            ub = u_ref[t, :, kt * MXU_DIM:(kt + 1) * MXU_DIM].astype(BF16)
            cs = slice(kt * ST_PER_KT, (kt + 1) * ST_PER_KT)
            xr_ref[:, cs] = jnp.dot(ub, bre_ref[kt], preferred_element_type=F32)
            xi_ref[:, cs] = jnp.dot(ub, bim_ref[kt], preferred_element_type=F32)
        ar, ai = ar_ref[...], ai_ref[...]
        s_r, s_i = sr_ref[...], si_ref[...]
        sr_ref[...] = xr_ref[...] + (ar * s_r - ai * s_i)
        si_ref[...] = xi_ref[...] + (ar * s_i + ai * s_r)
        for kt in range(N_KT):
            ys = slice(kt * MXU_DIM, (kt + 1) * MXU_DIM)
            y_ref[t, :, ys] = _cproj(sr_ref, si_ref, cre_ref, cimn_ref, kt) + d_ref[:, ys] * u_ref[t, :, ys]


def _s5_sample(u_tb, wts, ar, ai, d_skip, h0r, h0i):
    nt, nb, _ = u_tb.shape
    st = jax.ShapeDtypeStruct((nb, N_STATE), F32)
    args = (u_tb, *wts, ar, ai, d_skip, h0r, h0i)
    full = lambda a: pl.BlockSpec(a.shape, lambda i: (0,) * a.ndim)
    return pl.pallas_call(
        functools.partial(_s5_sample_kernel, nt=nt),
        out_shape=(jax.ShapeDtypeStruct((nt, nb, D_SSM), F32), st, st),
        grid=(1,),
        in_specs=[full(a) for a in args],
        out_specs=(pl.BlockSpec((nt, nb, D_SSM), lambda i: (0, 0, 0)), pl.BlockSpec((nb, N_STATE), lambda i: (0, 0)),
                   pl.BlockSpec((nb, N_STATE), lambda i: (0, 0))),
        scratch_shapes=[pltpu.VMEM((nb, N_STATE), F32), pltpu.VMEM((nb, N_STATE), F32)],
        compiler_params=_cparams(("arbitrary",)),
        name="s5_sample",
    )(*args)


def _gather_rows(src_ref, ids_ref, id_stride, id_offset, dst_ref, sem, priorities):
    def body(g, c):
        for j in range(SUBLANES):
            t = ids_ref[0, id_stride * (g * SUBLANES + j) + id_offset]
            pltpu.make_async_copy(src_ref.at[t >> 3, pl.ds(t & (SUBLANES - 1), 1)], dst_ref.at[g, pl.ds(j, 1)],
                                  sem).start(priority=priorities[j % len(priorities)])
        return c
    lax.fori_loop(0, dst_ref.shape[0], body, 0)


def _index_specs(n_cols, first_tile, n_steps, depth=1):
    def spec(fn):
        return pl.BlockSpec((None, 1, n_cols), fn, memory_space=pltpu.SMEM)

    head = [spec(lambda s, *_, k=k: (first_tile + min(k, n_steps - 1), 0, 0)) for k in range(depth)]
    return (*head, spec(lambda s, *_: (first_tile + jnp.minimum(s + depth, n_steps - 1), 0, 0)))


N_POST_INPUTS = 17


def _post_kernel(*refs, has_tail):
    ins, outs = refs[:N_POST_INPUTS], refs[N_POST_INPUTS + int(has_tail):]
    if not has_tail:
        _post_body(*ins, *outs)
        return
    tail_ref, h2_ref = refs[N_POST_INPUTS], outs[1]
    last = pl.num_programs(0) - 1

    @pl.when(pl.program_id(0) == last)
    def _():
        h2_ref[...] = tail_ref[...]

    @pl.when(pl.program_id(0) < last)
    def _():
        _post_body(*ins, *outs)


def _post_body(attn_ref, yssm_ref, x_ref, g1_ref, sh2_ref, sc2_ref, gluw_ref, glub_ref, ga_ref, gs_ref, wout_ref,
               l1g_ref, l1b_ref, wr_ref, br_ref, tri_ref, cnt0_ref, x1_ref, h2_ref, route_ref, cnt_ref):
    tm = x_ref.shape[0]

    @pl.when(pl.program_id(0) == 0)
    def _():
        cnt_ref[...] = cnt0_ref[...]

    z = jax.nn.gelu(yssm_ref[...])
    ssm = z * jax.nn.sigmoid(jnp.dot(z.astype(BF16), gluw_ref[...], preferred_element_type=F32) + glub_ref[...])
    mixed_a = (_rms(attn_ref[...]) * ga_ref[...]).astype(BF16)
    mixed_s = (_rms(ssm) * gs_ref[...]).astype(BF16)
    o = (jnp.dot(mixed_a, wout_ref[:D_ATTN, :], preferred_element_type=F32)
         + jnp.dot(mixed_s, wout_ref[D_ATTN:, :], preferred_element_type=F32))
    x1 = _ln(DEEPNORM_ALPHA * x_ref[...] + g1_ref[...] * o) * l1g_ref[...] + l1b_ref[...]
    x1_ref[...] = x1
    h2 = _ln(x1) * (1.0 + sc2_ref[...]) + sh2_ref[...]
    h2_ref[...] = h2
    logits = jnp.dot(h2.astype(BF16), wr_ref[...], preferred_element_type=F32) + br_ref[...]

    lane = lax.broadcasted_iota(jnp.int32, (tm, LANES), 1).astype(F32)
    big = float(4 * LANES)
    neg = -jnp.inf
    gl = jnp.where(lane < N_EXPERT_GROUPS, logits, neg)
    gp = jnp.exp(gl - jnp.max(gl, axis=-1, keepdims=True))
    gp = gp / jnp.sum(gp, axis=-1, keepdims=True)
    g_val = jnp.max(gp, axis=-1, keepdims=True)
    g_idx = jnp.min(jnp.where(gp == g_val, lane, big), axis=-1, keepdims=True)
    lo = ROUTE_LANE0 + EXPERTS_PER_GROUP * g_idx
    emask = (lane >= lo) & (lane < lo + EXPERTS_PER_GROUP)
    el = jnp.where(emask, logits, neg)
    ep = jnp.exp(el - jnp.max(el, axis=-1, keepdims=True))
    ep = jnp.where(emask, ep / jnp.sum(ep, axis=-1, keepdims=True), -1.0)
    v1 = jnp.max(ep, axis=-1, keepdims=True)
    i1 = jnp.min(jnp.where(ep == v1, lane, big), axis=-1, keepdims=True)
    ep2 = jnp.where(lane == i1, -1.0, ep)
    v2 = jnp.max(ep2, axis=-1, keepdims=True)
    i2 = jnp.min(jnp.where(ep2 == v2, lane, big), axis=-1, keepdims=True)
    vs = v1 + v2
    w1 = g_val * (v1 / vs)
    w2 = g_val * (v2 / vs)
    hit = jnp.where((lane == i1) | (lane == i2), 1.0, 0.0)
    before = jnp.dot(tri_ref[...], hit.astype(BF16), preferred_element_type=F32) + cnt_ref[0:1, :]
    r1 = jnp.sum(jnp.where(lane == i1, before, 0.0), axis=-1, keepdims=True)
    r2 = jnp.sum(jnp.where(lane == i2, before, 0.0), axis=-1, keepdims=True)
    cnt_ref[...] = cnt_ref[...] + jnp.sum(hit, axis=0, keepdims=True)
    e1 = i1 - ROUTE_LANE0
    e2 = i2 - ROUTE_LANE0
    route = jnp.zeros((tm, LANES), F32)
    for n, val in enumerate((e1, e2, w1, w2, r1, r2)):
        route = jnp.where(lane == n, val, route)
    route_ref[...] = route


def _post(attn, yssm, x, mod, per_row, w, cnt0, tail=None):
    nb, l, _ = x.shape
    tm = min(TM, l)
    nt = l // tm
    n_body = nb * nt
    has_tail = tail is not None
    if has_tail:
        assert tail.shape == (tm, D_MODEL)

    def bi(s):
        s = jnp.minimum(s, n_body - 1)
        return s // nt, s % nt

    row = lambda wd: pl.BlockSpec((None, tm, wd), lambda s: (*bi(s), 0))
    if per_row:
        mspec = lambda chunk: pl.BlockSpec((None, tm, D_MODEL), lambda s: (*bi(s), chunk))
    else:
        mspec = lambda chunk: pl.BlockSpec((None, 1, D_MODEL), lambda s: (bi(s)[0], 0, chunk))
    full = lambda a: pl.BlockSpec(a.shape, lambda s: (0,) * a.ndim)
    tri = jnp.tril(jnp.ones((tm, tm), F32), -1).astype(BF16)
    consts = (w["glu_w"], w["glu_b"], w["attn_g"], w["ssm_g"], w["w_out"], w["ln1_g"], w["ln1_b"], w["wr"], w["br"],
              tri, cnt0) + ((tail,) if has_tail else ())
    assert 6 + len(consts) == N_POST_INPUTS + int(has_tail)
    n_steps = n_body + int(has_tail)
    return pl.pallas_call(
        functools.partial(_post_kernel, has_tail=has_tail),
        out_shape=(jax.ShapeDtypeStruct((nb, l, D_MODEL), F32),
                   jax.ShapeDtypeStruct((n_steps * tm, D_MODEL), F32),
                   jax.ShapeDtypeStruct((nb, l, LANES), F32), jax.ShapeDtypeStruct((SUBLANES, LANES), F32)),
        grid=(n_steps,),
        in_specs=[row(D_ATTN), row(D_SSM), row(D_MODEL), mspec(2), mspec(3), mspec(4)] + [full(a) for a in consts],
        out_specs=(row(D_MODEL), pl.BlockSpec((tm, D_MODEL), lambda s: (s, 0)),
                   row(LANES), pl.BlockSpec((SUBLANES, LANES), lambda s: (0, 0))),
        compiler_params=_cparams(("arbitrary",)),
        name="post",
    )(attn, yssm, x, mod, mod, mod, *consts)


MOE_LOOKAHEAD = 2
MOE_SLOTS = MOE_LOOKAHEAD + 1


def _moe_kernel(te_ref, nv_ref, first_ref, nxt_ref, par_ref, rt0_ref, rt1_ref, rtn_ref, h2_ref, wg_hbm, wu_hbm, wd_hbm,
                y_ref, buf_ref, wgf_ref, wuf_ref, wdf_ref, wgb_ref, wub_ref, wdb_ref, sem, wsem):
    i = pl.program_id(0)
    nv = nv_ref[0]
    slot = lax.rem(i, MOE_SLOTS)
    w_pairs = ((wg_hbm, wgf_ref), (wu_hbm, wuf_ref), (wd_hbm, wdf_ref))

    def gather(ids_ref, dst_slot):
        _gather_rows(h2_ref, ids_ref, 1, 0, buf_ref.at[dst_slot], sem.at[dst_slot], (0,))

    def fetch_weights(e, ws):
        for src, dst in w_pairs:
            pltpu.make_async_copy(src.at[e], dst.at[ws], wsem.at[ws]).start(priority=1)

    @pl.when(i == 0)
    def _():
        gather(rt0_ref, 0)
        gather(rt1_ref, 1)
        fetch_weights(te_ref[0], par_ref[0])

    @pl.when(i < nv + MOE_LOOKAHEAD)
    def _():
        pltpu.make_async_copy(h2_ref.at[pl.ds(0, TM_MOE // SUBLANES)], buf_ref.at[slot], sem.at[slot]).wait()

    @pl.when((i < nv) & (first_ref[i] == 1))
    def _():
        ws = par_ref[i]
        for src, dst in w_pairs:
            pltpu.make_async_copy(src.at[0], dst.at[ws], wsem.at[ws]).wait()
        wgb_ref[...] = wgf_ref[ws].astype(BF16)
        wub_ref[...] = wuf_ref[ws].astype(BF16)
        wdb_ref[...] = wdf_ref[ws].astype(BF16)

        @pl.when(nxt_ref[i] >= 0)
        def _():
            fetch_weights(nxt_ref[i], 1 - ws)

    @pl.when(i < nv)
    def _():
        gather(rtn_ref, lax.rem(i + MOE_LOOKAHEAD, MOE_SLOTS))
        xb = buf_ref[slot].reshape(TM_MOE, D_MODEL).astype(BF16)
        hg = jnp.dot(xb, wgb_ref[...], preferred_element_type=F32)
        hu = jnp.dot(xb, wub_ref[...], preferred_element_type=F32)
        act = (hg * jax.nn.sigmoid(hg)) * hu
        y_ref[...] = jnp.dot(act.astype(BF16), wdb_ref[...], preferred_element_type=F32)

    @pl.when(i >= nv)
    def _():
        y_ref[...] = jnp.zeros(y_ref.shape, y_ref.dtype)


def _moe(tile_meta, n_valid, row_token, h2, w_gate, w_up, w_down):
    n_steps = row_token.shape[0] // TM_MOE
    rt = row_token.reshape(n_steps, 1, TM_MOE)
    te, first, nxt, par = tile_meta
    any_spec = pl.BlockSpec(memory_space=pl.ANY)
    return pl.pallas_call(
        _moe_kernel,
        out_shape=jax.ShapeDtypeStruct((n_steps * TM_MOE, D_MODEL), F32),
        grid_spec=pltpu.PrefetchScalarGridSpec(
            num_scalar_prefetch=5, grid=(n_steps,),
            in_specs=[*_index_specs(TM_MOE, 0, n_steps, MOE_LOOKAHEAD), any_spec, any_spec, any_spec, any_spec],
            out_specs=pl.BlockSpec((TM_MOE, D_MODEL), lambda i, *_: (i, 0)),
            scratch_shapes=[pltpu.VMEM((MOE_SLOTS, TM_MOE // SUBLANES, SUBLANES, D_MODEL), F32),
                            pltpu.VMEM((2, D_MODEL, D_EXPERT), F32), pltpu.VMEM((2, D_MODEL, D_EXPERT), F32),
                            pltpu.VMEM((2, D_EXPERT, D_MODEL), F32),
                            pltpu.VMEM((D_MODEL, D_EXPERT), BF16), pltpu.VMEM((D_MODEL, D_EXPERT), BF16),
                            pltpu.VMEM((D_EXPERT, D_MODEL), BF16),
                            pltpu.SemaphoreType.DMA((MOE_SLOTS,)), pltpu.SemaphoreType.DMA((2,))]),
        compiler_params=_cparams(("arbitrary",)),
        name="moe",
    )(te, n_valid, first, nxt, par, rt, rt, rt, h2.reshape(-1, SUBLANES, D_MODEL), w_gate, w_up, w_down)


def _final_kernel(pos0_ref, posn_ref, x1_ref, g2_ref, route_ref, l2g_ref, l2b_ref, ys_ref, o_ref, buf_ref, sem, *, tm):
    step = pl.program_id(0)
    slot = step & 1

    def gather(ids_ref, dst_slot):
        for k in range(2):
            _gather_rows(ys_ref, ids_ref, 2, k, buf_ref.at[dst_slot, k], sem.at[dst_slot], (0, 1))

    @pl.when(step == 0)
    def _():
        gather(pos0_ref, 0)

    for k in range(2):
        pltpu.make_async_copy(ys_ref.at[pl.ds(0, tm // SUBLANES)], buf_ref.at[slot, k], sem.at[slot]).wait()

    @pl.when(step + 1 < pl.num_programs(0))
    def _():
        gather(posn_ref, 1 - slot)

    route = route_ref[...]
    f = (route[:, 2:3] * buf_ref[slot, 0].reshape(tm, D_MODEL)
         + route[:, 3:4] * buf_ref[slot, 1].reshape(tm, D_MODEL))
    o_ref[...] = _ln(DEEPNORM_ALPHA * x1_ref[...] + g2_ref[...] * f) * l2g_ref[...] + l2b_ref[...]


def _final(pos_tiles, first_tile, x1, mod, per_row, route, ln2_g, ln2_b, ys):
    nb, l, _ = x1.shape
    tm = min(TM, l)
    nt = l // tm
    n_steps = nb * nt
    row = lambda wd: pl.BlockSpec((None, tm, wd), lambda s: (s // nt, s % nt, 0))
    if per_row:
        g2 = pl.BlockSpec((None, tm, D_MODEL), lambda s: (s // nt, s % nt, 5))
    else:
        g2 = pl.BlockSpec((None, 1, D_MODEL), lambda s: (s // nt, 0, 5))
    vec = pl.BlockSpec((1, D_MODEL), lambda s: (0, 0))
    return pl.pallas_call(
        functools.partial(_final_kernel, tm=tm),
        out_shape=jax.ShapeDtypeStruct((nb, l, D_MODEL), F32),
        grid=(n_steps,),
        in_specs=[*_index_specs(2 * tm, first_tile, n_steps), row(D_MODEL), g2, row(LANES), vec, vec,
                  pl.BlockSpec(memory_space=pl.ANY)],
        out_specs=row(D_MODEL),
        scratch_shapes=[pltpu.VMEM((2, 2, tm // SUBLANES, SUBLANES, D_MODEL), F32), pltpu.SemaphoreType.DMA((2,))],
        compiler_params=_cparams(("arbitrary",)),
        name="final",
    )(pos_tiles, pos_tiles, x1, mod, route, ln2_g, ln2_b, ys.reshape(-1, SUBLANES, D_MODEL))


def _rope_tables(pos):
    half = ROT_DIM // 2
    inv_freq = ROPE_THETA ** (-jnp.arange(half, dtype=jnp.float32) * 2.0 / ROT_DIM)
    ang = pos.astype(jnp.float32)[:, None] * inv_freq[None, :]
    cos, sin = jnp.cos(ang), jnp.sin(ang)
    n = pos.shape[0]
    one = jnp.ones((n, HEAD_DIM - ROT_DIM), F32)
    zero = jnp.zeros((n, HEAD_DIM - half), F32)
    c = jnp.concatenate([cos, cos, one], -1)
    a = jnp.concatenate([-sin, zero], -1)
    b = jnp.concatenate([jnp.zeros((n, half), F32), sin, jnp.zeros((n, HEAD_DIM - ROT_DIM), F32)], -1)
    return tuple(jnp.tile(t, (1, LANES // HEAD_DIM)) for t in (c, a, b))


def _s5_params(lam_re, lam_im, log_step, b_re, b_im, c_re, c_im):
    f32 = jnp.float32
    dt = jnp.exp(log_step.astype(f32))[:, None]
    lr, li = lam_re.astype(f32), lam_im.astype(f32)
    mag = jnp.exp(lr * dt)
    ar, ai = mag * jnp.cos(li * dt), mag * jnp.sin(li * dt)
    den = lr * lr + li * li
    cr = ((ar - 1.0) * lr + ai * li) / den
    ci = (ai * lr - (ar - 1.0) * li) / den
    br, bi = b_re.astype(f32), b_im.astype(f32)
    bbr = cr[..., None] * br - ci[..., None] * bi
    bbi = cr[..., None] * bi + ci[..., None] * br
    gpt = MXU_DIM // SSM_CH
    eye = jnp.eye(gpt, dtype=f32)

    def pack_b(m):
        m = m.reshape(N_KT, gpt, SSM_STATE, SSM_CH)
        return jnp.einsum("kgpc,gh->kgchp", m, eye).reshape(N_KT, MXU_DIM, ST_PER_KT).astype(BF16)

    def pack_c(m):
        m = m.astype(f32).reshape(N_KT, gpt, SSM_CH, SSM_STATE)
        return jnp.einsum("kgcp,gh->khpgc", m, eye).reshape(N_KT, ST_PER_KT, MXU_DIM).astype(BF16)

    wts = (pack_b(bbr), pack_b(bbi), pack_c(c_re), pack_c(-c_im.astype(f32)))
    return wts, ar.reshape(1, N_STATE), ai.reshape(1, N_STATE)


def _scan_tables(ar, ai, nb):
    assert SUBLANES // nb == 2
    a2r, a2i = ar * ar - ai * ai, 2.0 * ar * ai
    z = jnp.zeros_like(ar)
    rep = lambda first, second: jnp.concatenate([jnp.tile(first, (nb, 1)), jnp.tile(second, (nb, 1))], 0)
    return rep(z, ar), rep(z, ai), rep(ar, a2r), rep(ai, a2i)


def kernel(x_prompt, x_sample, cache_k_win, cache_v_win, state_ssm_re, state_ssm_im, c_prompt, c_sample, ada_w, ada_b,
           w_in, attn_sinks, ssm_lambda_re, ssm_lambda_im, ssm_log_step, ssm_b_re, ssm_b_im, ssm_c_re, ssm_c_im, ssm_d,
           ssm_glu_w, ssm_glu_b, attn_norm_g, ssm_norm_g, w_out, ln1_g, ln1_b, router_group_w, router_group_b,
           router_expert_w, router_expert_b, exp_w_gate, exp_w_up, exp_w_down, ln2_g, ln2_b):
    assert ada_w.shape[0] == DEPTH
    bp, lp, _ = x_prompt.shape
    bs, ls, _ = x_sample.shape
    lsp = SAMPLE_PAD

    w_in_bf = w_in[0].astype(BF16)
    wr = jnp.concatenate([router_group_w[0], router_expert_w[0]], -1)
    wr = jnp.pad(wr, ((0, 0), (0, LANES - wr.shape[1]))).astype(BF16)
    br = jnp.pad(jnp.concatenate([router_group_b[0], router_expert_b[0]], -1), (0, LANES - N_EXPERT_GROUPS - N_EXPERTS))
    wpost = dict(glu_w=ssm_glu_w[0].astype(BF16), glu_b=ssm_glu_b[0][None], attn_g=attn_norm_g[0][None],
                 ssm_g=ssm_norm_g[0][None], w_out=w_out[0].astype(BF16), ln1_g=ln1_g[0][None], ln1_b=ln1_b[0][None],
                 wr=wr, br=br[None])
    s5w, ar, ai = _s5_params(ssm_lambda_re[0], ssm_lambda_im[0], ssm_log_step[0], ssm_b_re[0], ssm_b_im[0],
                             ssm_c_re[0], ssm_c_im[0])
    d_skip = ssm_d[0][None]
    sinks = attn_sinks[0]

    n_c = bp + bs
    n_cp = -(-n_c // SUBLANES) * SUBLANES
    c_all = jnp.pad(jnp.concatenate([c_prompt, c_sample], 0), ((0, n_cp - n_c), (0, 0)))
    mod = _ada(c_all, ada_w[0], ada_b[0][None])
    mod_p = mod[:bp][:, None, :]
    mod_s = jnp.repeat(mod[bp:n_c], lsp, axis=0)[None]

    xs_pad = jnp.pad(x_sample, ((0, 0), (0, lsp - ls), (0, 0))).reshape(1, bs * lsp, D_MODEL)

    rope_p = _rope_tables(jnp.arange(lp))
    pos_s = PAST_LEN + jnp.minimum(jnp.arange(lsp), ls - 1)
    rope_s = tuple(jnp.tile(t, (bs, 1)) for t in _rope_tables(pos_s))
    q_p, k_p, v_p, u_p = _inproj(x_prompt, mod_p, w_in_bf, rope_p, False)
    q_s, k_s, v_s, u_s = _inproj(xs_pad, mod_s, w_in_bf, rope_s, True)

    attn_p = _attn_prompt(sinks, q_p, k_p, v_p)
    k_s3 = k_s.reshape(bs, lsp, D_KV)
    v_s3 = v_s.reshape(bs, lsp, D_KV)
    ck = cache_k_win[0].reshape(bs, WINDOW, D_KV)
    cv = cache_v_win[0].reshape(bs, WINDOW, D_KV)
    padk = lambda new: jnp.pad(new, ((0, 0), (0, WINDOW - lsp), (0, 0)))
    attn_s = _attn_sample(sinks, q_s.reshape(bs, lsp, D_ATTN), jnp.concatenate([ck, padk(k_s3)], 1),
                          jnp.concatenate([cv, padk(v_s3)], 1))

    tabs = _scan_tables(ar, ai, bp)
    zero_carry = jnp.zeros((SUBLANES, N_STATE), F32)
    y_p, hr_p, hi_p = _s5_prompt(u_p, s5w, tabs, d_skip, zero_carry, zero_carry)
    u_tb = jnp.transpose(u_s.reshape(bs, lsp, D_SSM)[:, :ls], (1, 0, 2))
    y_tb, hr_s, hi_s = _s5_sample(u_tb, s5w, ar, ai, d_skip, state_ssm_re[0].reshape(bs, N_STATE),
                                  state_ssm_im[0].reshape(bs, N_STATE))
    y_s = jnp.pad(jnp.transpose(y_tb, (1, 0, 2)), ((0, 0), (0, lsp - ls), (0, 0))).reshape(1, bs * lsp, D_SSM)

    cnt0 = jnp.zeros((SUBLANES, LANES), F32)
    n_p, n_s = bp * lp, bs * lsp
    n_tok = n_p + n_s
    x1_s, h2_s, route_s, cnt_s = _post(attn_s.reshape(1, bs * lsp, D_ATTN), y_s, xs_pad, mod_s, True, wpost, cnt0)
    x1_p, h2, route_p, cnt = _post(attn_p, y_p, x_prompt, mod_p, False, wpost, cnt_s, tail=h2_s)

    n_tiles = -(-(2 * n_tok + N_EXPERTS * (TM_MOE - 1)) // TM_MOE) + MOE_LOOKAHEAD
    route = jnp.concatenate([route_p.reshape(n_p, LANES), route_s.reshape(n_s, LANES)], 0)
    counts = cnt[0, ROUTE_LANE0:ROUTE_LANE0 + N_EXPERTS].astype(jnp.int32)
    padded = ((counts + TM_MOE - 1) // TM_MOE) * TM_MOE
    ends = jnp.cumsum(padded)
    offs = ends - padded
    eid = route[:, 0:2].astype(jnp.int32)
    pos = (offs[eid] + route[:, 4:6].astype(jnp.int32)).reshape(-1)
    n_valid = ends[-1] // TM_MOE
    tile_row = jnp.minimum(jnp.arange(n_tiles), n_valid - 1) * TM_MOE
    tile_expert = jnp.sum((ends[None, :] <= tile_row[:, None]).astype(jnp.int32), axis=1)
    first_tile = jnp.concatenate([jnp.ones((1,), jnp.int32), (tile_expert[1:] != tile_expert[:-1]).astype(jnp.int32)])
    first_tile = first_tile * (jnp.arange(n_tiles) < n_valid)
    nonempty = padded > 0
    expert_ids = jnp.arange(N_EXPERTS, dtype=jnp.int32)
    later = jnp.where(nonempty[None, :] & (expert_ids[None, :] > expert_ids[:, None]), expert_ids[None, :], N_EXPERTS)
    next_expert = jnp.min(later, axis=1)
    next_expert = jnp.where(next_expert < N_EXPERTS, next_expert, -1).astype(jnp.int32)
    weight_slot = ((jnp.cumsum(nonempty.astype(jnp.int32)) - 1) & 1).astype(jnp.int32)
    tile_meta = (tile_expert, first_tile.astype(jnp.int32), next_expert[tile_expert], weight_slot[tile_expert])
    token_of = jnp.repeat(jnp.arange(n_tok, dtype=jnp.int32), 2)
    row_token = (jnp.arange(n_tiles * TM_MOE, dtype=jnp.int32) % n_tok).at[pos].set(token_of)

    ys = _moe(tile_meta, n_valid.reshape(1).astype(jnp.int32), row_token, h2, exp_w_gate[0], exp_w_up[0],
              exp_w_down[0])
    pos_tiles = pos.reshape(n_tok // TM, 1, 2 * TM)
    y_prompt = _final(pos_tiles, 0, x1_p, mod_p, False, route_p, ln2_g[0][None], ln2_b[0][None], ys)
    y_samp = _final(pos_tiles, n_p // TM, x1_s, mod_s, True, route_s, ln2_g[0][None], ln2_b[0][None], ys)
    y_sample = y_samp.reshape(bs, lsp, D_MODEL)[:, :ls]

    kv5 = lambda a, n: a.reshape(1, a.shape[0], n, N_KV_HEADS, HEAD_DIM)
    k_win_p = kv5(k_p[:, lp - WINDOW:], WINDOW)
    v_win_p = kv5(v_p[:, lp - WINDOW:], WINDOW)
    k_win_s = kv5(jnp.concatenate([ck[:, ls:], k_s3[:, :ls]], 1), WINDOW)
    v_win_s = kv5(jnp.concatenate([cv[:, ls:], v_s3[:, :ls]], 1), WINDOW)
    st = lambda a: a.reshape(1, a.shape[0], N_SSM_GROUPS, SSM_STATE)
    return (y_prompt, y_sample, k_win_p, v_win_p, st(hr_p[bp:2 * bp]), st(hi_p[bp:2 * bp]),
            k_win_s, v_win_s, st(hr_s), st(hi_s))
```

```python
import functools
import math

import jax
import jax.numpy as jnp
from jax import lax
from jax.experimental import pallas as pl
from jax.experimental.pallas import tpu as pltpu

F32 = jnp.float32
BF16 = jnp.bfloat16

D_MODEL = 2048
D_ATTN = 1024
D_SSM = 1024
HEAD_DIM = 64
N_HEADS = 16
N_KV_HEADS = 4
REP = 4
D_KV = 256
ROT_DIM = 16
ROPE_THETA = 500000.0
WINDOW = 128
SSM_CH = 16
N_SSM_GROUPS = 64
SSM_STATE = 64
N_STATE = N_SSM_GROUPS * SSM_STATE
PROJ_COLS = D_ATTN + 2 * D_KV + D_SSM
N_EXPERT_GROUPS = 4
EXPERTS_PER_GROUP = 8
N_EXPERTS = 32
D_EXPERT = 512
DEPTH = 1
DEEPNORM_ALPHA = (2.0 * DEPTH) ** 0.25
LN_EPS = 1e-5
PAST_LEN = 16384

LANES = 128
SUBLANES = 8
MXU_DIM = 256
TM = 256
TM_MOE = 256
S5_TT = 64
SAMPLE_PAD = 8
ROUTE_LANE0 = N_EXPERT_GROUPS
VMEM_LIMIT = 56 * 1024 * 1024


def _cparams(sem):
    return pltpu.CompilerParams(dimension_semantics=sem, vmem_limit_bytes=VMEM_LIMIT)


def _ln(x):
    mu = jnp.mean(x, axis=-1, keepdims=True)
    xc = x - mu
    var = jnp.mean(xc * xc, axis=-1, keepdims=True)
    return xc * lax.rsqrt(var + LN_EPS)


def _rms(x):
    return x * lax.rsqrt(jnp.mean(x * x, axis=-1, keepdims=True) + LN_EPS)


def _ada_kernel(c_ref, w_ref, b_ref, o_ref):
    c = c_ref[...]
    s = c * jax.nn.sigmoid(c)
    o_ref[...] = jnp.dot(s.astype(BF16), w_ref[...].astype(BF16), preferred_element_type=F32) + b_ref[...]


def _ada(c_all, ada_w, ada_b):
    n, tn = c_all.shape[0], 1024
    return pl.pallas_call(
        _ada_kernel,
        out_shape=jax.ShapeDtypeStruct((n, 6 * D_MODEL), F32),
        grid=(6 * D_MODEL // tn,),
        in_specs=[pl.BlockSpec((n, D_MODEL), lambda j: (0, 0)),
                  pl.BlockSpec((D_MODEL, tn), lambda j: (0, j)),
                  pl.BlockSpec((1, tn), lambda j: (0, j))],
        out_specs=pl.BlockSpec((n, tn), lambda j: (0, j)),
        compiler_params=_cparams(("arbitrary",)),
        name="ada",
    )(c_all, ada_w, ada_b)


def _mod_spec(per_row, tm, chunk):
    if per_row:
        return pl.BlockSpec((None, tm, D_MODEL), lambda b, i: (b, i, chunk))
    return pl.BlockSpec((None, 1, D_MODEL), lambda b, i: (b, 0, chunk))


def _inproj_kernel(x_ref, sh_ref, sc_ref, w_ref, rc_ref, ra_ref, rb_ref, q_ref, k_ref, v_ref, u_ref):
    h = _ln(x_ref[...]) * (1.0 + sc_ref[...]) + sh_ref[...]
    proj = jnp.dot(h.astype(BF16), w_ref[...], preferred_element_type=F32)
    rc, ra, rb = rc_ref[...], ra_ref[...], rb_ref[...]

    def rope(t):
        return t * rc + pltpu.roll(t, LANES - ROT_DIM // 2, 1) * ra + pltpu.roll(t, ROT_DIM // 2, 1) * rb

    for j in range(D_ATTN // LANES):
        q_ref[:, j * LANES:(j + 1) * LANES] = (rope(proj[:, j * LANES:(j + 1) * LANES]) * HEAD_DIM ** -0.5).astype(BF16)
    for j in range(D_KV // LANES):
        c0 = D_ATTN + j * LANES
        k_ref[:, j * LANES:(j + 1) * LANES] = rope(proj[:, c0:c0 + LANES])
    v_ref[...] = proj[:, D_ATTN + D_KV:D_ATTN + 2 * D_KV]
    u_ref[...] = proj[:, D_ATTN + 2 * D_KV:]


def _inproj(x, mod, w_in_bf, rope_tabs, per_row):
    nb, l, _ = x.shape
    tm = min(TM, l)
    row = lambda w: pl.BlockSpec((None, tm, w), lambda b, i: (b, i, 0))
    tab = pl.BlockSpec((tm, LANES), lambda b, i: (i, 0))
    return pl.pallas_call(
        _inproj_kernel,
        out_shape=(jax.ShapeDtypeStruct((nb, l, D_ATTN), BF16), jax.ShapeDtypeStruct((nb, l, D_KV), F32),
                   jax.ShapeDtypeStruct((nb, l, D_KV), F32), jax.ShapeDtypeStruct((nb, l, D_SSM), F32)),
        grid=(nb, l // tm),
        in_specs=[row(D_MODEL), _mod_spec(per_row, tm, 0), _mod_spec(per_row, tm, 1),
                  pl.BlockSpec((D_MODEL, PROJ_COLS), lambda b, i: (0, 0)), tab, tab, tab],
        out_specs=(row(D_ATTN), row(D_KV), row(D_KV), row(D_SSM)),
        compiler_params=_cparams(("arbitrary", "arbitrary")),
        name="inproj",
    )(x, mod, mod, w_in_bf, *rope_tabs)


ATTN_SAMPLE_BATCH = 8


def _attn_kernel(sink_ref, q_ref, kp_ref, kc_ref, vp_ref, vc_ref, o_ref, *, lq, prev_from_block):
    m_rows = REP * lq
    ii = lax.broadcasted_iota(jnp.int32, (m_rows, WINDOW), 0) & (lq - 1)
    jj = lax.broadcasted_iota(jnp.int32, (m_rows, WINDOW), 1)
    from_prev = jj > ii
    if prev_from_block:
        dead = jj > ii + jnp.where(pl.program_id(1) > 0, WINDOW, 0)
    rr = lax.broadcasted_iota(jnp.int32, (m_rows, 1), 0)
    dn = (((1,), (1,)), ((), ()))
    for bi in range(q_ref.shape[0]):
        for g in range(N_KV_HEADS):
            qg = q_ref[bi, :, g * REP * HEAD_DIM:(g + 1) * REP * HEAD_DIM].astype(F32)
            qs = jnp.concatenate([qg[:, r * HEAD_DIM:(r + 1) * HEAD_DIM] for r in range(REP)], axis=0).astype(BF16)
            ks = slice(g * HEAD_DIM, (g + 1) * HEAD_DIM)
            s_p = lax.dot_general(qs, kp_ref[bi, :, ks].astype(BF16), dn, preferred_element_type=F32)
            s_c = lax.dot_general(qs, kc_ref[bi, :, ks].astype(BF16), dn, preferred_element_type=F32)
            s = jnp.where(from_prev, s_p, s_c)
            if prev_from_block:
                s = jnp.where(dead, -jnp.inf, s)
            sink = jnp.zeros((m_rows, 1), F32)
            for r in range(REP):
                sink = jnp.where((rr >= r * lq) & (rr < (r + 1) * lq), sink_ref[g * REP + r], sink)
            m = jnp.maximum(jnp.max(s, axis=-1, keepdims=True), sink)
            p = jnp.exp(s - m)
            p = p / (jnp.sum(p, axis=-1, keepdims=True) + jnp.exp(sink - m))
            o = (jnp.dot(jnp.where(from_prev, p, 0.0).astype(BF16), vp_ref[bi, :, ks].astype(BF16),
                         preferred_element_type=F32)
                 + jnp.dot(jnp.where(from_prev, 0.0, p).astype(BF16), vc_ref[bi, :, ks].astype(BF16),
                           preferred_element_type=F32))
            for r in range(REP):
                h = g * REP + r
                o_ref[bi, :, h * HEAD_DIM:(h + 1) * HEAD_DIM] = o[r * lq:(r + 1) * lq]


def _attn_prompt(sinks, q, k, v):
    nb, l, _ = q.shape
    nblk = l // WINDOW
    cur = lambda w: pl.BlockSpec((1, WINDOW, w), lambda b, n: (b, n, 0))
    prev = lambda w: pl.BlockSpec((1, WINDOW, w), lambda b, n: (b, jnp.maximum(n - 1, 0), 0))
    return pl.pallas_call(
        functools.partial(_attn_kernel, lq=WINDOW, prev_from_block=True),
        out_shape=jax.ShapeDtypeStruct((nb, l, D_ATTN), F32),
        grid=(nb, nblk),
        in_specs=[pl.BlockSpec(memory_space=pltpu.SMEM), cur(D_ATTN), prev(D_KV), cur(D_KV), prev(D_KV), cur(D_KV)],
        out_specs=cur(D_ATTN),
        compiler_params=_cparams(("arbitrary", "arbitrary")),
        name="attn_prompt",
    )(sinks, q, k, k, v, v)


def _attn_sample(sinks, q, kcat, vcat):
    nb, lq, _ = q.shape
    nbb = ATTN_SAMPLE_BATCH
    kblk = lambda n: pl.BlockSpec((nbb, WINDOW, D_KV), lambda b: (b, n, 0))
    return pl.pallas_call(
        functools.partial(_attn_kernel, lq=lq, prev_from_block=False),
        out_shape=jax.ShapeDtypeStruct((nb, lq, D_ATTN), F32),
        grid=(nb // nbb,),
        in_specs=[pl.BlockSpec(memory_space=pltpu.SMEM), pl.BlockSpec((nbb, lq, D_ATTN), lambda b: (b, 0, 0)),
                  kblk(0), kblk(1), kblk(0), kblk(1)],
        out_specs=pl.BlockSpec((nbb, lq, D_ATTN), lambda b: (b, 0, 0)),
        compiler_params=_cparams(("arbitrary",)),
        name="attn_sample",
    )(sinks, q, kcat, kcat, vcat, vcat)


N_KT = D_SSM // MXU_DIM
ST_PER_KT = N_STATE // N_KT
N_SLAB = D_SSM // LANES


def _cproj(hr_ref, hi_ref, cre_ref, cimn_ref, kt):
    cs = slice(kt * ST_PER_KT, (kt + 1) * ST_PER_KT)
    return (jnp.dot(hr_ref[:, cs].astype(BF16), cre_ref[kt], preferred_element_type=F32)
            + jnp.dot(hi_ref[:, cs].astype(BF16), cimn_ref[kt], preferred_element_type=F32))


def _s5_prompt_kernel(u_ref, bre_ref, bim_ref, cre_ref, cimn_ref, a1r_ref, a1i_ref, par_ref, pai_ref, d_ref,
                      h0r_ref, h0i_ref, y_ref, cr_ref, ci_ref, il_ref, xr_ref, xi_ref, *, nb, tt):
    rows = nb * tt

    @pl.when(pl.program_id(0) == 0)
    def _():
        cr_ref[...] = h0r_ref[...]
        ci_ref[...] = h0i_ref[...]

    for b in range(nb):
        for j in range(N_SLAB):
            il_ref[j, pl.ds(b, tt, stride=nb), :] = u_ref[b, :, j * LANES:(j + 1) * LANES]
    for kt in range(N_KT):
        ub = jnp.concatenate([il_ref[2 * kt], il_ref[2 * kt + 1]], axis=1).astype(BF16)
        cs = slice(kt * ST_PER_KT, (kt + 1) * ST_PER_KT)
        xr_ref[:, cs] = jnp.dot(ub, bre_ref[kt], preferred_element_type=F32)
        xi_ref[:, cs] = jnp.dot(ub, bim_ref[kt], preferred_element_type=F32)

    half = lax.broadcasted_iota(jnp.int32, (SUBLANES, LANES), 0) < nb

    def step(i, carry):
        r0 = pl.multiple_of(i * SUBLANES, SUBLANES)
        for j in range(N_STATE // LANES):
            ls = slice(j * LANES, (j + 1) * LANES)
            x_r = xr_ref[pl.ds(r0, SUBLANES), ls]
            x_i = xi_ref[pl.ds(r0, SUBLANES), ls]
            s_r = pltpu.roll(x_r, nb, 0)
            s_i = pltpu.roll(x_i, nb, 0)
            a1r, a1i = a1r_ref[:, ls], a1i_ref[:, ls]
            c_r, c_i = cr_ref[:, ls], ci_ref[:, ls]
            par, pai = par_ref[:, ls], pai_ref[:, ls]
            h_r = x_r + (a1r * s_r - a1i * s_i) + (par * c_r - pai * c_i)
            h_i = x_i + (a1r * s_i + a1i * s_r) + (par * c_i + pai * c_r)
            xr_ref[pl.ds(r0, SUBLANES), ls] = h_r
            xi_ref[pl.ds(r0, SUBLANES), ls] = h_i
            cr_ref[:, ls] = jnp.where(half, pltpu.roll(h_r, nb, 0), h_r)
            ci_ref[:, ls] = jnp.where(half, pltpu.roll(h_i, nb, 0), h_i)
        return carry

    lax.fori_loop(0, rows // SUBLANES, step, 0)

    for kt in range(N_KT):
        y = _cproj(xr_ref, xi_ref, cre_ref, cimn_ref, kt)
        il_ref[2 * kt] = y[:, :LANES]
        il_ref[2 * kt + 1] = y[:, LANES:]
    for b in range(nb):
        for j in range(N_SLAB):
            ls = slice(j * LANES, (j + 1) * LANES)
            y_ref[b, :, ls] = il_ref[j, pl.ds(b, tt, stride=nb), :] + d_ref[:, ls] * u_ref[b, :, ls]


def _s5_prompt(u, wts, tabs, d_skip, h0r, h0i):
    nb, l, _ = u.shape
    tt = S5_TT
    rows = nb * tt
    full = lambda a: pl.BlockSpec(a.shape, lambda i: (0,) * a.ndim)
    blk = pl.BlockSpec((nb, tt, D_SSM), lambda i: (0, i, 0))
    carry = jax.ShapeDtypeStruct((SUBLANES, N_STATE), F32)
    return pl.pallas_call(
        functools.partial(_s5_prompt_kernel, nb=nb, tt=tt),
        out_shape=(jax.ShapeDtypeStruct((nb, l, D_SSM), F32), carry, carry),
        grid=(l // tt,),
        in_specs=[blk] + [full(a) for a in (*wts, *tabs, d_skip, h0r, h0i)],
        out_specs=(blk, pl.BlockSpec((SUBLANES, N_STATE), lambda i: (0, 0)),
                   pl.BlockSpec((SUBLANES, N_STATE), lambda i: (0, 0))),
        scratch_shapes=[pltpu.VMEM((N_SLAB, rows, LANES), F32), pltpu.VMEM((rows, N_STATE), F32),
                        pltpu.VMEM((rows, N_STATE), F32)],
        compiler_params=_cparams(("arbitrary",)),
        name="s5_prompt",
    )(u, *wts, *tabs, d_skip, h0r, h0i)


def _s5_sample_kernel(u_ref, bre_ref, bim_ref, cre_ref, cimn_ref, ar_ref, ai_ref, d_ref, h0r_ref, h0i_ref,
                      y_ref, sr_ref, si_ref, xr_ref, xi_ref, *, nt):
    sr_ref[...] = h0r_ref[...]
    si_ref[...] = h0i_ref[...]
    for t in range(nt):
        for kt in range(N_KT):
            ub = u_ref[t, :, kt * MXU_DIM:(kt + 1) * MXU_DIM].astype(BF16)
            cs = slice(kt * ST_PER_KT, (kt + 1) * ST_PER_KT)
            xr_ref[:, cs] = jnp.dot(ub, bre_ref[kt], preferred_element_type=F32)
            xi_ref[:, cs] = jnp.dot(ub, bim_ref[kt], preferred_element_type=F32)
        ar, ai = ar_ref[...], ai_ref[...]
        s_r, s_i = sr_ref[...], si_ref[...]
        sr_ref[...] = xr_ref[...] + (ar * s_r - ai * s_i)
        si_ref[...] = xi_ref[...] + (ar * s_i + ai * s_r)
        for kt in range(N_KT):
            ys = slice(kt * MXU_DIM, (kt + 1) * MXU_DIM)
            y_ref[t, :, ys] = _cproj(sr_ref, si_ref, cre_ref, cimn_ref, kt) + d_ref[:, ys] * u_ref[t, :, ys]


def _s5_sample(u_tb, wts, ar, ai, d_skip, h0r, h0i):
    nt, nb, _ = u_tb.shape
    st = jax.ShapeDtypeStruct((nb, N_STATE), F32)
    args = (u_tb, *wts, ar, ai, d_skip, h0r, h0i)
    full = lambda a: pl.BlockSpec(a.shape, lambda i: (0,) * a.ndim)
    return pl.pallas_call(
        functools.partial(_s5_sample_kernel, nt=nt),
        out_shape=(jax.ShapeDtypeStruct((nt, nb, D_SSM), F32), st, st),
        grid=(1,),
        in_specs=[full(a) for a in args],
        out_specs=(pl.BlockSpec((nt, nb, D_SSM), lambda i: (0, 0, 0)), pl.BlockSpec((nb, N_STATE), lambda i: (0, 0)),
                   pl.BlockSpec((nb, N_STATE), lambda i: (0, 0))),
        scratch_shapes=[pltpu.VMEM((nb, N_STATE), F32), pltpu.VMEM((nb, N_STATE), F32)],
        compiler_params=_cparams(("arbitrary",)),
        name="s5_sample",
    )(*args)


def _gather_group(src_ref, ids_ref, id_stride, id_offset, dst_ref, sem, priorities, g):
    for j in range(SUBLANES):
        t = ids_ref[0, id_stride * (g * SUBLANES + j) + id_offset]
        pltpu.make_async_copy(src_ref.at[t >> 3, pl.ds(t & (SUBLANES - 1), 1)], dst_ref.at[g, pl.ds(j, 1)],
                              sem).start(priority=priorities[j % len(priorities)])


def _gather_rows(src_ref, ids_ref, id_stride, id_offset, dst_ref, sem, priorities):
    def body(g, c):
        _gather_group(src_ref, ids_ref, id_stride, id_offset, dst_ref, sem, priorities, g)
        return c
    lax.fori_loop(0, dst_ref.shape[0], body, 0)


def _index_specs(n_cols, first_tile, n_steps, depth=1):
    def spec(fn):
        return pl.BlockSpec((None, 1, n_cols), fn, memory_space=pltpu.SMEM)

    head = [spec(lambda s, *_, k=k: (first_tile + min(k, n_steps - 1), 0, 0)) for k in range(depth)]
    return (*head, spec(lambda s, *_: (first_tile + jnp.minimum(s + depth, n_steps - 1), 0, 0)))


N_POST_INPUTS = 17
POST_ROW_PARTS = 1


def _post_kernel(*refs, has_tail):
    ins, outs = refs[:N_POST_INPUTS], refs[N_POST_INPUTS + int(has_tail):]
    if not has_tail:
        _post_body(*ins, *outs)
        return
    tail_ref, h2_ref = refs[N_POST_INPUTS], outs[1]
    last = pl.num_programs(0) - 1

    @pl.when(pl.program_id(0) == last)
    def _():
        h2_ref[...] = tail_ref[...]

    @pl.when(pl.program_id(0) < last)
    def _():
        _post_body(*ins, *outs)


def _post_body(attn_ref, yssm_ref, x_ref, g1_ref, sh2_ref, sc2_ref, gluw_ref, glub_ref, ga_ref, gs_ref, wout_ref,
               l1g_ref, l1b_ref, wr_ref, br_ref, tri_ref, cnt0_ref, x1_ref, h2_ref, route_ref, cnt_ref):
    tm = x_ref.shape[0]

    @pl.when(pl.program_id(0) == 0)
    def _():
        cnt_ref[...] = cnt0_ref[...]

    hm = tm // POST_ROW_PARTS
    lane = lax.broadcasted_iota(jnp.int32, (hm, LANES), 1).astype(F32)
    big = float(4 * LANES)
    neg = -jnp.inf
    cnt = cnt_ref[0:1, :]
    for part in range(POST_ROW_PARTS):
        rs = slice(part * hm, (part + 1) * hm)
        rows = lambda ref: ref[rs, :] if ref.shape[0] == tm else ref[...]
        z = jax.nn.gelu(yssm_ref[rs, :])
        ssm = z * jax.nn.sigmoid(jnp.dot(z.astype(BF16), gluw_ref[...], preferred_element_type=F32) + glub_ref[...])
        mixed_a = (_rms(attn_ref[rs, :]) * ga_ref[...]).astype(BF16)
        mixed_s = (_rms(ssm) * gs_ref[...]).astype(BF16)
        o = (jnp.dot(mixed_a, wout_ref[:D_ATTN, :], preferred_element_type=F32)
             + jnp.dot(mixed_s, wout_ref[D_ATTN:, :], preferred_element_type=F32))
        x1 = _ln(DEEPNORM_ALPHA * x_ref[rs, :] + rows(g1_ref) * o) * l1g_ref[...] + l1b_ref[...]
        x1_ref[rs, :] = x1
        h2 = _ln(x1) * (1.0 + rows(sc2_ref)) + rows(sh2_ref)
        h2_ref[rs, :] = h2
        logits = jnp.dot(h2.astype(BF16), wr_ref[...], preferred_element_type=F32) + br_ref[...]

        gl = jnp.where(lane < N_EXPERT_GROUPS, logits, neg)
        gp = jnp.exp(gl - jnp.max(gl, axis=-1, keepdims=True))
        gp = gp / jnp.sum(gp, axis=-1, keepdims=True)
        g_val = jnp.max(gp, axis=-1, keepdims=True)
        g_idx = jnp.min(jnp.where(gp == g_val, lane, big), axis=-1, keepdims=True)
        lo = ROUTE_LANE0 + EXPERTS_PER_GROUP * g_idx
        emask = (lane >= lo) & (lane < lo + EXPERTS_PER_GROUP)
        el = jnp.where(emask, logits, neg)
        ep = jnp.exp(el - jnp.max(el, axis=-1, keepdims=True))
        ep = jnp.where(emask, ep / jnp.sum(ep, axis=-1, keepdims=True), -1.0)
        v1 = jnp.max(ep, axis=-1, keepdims=True)
        i1 = jnp.min(jnp.where(ep == v1, lane, big), axis=-1, keepdims=True)
        ep2 = jnp.where(lane == i1, -1.0, ep)
        v2 = jnp.max(ep2, axis=-1, keepdims=True)
        i2 = jnp.min(jnp.where(ep2 == v2, lane, big), axis=-1, keepdims=True)
        vs = v1 + v2
        w1 = g_val * (v1 / vs)
        w2 = g_val * (v2 / vs)
        hit = jnp.where((lane == i1) | (lane == i2), 1.0, 0.0)
        before = jnp.dot(tri_ref[:hm, :hm], hit.astype(BF16), preferred_element_type=F32) + cnt
        r1 = jnp.sum(jnp.where(lane == i1, before, 0.0), axis=-1, keepdims=True)
        r2 = jnp.sum(jnp.where(lane == i2, before, 0.0), axis=-1, keepdims=True)
        cnt = cnt + jnp.sum(hit, axis=0, keepdims=True)
        e1 = i1 - ROUTE_LANE0
        e2 = i2 - ROUTE_LANE0
        route = jnp.zeros((hm, LANES), F32)
        for n, val in enumerate((e1, e2, w1, w2, r1, r2)):
            route = jnp.where(lane == n, val, route)
        route_ref[rs, :] = route
    cnt_ref[...] = jnp.broadcast_to(cnt, cnt_ref.shape)


def _post(attn, yssm, x, mod, per_row, w, cnt0, tail=None):
    nb, l, _ = x.shape
    tm = min(TM, l)
    nt = l // tm
    n_body = nb * nt
    has_tail = tail is not None
    if has_tail:
        assert tail.shape == (tm, D_MODEL)

    def bi(s):
        s = jnp.minimum(s, n_body - 1)
        return s // nt, s % nt

    row = lambda wd: pl.BlockSpec((None, tm, wd), lambda s: (*bi(s), 0))
    if per_row:
        mspec = lambda chunk: pl.BlockSpec((None, tm, D_MODEL), lambda s: (*bi(s), chunk))
    else:
        mspec = lambda chunk: pl.BlockSpec((None, 1, D_MODEL), lambda s: (bi(s)[0], 0, chunk))
    full = lambda a: pl.BlockSpec(a.shape, lambda s: (0,) * a.ndim)
    tri = jnp.tril(jnp.ones((tm, tm), F32), -1).astype(BF16)
    consts = (w["glu_w"], w["glu_b"], w["attn_g"], w["ssm_g"], w["w_out"], w["ln1_g"], w["ln1_b"], w["wr"], w["br"],
              tri, cnt0) + ((tail,) if has_tail else ())
    assert 6 + len(consts) == N_POST_INPUTS + int(has_tail)
    n_steps = n_body + int(has_tail)
    return pl.pallas_call(
        functools.partial(_post_kernel, has_tail=has_tail),
        out_shape=(jax.ShapeDtypeStruct((nb, l, D_MODEL), F32),
                   jax.ShapeDtypeStruct((n_steps * tm, D_MODEL), F32),
                   jax.ShapeDtypeStruct((nb, l, LANES), F32), jax.ShapeDtypeStruct((SUBLANES, LANES), F32)),
        grid=(n_steps,),
        in_specs=[row(D_ATTN), row(D_SSM), row(D_MODEL), mspec(2), mspec(3), mspec(4)] + [full(a) for a in consts],
        out_specs=(row(D_MODEL), pl.BlockSpec((tm, D_MODEL), lambda s: (s, 0)),
                   row(LANES), pl.BlockSpec((SUBLANES, LANES), lambda s: (0, 0))),
        compiler_params=_cparams(("arbitrary",)),
        name="post",
    )(attn, yssm, x, mod, mod, mod, *consts)


MOE_LOOKAHEAD = 2
MOE_SLOTS = MOE_LOOKAHEAD + 1


def _moe_kernel(te_ref, nv_ref, first_ref, nxt_ref, par_ref, rt0_ref, rt1_ref, rtn_ref, h2_ref, wg_hbm, wu_hbm, wd_hbm,
                y_ref, buf_ref, wgf_ref, wuf_ref, wdf_ref, wgb_ref, wub_ref, wdb_ref, sem, wsem):
    i = pl.program_id(0)
    nv = nv_ref[0]
    slot = lax.rem(i, MOE_SLOTS)
    w_pairs = ((wg_hbm, wgf_ref), (wu_hbm, wuf_ref), (wd_hbm, wdf_ref))

    def gather(ids_ref, dst_slot):
        _gather_rows(h2_ref, ids_ref, 1, 0, buf_ref.at[dst_slot], sem.at[dst_slot], (0,))

    def fetch_weights(e, ws):
        for src, dst in w_pairs:
            pltpu.make_async_copy(src.at[e], dst.at[ws], wsem.at[ws]).start(priority=1)

    @pl.when(i == 0)
    def _():
        gather(rt0_ref, 0)
        gather(rt1_ref, 1)
        fetch_weights(te_ref[0], par_ref[0])

    @pl.when(i < nv + MOE_LOOKAHEAD)
    def _():
        pltpu.make_async_copy(h2_ref.at[pl.ds(0, TM_MOE // SUBLANES)], buf_ref.at[slot], sem.at[slot]).wait()

    @pl.when((i < nv) & (first_ref[i] == 1))
    def _():
        ws = par_ref[i]
        for src, dst in w_pairs:
            pltpu.make_async_copy(src.at[0], dst.at[ws], wsem.at[ws]).wait()
        wgb_ref[...] = wgf_ref[ws].astype(BF16)
        wub_ref[...] = wuf_ref[ws].astype(BF16)
        wdb_ref[...] = wdf_ref[ws].astype(BF16)

        @pl.when(nxt_ref[i] >= 0)
        def _():
            fetch_weights(nxt_ref[i], 1 - ws)

    @pl.when(i < nv)
    def _():
        nslot = lax.rem(i + MOE_LOOKAHEAD, MOE_SLOTS)
        n_kc = D_MODEL // MXU_DIM
        groups_per_kc = TM_MOE // SUBLANES // n_kc
        hg = hu = None
        for kc in range(n_kc):
            for gg in range(groups_per_kc):
                _gather_group(h2_ref, rtn_ref, 1, 0, buf_ref.at[nslot], sem.at[nslot], (0,), kc * groups_per_kc + gg)
            ks = slice(kc * MXU_DIM, (kc + 1) * MXU_DIM)
            xk = buf_ref[slot, :, :, ks].reshape(TM_MOE, MXU_DIM).astype(BF16)
            pg = jnp.dot(xk, wgb_ref[ks, :], preferred_element_type=F32)
            pu = jnp.dot(xk, wub_ref[ks, :], preferred_element_type=F32)
            hg = pg if hg is None else hg + pg
            hu = pu if hu is None else hu + pu
        act = (hg * jax.nn.sigmoid(hg)) * hu
        y_ref[...] = jnp.dot(act.astype(BF16), wdb_ref[...], preferred_element_type=F32)

    @pl.when(i >= nv)
    def _():
        y_ref[...] = jnp.zeros(y_ref.shape, y_ref.dtype)


def _moe(tile_meta, n_valid, row_token, h2, w_gate, w_up, w_down):
    n_steps = row_token.shape[0] // TM_MOE
    rt = row_token.reshape(n_steps, 1, TM_MOE)
    te, first, nxt, par = tile_meta
    any_spec = pl.BlockSpec(memory_space=pl.ANY)
    return pl.pallas_call(
        _moe_kernel,
        out_shape=jax.ShapeDtypeStruct((n_steps * TM_MOE, D_MODEL), F32),
        grid_spec=pltpu.PrefetchScalarGridSpec(
            num_scalar_prefetch=5, grid=(n_steps,),
            in_specs=[*_index_specs(TM_MOE, 0, n_steps, MOE_LOOKAHEAD), any_spec, any_spec, any_spec, any_spec],
            out_specs=pl.BlockSpec((TM_MOE, D_MODEL), lambda i, *_: (i, 0)),
            scratch_shapes=[pltpu.VMEM((MOE_SLOTS, TM_MOE // SUBLANES, SUBLANES, D_MODEL), F32),
                            pltpu.VMEM((2, D_MODEL, D_EXPERT), F32), pltpu.VMEM((2, D_MODEL, D_EXPERT), F32),
                            pltpu.VMEM((2, D_EXPERT, D_MODEL), F32),
                            pltpu.VMEM((D_MODEL, D_EXPERT), BF16), pltpu.VMEM((D_MODEL, D_EXPERT), BF16),
                            pltpu.VMEM((D_EXPERT, D_MODEL), BF16),
                            pltpu.SemaphoreType.DMA((MOE_SLOTS,)), pltpu.SemaphoreType.DMA((2,))]),
        compiler_params=_cparams(("arbitrary",)),
        name="moe",
    )(te, n_valid, first, nxt, par, rt, rt, rt, h2.reshape(-1, SUBLANES, D_MODEL), w_gate, w_up, w_down)


def _final_kernel(pos0_ref, posn_ref, x1_ref, g2_ref, route_ref, l2g_ref, l2b_ref, ys_ref, o_ref, buf_ref, sem, *, tm):
    step = pl.program_id(0)
    slot = step & 1

    def gather(ids_ref, dst_slot):
        for k in range(2):
            _gather_rows(ys_ref, ids_ref, 1, k * tm, buf_ref.at[dst_slot, k], sem.at[dst_slot], (0, 1))

    @pl.when(step == 0)
    def _():
        gather(pos0_ref, 0)

    for k in range(2):
        pltpu.make_async_copy(ys_ref.at[pl.ds(0, tm // SUBLANES)], buf_ref.at[slot, k], sem.at[slot]).wait()

    route = route_ref[...]
    w1, w2 = route[:, 2:3], route[:, 3:4]
    n_groups = tm // SUBLANES

    def combine(issue_next):
        n_chunks = D_MODEL // MXU_DIM
        per_chunk = 2 * n_groups // n_chunks
        for c in range(n_chunks):
            if issue_next:
                for q in range(c * per_chunk, (c + 1) * per_chunk):
                    k, g = q // n_groups, q % n_groups
                    _gather_group(ys_ref, posn_ref, 1, k * tm, buf_ref.at[1 - slot, k], sem.at[1 - slot], (0, 1), g)
            cs = slice(c * MXU_DIM, (c + 1) * MXU_DIM)
            f = (w1 * buf_ref[slot, 0, :, :, cs].reshape(tm, MXU_DIM)
                 + w2 * buf_ref[slot, 1, :, :, cs].reshape(tm, MXU_DIM))
            o_ref[:, cs] = DEEPNORM_ALPHA * x1_ref[:, cs] + g2_ref[:, cs] * f

    @pl.when(step + 1 < pl.num_programs(0))
    def _():
        combine(True)

    @pl.when(step + 1 == pl.num_programs(0))
    def _():
        combine(False)

    o_ref[...] = _ln(o_ref[...]) * l2g_ref[...] + l2b_ref[...]


def _final(pos_tiles, first_tile, x1, mod, per_row, route, ln2_g, ln2_b, ys):
    nb, l, _ = x1.shape
    tm = min(TM, l)
    nt = l // tm
    n_steps = nb * nt
    row = lambda wd: pl.BlockSpec((None, tm, wd), lambda s: (s // nt, s % nt, 0))
    if per_row:
        g2 = pl.BlockSpec((None, tm, D_MODEL), lambda s: (s // nt, s % nt, 5))
    else:
        g2 = pl.BlockSpec((None, 1, D_MODEL), lambda s: (s // nt, 0, 5))
    vec = pl.BlockSpec((1, D_MODEL), lambda s: (0, 0))
    return pl.pallas_call(
        functools.partial(_final_kernel, tm=tm),
        out_shape=jax.ShapeDtypeStruct((nb, l, D_MODEL), F32),
        grid=(n_steps,),
        in_specs=[*_index_specs(2 * tm, first_tile, n_steps), row(D_MODEL), g2, row(LANES), vec, vec,
                  pl.BlockSpec(memory_space=pl.ANY)],
        out_specs=row(D_MODEL),
        scratch_shapes=[pltpu.VMEM((2, 2, tm // SUBLANES, SUBLANES, D_MODEL), F32), pltpu.SemaphoreType.DMA((2,))],
        compiler_params=_cparams(("arbitrary",)),
        name="final",
    )(pos_tiles, pos_tiles, x1, mod, route, ln2_g, ln2_b, ys.reshape(-1, SUBLANES, D_MODEL))


def _rope_tables(pos):
    half = ROT_DIM // 2
    inv_freq = ROPE_THETA ** (-jnp.arange(half, dtype=jnp.float32) * 2.0 / ROT_DIM)
    ang = pos.astype(jnp.float32)[:, None] * inv_freq[None, :]
    cos, sin = jnp.cos(ang), jnp.sin(ang)
    n = pos.shape[0]
    one = jnp.ones((n, HEAD_DIM - ROT_DIM), F32)
    zero = jnp.zeros((n, HEAD_DIM - half), F32)
    c = jnp.concatenate([cos, cos, one], -1)
    a = jnp.concatenate([-sin, zero], -1)
    b = jnp.concatenate([jnp.zeros((n, half), F32), sin, jnp.zeros((n, HEAD_DIM - ROT_DIM), F32)], -1)
    return tuple(jnp.tile(t, (1, LANES // HEAD_DIM)) for t in (c, a, b))


def _s5_params(lam_re, lam_im, log_step, b_re, b_im, c_re, c_im):
    f32 = jnp.float32
    dt = jnp.exp(log_step.astype(f32))[:, None]
    lr, li = lam_re.astype(f32), lam_im.astype(f32)
    mag = jnp.exp(lr * dt)
    ar, ai = mag * jnp.cos(li * dt), mag * jnp.sin(li * dt)
    den = lr * lr + li * li
    cr = ((ar - 1.0) * lr + ai * li) / den
    ci = (ai * lr - (ar - 1.0) * li) / den
    br, bi = b_re.astype(f32), b_im.astype(f32)
    bbr = cr[..., None] * br - ci[..., None] * bi
    bbi = cr[..., None] * bi + ci[..., None] * br
    gpt = MXU_DIM // SSM_CH
    eye = jnp.eye(gpt, dtype=f32)

    def pack_b(m):
        m = m.reshape(N_KT, gpt, SSM_STATE, SSM_CH)
        return jnp.einsum("kgpc,gh->kgchp", m, eye).reshape(N_KT, MXU_DIM, ST_PER_KT).astype(BF16)

    def pack_c(m):
        m = m.astype(f32).reshape(N_KT, gpt, SSM_CH, SSM_STATE)
        return jnp.einsum("kgcp,gh->khpgc", m, eye).reshape(N_KT, ST_PER_KT, MXU_DIM).astype(BF16)

    wts = (pack_b(bbr), pack_b(bbi), pack_c(c_re), pack_c(-c_im.astype(f32)))
    return wts, ar.reshape(1, N_STATE), ai.reshape(1, N_STATE)


def _scan_tables(ar, ai, nb):
    assert SUBLANES // nb == 2
    a2r, a2i = ar * ar - ai * ai, 2.0 * ar * ai
    z = jnp.zeros_like(ar)
    rep = lambda first, second: jnp.concatenate([jnp.tile(first, (nb, 1)), jnp.tile(second, (nb, 1))], 0)
    return rep(z, ar), rep(z, ai), rep(ar, a2r), rep(ai, a2i)


def kernel(x_prompt, x_sample, cache_k_win, cache_v_win, state_ssm_re, state_ssm_im, c_prompt, c_sample, ada_w, ada_b,
           w_in, attn_sinks, ssm_lambda_re, ssm_lambda_im, ssm_log_step, ssm_b_re, ssm_b_im, ssm_c_re, ssm_c_im, ssm_d,
           ssm_glu_w, ssm_glu_b, attn_norm_g, ssm_norm_g, w_out, ln1_g, ln1_b, router_group_w, router_group_b,
           router_expert_w, router_expert_b, exp_w_gate, exp_w_up, exp_w_down, ln2_g, ln2_b):
    assert ada_w.shape[0] == DEPTH
    bp, lp, _ = x_prompt.shape
    bs, ls, _ = x_sample.shape
    lsp = SAMPLE_PAD

    w_in_bf = w_in[0].astype(BF16)
    wr = jnp.concatenate([router_group_w[0], router_expert_w[0]], -1)
    wr = jnp.pad(wr, ((0, 0), (0, LANES - wr.shape[1]))).astype(BF16)
    br = jnp.pad(jnp.concatenate([router_group_b[0], router_expert_b[0]], -1), (0, LANES - N_EXPERT_GROUPS - N_EXPERTS))
    wpost = dict(glu_w=ssm_glu_w[0].astype(BF16), glu_b=ssm_glu_b[0][None], attn_g=attn_norm_g[0][None],
                 ssm_g=ssm_norm_g[0][None], w_out=w_out[0].astype(BF16), ln1_g=ln1_g[0][None], ln1_b=ln1_b[0][None],
                 wr=wr, br=br[None])
    s5w, ar, ai = _s5_params(ssm_lambda_re[0], ssm_lambda_im[0], ssm_log_step[0], ssm_b_re[0], ssm_b_im[0],
                             ssm_c_re[0], ssm_c_im[0])
    d_skip = ssm_d[0][None]
    sinks = attn_sinks[0]

    n_c = bp + bs
    n_cp = -(-n_c // SUBLANES) * SUBLANES
    c_all = jnp.pad(jnp.concatenate([c_prompt, c_sample], 0), ((0, n_cp - n_c), (0, 0)))
    mod = _ada(c_all, ada_w[0], ada_b[0][None])
    mod_p = mod[:bp][:, None, :]
    mod_s = jnp.repeat(mod[bp:n_c], lsp, axis=0)[None]

    xs_pad = jnp.pad(x_sample, ((0, 0), (0, lsp - ls), (0, 0))).reshape(1, bs * lsp, D_MODEL)

    rope_p = _rope_tables(jnp.arange(lp))
    pos_s = PAST_LEN + jnp.minimum(jnp.arange(lsp), ls - 1)
    rope_s = tuple(jnp.tile(t, (bs, 1)) for t in _rope_tables(pos_s))
    q_p, k_p, v_p, u_p = _inproj(x_prompt, mod_p, w_in_bf, rope_p, False)
    q_s, k_s, v_s, u_s = _inproj(xs_pad, mod_s, w_in_bf, rope_s, True)

    attn_p = _attn_prompt(sinks, q_p, k_p, v_p)
    k_s3 = k_s.reshape(bs, lsp, D_KV)
    v_s3 = v_s.reshape(bs, lsp, D_KV)
    ck = cache_k_win[0].reshape(bs, WINDOW, D_KV)
    cv = cache_v_win[0].reshape(bs, WINDOW, D_KV)
    padk = lambda new: jnp.pad(new, ((0, 0), (0, WINDOW - lsp), (0, 0)))
    attn_s = _attn_sample(sinks, q_s.reshape(bs, lsp, D_ATTN), jnp.concatenate([ck, padk(k_s3)], 1),
                          jnp.concatenate([cv, padk(v_s3)], 1))

    tabs = _scan_tables(ar, ai, bp)
    zero_carry = jnp.zeros((SUBLANES, N_STATE), F32)
    y_p, hr_p, hi_p = _s5_prompt(u_p, s5w, tabs, d_skip, zero_carry, zero_carry)
    u_tb = jnp.transpose(u_s.reshape(bs, lsp, D_SSM)[:, :ls], (1, 0, 2))
    y_tb, hr_s, hi_s = _s5_sample(u_tb, s5w, ar, ai, d_skip, state_ssm_re[0].reshape(bs, N_STATE),
                                  state_ssm_im[0].reshape(bs, N_STATE))
    y_s = jnp.pad(jnp.transpose(y_tb, (1, 0, 2)), ((0, 0), (0, lsp - ls), (0, 0))).reshape(1, bs * lsp, D_SSM)

    cnt0 = jnp.zeros((SUBLANES, LANES), F32)
    n_p, n_s = bp * lp, bs * lsp
    n_tok = n_p + n_s
    x1_s, h2_s, route_s, cnt_s = _post(attn_s.reshape(1, bs * lsp, D_ATTN), y_s, xs_pad, mod_s, True, wpost, cnt0)
    x1_p, h2, route_p, cnt = _post(attn_p, y_p, x_prompt, mod_p, False, wpost, cnt_s, tail=h2_s)

    n_tiles = -(-(2 * n_tok + N_EXPERTS * (TM_MOE - 1)) // TM_MOE) + MOE_LOOKAHEAD
    route = jnp.concatenate([route_p.reshape(n_p, LANES), route_s.reshape(n_s, LANES)], 0)
    counts = cnt[0, ROUTE_LANE0:ROUTE_LANE0 + N_EXPERTS].astype(jnp.int32)
    padded = ((counts + TM_MOE - 1) // TM_MOE) * TM_MOE
    ends = jnp.cumsum(padded)
    offs = ends - padded
    col = lambda c: route[:, c].astype(jnp.int32)
    pos_k = [offs[col(k)] + col(4 + k) for k in range(2)]
    n_valid = ends[-1] // TM_MOE
    tile_row = jnp.minimum(jnp.arange(n_tiles), n_valid - 1) * TM_MOE
    tile_expert = jnp.sum((ends[None, :] <= tile_row[:, None]).astype(jnp.int32), axis=1)
    first_tile = jnp.concatenate([jnp.ones((1,), jnp.int32), (tile_expert[1:] != tile_expert[:-1]).astype(jnp.int32)])
    first_tile = first_tile * (jnp.arange(n_tiles) < n_valid)
    nonempty = padded > 0
    expert_ids = jnp.arange(N_EXPERTS, dtype=jnp.int32)
    later = jnp.where(nonempty[None, :] & (expert_ids[None, :] > expert_ids[:, None]), expert_ids[None, :], N_EXPERTS)
    next_expert = jnp.min(later, axis=1)
    next_expert = jnp.where(next_expert < N_EXPERTS, next_expert, -1).astype(jnp.int32)
    weight_slot = ((jnp.cumsum(nonempty.astype(jnp.int32)) - 1) & 1).astype(jnp.int32)
    tile_meta = (tile_expert, first_tile.astype(jnp.int32), next_expert[tile_expert], weight_slot[tile_expert])
    token_ids = jnp.arange(n_tok, dtype=jnp.int32)
    row_token = (jnp.arange(n_tiles * TM_MOE, dtype=jnp.int32) % n_tok).at[jnp.concatenate(pos_k)].set(
        jnp.concatenate([token_ids, token_ids]))

    ys = _moe(tile_meta, n_valid.reshape(1).astype(jnp.int32), row_token, h2, exp_w_gate[0], exp_w_up[0],
              exp_w_down[0])
    pos_tiles = jnp.concatenate([p.reshape(n_tok // TM, 1, TM) for p in pos_k], axis=2)
    y_prompt = _final(pos_tiles, 0, x1_p, mod_p, False, route_p, ln2_g[0][None], ln2_b[0][None], ys)
    y_samp = _final(pos_tiles, n_p // TM, x1_s, mod_s, True, route_s, ln2_g[0][None], ln2_b[0][None], ys)
    y_sample = y_samp.reshape(bs, lsp, D_MODEL)[:, :ls]

    kv5 = lambda a, n: a.reshape(1, a.shape[0], n, N_KV_HEADS, HEAD_DIM)
    k_win_p = kv5(k_p[:, lp - WINDOW:], WINDOW)
    v_win_p = kv5(v_p[:, lp - WINDOW:], WINDOW)
    k_win_s = kv5(jnp.concatenate([ck[:, ls:], k_s3[:, :ls]], 1), WINDOW)
    v_win_s = kv5(jnp.concatenate([cv[:, ls:], v_s3[:, :ls]], 1), WINDOW)
    st = lambda a: a.reshape(1, a.shape[0], N_SSM_GROUPS, SSM_STATE)
    return (y_prompt, y_sample, k_win_p, v_win_p, st(hr_p[bp:2 * bp]), st(hi_p[bp:2 * bp]),
            k_win_s, v_win_s, st(hr_s), st(hi_s))
```

```python
import functools
import math

import jax
import jax.numpy as jnp
from jax import lax
from jax.experimental import pallas as pl
from jax.experimental.pallas import tpu as pltpu

F32 = jnp.float32
BF16 = jnp.bfloat16

D_MODEL = 2048
D_ATTN = 1024
D_SSM = 1024
HEAD_DIM = 64
N_HEADS = 16
N_KV_HEADS = 4
REP = 4
D_KV = 256
ROT_DIM = 16
ROPE_THETA = 500000.0
WINDOW = 128
SSM_CH = 16
N_SSM_GROUPS = 64
SSM_STATE = 64
N_STATE = N_SSM_GROUPS * SSM_STATE
PROJ_COLS = D_ATTN + 2 * D_KV + D_SSM
N_EXPERT_GROUPS = 4
EXPERTS_PER_GROUP = 8
N_EXPERTS = 32
D_EXPERT = 512
DEPTH = 1
DEEPNORM_ALPHA = (2.0 * DEPTH) ** 0.25
LN_EPS = 1e-5
PAST_LEN = 16384

LANES = 128
SUBLANES = 8
MXU_DIM = 256
TM = 256
TM_MOE = 256
TM_INPROJ = 512
S5_TT = 128
SAMPLE_PAD = 8
ROUTE_LANE0 = N_EXPERT_GROUPS
VMEM_LIMIT = 56 * 1024 * 1024


def _cparams(sem):
    return pltpu.CompilerParams(dimension_semantics=sem, vmem_limit_bytes=VMEM_LIMIT)


def _ln(x):
    mu = jnp.mean(x, axis=-1, keepdims=True)
    xc = x - mu
    var = jnp.mean(xc * xc, axis=-1, keepdims=True)
    return xc * lax.rsqrt(var + LN_EPS)


def _rms(x):
    return x * lax.rsqrt(jnp.mean(x * x, axis=-1, keepdims=True) + LN_EPS)


def _ada_kernel(c_ref, w_ref, b_ref, o_ref):
    c = c_ref[...]
    s = c * jax.nn.sigmoid(c)
    o_ref[...] = jnp.dot(s.astype(BF16), w_ref[...].astype(BF16), preferred_element_type=F32) + b_ref[...]


def _ada(c_all, ada_w, ada_b):
    n, tn = c_all.shape[0], 1024
    return pl.pallas_call(
        _ada_kernel,
        out_shape=jax.ShapeDtypeStruct((n, 6 * D_MODEL), F32),
        grid=(6 * D_MODEL // tn,),
        in_specs=[pl.BlockSpec((n, D_MODEL), lambda j: (0, 0)),
                  pl.BlockSpec((D_MODEL, tn), lambda j: (0, j)),
                  pl.BlockSpec((1, tn), lambda j: (0, j))],
        out_specs=pl.BlockSpec((n, tn), lambda j: (0, j)),
        compiler_params=_cparams(("arbitrary",)),
        name="ada",
    )(c_all, ada_w, ada_b)


def _mod_spec(per_row, tm, chunk):
    if per_row:
        return pl.BlockSpec((None, tm, D_MODEL), lambda b, i: (b, i, chunk))
    return pl.BlockSpec((None, 1, D_MODEL), lambda b, i: (b, 0, chunk))


def _inproj_kernel(x_ref, sh_ref, sc_ref, w_ref, rc_ref, ra_ref, rb_ref, q_ref, k_ref, v_ref, u_ref):
    h = _ln(x_ref[...]) * (1.0 + sc_ref[...]) + sh_ref[...]
    proj = jnp.dot(h.astype(BF16), w_ref[...], preferred_element_type=F32)
    rc, ra, rb = rc_ref[...], ra_ref[...], rb_ref[...]

    def rope(t):
        return t * rc + pltpu.roll(t, LANES - ROT_DIM // 2, 1) * ra + pltpu.roll(t, ROT_DIM // 2, 1) * rb

    for j in range(D_ATTN // LANES):
        q_ref[:, j * LANES:(j + 1) * LANES] = (rope(proj[:, j * LANES:(j + 1) * LANES]) * HEAD_DIM ** -0.5).astype(BF16)
    for j in range(D_KV // LANES):
        c0 = D_ATTN + j * LANES
        k_ref[:, j * LANES:(j + 1) * LANES] = rope(proj[:, c0:c0 + LANES])
    v_ref[...] = proj[:, D_ATTN + D_KV:D_ATTN + 2 * D_KV]
    u_ref[...] = proj[:, D_ATTN + 2 * D_KV:]


def _inproj(x, mod, w_in_bf, rope_tabs, per_row):
    nb, l, _ = x.shape
    tm = min(TM_INPROJ, l)
    row = lambda w: pl.BlockSpec((None, tm, w), lambda b, i: (b, i, 0))
    tab = pl.BlockSpec((tm, LANES), lambda b, i: (i, 0))
    return pl.pallas_call(
        _inproj_kernel,
        out_shape=(jax.ShapeDtypeStruct((nb, l, D_ATTN), BF16), jax.ShapeDtypeStruct((nb, l, D_KV), F32),
                   jax.ShapeDtypeStruct((nb, l, D_KV), F32), jax.ShapeDtypeStruct((nb, l, D_SSM), F32)),
        grid=(nb, l // tm),
        in_specs=[row(D_MODEL), _mod_spec(per_row, tm, 0), _mod_spec(per_row, tm, 1),
                  pl.BlockSpec((D_MODEL, PROJ_COLS), lambda b, i: (0, 0)), tab, tab, tab],
        out_specs=(row(D_ATTN), row(D_KV), row(D_KV), row(D_SSM)),
        compiler_params=_cparams(("arbitrary", "arbitrary")),
        name="inproj",
    )(x, mod, mod, w_in_bf, *rope_tabs)


ATTN_SAMPLE_BATCH = 8


def _attn_kernel(sink_ref, q_ref, kp_ref, kc_ref, vp_ref, vc_ref, o_ref, *, lq, prev_from_block):
    m_rows = REP * lq
    ii = lax.broadcasted_iota(jnp.int32, (m_rows, WINDOW), 0) & (lq - 1)
    jj = lax.broadcasted_iota(jnp.int32, (m_rows, WINDOW), 1)
    from_prev = jj > ii
    if prev_from_block:
        dead = jj > ii + jnp.where(pl.program_id(1) > 0, WINDOW, 0)
    rr = lax.broadcasted_iota(jnp.int32, (m_rows, 1), 0)
    dn = (((1,), (1,)), ((), ()))
    for bi in range(q_ref.shape[0]):
        for g in range(N_KV_HEADS):
            qg = q_ref[bi, :, g * REP * HEAD_DIM:(g + 1) * REP * HEAD_DIM].astype(F32)
            qs = jnp.concatenate([qg[:, r * HEAD_DIM:(r + 1) * HEAD_DIM] for r in range(REP)], axis=0).astype(BF16)
            ks = slice(g * HEAD_DIM, (g + 1) * HEAD_DIM)
            s_p = lax.dot_general(qs, kp_ref[bi, :, ks].astype(BF16), dn, preferred_element_type=F32)
            s_c = lax.dot_general(qs, kc_ref[bi, :, ks].astype(BF16), dn, preferred_element_type=F32)
            s = jnp.where(from_prev, s_p, s_c)
            if prev_from_block:
                s = jnp.where(dead, -jnp.inf, s)
            sink = jnp.zeros((m_rows, 1), F32)
            for r in range(REP):
                sink = jnp.where((rr >= r * lq) & (rr < (r + 1) * lq), sink_ref[g * REP + r], sink)
            m = jnp.maximum(jnp.max(s, axis=-1, keepdims=True), sink)
            p = jnp.exp(s - m)
            p = p / (jnp.sum(p, axis=-1, keepdims=True) + jnp.exp(sink - m))
            o = (jnp.dot(jnp.where(from_prev, p, 0.0).astype(BF16), vp_ref[bi, :, ks].astype(BF16),
                         preferred_element_type=F32)
                 + jnp.dot(jnp.where(from_prev, 0.0, p).astype(BF16), vc_ref[bi, :, ks].astype(BF16),
                           preferred_element_type=F32))
            for r in range(REP):
                h = g * REP + r
                o_ref[bi, :, h * HEAD_DIM:(h + 1) * HEAD_DIM] = o[r * lq:(r + 1) * lq]


def _attn_prompt(sinks, q, k, v):
    nb, l, _ = q.shape
    nblk = l // WINDOW
    cur = lambda w: pl.BlockSpec((1, WINDOW, w), lambda b, n: (b, n, 0))
    prev = lambda w: pl.BlockSpec((1, WINDOW, w), lambda b, n: (b, jnp.maximum(n - 1, 0), 0))
    return pl.pallas_call(
        functools.partial(_attn_kernel, lq=WINDOW, prev_from_block=True),
        out_shape=jax.ShapeDtypeStruct((nb, l, D_ATTN), F32),
        grid=(nb, nblk),
        in_specs=[pl.BlockSpec(memory_space=pltpu.SMEM), cur(D_ATTN), prev(D_KV), cur(D_KV), prev(D_KV), cur(D_KV)],
        out_specs=cur(D_ATTN),
        compiler_params=_cparams(("arbitrary", "arbitrary")),
        name="attn_prompt",
    )(sinks, q, k, k, v, v)


def _attn_sample(sinks, q, kcat, vcat):
    nb, lq, _ = q.shape
    nbb = ATTN_SAMPLE_BATCH
    kblk = lambda n: pl.BlockSpec((nbb, WINDOW, D_KV), lambda b: (b, n, 0))
    return pl.pallas_call(
        functools.partial(_attn_kernel, lq=lq, prev_from_block=False),
        out_shape=jax.ShapeDtypeStruct((nb, lq, D_ATTN), F32),
        grid=(nb // nbb,),
        in_specs=[pl.BlockSpec(memory_space=pltpu.SMEM), pl.BlockSpec((nbb, lq, D_ATTN), lambda b: (b, 0, 0)),
                  kblk(0), kblk(1), kblk(0), kblk(1)],
        out_specs=pl.BlockSpec((nbb, lq, D_ATTN), lambda b: (b, 0, 0)),
        compiler_params=_cparams(("arbitrary",)),
        name="attn_sample",
    )(sinks, q, kcat, kcat, vcat, vcat)


N_KT = D_SSM // MXU_DIM
ST_PER_KT = N_STATE // N_KT
N_SLAB = D_SSM // LANES


def _cproj(hr_ref, hi_ref, cre_ref, cimn_ref, kt):
    cs = slice(kt * ST_PER_KT, (kt + 1) * ST_PER_KT)
    return (jnp.dot(hr_ref[:, cs].astype(BF16), cre_ref[kt], preferred_element_type=F32)
            + jnp.dot(hi_ref[:, cs].astype(BF16), cimn_ref[kt], preferred_element_type=F32))


def _s5_prompt_kernel(u_ref, bre_ref, bim_ref, cre_ref, cimn_ref, a1r_ref, a1i_ref, par_ref, pai_ref, d_ref,
                      h0r_ref, h0i_ref, y_ref, cr_ref, ci_ref, il_ref, xr_ref, xi_ref, *, nb, tt):
    rows = nb * tt

    @pl.when(pl.program_id(0) == 0)
    def _():
        cr_ref[...] = h0r_ref[...]
        ci_ref[...] = h0i_ref[...]

    for b in range(nb):
        for j in range(N_SLAB):
            il_ref[j, pl.ds(b, tt, stride=nb), :] = u_ref[b, :, j * LANES:(j + 1) * LANES]
    for kt in range(N_KT):
        ub = jnp.concatenate([il_ref[2 * kt], il_ref[2 * kt + 1]], axis=1).astype(BF16)
        cs = slice(kt * ST_PER_KT, (kt + 1) * ST_PER_KT)
        xr_ref[:, cs] = jnp.dot(ub, bre_ref[kt], preferred_element_type=F32)
        xi_ref[:, cs] = jnp.dot(ub, bim_ref[kt], preferred_element_type=F32)

    half = lax.broadcasted_iota(jnp.int32, (SUBLANES, LANES), 0) < nb

    def step(i, carry):
        r0 = pl.multiple_of(i * SUBLANES, SUBLANES)
        for j in range(N_STATE // LANES):
            ls = slice(j * LANES, (j + 1) * LANES)
            x_r = xr_ref[pl.ds(r0, SUBLANES), ls]
            x_i = xi_ref[pl.ds(r0, SUBLANES), ls]
            s_r = pltpu.roll(x_r, nb, 0)
            s_i = pltpu.roll(x_i, nb, 0)
            a1r, a1i = a1r_ref[:, ls], a1i_ref[:, ls]
            c_r, c_i = cr_ref[:, ls], ci_ref[:, ls]
            par, pai = par_ref[:, ls], pai_ref[:, ls]
            h_r = x_r + (a1r * s_r - a1i * s_i) + (par * c_r - pai * c_i)
            h_i = x_i + (a1r * s_i + a1i * s_r) + (par * c_i + pai * c_r)
            xr_ref[pl.ds(r0, SUBLANES), ls] = h_r
            xi_ref[pl.ds(r0, SUBLANES), ls] = h_i
            cr_ref[:, ls] = jnp.where(half, pltpu.roll(h_r, nb, 0), h_r)
            ci_ref[:, ls] = jnp.where(half, pltpu.roll(h_i, nb, 0), h_i)
        return carry

    lax.fori_loop(0, rows // SUBLANES, step, 0)

    for kt in range(N_KT):
        y = _cproj(xr_ref, xi_ref, cre_ref, cimn_ref, kt)
        il_ref[2 * kt] = y[:, :LANES]
        il_ref[2 * kt + 1] = y[:, LANES:]
    for b in range(nb):
        for j in range(N_SLAB):
            ls = slice(j * LANES, (j + 1) * LANES)
            y_ref[b, :, ls] = il_ref[j, pl.ds(b, tt, stride=nb), :] + d_ref[:, ls] * u_ref[b, :, ls]


def _s5_prompt(u, wts, tabs, d_skip, h0r, h0i):
    nb, l, _ = u.shape
    tt = S5_TT
    rows = nb * tt
    full = lambda a: pl.BlockSpec(a.shape, lambda i: (0,) * a.ndim)
    blk = pl.BlockSpec((nb, tt, D_SSM), lambda i: (0, i, 0))
    carry = jax.ShapeDtypeStruct((SUBLANES, N_STATE), F32)
    return pl.pallas_call(
        functools.partial(_s5_prompt_kernel, nb=nb, tt=tt),
        out_shape=(jax.ShapeDtypeStruct((nb, l, D_SSM), F32), carry, carry),
        grid=(l // tt,),
        in_specs=[blk] + [full(a) for a in (*wts, *tabs, d_skip, h0r, h0i)],
        out_specs=(blk, pl.BlockSpec((SUBLANES, N_STATE), lambda i: (0, 0)),
                   pl.BlockSpec((SUBLANES, N_STATE), lambda i: (0, 0))),
        scratch_shapes=[pltpu.VMEM((N_SLAB, rows, LANES), F32), pltpu.VMEM((rows, N_STATE), F32),
                        pltpu.VMEM((rows, N_STATE), F32)],
        compiler_params=_cparams(("arbitrary",)),
        name="s5_prompt",
    )(u, *wts, *tabs, d_skip, h0r, h0i)


def _s5_sample_kernel(u_ref, bre_ref, bim_ref, cre_ref, cimn_ref, ar_ref, ai_ref, d_ref, h0r_ref, h0i_ref,
                      y_ref, sr_ref, si_ref, xr_ref, xi_ref, *, nt):
    sr_ref[...] = h0r_ref[...]
    si_ref[...] = h0i_ref[...]
    for t in range(nt):
        for kt in range(N_KT):
            ub = u_ref[t, :, kt * MXU_DIM:(kt + 1) * MXU_DIM].astype(BF16)
            cs = slice(kt * ST_PER_KT, (kt + 1) * ST_PER_KT)
            xr_ref[:, cs] = jnp.dot(ub, bre_ref[kt], preferred_element_type=F32)
            xi_ref[:, cs] = jnp.dot(ub, bim_ref[kt], preferred_element_type=F32)
        ar, ai = ar_ref[...], ai_ref[...]
        s_r, s_i = sr_ref[...], si_ref[...]
        sr_ref[...] = xr_ref[...] + (ar * s_r - ai * s_i)
        si_ref[...] = xi_ref[...] + (ar * s_i + ai * s_r)
        for kt in range(N_KT):
            ys = slice(kt * MXU_DIM, (kt + 1) * MXU_DIM)
            y_ref[t, :, ys] = _cproj(sr_ref, si_ref, cre_ref, cimn_ref, kt) + d_ref[:, ys] * u_ref[t, :, ys]


def _s5_sample(u_tb, wts, ar, ai, d_skip, h0r, h0i):
    nt, nb, _ = u_tb.shape
    st = jax.ShapeDtypeStruct((nb, N_STATE), F32)
    args = (u_tb, *wts, ar, ai, d_skip, h0r, h0i)
    full = lambda a: pl.BlockSpec(a.shape, lambda i: (0,) * a.ndim)
    return pl.pallas_call(
        functools.partial(_s5_sample_kernel, nt=nt),
        out_shape=(jax.ShapeDtypeStruct((nt, nb, D_SSM), F32), st, st),
        grid=(1,),
        in_specs=[full(a) for a in args],
        out_specs=(pl.BlockSpec((nt, nb, D_SSM), lambda i: (0, 0, 0)), pl.BlockSpec((nb, N_STATE), lambda i: (0, 0)),
                   pl.BlockSpec((nb, N_STATE), lambda i: (0, 0))),
        scratch_shapes=[pltpu.VMEM((nb, N_STATE), F32), pltpu.VMEM((nb, N_STATE), F32)],
        compiler_params=_cparams(("arbitrary",)),
        name="s5_sample",
    )(*args)


def _gather_group(src_ref, ids_ref, id_stride, id_offset, dst_ref, sem, priorities, g):
    for j in range(SUBLANES):
        t = ids_ref[0, id_stride * (g * SUBLANES + j) + id_offset]
        pltpu.make_async_copy(src_ref.at[t >> 3, pl.ds(t & (SUBLANES - 1), 1)], dst_ref.at[g, pl.ds(j, 1)],
                              sem).start(priority=priorities[j % len(priorities)])


def _gather_rows(src_ref, ids_ref, id_stride, id_offset, dst_ref, sem, priorities):
    def body(g, c):
        _gather_group(src_ref, ids_ref, id_stride, id_offset, dst_ref, sem, priorities, g)
        return c
    lax.fori_loop(0, dst_ref.shape[0], body, 0)


def _index_specs(n_cols, first_tile, n_steps, depth=1):
    def spec(fn):
        return pl.BlockSpec((None, 1, n_cols), fn, memory_space=pltpu.SMEM)

    head = [spec(lambda s, *_, k=k: (first_tile + min(k, n_steps - 1), 0, 0)) for k in range(depth)]
    return (*head, spec(lambda s, *_: (first_tile + jnp.minimum(s + depth, n_steps - 1), 0, 0)))


N_POST_INPUTS = 17
POST_ROW_PARTS = 1


def _post_kernel(*refs, has_tail):
    ins, outs = refs[:N_POST_INPUTS], refs[N_POST_INPUTS + int(has_tail):]
    if not has_tail:
        _post_body(*ins, *outs)
        return
    tail_ref, h2_ref = refs[N_POST_INPUTS], outs[1]
    last = pl.num_programs(0) - 1

    @pl.when(pl.program_id(0) == last)
    def _():
        h2_ref[...] = tail_ref[...]

    @pl.when(pl.program_id(0) < last)
    def _():
        _post_body(*ins, *outs)


def _post_body(attn_ref, yssm_ref, x_ref, g1_ref, sh2_ref, sc2_ref, gluw_ref, glub_ref, ga_ref, gs_ref, wout_ref,
               l1g_ref, l1b_ref, wr_ref, br_ref, tri_ref, cnt0_ref, x1_ref, h2_ref, route_ref, cnt_ref):
    tm = x_ref.shape[0]

    @pl.when(pl.program_id(0) == 0)
    def _():
        cnt_ref[...] = cnt0_ref[...]

    hm = tm // POST_ROW_PARTS
    lane = lax.broadcasted_iota(jnp.int32, (hm, LANES), 1).astype(F32)
    big = float(4 * LANES)
    neg = -jnp.inf
    cnt = cnt_ref[0:1, :]
    for part in range(POST_ROW_PARTS):
        rs = slice(part * hm, (part + 1) * hm)
        rows = lambda ref: ref[rs, :] if ref.shape[0] == tm else ref[...]
        z = jax.nn.gelu(yssm_ref[rs, :])
        ssm = z * jax.nn.sigmoid(jnp.dot(z.astype(BF16), gluw_ref[...], preferred_element_type=F32) + glub_ref[...])
        mixed_a = (_rms(attn_ref[rs, :]) * ga_ref[...]).astype(BF16)
        mixed_s = (_rms(ssm) * gs_ref[...]).astype(BF16)
        o = (jnp.dot(mixed_a, wout_ref[:D_ATTN, :], preferred_element_type=F32)
             + jnp.dot(mixed_s, wout_ref[D_ATTN:, :], preferred_element_type=F32))
        x1 = _ln(DEEPNORM_ALPHA * x_ref[rs, :] + rows(g1_ref) * o) * l1g_ref[...] + l1b_ref[...]
        x1_ref[rs, :] = x1
        h2 = _ln(x1) * (1.0 + rows(sc2_ref)) + rows(sh2_ref)
        h2_ref[rs, :] = h2
        logits = jnp.dot(h2.astype(BF16), wr_ref[...], preferred_element_type=F32) + br_ref[...]

        gl = jnp.where(lane < N_EXPERT_GROUPS, logits, neg)
        gp = jnp.exp(gl - jnp.max(gl, axis=-1, keepdims=True))
        gp = gp / jnp.sum(gp, axis=-1, keepdims=True)
        g_val = jnp.max(gp, axis=-1, keepdims=True)
        g_idx = jnp.min(jnp.where(gp == g_val, lane, big), axis=-1, keepdims=True)
        lo = ROUTE_LANE0 + EXPERTS_PER_GROUP * g_idx
        emask = (lane >= lo) & (lane < lo + EXPERTS_PER_GROUP)
        el = jnp.where(emask, logits, neg)
        ep = jnp.exp(el - jnp.max(el, axis=-1, keepdims=True))
        ep = jnp.where(emask, ep / jnp.sum(ep, axis=-1, keepdims=True), -1.0)
        v1 = jnp.max(ep, axis=-1, keepdims=True)
        i1 = jnp.min(jnp.where(ep == v1, lane, big), axis=-1, keepdims=True)
        ep2 = jnp.where(lane == i1, -1.0, ep)
        v2 = jnp.max(ep2, axis=-1, keepdims=True)
        i2 = jnp.min(jnp.where(ep2 == v2, lane, big), axis=-1, keepdims=True)
        vs = v1 + v2
        w1 = g_val * (v1 / vs)
        w2 = g_val * (v2 / vs)
        hit = jnp.where((lane == i1) | (lane == i2), 1.0, 0.0)
        before = jnp.dot(tri_ref[:hm, :hm], hit.astype(BF16), preferred_element_type=F32) + cnt
        r1 = jnp.sum(jnp.where(lane == i1, before, 0.0), axis=-1, keepdims=True)
        r2 = jnp.sum(jnp.where(lane == i2, before, 0.0), axis=-1, keepdims=True)
        cnt = cnt + jnp.sum(hit, axis=0, keepdims=True)
        e1 = i1 - ROUTE_LANE0
        e2 = i2 - ROUTE_LANE0
        route = jnp.zeros((hm, LANES), F32)
        for n, val in enumerate((e1, e2, w1, w2, r1, r2)):
            route = jnp.where(lane == n, val, route)
        route_ref[rs, :] = route
    cnt_ref[...] = jnp.broadcast_to(cnt, cnt_ref.shape)


def _post(attn, yssm, x, mod, per_row, w, cnt0, tail=None):
    nb, l, _ = x.shape
    tm = min(TM, l)
    nt = l // tm
    n_body = nb * nt
    has_tail = tail is not None
    if has_tail:
        assert tail.shape == (tm, D_MODEL)

    def bi(s):
        s = jnp.minimum(s, n_body - 1)
        return s // nt, s % nt

    row = lambda wd: pl.BlockSpec((None, tm, wd), lambda s: (*bi(s), 0))
    if per_row:
        mspec = lambda chunk: pl.BlockSpec((None, tm, D_MODEL), lambda s: (*bi(s), chunk))
    else:
        mspec = lambda chunk: pl.BlockSpec((None, 1, D_MODEL), lambda s: (bi(s)[0], 0, chunk))
    full = lambda a: pl.BlockSpec(a.shape, lambda s: (0,) * a.ndim)
    tri = jnp.tril(jnp.ones((tm, tm), F32), -1).astype(BF16)
    consts = (w["glu_w"], w["glu_b"], w["attn_g"], w["ssm_g"], w["w_out"], w["ln1_g"], w["ln1_b"], w["wr"], w["br"],
              tri, cnt0) + ((tail,) if has_tail else ())
    assert 6 + len(consts) == N_POST_INPUTS + int(has_tail)
    n_steps = n_body + int(has_tail)
    return pl.pallas_call(
        functools.partial(_post_kernel, has_tail=has_tail),
        out_shape=(jax.ShapeDtypeStruct((nb, l, D_MODEL), F32),
                   jax.ShapeDtypeStruct((n_steps * tm, D_MODEL), F32),
                   jax.ShapeDtypeStruct((nb, l, LANES), F32), jax.ShapeDtypeStruct((SUBLANES, LANES), F32)),
        grid=(n_steps,),
        in_specs=[row(D_ATTN), row(D_SSM), row(D_MODEL), mspec(2), mspec(3), mspec(4)] + [full(a) for a in consts],
        out_specs=(row(D_MODEL), pl.BlockSpec((tm, D_MODEL), lambda s: (s, 0)),
                   row(LANES), pl.BlockSpec((SUBLANES, LANES), lambda s: (0, 0))),
        compiler_params=_cparams(("arbitrary",)),
        name="post",
    )(attn, yssm, x, mod, mod, mod, *consts)


MOE_LOOKAHEAD = 2
MOE_SLOTS = MOE_LOOKAHEAD + 1


def _moe_kernel(te_ref, nv_ref, first_ref, nxt_ref, par_ref, rt0_ref, rt1_ref, rtn_ref, h2_ref, wg_hbm, wu_hbm, wd_hbm,
                y_ref, buf_ref, wgf_ref, wuf_ref, wdf_ref, wgb_ref, wub_ref, wdb_ref, sem, wsem):
    i = pl.program_id(0)
    nv = nv_ref[0]
    slot = lax.rem(i, MOE_SLOTS)
    w_pairs = ((wg_hbm, wgf_ref), (wu_hbm, wuf_ref), (wd_hbm, wdf_ref))

    def gather(ids_ref, dst_slot):
        _gather_rows(h2_ref, ids_ref, 1, 0, buf_ref.at[dst_slot], sem.at[dst_slot], (0,))

    def fetch_weights(e, ws):
        for src, dst in w_pairs:
            pltpu.make_async_copy(src.at[e], dst.at[ws], wsem.at[ws]).start(priority=1)

    @pl.when(i == 0)
    def _():
        gather(rt0_ref, 0)
        gather(rt1_ref, 1)
        fetch_weights(te_ref[0], par_ref[0])

    @pl.when(i < nv + MOE_LOOKAHEAD)
    def _():
        pltpu.make_async_copy(h2_ref.at[pl.ds(0, TM_MOE // SUBLANES)], buf_ref.at[slot], sem.at[slot]).wait()

    @pl.when((i < nv) & (first_ref[i] == 1))
    def _():
        ws = par_ref[i]
        for src, dst in w_pairs:
            pltpu.make_async_copy(src.at[0], dst.at[ws], wsem.at[ws]).wait()
        wgb_ref[...] = wgf_ref[ws].astype(BF16)
        wub_ref[...] = wuf_ref[ws].astype(BF16)
        wdb_ref[...] = wdf_ref[ws].astype(BF16)

        @pl.when(nxt_ref[i] >= 0)
        def _():
            fetch_weights(nxt_ref[i], 1 - ws)

    @pl.when(i < nv)
    def _():
        nslot = lax.rem(i + MOE_LOOKAHEAD, MOE_SLOTS)
        n_kc = D_MODEL // MXU_DIM
        groups_per_kc = TM_MOE // SUBLANES // n_kc
        hg = hu = None
        for kc in range(n_kc):
            for gg in range(groups_per_kc):
                _gather_group(h2_ref, rtn_ref, 1, 0, buf_ref.at[nslot], sem.at[nslot], (0,), kc * groups_per_kc + gg)
            ks = slice(kc * MXU_DIM, (kc + 1) * MXU_DIM)
            xk = buf_ref[slot, :, :, ks].reshape(TM_MOE, MXU_DIM).astype(BF16)
            pg = jnp.dot(xk, wgb_ref[ks, :], preferred_element_type=F32)
            pu = jnp.dot(xk, wub_ref[ks, :], preferred_element_type=F32)
            hg = pg if hg is None else hg + pg
            hu = pu if hu is None else hu + pu
        act = (hg * jax.nn.sigmoid(hg)) * hu
        y_ref[...] = jnp.dot(act.astype(BF16), wdb_ref[...], preferred_element_type=F32)

    @pl.when(i >= nv)
    def _():
        y_ref[...] = jnp.zeros(y_ref.shape, y_ref.dtype)


def _moe(tile_meta, n_valid, row_token, h2, w_gate, w_up, w_down):
    n_steps = row_token.shape[0] // TM_MOE
    rt = row_token.reshape(n_steps, 1, TM_MOE)
    te, first, nxt, par = tile_meta
    any_spec = pl.BlockSpec(memory_space=pl.ANY)
    return pl.pallas_call(
        _moe_kernel,
        out_shape=jax.ShapeDtypeStruct((n_steps * TM_MOE, D_MODEL), F32),
        grid_spec=pltpu.PrefetchScalarGridSpec(
            num_scalar_prefetch=5, grid=(n_steps,),
            in_specs=[*_index_specs(TM_MOE, 0, n_steps, MOE_LOOKAHEAD), any_spec, any_spec, any_spec, any_spec],
            out_specs=pl.BlockSpec((TM_MOE, D_MODEL), lambda i, *_: (i, 0)),
            scratch_shapes=[pltpu.VMEM((MOE_SLOTS, TM_MOE // SUBLANES, SUBLANES, D_MODEL), F32),
                            pltpu.VMEM((2, D_MODEL, D_EXPERT), F32), pltpu.VMEM((2, D_MODEL, D_EXPERT), F32),
                            pltpu.VMEM((2, D_EXPERT, D_MODEL), F32),
                            pltpu.VMEM((D_MODEL, D_EXPERT), BF16), pltpu.VMEM((D_MODEL, D_EXPERT), BF16),
                            pltpu.VMEM((D_EXPERT, D_MODEL), BF16),
                            pltpu.SemaphoreType.DMA((MOE_SLOTS,)), pltpu.SemaphoreType.DMA((2,))]),
        compiler_params=_cparams(("arbitrary",)),
        name="moe",
    )(te, n_valid, first, nxt, par, rt, rt, rt, h2.reshape(-1, SUBLANES, D_MODEL), w_gate, w_up, w_down)


def _final_kernel(pos0_ref, posn_ref, x1_ref, g2_ref, route_ref, l2g_ref, l2b_ref, ys_ref, o_ref, buf_ref, sem, *, tm):
    step = pl.program_id(0)
    slot = step & 1

    def gather(ids_ref, dst_slot):
        for k in range(2):
            _gather_rows(ys_ref, ids_ref, 1, k * tm, buf_ref.at[dst_slot, k], sem.at[dst_slot], (0, 1))

    @pl.when(step == 0)
    def _():
        gather(pos0_ref, 0)

    for k in range(2):
        pltpu.make_async_copy(ys_ref.at[pl.ds(0, tm // SUBLANES)], buf_ref.at[slot, k], sem.at[slot]).wait()

    route = route_ref[...]
    w1, w2 = route[:, 2:3], route[:, 3:4]
    n_groups = tm // SUBLANES

    def combine(issue_next):
        n_chunks = D_MODEL // MXU_DIM
        per_chunk = 2 * n_groups // n_chunks
        for c in range(n_chunks):
            if issue_next:
                for q in range(c * per_chunk, (c + 1) * per_chunk):
                    k, g = q // n_groups, q % n_groups
                    _gather_group(ys_ref, posn_ref, 1, k * tm, buf_ref.at[1 - slot, k], sem.at[1 - slot], (0, 1), g)
            cs = slice(c * MXU_DIM, (c + 1) * MXU_DIM)
            f = (w1 * buf_ref[slot, 0, :, :, cs].reshape(tm, MXU_DIM)
                 + w2 * buf_ref[slot, 1, :, :, cs].reshape(tm, MXU_DIM))
            o_ref[:, cs] = DEEPNORM_ALPHA * x1_ref[:, cs] + g2_ref[:, cs] * f

    @pl.when(step + 1 < pl.num_programs(0))
    def _():
        combine(True)

    @pl.when(step + 1 == pl.num_programs(0))
    def _():
        combine(False)

    o_ref[...] = _ln(o_ref[...]) * l2g_ref[...] + l2b_ref[...]


def _final(pos_tiles, first_tile, x1, mod, per_row, route, ln2_g, ln2_b, ys):
    nb, l, _ = x1.shape
    tm = min(TM, l)
    nt = l // tm
    n_steps = nb * nt
    row = lambda wd: pl.BlockSpec((None, tm, wd), lambda s: (s // nt, s % nt, 0))
    if per_row:
        g2 = pl.BlockSpec((None, tm, D_MODEL), lambda s: (s // nt, s % nt, 5))
    else:
        g2 = pl.BlockSpec((None, 1, D_MODEL), lambda s: (s // nt, 0, 5))
    vec = pl.BlockSpec((1, D_MODEL), lambda s: (0, 0))
    return pl.pallas_call(
        functools.partial(_final_kernel, tm=tm),
        out_shape=jax.ShapeDtypeStruct((nb, l, D_MODEL), F32),
        grid=(n_steps,),
        in_specs=[*_index_specs(2 * tm, first_tile, n_steps), row(D_MODEL), g2, row(LANES), vec, vec,
                  pl.BlockSpec(memory_space=pl.ANY)],
        out_specs=row(D_MODEL),
        scratch_shapes=[pltpu.VMEM((2, 2, tm // SUBLANES, SUBLANES, D_MODEL), F32), pltpu.SemaphoreType.DMA((2,))],
        compiler_params=_cparams(("arbitrary",)),
        name="final",
    )(pos_tiles, pos_tiles, x1, mod, route, ln2_g, ln2_b, ys.reshape(-1, SUBLANES, D_MODEL))


def _rope_tables(pos):
    half = ROT_DIM // 2
    inv_freq = ROPE_THETA ** (-jnp.arange(half, dtype=jnp.float32) * 2.0 / ROT_DIM)
    ang = pos.astype(jnp.float32)[:, None] * inv_freq[None, :]
    cos, sin = jnp.cos(ang), jnp.sin(ang)
    n = pos.shape[0]
    one = jnp.ones((n, HEAD_DIM - ROT_DIM), F32)
    zero = jnp.zeros((n, HEAD_DIM - half), F32)
    c = jnp.concatenate([cos, cos, one], -1)
    a = jnp.concatenate([-sin, zero], -1)
    b = jnp.concatenate([jnp.zeros((n, half), F32), sin, jnp.zeros((n, HEAD_DIM - ROT_DIM), F32)], -1)
    return tuple(jnp.tile(t, (1, LANES // HEAD_DIM)) for t in (c, a, b))


def _s5_params(lam_re, lam_im, log_step, b_re, b_im, c_re, c_im):
    f32 = jnp.float32
    dt = jnp.exp(log_step.astype(f32))[:, None]
    lr, li = lam_re.astype(f32), lam_im.astype(f32)
    mag = jnp.exp(lr * dt)
    ar, ai = mag * jnp.cos(li * dt), mag * jnp.sin(li * dt)
    den = lr * lr + li * li
    cr = ((ar - 1.0) * lr + ai * li) / den
    ci = (ai * lr - (ar - 1.0) * li) / den
    br, bi = b_re.astype(f32), b_im.astype(f32)
    bbr = cr[..., None] * br - ci[..., None] * bi
    bbi = cr[..., None] * bi + ci[..., None] * br
    gpt = MXU_DIM // SSM_CH
    eye = jnp.eye(gpt, dtype=f32)

    def pack_b(m):
        m = m.reshape(N_KT, gpt, SSM_STATE, SSM_CH)
        return jnp.einsum("kgpc,gh->kgchp", m, eye).reshape(N_KT, MXU_DIM, ST_PER_KT).astype(BF16)

    def pack_c(m):
        m = m.astype(f32).reshape(N_KT, gpt, SSM_CH, SSM_STATE)
        return jnp.einsum("kgcp,gh->khpgc", m, eye).reshape(N_KT, ST_PER_KT, MXU_DIM).astype(BF16)

    wts = (pack_b(bbr), pack_b(bbi), pack_c(c_re), pack_c(-c_im.astype(f32)))
    return wts, ar.reshape(1, N_STATE), ai.reshape(1, N_STATE)


def _scan_tables(ar, ai, nb):
    assert SUBLANES // nb == 2
    a2r, a2i = ar * ar - ai * ai, 2.0 * ar * ai
    z = jnp.zeros_like(ar)
    rep = lambda first, second: jnp.concatenate([jnp.tile(first, (nb, 1)), jnp.tile(second, (nb, 1))], 0)
    return rep(z, ar), rep(z, ai), rep(ar, a2r), rep(ai, a2i)


def kernel(x_prompt, x_sample, cache_k_win, cache_v_win, state_ssm_re, state_ssm_im, c_prompt, c_sample, ada_w, ada_b,
           w_in, attn_sinks, ssm_lambda_re, ssm_lambda_im, ssm_log_step, ssm_b_re, ssm_b_im, ssm_c_re, ssm_c_im, ssm_d,
           ssm_glu_w, ssm_glu_b, attn_norm_g, ssm_norm_g, w_out, ln1_g, ln1_b, router_group_w, router_group_b,
           router_expert_w, router_expert_b, exp_w_gate, exp_w_up, exp_w_down, ln2_g, ln2_b):
    assert ada_w.shape[0] == DEPTH
    bp, lp, _ = x_prompt.shape
    bs, ls, _ = x_sample.shape
    lsp = SAMPLE_PAD

    w_in_bf = w_in[0].astype(BF16)
    wr = jnp.concatenate([router_group_w[0], router_expert_w[0]], -1)
    wr = jnp.pad(wr, ((0, 0), (0, LANES - wr.shape[1]))).astype(BF16)
    br = jnp.pad(jnp.concatenate([router_group_b[0], router_expert_b[0]], -1), (0, LANES - N_EXPERT_GROUPS - N_EXPERTS))
    wpost = dict(glu_w=ssm_glu_w[0].astype(BF16), glu_b=ssm_glu_b[0][None], attn_g=attn_norm_g[0][None],
                 ssm_g=ssm_norm_g[0][None], w_out=w_out[0].astype(BF16), ln1_g=ln1_g[0][None], ln1_b=ln1_b[0][None],
                 wr=wr, br=br[None])
    s5w, ar, ai = _s5_params(ssm_lambda_re[0], ssm_lambda_im[0], ssm_log_step[0], ssm_b_re[0], ssm_b_im[0],
                             ssm_c_re[0], ssm_c_im[0])
    d_skip = ssm_d[0][None]
    sinks = attn_sinks[0]

    n_c = bp + bs
    n_cp = -(-n_c // SUBLANES) * SUBLANES
    c_all = jnp.pad(jnp.concatenate([c_prompt, c_sample], 0), ((0, n_cp - n_c), (0, 0)))
    mod = _ada(c_all, ada_w[0], ada_b[0][None])
    mod_p = mod[:bp][:, None, :]
    mod_s = jnp.repeat(mod[bp:n_c], lsp, axis=0)[None]

    xs_pad = jnp.pad(x_sample, ((0, 0), (0, lsp - ls), (0, 0))).reshape(1, bs * lsp, D_MODEL)

    rope_p = _rope_tables(jnp.arange(lp))
    pos_s = PAST_LEN + jnp.minimum(jnp.arange(lsp), ls - 1)
    rope_s = tuple(jnp.tile(t, (bs, 1)) for t in _rope_tables(pos_s))
    q_p, k_p, v_p, u_p = _inproj(x_prompt, mod_p, w_in_bf, rope_p, False)
    q_s, k_s, v_s, u_s = _inproj(xs_pad, mod_s, w_in_bf, rope_s, True)

    attn_p = _attn_prompt(sinks, q_p, k_p, v_p)
    k_s3 = k_s.reshape(bs, lsp, D_KV)
    v_s3 = v_s.reshape(bs, lsp, D_KV)
    ck = cache_k_win[0].reshape(bs, WINDOW, D_KV)
    cv = cache_v_win[0].reshape(bs, WINDOW, D_KV)
    padk = lambda new: jnp.pad(new, ((0, 0), (0, WINDOW - lsp), (0, 0)))
    attn_s = _attn_sample(sinks, q_s.reshape(bs, lsp, D_ATTN), jnp.concatenate([ck, padk(k_s3)], 1),
                          jnp.concatenate([cv, padk(v_s3)], 1))

    tabs = _scan_tables(ar, ai, bp)
    zero_carry = jnp.zeros((SUBLANES, N_STATE), F32)
    y_p, hr_p, hi_p = _s5_prompt(u_p, s5w, tabs, d_skip, zero_carry, zero_carry)
    u_tb = jnp.transpose(u_s.reshape(bs, lsp, D_SSM)[:, :ls], (1, 0, 2))
    y_tb, hr_s, hi_s = _s5_sample(u_tb, s5w, ar, ai, d_skip, state_ssm_re[0].reshape(bs, N_STATE),
                                  state_ssm_im[0].reshape(bs, N_STATE))
    y_s = jnp.pad(jnp.transpose(y_tb, (1, 0, 2)), ((0, 0), (0, lsp - ls), (0, 0))).reshape(1, bs * lsp, D_SSM)

    cnt0 = jnp.zeros((SUBLANES, LANES), F32)
    n_p, n_s = bp * lp, bs * lsp
    n_tok = n_p + n_s
    x1_s, h2_s, route_s, cnt_s = _post(attn_s.reshape(1, bs * lsp, D_ATTN), y_s, xs_pad, mod_s, True, wpost, cnt0)
    x1_p, h2, route_p, cnt = _post(attn_p, y_p, x_prompt, mod_p, False, wpost, cnt_s, tail=h2_s)

    n_tiles = -(-(2 * n_tok + N_EXPERTS * (TM_MOE - 1)) // TM_MOE) + MOE_LOOKAHEAD
    route = jnp.concatenate([route_p.reshape(n_p, LANES), route_s.reshape(n_s, LANES)], 0)
    counts = cnt[0, ROUTE_LANE0:ROUTE_LANE0 + N_EXPERTS].astype(jnp.int32)
    padded = ((counts + TM_MOE - 1) // TM_MOE) * TM_MOE
    ends = jnp.cumsum(padded)
    offs = ends - padded
    col = lambda c: route[:, c].astype(jnp.int32)
    expert_ids = jnp.arange(N_EXPERTS, dtype=jnp.int32)
    seg_start = lambda e: jnp.sum(jnp.where(e[:, None] == expert_ids[None, :], offs[None, :], 0), axis=1)
    pos_k = [seg_start(col(k)) + col(4 + k) for k in range(2)]
    n_valid = ends[-1] // TM_MOE
    tile_row = jnp.minimum(jnp.arange(n_tiles), n_valid - 1) * TM_MOE
    tile_expert = jnp.sum((ends[None, :] <= tile_row[:, None]).astype(jnp.int32), axis=1)
    first_tile = jnp.concatenate([jnp.ones((1,), jnp.int32), (tile_expert[1:] != tile_expert[:-1]).astype(jnp.int32)])
    first_tile = first_tile * (jnp.arange(n_tiles) < n_valid)
    nonempty = padded > 0
    later =jnp.where(nonempty[None, :] & (expert_ids[None, :] > expert_ids[:, None]), expert_ids[None, :], N_EXPERTS)
    next_expert = jnp.min(later, axis=1)
    next_expert = jnp.where(next_expert < N_EXPERTS, next_expert, -1).astype(jnp.int32)
    weight_slot = ((jnp.cumsum(nonempty.astype(jnp.int32)) - 1) & 1).astype(jnp.int32)
    tile_meta = (tile_expert, first_tile.astype(jnp.int32), next_expert[tile_expert], weight_slot[tile_expert])
    token_ids = jnp.arange(n_tok, dtype=jnp.int32)
    row_token = (jnp.arange(n_tiles * TM_MOE, dtype=jnp.int32) % n_tok).at[jnp.concatenate(pos_k)].set(
        jnp.concatenate([token_ids, token_ids]))

    ys = _moe(tile_meta, n_valid.reshape(1).astype(jnp.int32), row_token, h2, exp_w_gate[0], exp_w_up[0],
              exp_w_down[0])
    pos_tiles = jnp.concatenate([p.reshape(n_tok // TM, 1, TM) for p in pos_k], axis=2)
    y_prompt = _final(pos_tiles, 0, x1_p, mod_p, False, route_p, ln2_g[0][None], ln2_b[0][None], ys)
    y_samp = _final(pos_tiles, n_p // TM, x1_s, mod_s, True, route_s, ln2_g[0][None], ln2_b[0][None], ys)
    y_sample = y_samp.reshape(bs, lsp, D_MODEL)[:, :ls]

    kv5 = lambda a, n: a.reshape(1, a.shape[0], n, N_KV_HEADS, HEAD_DIM)
    k_win_p = kv5(k_p[:, lp - WINDOW:], WINDOW)
    v_win_p = kv5(v_p[:, lp - WINDOW:], WINDOW)
    k_win_s = kv5(jnp.concatenate([ck[:, ls:], k_s3[:, :ls]], 1), WINDOW)
    v_win_s = kv5(jnp.concatenate([cv[:, ls:], v_s3[:, :ls]], 1), WINDOW)
    st = lambda a: a.reshape(1, a.shape[0], N_SSM_GROUPS, SSM_STATE)
    return (y_prompt, y_sample, k_win_p, v_win_p, st(hr_p[bp:2 * bp]), st(hi_p[bp:2 * bp]),
            k_win_s, v_win_s, st(hr_s), st(hi_s))
```

```python
import functools
import math

import jax
import jax.numpy as jnp
from jax import lax
from jax.experimental import pallas as pl
from jax.experimental.pallas import tpu as pltpu

F32 = jnp.float32
BF16 = jnp.bfloat16

D_MODEL = 2048
D_ATTN = 1024
D_SSM = 1024
HEAD_DIM = 64
N_HEADS = 16
N_KV_HEADS = 4
REP = 4
D_KV = 256
ROT_DIM = 16
ROPE_THETA = 500000.0
WINDOW = 128
SSM_CH = 16
N_SSM_GROUPS = 64
SSM_STATE = 64
N_STATE = N_SSM_GROUPS * SSM_STATE
PROJ_COLS = D_ATTN + 2 * D_KV + D_SSM
N_EXPERT_GROUPS = 4
EXPERTS_PER_GROUP = 8
N_EXPERTS = 32
D_EXPERT = 512
DEPTH = 1
DEEPNORM_ALPHA = (2.0 * DEPTH) ** 0.25
LN_EPS = 1e-5
PAST_LEN = 16384

LANES = 128
SUBLANES = 8
MXU_DIM = 256
TM = 256
TM_MOE = 256
TM_INPROJ = 512
S5_TT = 128
SAMPLE_PAD = 8
ROUTE_LANE0 = N_EXPERT_GROUPS
VMEM_LIMIT = 56 * 1024 * 1024


def _cparams(sem):
    return pltpu.CompilerParams(dimension_semantics=sem, vmem_limit_bytes=VMEM_LIMIT)


def _ln(x):
    mu = jnp.mean(x, axis=-1, keepdims=True)
    xc = x - mu
    var = jnp.mean(xc * xc, axis=-1, keepdims=True)
    return xc * lax.rsqrt(var + LN_EPS)


def _rms(x):
    return x * lax.rsqrt(jnp.mean(x * x, axis=-1, keepdims=True) + LN_EPS)


def _ada_kernel(c_ref, w_ref, b_ref, o_ref):
    c = c_ref[...]
    s = c * jax.nn.sigmoid(c)
    o_ref[...] = jnp.dot(s.astype(BF16), w_ref[...].astype(BF16), preferred_element_type=F32) + b_ref[...]


def _ada(c_all, ada_w, ada_b):
    n, tn = c_all.shape[0], 1024
    return pl.pallas_call(
        _ada_kernel,
        out_shape=jax.ShapeDtypeStruct((n, 6 * D_MODEL), F32),
        grid=(6 * D_MODEL // tn,),
        in_specs=[pl.BlockSpec((n, D_MODEL), lambda j: (0, 0)),
                  pl.BlockSpec((D_MODEL, tn), lambda j: (0, j)),
                  pl.BlockSpec((1, tn), lambda j: (0, j))],
        out_specs=pl.BlockSpec((n, tn), lambda j: (0, j)),
        compiler_params=_cparams(("arbitrary",)),
        name="ada",
    )(c_all, ada_w, ada_b)


def _mod_spec(per_row, tm, chunk):
    if per_row:
        return pl.BlockSpec((None, tm, D_MODEL), lambda b, i: (b, i, chunk))
    return pl.BlockSpec((None, 1, D_MODEL), lambda b, i: (b, 0, chunk))


def _inproj_kernel(x_ref, sh_ref, sc_ref, w_ref, rc_ref, ra_ref, rb_ref, q_ref, k_ref, v_ref, u_ref):
    h = _ln(x_ref[...]) * (1.0 + sc_ref[...]) + sh_ref[...]
    proj = jnp.dot(h.astype(BF16), w_ref[...], preferred_element_type=F32)
    rc, ra, rb = rc_ref[...], ra_ref[...], rb_ref[...]

    def rope(t):
        return t * rc + pltpu.roll(t, LANES - ROT_DIM // 2, 1) * ra + pltpu.roll(t, ROT_DIM // 2, 1) * rb

    for j in range(D_ATTN // LANES):
        q_ref[:, j * LANES:(j + 1) * LANES] = (rope(proj[:, j * LANES:(j + 1) * LANES]) * HEAD_DIM ** -0.5).astype(BF16)
    for j in range(D_KV // LANES):
        c0 = D_ATTN + j * LANES
        k_ref[:, j * LANES:(j + 1) * LANES] = rope(proj[:, c0:c0 + LANES])
    v_ref[...] = proj[:, D_ATTN + D_KV:D_ATTN + 2 * D_KV]
    u_ref[...] = proj[:, D_ATTN + 2 * D_KV:]


def _inproj(x, mod, w_in_bf, rope_tabs, per_row):
    nb, l, _ = x.shape
    tm = min(TM_INPROJ, l)
    row = lambda w: pl.BlockSpec((None, tm, w), lambda b, i: (b, i, 0))
    tab = pl.BlockSpec((tm, LANES), lambda b, i: (i, 0))
    return pl.pallas_call(
        _inproj_kernel,
        out_shape=(jax.ShapeDtypeStruct((nb, l, D_ATTN), BF16), jax.ShapeDtypeStruct((nb, l, D_KV), F32),
                   jax.ShapeDtypeStruct((nb, l, D_KV), F32), jax.ShapeDtypeStruct((nb, l, D_SSM), F32)),
        grid=(nb, l // tm),
        in_specs=[row(D_MODEL), _mod_spec(per_row, tm, 0), _mod_spec(per_row, tm, 1),
                  pl.BlockSpec((D_MODEL, PROJ_COLS), lambda b, i: (0, 0)), tab, tab, tab],
        out_specs=(row(D_ATTN), row(D_KV), row(D_KV), row(D_SSM)),
        compiler_params=_cparams(("arbitrary", "arbitrary")),
        name="inproj",
    )(x, mod, mod, w_in_bf, *rope_tabs)


ATTN_SAMPLE_BATCH = 8


def _attn_kernel(sink_ref, q_ref, kp_ref, kc_ref, vp_ref, vc_ref, o_ref, *, lq, prev_from_block):
    m_rows = REP * lq
    ii = lax.broadcasted_iota(jnp.int32, (m_rows, WINDOW), 0) & (lq - 1)
    jj = lax.broadcasted_iota(jnp.int32, (m_rows, WINDOW), 1)
    from_prev = jj > ii
    if prev_from_block:
        dead = jj > ii + jnp.where(pl.program_id(1) > 0, WINDOW, 0)
    rr = lax.broadcasted_iota(jnp.int32, (m_rows, 1), 0)
    dn = (((1,), (1,)), ((), ()))
    for bi in range(q_ref.shape[0]):
        for g in range(N_KV_HEADS):
            qg = q_ref[bi, :, g * REP * HEAD_DIM:(g + 1) * REP * HEAD_DIM].astype(F32)
            qs = jnp.concatenate([qg[:, r * HEAD_DIM:(r + 1) * HEAD_DIM] for r in range(REP)], axis=0).astype(BF16)
            ks = slice(g * HEAD_DIM, (g + 1) * HEAD_DIM)
            s_p = lax.dot_general(qs, kp_ref[bi, :, ks].astype(BF16), dn, preferred_element_type=F32)
            s_c = lax.dot_general(qs, kc_ref[bi, :, ks].astype(BF16), dn, preferred_element_type=F32)
            s = jnp.where(from_prev, s_p, s_c)
            if prev_from_block:
                s = jnp.where(dead, -jnp.inf, s)
            sink = jnp.zeros((m_rows, 1), F32)
            for r in range(REP):
                sink = jnp.where((rr >= r * lq) & (rr < (r + 1) * lq), sink_ref[g * REP + r], sink)
            m = jnp.maximum(jnp.max(s, axis=-1, keepdims=True), sink)
            p = jnp.exp(s - m)
            p = p / (jnp.sum(p, axis=-1, keepdims=True) + jnp.exp(sink - m))
            o = (jnp.dot(jnp.where(from_prev, p, 0.0).astype(BF16), vp_ref[bi, :, ks].astype(BF16),
                         preferred_element_type=F32)
                 + jnp.dot(jnp.where(from_prev, 0.0, p).astype(BF16), vc_ref[bi, :, ks].astype(BF16),
                           preferred_element_type=F32))
            for r in range(REP):
                h = g * REP + r
                o_ref[bi, :, h * HEAD_DIM:(h + 1) * HEAD_DIM] = o[r * lq:(r + 1) * lq]


def _attn_prompt(sinks, q, k, v):
    nb, l, _ = q.shape
    nblk = l // WINDOW
    cur = lambda w: pl.BlockSpec((1, WINDOW, w), lambda b, n: (b, n, 0))
    prev = lambda w: pl.BlockSpec((1, WINDOW, w), lambda b, n: (b, jnp.maximum(n - 1, 0), 0))
    return pl.pallas_call(
        functools.partial(_attn_kernel, lq=WINDOW, prev_from_block=True),
        out_shape=jax.ShapeDtypeStruct((nb, l, D_ATTN), F32),
        grid=(nb, nblk),
        in_specs=[pl.BlockSpec(memory_space=pltpu.SMEM), cur(D_ATTN), prev(D_KV), cur(D_KV), prev(D_KV), cur(D_KV)],
        out_specs=cur(D_ATTN),
        compiler_params=_cparams(("arbitrary", "arbitrary")),
        name="attn_prompt",
    )(sinks, q, k, k, v, v)


ATTN_SAMPLE_JOINT = 4


def _attn_sample_kernel(sink_ref, q_ref, kp_ref, kc_ref, vp_ref, vc_ref, o_ref, *, lq):
    gb = ATTN_SAMPLE_JOINT
    m_b = REP * lq
    m, n = gb * m_b, gb * WINDOW
    row = lax.broadcasted_iota(jnp.int32, (m, n), 0)
    col = lax.broadcasted_iota(jnp.int32, (m, n), 1)
    same = (row // m_b) == (col // WINDOW)
    from_prev = (col & (WINDOW - 1)) > (row & (lq - 1))
    rep_of_row = (lax.broadcasted_iota(jnp.int32, (m, 1), 0) // lq) % REP
    dn = (((1,), (1,)), ((), ()))
    for blk in range(q_ref.shape[0] // gb):
        bs = range(blk * gb, (blk + 1) * gb)
        for g in range(N_KV_HEADS):
            qg = [q_ref[b, :, g * REP * HEAD_DIM:(g + 1) * REP * HEAD_DIM].astype(F32) for b in bs]
            qs = jnp.concatenate([qb[:, r * HEAD_DIM:(r + 1) * HEAD_DIM] for qb in qg for r in range(REP)],
                                 axis=0).astype(BF16)
            ks = slice(g * HEAD_DIM, (g + 1) * HEAD_DIM)
            stack = lambda ref: jnp.concatenate([ref[b, :, ks] for b in bs], axis=0).astype(BF16)
            s_p = lax.dot_general(qs, stack(kp_ref), dn, preferred_element_type=F32)
            s_c = lax.dot_general(qs, stack(kc_ref), dn, preferred_element_type=F32)
            s = jnp.where(same, jnp.where(from_prev, s_p, s_c), -jnp.inf)
            sink = jnp.zeros((m, 1), F32)
            for r in range(REP):
                sink = jnp.where(rep_of_row == r, sink_ref[g * REP + r], sink)
            mx = jnp.maximum(jnp.max(s, axis=-1, keepdims=True), sink)
            p = jnp.exp(s - mx)
            p = p / (jnp.sum(p, axis=-1, keepdims=True) + jnp.exp(sink - mx))
            o = (jnp.dot(jnp.where(from_prev, p, 0.0).astype(BF16), stack(vp_ref), preferred_element_type=F32)
                 + jnp.dot(jnp.where(from_prev, 0.0, p).astype(BF16), stack(vc_ref), preferred_element_type=F32))
            for bi, b in enumerate(bs):
                for r in range(REP):
                    h = g * REP + r
                    o_ref[b, :, h * HEAD_DIM:(h + 1) * HEAD_DIM] = o[bi * m_b + r * lq:bi * m_b + (r + 1) * lq]


def _attn_sample(sinks, q, kcat, vcat):
    nb, lq, _ = q.shape
    nbb = ATTN_SAMPLE_BATCH
    kblk = lambda n: pl.BlockSpec((nbb, WINDOW, D_KV), lambda b: (b, n, 0))
    return pl.pallas_call(
        functools.partial(_attn_sample_kernel, lq=lq),
        out_shape=jax.ShapeDtypeStruct((nb, lq, D_ATTN), F32),
        grid=(nb // nbb,),
        in_specs=[pl.BlockSpec(memory_space=pltpu.SMEM), pl.BlockSpec((nbb, lq, D_ATTN), lambda b: (b, 0, 0)),
                  kblk(0), kblk(1), kblk(0), kblk(1)],
        out_specs=pl.BlockSpec((nbb, lq, D_ATTN), lambda b: (b, 0, 0)),
        compiler_params=_cparams(("arbitrary",)),
        name="attn_sample",
    )(sinks, q, kcat, kcat, vcat, vcat)


N_KT = D_SSM // MXU_DIM
ST_PER_KT = N_STATE // N_KT
N_SLAB = D_SSM // LANES


def _cproj(hr_ref, hi_ref, cre_ref, cimn_ref, kt):
    cs = slice(kt * ST_PER_KT, (kt + 1) * ST_PER_KT)
    return (jnp.dot(hr_ref[:, cs].astype(BF16), cre_ref[kt], preferred_element_type=F32)
            + jnp.dot(hi_ref[:, cs].astype(BF16), cimn_ref[kt], preferred_element_type=F32))


def _s5_prompt_kernel(u_ref, bre_ref, bim_ref, cre_ref, cimn_ref, a1r_ref, a1i_ref, par_ref, pai_ref, d_ref,
                      h0r_ref, h0i_ref, y_ref, cr_ref, ci_ref, il_ref, xr_ref, xi_ref, *, nb, tt):
    rows = nb * tt

    @pl.when(pl.program_id(0) == 0)
    def _():
        cr_ref[...] = h0r_ref[...]
        ci_ref[...] = h0i_ref[...]

    for b in range(nb):
        for j in range(N_SLAB):
            il_ref[j, pl.ds(b, tt, stride=nb), :] = u_ref[b, :, j * LANES:(j + 1) * LANES]
    for kt in range(N_KT):
        ub = jnp.concatenate([il_ref[2 * kt], il_ref[2 * kt + 1]], axis=1).astype(BF16)
        cs = slice(kt * ST_PER_KT, (kt + 1) * ST_PER_KT)
        xr_ref[:, cs] = jnp.dot(ub, bre_ref[kt], preferred_element_type=F32)
        xi_ref[:, cs] = jnp.dot(ub, bim_ref[kt], preferred_element_type=F32)

    half = lax.broadcasted_iota(jnp.int32, (SUBLANES, LANES), 0) < nb

    def step(i, carry):
        r0 = pl.multiple_of(i * SUBLANES, SUBLANES)
        for j in range(N_STATE // LANES):
            ls = slice(j * LANES, (j + 1) * LANES)
            x_r = xr_ref[pl.ds(r0, SUBLANES), ls]
            x_i = xi_ref[pl.ds(r0, SUBLANES), ls]
            s_r = pltpu.roll(x_r, nb, 0)
            s_i = pltpu.roll(x_i, nb, 0)
            a1r, a1i = a1r_ref[:, ls], a1i_ref[:, ls]
            c_r, c_i = cr_ref[:, ls], ci_ref[:, ls]
            par, pai = par_ref[:, ls], pai_ref[:, ls]
            h_r = x_r + (a1r * s_r - a1i * s_i) + (par * c_r - pai * c_i)
            h_i = x_i + (a1r * s_i + a1i * s_r) + (par * c_i + pai * c_r)
            xr_ref[pl.ds(r0, SUBLANES), ls] = h_r
            xi_ref[pl.ds(r0, SUBLANES), ls] = h_i
            cr_ref[:, ls] = jnp.where(half, pltpu.roll(h_r, nb, 0), h_r)
            ci_ref[:, ls] = jnp.where(half, pltpu.roll(h_i, nb, 0), h_i)
        return carry

    lax.fori_loop(0, rows // SUBLANES, step, 0)

    for kt in range(N_KT):
        y = _cproj(xr_ref, xi_ref, cre_ref, cimn_ref, kt)
        il_ref[2 * kt] = y[:, :LANES]
        il_ref[2 * kt + 1] = y[:, LANES:]
    for b in range(nb):
        for j in range(N_SLAB):
            ls = slice(j * LANES, (j + 1) * LANES)
            y_ref[b, :, ls] = il_ref[j, pl.ds(b, tt, stride=nb), :] + d_ref[:, ls] * u_ref[b, :, ls]


def _s5_prompt(u, wts, tabs, d_skip, h0r, h0i):
    nb, l, _ = u.shape
    tt = S5_TT
    rows = nb * tt
    full = lambda a: pl.BlockSpec(a.shape, lambda i: (0,) * a.ndim)
    blk = pl.BlockSpec((nb, tt, D_SSM), lambda i: (0, i, 0))
    carry = jax.ShapeDtypeStruct((SUBLANES, N_STATE), F32)
    return pl.pallas_call(
        functools.partial(_s5_prompt_kernel, nb=nb, tt=tt),
        out_shape=(jax.ShapeDtypeStruct((nb, l, D_SSM), F32), carry, carry),
        grid=(l // tt,),
        in_specs=[blk] + [full(a) for a in (*wts, *tabs, d_skip, h0r, h0i)],
        out_specs=(blk, pl.BlockSpec((SUBLANES, N_STATE), lambda i: (0, 0)),
                   pl.BlockSpec((SUBLANES, N_STATE), lambda i: (0, 0))),
        scratch_shapes=[pltpu.VMEM((N_SLAB, rows, LANES), F32), pltpu.VMEM((rows, N_STATE), F32),
                        pltpu.VMEM((rows, N_STATE), F32)],
        compiler_params=_cparams(("arbitrary",)),
        name="s5_prompt",
    )(u, *wts, *tabs, d_skip, h0r, h0i)


def _s5_sample_kernel(u_ref, bre_ref, bim_ref, cre_ref, cimn_ref, ar_ref, ai_ref, d_ref, h0r_ref, h0i_ref,
                      y_ref, sr_ref, si_ref, xr_ref, xi_ref, *, nt):
    sr_ref[...] = h0r_ref[...]
    si_ref[...] = h0i_ref[...]
    for t in range(nt):
        for kt in range(N_KT):
            ub = u_ref[t, :, kt * MXU_DIM:(kt + 1) * MXU_DIM].astype(BF16)
            cs = slice(kt * ST_PER_KT, (kt + 1) * ST_PER_KT)
            xr_ref[:, cs] = jnp.dot(ub, bre_ref[kt], preferred_element_type=F32)
            xi_ref[:, cs] = jnp.dot(ub, bim_ref[kt], preferred_element_type=F32)
        ar, ai = ar_ref[...], ai_ref[...]
        s_r, s_i = sr_ref[...], si_ref[...]
        sr_ref[...] = xr_ref[...] + (ar * s_r - ai * s_i)
        si_ref[...] = xi_ref[...] + (ar * s_i + ai * s_r)
        for kt in range(N_KT):
            ys = slice(kt * MXU_DIM, (kt + 1) * MXU_DIM)
            y_ref[t, :, ys] = _cproj(sr_ref, si_ref, cre_ref, cimn_ref, kt) + d_ref[:, ys] * u_ref[t, :, ys]


def _s5_sample(u_tb, wts, ar, ai, d_skip, h0r, h0i):
    nt, nb, _ = u_tb.shape
    st = jax.ShapeDtypeStruct((nb, N_STATE), F32)
    args = (u_tb, *wts, ar, ai, d_skip, h0r, h0i)
    full = lambda a: pl.BlockSpec(a.shape, lambda i: (0,) * a.ndim)
    return pl.pallas_call(
        functools.partial(_s5_sample_kernel, nt=nt),
        out_shape=(jax.ShapeDtypeStruct((nt, nb, D_SSM), F32), st, st),
        grid=(1,),
        in_specs=[full(a) for a in args],
        out_specs=(pl.BlockSpec((nt, nb, D_SSM), lambda i: (0, 0, 0)), pl.BlockSpec((nb, N_STATE), lambda i: (0, 0)),
                   pl.BlockSpec((nb, N_STATE), lambda i: (0, 0))),
        scratch_shapes=[pltpu.VMEM((nb, N_STATE), F32), pltpu.VMEM((nb, N_STATE), F32)],
        compiler_params=_cparams(("arbitrary",)),
        name="s5_sample",
    )(*args)


def _gather_group(src_ref, ids_ref, id_stride, id_offset, dst_ref, sem, priorities, g):
    for j in range(SUBLANES):
        t = ids_ref[0, id_stride * (g * SUBLANES + j) + id_offset]
        pltpu.make_async_copy(src_ref.at[t >> 3, pl.ds(t & (SUBLANES - 1), 1)], dst_ref.at[g, pl.ds(j, 1)],
                              sem).start(priority=priorities[j % len(priorities)])


def _gather_rows(src_ref, ids_ref, id_stride, id_offset, dst_ref, sem, priorities):
    def body(g, c):
        _gather_group(src_ref, ids_ref, id_stride, id_offset, dst_ref, sem, priorities, g)
        return c
    lax.fori_loop(0, dst_ref.shape[0], body, 0)


def _index_specs(n_cols, first_tile, n_steps, depth=1):
    def spec(fn):
        return pl.BlockSpec((None, 1, n_cols), fn, memory_space=pltpu.SMEM)

    head = [spec(lambda s, *_, k=k: (first_tile + min(k, n_steps - 1), 0, 0)) for k in range(depth)]
    return (*head, spec(lambda s, *_: (first_tile + jnp.minimum(s + depth, n_steps - 1), 0, 0)))


N_POST_INPUTS = 17
POST_ROW_PARTS = 1


def _post_kernel(*refs, has_tail):
    ins, outs = refs[:N_POST_INPUTS], refs[N_POST_INPUTS + int(has_tail):]
    if not has_tail:
        _post_body(*ins, *outs)
        return
    tail_ref, h2_ref = refs[N_POST_INPUTS], outs[1]
    last = pl.num_programs(0) - 1

    @pl.when(pl.program_id(0) == last)
    def _():
        h2_ref[...] = tail_ref[...]

    @pl.when(pl.program_id(0) < last)
    def _():
        _post_body(*ins, *outs)


def _post_body(attn_ref, yssm_ref, x_ref, g1_ref, sh2_ref, sc2_ref, gluw_ref, glub_ref, ga_ref, gs_ref, wout_ref,
               l1g_ref, l1b_ref, wr_ref, br_ref, tri_ref, cnt0_ref, x1_ref, h2_ref, route_ref, cnt_ref):
    tm = x_ref.shape[0]

    @pl.when(pl.program_id(0) == 0)
    def _():
        cnt_ref[...] = cnt0_ref[...]

    hm = tm // POST_ROW_PARTS
    lane = lax.broadcasted_iota(jnp.int32, (hm, LANES), 1).astype(F32)
    big = float(4 * LANES)
    neg = -jnp.inf
    cnt = cnt_ref[0:1, :]
    for part in range(POST_ROW_PARTS):
        rs = slice(part * hm, (part + 1) * hm)
        rows = lambda ref: ref[rs, :] if ref.shape[0] == tm else ref[...]
        z = jax.nn.gelu(yssm_ref[rs, :])
        ssm = z * jax.nn.sigmoid(jnp.dot(z.astype(BF16), gluw_ref[...], preferred_element_type=F32) + glub_ref[...])
        mixed_a = (_rms(attn_ref[rs, :]) * ga_ref[...]).astype(BF16)
        mixed_s = (_rms(ssm) * gs_ref[...]).astype(BF16)
        o = (jnp.dot(mixed_a, wout_ref[:D_ATTN, :], preferred_element_type=F32)
             + jnp.dot(mixed_s, wout_ref[D_ATTN:, :], preferred_element_type=F32))
        x1 = _ln(DEEPNORM_ALPHA * x_ref[rs, :] + rows(g1_ref) * o) * l1g_ref[...] + l1b_ref[...]
        x1_ref[rs, :] = x1
        h2 = _ln(x1) * (1.0 + rows(sc2_ref)) + rows(sh2_ref)
        h2_ref[rs, :] = h2
        logits = jnp.dot(h2.astype(BF16), wr_ref[...], preferred_element_type=F32) + br_ref[...]

        gl = jnp.where(lane < N_EXPERT_GROUPS, logits, neg)
        gp = jnp.exp(gl - jnp.max(gl, axis=-1, keepdims=True))
        gp = gp / jnp.sum(gp, axis=-1, keepdims=True)
        g_val = jnp.max(gp, axis=-1, keepdims=True)
        g_idx = jnp.min(jnp.where(gp == g_val, lane, big), axis=-1, keepdims=True)
        lo = ROUTE_LANE0 + EXPERTS_PER_GROUP * g_idx
        emask = (lane >= lo) & (lane < lo + EXPERTS_PER_GROUP)
        el = jnp.where(emask, logits, neg)
        ep = jnp.exp(el - jnp.max(el, axis=-1, keepdims=True))
        ep = jnp.where(emask, ep / jnp.sum(ep, axis=-1, keepdims=True), -1.0)
        v1 = jnp.max(ep, axis=-1, keepdims=True)
        i1 = jnp.min(jnp.where(ep == v1, lane, big), axis=-1, keepdims=True)
        ep2 = jnp.where(lane == i1, -1.0, ep)
        v2 = jnp.max(ep2, axis=-1, keepdims=True)
        i2 = jnp.min(jnp.where(ep2 == v2, lane, big), axis=-1, keepdims=True)
        vs = v1 + v2
        w1 = g_val * (v1 / vs)
        w2 = g_val * (v2 / vs)
        hit = jnp.where((lane == i1) | (lane == i2), 1.0, 0.0)
        before = jnp.dot(tri_ref[:hm, :hm], hit.astype(BF16), preferred_element_type=F32) + cnt
        r1 = jnp.sum(jnp.where(lane == i1, before, 0.0), axis=-1, keepdims=True)
        r2 = jnp.sum(jnp.where(lane == i2, before, 0.0), axis=-1, keepdims=True)
        cnt = cnt + jnp.sum(hit, axis=0, keepdims=True)
        e1 = i1 - ROUTE_LANE0
        e2 = i2 - ROUTE_LANE0
        route = jnp.zeros((hm, LANES), F32)
        for n, val in enumerate((e1, e2, w1, w2, r1, r2)):
            route = jnp.where(lane == n, val, route)
        route_ref[rs, :] = route
    cnt_ref[...] = jnp.broadcast_to(cnt, cnt_ref.shape)


def _post(attn, yssm, x, mod, per_row, w, cnt0, tail=None):
    nb, l, _ = x.shape
    tm = min(TM, l)
    nt = l // tm
    n_body = nb * nt
    has_tail = tail is not None
    if has_tail:
        assert tail.shape == (tm, D_MODEL)

    def bi(s):
        s = jnp.minimum(s, n_body - 1)
        return s // nt, s % nt

    row = lambda wd: pl.BlockSpec((None, tm, wd), lambda s: (*bi(s), 0))
    if per_row:
        mspec = lambda chunk: pl.BlockSpec((None, tm, D_MODEL), lambda s: (*bi(s), chunk))
    else:
        mspec = lambda chunk: pl.BlockSpec((None, 1, D_MODEL), lambda s: (bi(s)[0], 0, chunk))
    full = lambda a: pl.BlockSpec(a.shape, lambda s: (0,) * a.ndim)
    tri = jnp.tril(jnp.ones((tm, tm), F32), -1).astype(BF16)
    consts = (w["glu_w"], w["glu_b"], w["attn_g"], w["ssm_g"], w["w_out"], w["ln1_g"], w["ln1_b"], w["wr"], w["br"],
              tri, cnt0) + ((tail,) if has_tail else ())
    assert 6 + len(consts) == N_POST_INPUTS + int(has_tail)
    n_steps = n_body + int(has_tail)
    return pl.pallas_call(
        functools.partial(_post_kernel, has_tail=has_tail),
        out_shape=(jax.ShapeDtypeStruct((nb, l, D_MODEL), F32),
                   jax.ShapeDtypeStruct((n_steps * tm, D_MODEL), F32),
                   jax.ShapeDtypeStruct((nb, l, LANES), F32), jax.ShapeDtypeStruct((SUBLANES, LANES), F32)),
        grid=(n_steps,),
        in_specs=[row(D_ATTN), row(D_SSM), row(D_MODEL), mspec(2), mspec(3), mspec(4)] + [full(a) for a in consts],
        out_specs=(row(D_MODEL), pl.BlockSpec((tm, D_MODEL), lambda s: (s, 0)),
                   row(LANES), pl.BlockSpec((SUBLANES, LANES), lambda s: (0, 0))),
        compiler_params=_cparams(("arbitrary",)),
        name="post",
    )(attn, yssm, x, mod, mod, mod, *consts)


MOE_LOOKAHEAD = 2
MOE_SLOTS = MOE_LOOKAHEAD + 1


def _moe_kernel(te_ref, nv_ref, first_ref, nxt_ref, par_ref, rt0_ref, rt1_ref, rtn_ref, h2_ref, wg_hbm, wu_hbm, wd_hbm,
                y_ref, buf_ref, wgf_ref, wuf_ref, wdf_ref, wgb_ref, wub_ref, wdb_ref, sem, wsem):
    i = pl.program_id(0)
    nv = nv_ref[0]
    slot = lax.rem(i, MOE_SLOTS)
    w_pairs = ((wg_hbm, wgf_ref), (wu_hbm, wuf_ref), (wd_hbm, wdf_ref))

    def gather(ids_ref, dst_slot):
        _gather_rows(h2_ref, ids_ref, 1, 0, buf_ref.at[dst_slot], sem.at[dst_slot], (0,))

    def fetch_weights(e, ws):
        for src, dst in w_pairs:
            pltpu.make_async_copy(src.at[e], dst.at[ws], wsem.at[ws]).start(priority=1)

    @pl.when(i == 0)
    def _():
        gather(rt0_ref, 0)
        gather(rt1_ref, 1)
        fetch_weights(te_ref[0], par_ref[0])

    @pl.when(i < nv + MOE_LOOKAHEAD)
    def _():
        pltpu.make_async_copy(h2_ref.at[pl.ds(0, TM_MOE // SUBLANES)], buf_ref.at[slot], sem.at[slot]).wait()

    @pl.when((i < nv) & (first_ref[i] == 1))
    def _():
        ws = par_ref[i]
        for src, dst in w_pairs:
            pltpu.make_async_copy(src.at[0], dst.at[ws], wsem.at[ws]).wait()
        wgb_ref[...] = wgf_ref[ws].astype(BF16)
        wub_ref[...] = wuf_ref[ws].astype(BF16)
        wdb_ref[...] = wdf_ref[ws].astype(BF16)

        @pl.when(nxt_ref[i] >= 0)
        def _():
            fetch_weights(nxt_ref[i], 1 - ws)

    @pl.when(i < nv)
    def _():
        nslot = lax.rem(i + MOE_LOOKAHEAD, MOE_SLOTS)
        n_kc = D_MODEL // MXU_DIM
        groups_per_kc = TM_MOE // SUBLANES // n_kc
        hg = hu = None
        for kc in range(n_kc):
            for gg in range(groups_per_kc):
                _gather_group(h2_ref, rtn_ref, 1, 0, buf_ref.at[nslot], sem.at[nslot], (0,), kc * groups_per_kc + gg)
            ks = slice(kc * MXU_DIM, (kc + 1) * MXU_DIM)
            xk = buf_ref[slot, :, :, ks].reshape(TM_MOE, MXU_DIM).astype(BF16)
            pg = jnp.dot(xk, wgb_ref[ks, :], preferred_element_type=F32)
            pu = jnp.dot(xk, wub_ref[ks, :], preferred_element_type=F32)
            hg = pg if hg is None else hg + pg
            hu = pu if hu is None else hu + pu
        act = (hg * jax.nn.sigmoid(hg)) * hu
        y_ref[...] = jnp.dot(act.astype(BF16), wdb_ref[...], preferred_element_type=F32)

    @pl.when(i >= nv)
    def _():
        y_ref[...] = jnp.zeros(y_ref.shape, y_ref.dtype)


def _moe(tile_meta, n_valid, row_token, h2, w_gate, w_up, w_down):
    n_steps = row_token.shape[0] // TM_MOE
    rt = row_token.reshape(n_steps, 1, TM_MOE)
    te, first, nxt, par = tile_meta
    any_spec = pl.BlockSpec(memory_space=pl.ANY)
    return pl.pallas_call(
        _moe_kernel,
        out_shape=jax.ShapeDtypeStruct((n_steps * TM_MOE, D_MODEL), F32),
        grid_spec=pltpu.PrefetchScalarGridSpec(
            num_scalar_prefetch=5, grid=(n_steps,),
            in_specs=[*_index_specs(TM_MOE, 0, n_steps, MOE_LOOKAHEAD), any_spec, any_spec, any_spec, any_spec],
            out_specs=pl.BlockSpec((TM_MOE, D_MODEL), lambda i, *_: (i, 0)),
            scratch_shapes=[pltpu.VMEM((MOE_SLOTS, TM_MOE // SUBLANES, SUBLANES, D_MODEL), F32),
                            pltpu.VMEM((2, D_MODEL, D_EXPERT), F32), pltpu.VMEM((2, D_MODEL, D_EXPERT), F32),
                            pltpu.VMEM((2, D_EXPERT, D_MODEL), F32),
                            pltpu.VMEM((D_MODEL, D_EXPERT), BF16), pltpu.VMEM((D_MODEL, D_EXPERT), BF16),
                            pltpu.VMEM((D_EXPERT, D_MODEL), BF16),
                            pltpu.SemaphoreType.DMA((MOE_SLOTS,)), pltpu.SemaphoreType.DMA((2,))]),
        compiler_params=_cparams(("arbitrary",)),
        name="moe",
    )(te, n_valid, first, nxt, par, rt, rt, rt, h2.reshape(-1, SUBLANES, D_MODEL), w_gate, w_up, w_down)


def _final_kernel(pos0_ref, posn_ref, x1_ref, g2_ref, route_ref, l2g_ref, l2b_ref, ys_ref, o_ref, buf_ref, sem, *, tm):
    step = pl.program_id(0)
    slot = step & 1

    def gather(ids_ref, dst_slot):
        for k in range(2):
            _gather_rows(ys_ref, ids_ref, 1, k * tm, buf_ref.at[dst_slot, k], sem.at[dst_slot], (0, 1))

    @pl.when(step == 0)
    def _():
        gather(pos0_ref, 0)

    for k in range(2):
        pltpu.make_async_copy(ys_ref.at[pl.ds(0, tm // SUBLANES)], buf_ref.at[slot, k], sem.at[slot]).wait()

    route = route_ref[...]
    w1, w2 = route[:, 2:3], route[:, 3:4]
    n_groups = tm // SUBLANES

    def combine(issue_next):
        n_chunks = D_MODEL // MXU_DIM
        per_chunk = 2 * n_groups // n_chunks
        for c in range(n_chunks):
            if issue_next:
                for q in range(c * per_chunk, (c + 1) * per_chunk):
                    k, g = q // n_groups, q % n_groups
                    _gather_group(ys_ref, posn_ref, 1, k * tm, buf_ref.at[1 - slot, k], sem.at[1 - slot], (0, 1), g)
            cs = slice(c * MXU_DIM, (c + 1) * MXU_DIM)
            f = (w1 * buf_ref[slot, 0, :, :, cs].reshape(tm, MXU_DIM)
                 + w2 * buf_ref[slot, 1, :, :, cs].reshape(tm, MXU_DIM))
            o_ref[:, cs] = DEEPNORM_ALPHA * x1_ref[:, cs] + g2_ref[:, cs] * f

    @pl.when(step + 1 < pl.num_programs(0))
    def _():
        combine(True)

    @pl.when(step + 1 == pl.num_programs(0))
    def _():
        combine(False)

    o_ref[...] = _ln(o_ref[...]) * l2g_ref[...] + l2b_ref[...]


def _final(pos_tiles, first_tile, x1, mod, per_row, route, ln2_g, ln2_b, ys):
    nb, l, _ = x1.shape
    tm = min(TM, l)
    nt = l // tm
    n_steps = nb * nt
    row = lambda wd: pl.BlockSpec((None, tm, wd), lambda s: (s // nt, s % nt, 0))
    if per_row:
        g2 = pl.BlockSpec((None, tm, D_MODEL), lambda s: (s // nt, s % nt, 5))
    else:
        g2 = pl.BlockSpec((None, 1, D_MODEL), lambda s: (s // nt, 0, 5))
    vec = pl.BlockSpec((1, D_MODEL), lambda s: (0, 0))
    return pl.pallas_call(
        functools.partial(_final_kernel, tm=tm),
        out_shape=jax.ShapeDtypeStruct((nb, l, D_MODEL), F32),
        grid=(n_steps,),
        in_specs=[*_index_specs(2 * tm, first_tile, n_steps), row(D_MODEL), g2, row(LANES), vec, vec,
                  pl.BlockSpec(memory_space=pl.ANY)],
        out_specs=row(D_MODEL),
        scratch_shapes=[pltpu.VMEM((2, 2, tm // SUBLANES, SUBLANES, D_MODEL), F32), pltpu.SemaphoreType.DMA((2,))],
        compiler_params=_cparams(("arbitrary",)),
        name="final",
    )(pos_tiles, pos_tiles, x1, mod, route, ln2_g, ln2_b, ys.reshape(-1, SUBLANES, D_MODEL))


def _rope_tables(pos):
    half = ROT_DIM // 2
    inv_freq = ROPE_THETA ** (-jnp.arange(half, dtype=jnp.float32) * 2.0 / ROT_DIM)
    ang = pos.astype(jnp.float32)[:, None] * inv_freq[None, :]
    cos, sin = jnp.cos(ang), jnp.sin(ang)
    n = pos.shape[0]
    one = jnp.ones((n, HEAD_DIM - ROT_DIM), F32)
    zero = jnp.zeros((n, HEAD_DIM - half), F32)
    c = jnp.concatenate([cos, cos, one], -1)
    a = jnp.concatenate([-sin, zero], -1)
    b = jnp.concatenate([jnp.zeros((n, half), F32), sin, jnp.zeros((n, HEAD_DIM - ROT_DIM), F32)], -1)
    return tuple(jnp.tile(t, (1, LANES // HEAD_DIM)) for t in (c, a, b))


def _s5_params(lam_re, lam_im, log_step, b_re, b_im, c_re, c_im):
    f32 = jnp.float32
    dt = jnp.exp(log_step.astype(f32))[:, None]
    lr, li = lam_re.astype(f32), lam_im.astype(f32)
    mag = jnp.exp(lr * dt)
    ar, ai = mag * jnp.cos(li * dt), mag * jnp.sin(li * dt)
    den = lr * lr + li * li
    cr = ((ar - 1.0) * lr + ai * li) / den
    ci = (ai * lr - (ar - 1.0) * li) / den
    br, bi = b_re.astype(f32), b_im.astype(f32)
    bbr = cr[..., None] * br - ci[..., None] * bi
    bbi = cr[..., None] * bi + ci[..., None] * br
    gpt = MXU_DIM // SSM_CH
    eye = jnp.eye(gpt, dtype=f32)

    def pack_b(m):
        m = m.reshape(N_KT, gpt, SSM_STATE, SSM_CH)
        return jnp.einsum("kgpc,gh->kgchp", m, eye).reshape(N_KT, MXU_DIM, ST_PER_KT).astype(BF16)

    def pack_c(m):
        m = m.astype(f32).reshape(N_KT, gpt, SSM_CH, SSM_STATE)
        return jnp.einsum("kgcp,gh->khpgc", m, eye).reshape(N_KT, ST_PER_KT, MXU_DIM).astype(BF16)

    wts = (pack_b(bbr), pack_b(bbi), pack_c(c_re), pack_c(-c_im.astype(f32)))
    return wts, ar.reshape(1, N_STATE), ai.reshape(1, N_STATE)


def _scan_tables(ar, ai, nb):
    assert SUBLANES // nb == 2
    a2r, a2i = ar * ar - ai * ai, 2.0 * ar * ai
    z = jnp.zeros_like(ar)
    rep = lambda first, second: jnp.concatenate([jnp.tile(first, (nb, 1)), jnp.tile(second, (nb, 1))], 0)
    return rep(z, ar), rep(z, ai), rep(ar, a2r), rep(ai, a2i)


def kernel(x_prompt, x_sample, cache_k_win, cache_v_win, state_ssm_re, state_ssm_im, c_prompt, c_sample, ada_w, ada_b,
           w_in, attn_sinks, ssm_lambda_re, ssm_lambda_im, ssm_log_step, ssm_b_re, ssm_b_im, ssm_c_re, ssm_c_im, ssm_d,
           ssm_glu_w, ssm_glu_b, attn_norm_g, ssm_norm_g, w_out, ln1_g, ln1_b, router_group_w, router_group_b,
           router_expert_w, router_expert_b, exp_w_gate, exp_w_up, exp_w_down, ln2_g, ln2_b):
    assert ada_w.shape[0] == DEPTH
    bp, lp, _ = x_prompt.shape
    bs, ls, _ = x_sample.shape
    lsp = SAMPLE_PAD

    w_in_bf = w_in[0].astype(BF16)
    wr = jnp.concatenate([router_group_w[0], router_expert_w[0]], -1)
    wr = jnp.pad(wr, ((0, 0), (0, LANES - wr.shape[1]))).astype(BF16)
    br = jnp.pad(jnp.concatenate([router_group_b[0], router_expert_b[0]], -1), (0, LANES - N_EXPERT_GROUPS - N_EXPERTS))
    wpost = dict(glu_w=ssm_glu_w[0].astype(BF16), glu_b=ssm_glu_b[0][None], attn_g=attn_norm_g[0][None],
                 ssm_g=ssm_norm_g[0][None], w_out=w_out[0].astype(BF16), ln1_g=ln1_g[0][None], ln1_b=ln1_b[0][None],
                 wr=wr, br=br[None])
    s5w, ar, ai = _s5_params(ssm_lambda_re[0], ssm_lambda_im[0], ssm_log_step[0], ssm_b_re[0], ssm_b_im[0],
                             ssm_c_re[0], ssm_c_im[0])
    d_skip = ssm_d[0][None]
    sinks = attn_sinks[0]

    n_c = bp + bs
    n_cp = -(-n_c // SUBLANES) * SUBLANES
    c_all = jnp.pad(jnp.concatenate([c_prompt, c_sample], 0), ((0, n_cp - n_c), (0, 0)))
    mod = _ada(c_all, ada_w[0], ada_b[0][None])
    mod_p = mod[:bp][:, None, :]
    mod_s = jnp.repeat(mod[bp:n_c], lsp, axis=0)[None]

    xs_pad = jnp.pad(x_sample, ((0, 0), (0, lsp - ls), (0, 0))).reshape(1, bs * lsp, D_MODEL)

    rope_p = _rope_tables(jnp.arange(lp))
    pos_s = PAST_LEN + jnp.minimum(jnp.arange(lsp), ls - 1)
    rope_s = tuple(jnp.tile(t, (bs, 1)) for t in _rope_tables(pos_s))
    q_p, k_p, v_p, u_p = _inproj(x_prompt, mod_p, w_in_bf, rope_p, False)
    q_s, k_s, v_s, u_s = _inproj(xs_pad, mod_s, w_in_bf, rope_s, True)

    attn_p = _attn_prompt(sinks, q_p, k_p, v_p)
    k_s3 = k_s.reshape(bs, lsp, D_KV)
    v_s3 = v_s.reshape(bs, lsp, D_KV)
    ck = cache_k_win[0].reshape(bs, WINDOW, D_KV)
    cv = cache_v_win[0].reshape(bs, WINDOW, D_KV)
    padk = lambda new: jnp.pad(new, ((0, 0), (0, WINDOW - lsp), (0, 0)))
    attn_s = _attn_sample(sinks, q_s.reshape(bs, lsp, D_ATTN), jnp.concatenate([ck, padk(k_s3)], 1),
                          jnp.concatenate([cv, padk(v_s3)], 1))

    tabs = _scan_tables(ar, ai, bp)
    zero_carry = jnp.zeros((SUBLANES, N_STATE), F32)
    y_p, hr_p, hi_p = _s5_prompt(u_p, s5w, tabs, d_skip, zero_carry, zero_carry)
    u_tb = jnp.transpose(u_s.reshape(bs, lsp, D_SSM)[:, :ls], (1, 0, 2))
    y_tb, hr_s, hi_s = _s5_sample(u_tb, s5w, ar, ai, d_skip, state_ssm_re[0].reshape(bs, N_STATE),
                                  state_ssm_im[0].reshape(bs, N_STATE))
    y_s = jnp.pad(jnp.transpose(y_tb, (1, 0, 2)), ((0, 0), (0, lsp - ls), (0, 0))).reshape(1, bs * lsp, D_SSM)

    cnt0 = jnp.zeros((SUBLANES, LANES), F32)
    n_p, n_s = bp * lp, bs * lsp
    n_tok = n_p + n_s
    x1_s, h2_s, route_s, cnt_s = _post(attn_s.reshape(1, bs * lsp, D_ATTN), y_s, xs_pad, mod_s, True, wpost, cnt0)
    x1_p, h2, route_p, cnt = _post(attn_p, y_p, x_prompt, mod_p, False, wpost, cnt_s, tail=h2_s)

    n_tiles = -(-(2 * n_tok + N_EXPERTS * (TM_MOE - 1)) // TM_MOE) + MOE_LOOKAHEAD
    route = jnp.concatenate([route_p.reshape(n_p, LANES), route_s.reshape(n_s, LANES)], 0)
    counts = cnt[0, ROUTE_LANE0:ROUTE_LANE0 + N_EXPERTS].astype(jnp.int32)
    padded = ((counts + TM_MOE - 1) // TM_MOE) * TM_MOE
    ends = jnp.cumsum(padded)
    offs = ends - padded
    col = lambda c: route[:, c].astype(jnp.int32)
    expert_ids = jnp.arange(N_EXPERTS, dtype=jnp.int32)
    seg_start = lambda e: jnp.sum(jnp.where(e[:, None] == expert_ids[None, :], offs[None, :], 0), axis=1)
    pos_k = [seg_start(col(k)) + col(4 + k) for k in range(2)]
    n_valid = ends[-1] // TM_MOE
    tile_row = jnp.minimum(jnp.arange(n_tiles), n_valid - 1) * TM_MOE
    tile_expert = jnp.sum((ends[None, :] <= tile_row[:, None]).astype(jnp.int32), axis=1)
    first_tile = jnp.concatenate([jnp.ones((1,), jnp.int32), (tile_expert[1:] != tile_expert[:-1]).astype(jnp.int32)])
    first_tile = first_tile * (jnp.arange(n_tiles) < n_valid)
    nonempty = padded > 0
    later =jnp.where(nonempty[None, :] & (expert_ids[None, :] > expert_ids[:, None]), expert_ids[None, :], N_EXPERTS)
    next_expert = jnp.min(later, axis=1)
    next_expert = jnp.where(next_expert < N_EXPERTS, next_expert, -1).astype(jnp.int32)
    weight_slot = ((jnp.cumsum(nonempty.astype(jnp.int32)) - 1) & 1).astype(jnp.int32)
    tile_meta = (tile_expert, first_tile.astype(jnp.int32), next_expert[tile_expert], weight_slot[tile_expert])
    token_ids = jnp.arange(n_tok, dtype=jnp.int32)
    row_token = (jnp.arange(n_tiles * TM_MOE, dtype=jnp.int32) % n_tok).at[jnp.concatenate(pos_k)].set(
        jnp.concatenate([token_ids, token_ids]))

    ys = _moe(tile_meta, n_valid.reshape(1).astype(jnp.int32), row_token, h2, exp_w_gate[0], exp_w_up[0],
              exp_w_down[0])
    pos_tiles = jnp.concatenate([p.reshape(n_tok // TM, 1, TM) for p in pos_k], axis=2)
    y_prompt = _final(pos_tiles, 0, x1_p, mod_p, False, route_p, ln2_g[0][None], ln2_b[0][None], ys)
    y_samp = _final(pos_tiles, n_p // TM, x1_s, mod_s, True, route_s, ln2_g[0][None], ln2_b[0][None], ys)
    y_sample = y_samp.reshape(bs, lsp, D_MODEL)[:, :ls]

    kv5 = lambda a, n: a.reshape(1, a.shape[0], n, N_KV_HEADS, HEAD_DIM)
    k_win_p = kv5(k_p[:, lp - WINDOW:], WINDOW)
    v_win_p = kv5(v_p[:, lp - WINDOW:], WINDOW)
    k_win_s = kv5(jnp.concatenate([ck[:, ls:], k_s3[:, :ls]], 1), WINDOW)
    v_win_s = kv5(jnp.concatenate([cv[:, ls:], v_s3[:, :ls]], 1), WINDOW)
    st = lambda a: a.reshape(1, a.shape[0], N_SSM_GROUPS, SSM_STATE)
    return (y_prompt, y_sample, k_win_p, v_win_p, st(hr_p[bp:2 * bp]), st(hi_p[bp:2 * bp]),
            k_win_s, v_win_s, st(hr_s), st(hi_s))
```

```python
import functools
import math

import jax
import jax.numpy as jnp
from jax import lax
from jax.experimental import pallas as pl
from jax.experimental.pallas import tpu as pltpu

F32 = jnp.float32
BF16 = jnp.bfloat16

D_MODEL = 2048
D_ATTN = 1024
D_SSM = 1024
HEAD_DIM = 64
N_HEADS = 16
N_KV_HEADS = 4
REP = 4
D_KV = 256
ROT_DIM = 16
ROPE_THETA = 500000.0
WINDOW = 128
SSM_CH = 16
N_SSM_GROUPS = 64
SSM_STATE = 64
N_STATE = N_SSM_GROUPS * SSM_STATE
PROJ_COLS = D_ATTN + 2 * D_KV + D_SSM
N_EXPERT_GROUPS = 4
EXPERTS_PER_GROUP = 8
N_EXPERTS = 32
D_EXPERT = 512
DEPTH = 1
DEEPNORM_ALPHA = (2.0 * DEPTH) ** 0.25
LN_EPS = 1e-5
PAST_LEN = 16384

LANES = 128
SUBLANES = 8
MXU_DIM = 256
TM = 256
TM_MOE = 256
TM_INPROJ = 512
S5_TT = 128
SAMPLE_PAD = 8
ROUTE_LANE0 = N_EXPERT_GROUPS
VMEM_LIMIT = 56 * 1024 * 1024


def _cparams(sem):
    return pltpu.CompilerParams(dimension_semantics=sem, vmem_limit_bytes=VMEM_LIMIT)


def _ln(x):
    mu = jnp.mean(x, axis=-1, keepdims=True)
    xc = x - mu
    var = jnp.mean(xc * xc, axis=-1, keepdims=True)
    return xc * lax.rsqrt(var + LN_EPS)


def _rms(x):
    return x * lax.rsqrt(jnp.mean(x * x, axis=-1, keepdims=True) + LN_EPS)


def _ada_kernel(c_ref, w_ref, b_ref, o_ref):
    c = c_ref[...]
    s = c * jax.nn.sigmoid(c)
    o_ref[...] = jnp.dot(s.astype(BF16), w_ref[...].astype(BF16), preferred_element_type=F32) + b_ref[...]


def _ada(c_all, ada_w, ada_b):
    n, tn = c_all.shape[0], 1024
    return pl.pallas_call(
        _ada_kernel,
        out_shape=jax.ShapeDtypeStruct((n, 6 * D_MODEL), F32),
        grid=(6 * D_MODEL // tn,),
        in_specs=[pl.BlockSpec((n, D_MODEL), lambda j: (0, 0)),
                  pl.BlockSpec((D_MODEL, tn), lambda j: (0, j)),
                  pl.BlockSpec((1, tn), lambda j: (0, j))],
        out_specs=pl.BlockSpec((n, tn), lambda j: (0, j)),
        compiler_params=_cparams(("arbitrary",)),
        name="ada",
    )(c_all, ada_w, ada_b)


def _mod_spec(per_row, tm, chunk):
    if per_row:
        return pl.BlockSpec((None, tm, D_MODEL), lambda b, i: (b, i, chunk))
    return pl.BlockSpec((None, 1, D_MODEL), lambda b, i: (b, 0, chunk))


def _inproj_kernel(x_ref, sh_ref, sc_ref, w_ref, rc_ref, ra_ref, rb_ref, q_ref, k_ref, v_ref, u_ref):
    h = _ln(x_ref[...]) * (1.0 + sc_ref[...]) + sh_ref[...]
    proj = jnp.dot(h.astype(BF16), w_ref[...], preferred_element_type=F32)
    rc, ra, rb = rc_ref[...], ra_ref[...], rb_ref[...]

    def rope(t):
        return t * rc + pltpu.roll(t, LANES - ROT_DIM // 2, 1) * ra + pltpu.roll(t, ROT_DIM // 2, 1) * rb

    for j in range(D_ATTN // LANES):
        q_ref[:, j * LANES:(j + 1) * LANES] = (rope(proj[:, j * LANES:(j + 1) * LANES]) * HEAD_DIM ** -0.5).astype(BF16)
    for j in range(D_KV // LANES):
        c0 = D_ATTN + j * LANES
        k_ref[:, j * LANES:(j + 1) * LANES] = rope(proj[:, c0:c0 + LANES])
    v_ref[...] = proj[:, D_ATTN + D_KV:D_ATTN + 2 * D_KV]
    u_ref[...] = proj[:, D_ATTN + 2 * D_KV:]


def _inproj(x, mod, w_in_bf, rope_tabs, per_row):
    nb, l, _ = x.shape
    tm = min(TM_INPROJ, l)
    row = lambda w: pl.BlockSpec((None, tm, w), lambda b, i: (b, i, 0))
    tab = pl.BlockSpec((tm, LANES), lambda b, i: (i, 0))
    return pl.pallas_call(
        _inproj_kernel,
        out_shape=(jax.ShapeDtypeStruct((nb, l, D_ATTN), BF16), jax.ShapeDtypeStruct((nb, l, D_KV), F32),
                   jax.ShapeDtypeStruct((nb, l, D_KV), F32), jax.ShapeDtypeStruct((nb, l, D_SSM), F32)),
        grid=(nb, l // tm),
        in_specs=[row(D_MODEL), _mod_spec(per_row, tm, 0), _mod_spec(per_row, tm, 1),
                  pl.BlockSpec((D_MODEL, PROJ_COLS), lambda b, i: (0, 0)), tab, tab, tab],
        out_specs=(row(D_ATTN), row(D_KV), row(D_KV), row(D_SSM)),
        compiler_params=_cparams(("arbitrary", "arbitrary")),
        name="inproj",
    )(x, mod, mod, w_in_bf, *rope_tabs)


ATTN_SAMPLE_BATCH = 8


def _attn_kernel(sink_ref, q_ref, kp_ref, kc_ref, vp_ref, vc_ref, o_ref, *, lq, prev_from_block):
    m_rows = REP * lq
    ii = lax.broadcasted_iota(jnp.int32, (m_rows, WINDOW), 0) & (lq - 1)
    jj = lax.broadcasted_iota(jnp.int32, (m_rows, WINDOW), 1)
    from_prev = jj > ii
    if prev_from_block:
        dead = jj > ii + jnp.where(pl.program_id(1) > 0, WINDOW, 0)
    rr = lax.broadcasted_iota(jnp.int32, (m_rows, 1), 0)
    dn = (((1,), (1,)), ((), ()))
    for bi in range(q_ref.shape[0]):
        for g in range(N_KV_HEADS):
            qg = q_ref[bi, :, g * REP * HEAD_DIM:(g + 1) * REP * HEAD_DIM].astype(F32)
            qs = jnp.concatenate([qg[:, r * HEAD_DIM:(r + 1) * HEAD_DIM] for r in range(REP)], axis=0).astype(BF16)
            ks = slice(g * HEAD_DIM, (g + 1) * HEAD_DIM)
            s_p = lax.dot_general(qs, kp_ref[bi, :, ks].astype(BF16), dn, preferred_element_type=F32)
            s_c = lax.dot_general(qs, kc_ref[bi, :, ks].astype(BF16), dn, preferred_element_type=F32)
            s = jnp.where(from_prev, s_p, s_c)
            if prev_from_block:
                s = jnp.where(dead, -jnp.inf, s)
            sink = jnp.zeros((m_rows, 1), F32)
            for r in range(REP):
                sink = jnp.where((rr >= r * lq) & (rr < (r + 1) * lq), sink_ref[g * REP + r], sink)
            m = jnp.maximum(jnp.max(s, axis=-1, keepdims=True), sink)
            p = jnp.exp(s - m)
            p = p / (jnp.sum(p, axis=-1, keepdims=True) + jnp.exp(sink - m))
            o = (jnp.dot(jnp.where(from_prev, p, 0.0).astype(BF16), vp_ref[bi, :, ks].astype(BF16),
                         preferred_element_type=F32)
                 + jnp.dot(jnp.where(from_prev, 0.0, p).astype(BF16), vc_ref[bi, :, ks].astype(BF16),
                           preferred_element_type=F32))
            for r in range(REP):
                h = g * REP + r
                o_ref[bi, :, h * HEAD_DIM:(h + 1) * HEAD_DIM] = o[r * lq:(r + 1) * lq]


def _attn_prompt(sinks, q, k, v):
    nb, l, _ = q.shape
    nblk = l // WINDOW
    cur = lambda w: pl.BlockSpec((1, WINDOW, w), lambda b, n: (b, n, 0))
    prev = lambda w: pl.BlockSpec((1, WINDOW, w), lambda b, n: (b, jnp.maximum(n - 1, 0), 0))
    return pl.pallas_call(
        functools.partial(_attn_kernel, lq=WINDOW, prev_from_block=True),
        out_shape=jax.ShapeDtypeStruct((nb, l, D_ATTN), F32),
        grid=(nb, nblk),
        in_specs=[pl.BlockSpec(memory_space=pltpu.SMEM), cur(D_ATTN), prev(D_KV), cur(D_KV), prev(D_KV), cur(D_KV)],
        out_specs=cur(D_ATTN),
        compiler_params=_cparams(("arbitrary", "arbitrary")),
        name="attn_prompt",
    )(sinks, q, k, k, v, v)


ATTN_SAMPLE_JOINT = 4


def _attn_sample_kernel(sink_ref, q_ref, kp_ref, kc_ref, vp_ref, vc_ref, o_ref, *, lq):
    gb = ATTN_SAMPLE_JOINT
    m_b = REP * lq
    m, n = gb * m_b, gb * WINDOW
    row = lax.broadcasted_iota(jnp.int32, (m, n), 0)
    col = lax.broadcasted_iota(jnp.int32, (m, n), 1)
    same = (row // m_b) == (col // WINDOW)
    from_prev = (col & (WINDOW - 1)) > (row & (lq - 1))
    rep_of_row = (lax.broadcasted_iota(jnp.int32, (m, 1), 0) // lq) % REP
    dn = (((1,), (1,)), ((), ()))
    for blk in range(q_ref.shape[0] // gb):
        bs = range(blk * gb, (blk + 1) * gb)
        for g in range(N_KV_HEADS):
            qg = [q_ref[b, :, g * REP * HEAD_DIM:(g + 1) * REP * HEAD_DIM].astype(F32) for b in bs]
            qs = jnp.concatenate([qb[:, r * HEAD_DIM:(r + 1) * HEAD_DIM] for qb in qg for r in range(REP)],
                                 axis=0).astype(BF16)
            ks = slice(g * HEAD_DIM, (g + 1) * HEAD_DIM)
            stack = lambda ref: jnp.concatenate([ref[b, :, ks] for b in bs], axis=0).astype(BF16)
            s_p = lax.dot_general(qs, stack(kp_ref), dn, preferred_element_type=F32)
            s_c = lax.dot_general(qs, stack(kc_ref), dn, preferred_element_type=F32)
            s = jnp.where(same, jnp.where(from_prev, s_p, s_c), -jnp.inf)
            sink = jnp.zeros((m, 1), F32)
            for r in range(REP):
                sink = jnp.where(rep_of_row == r, sink_ref[g * REP + r], sink)
            mx = jnp.maximum(jnp.max(s, axis=-1, keepdims=True), sink)
            p = jnp.exp(s - mx)
            p = p / (jnp.sum(p, axis=-1, keepdims=True) + jnp.exp(sink - mx))
            o = (jnp.dot(jnp.where(from_prev, p, 0.0).astype(BF16), stack(vp_ref), preferred_element_type=F32)
                 + jnp.dot(jnp.where(from_prev, 0.0, p).astype(BF16), stack(vc_ref), preferred_element_type=F32))
            for bi, b in enumerate(bs):
                for r in range(REP):
                    h = g * REP + r
                    o_ref[b, :, h * HEAD_DIM:(h + 1) * HEAD_DIM] = o[bi * m_b + r * lq:bi * m_b + (r + 1) * lq]


def _attn_sample(sinks, q, k_cache, k_new, v_cache, v_new):
    nb, lq, _ = q.shape
    nbb = ATTN_SAMPLE_BATCH
    kblk = pl.BlockSpec((nbb, WINDOW, D_KV), lambda b: (b, 0, 0))
    return pl.pallas_call(
        functools.partial(_attn_sample_kernel, lq=lq),
        out_shape=jax.ShapeDtypeStruct((nb, lq, D_ATTN), F32),
        grid=(nb // nbb,),
        in_specs=[pl.BlockSpec(memory_space=pltpu.SMEM), pl.BlockSpec((nbb, lq, D_ATTN), lambda b: (b, 0, 0)),
                  kblk, kblk, kblk, kblk],
        out_specs=pl.BlockSpec((nbb, lq, D_ATTN), lambda b: (b, 0, 0)),
        compiler_params=_cparams(("arbitrary",)),
        name="attn_sample",
    )(sinks, q, k_cache, k_new, v_cache, v_new)


N_KT = D_SSM // MXU_DIM
ST_PER_KT = N_STATE // N_KT
N_SLAB = D_SSM // LANES


def _cproj(hr_ref, hi_ref, cre_ref, cimn_ref, kt):
    cs = slice(kt * ST_PER_KT, (kt + 1) * ST_PER_KT)
    return (jnp.dot(hr_ref[:, cs].astype(BF16), cre_ref[kt], preferred_element_type=F32)
            + jnp.dot(hi_ref[:, cs].astype(BF16), cimn_ref[kt], preferred_element_type=F32))


def _s5_prompt_kernel(u_ref, bre_ref, bim_ref, cre_ref, cimn_ref, a1r_ref, a1i_ref, par_ref, pai_ref, d_ref,
                      h0r_ref, h0i_ref, y_ref, cr_ref, ci_ref, il_ref, xr_ref, xi_ref, *, nb, tt):
    rows = nb * tt

    @pl.when(pl.program_id(0) == 0)
    def _():
        cr_ref[...] = h0r_ref[...]
        ci_ref[...] = h0i_ref[...]

    for b in range(nb):
        for j in range(N_SLAB):
            il_ref[j, pl.ds(b, tt, stride=nb), :] = u_ref[b, :, j * LANES:(j + 1) * LANES]
    for kt in range(N_KT):
        ub = jnp.concatenate([il_ref[2 * kt], il_ref[2 * kt + 1]], axis=1).astype(BF16)
        cs = slice(kt * ST_PER_KT, (kt + 1) * ST_PER_KT)
        xr_ref[:, cs] = jnp.dot(ub, bre_ref[kt], preferred_element_type=F32)
        xi_ref[:, cs] = jnp.dot(ub, bim_ref[kt], preferred_element_type=F32)

    half = lax.broadcasted_iota(jnp.int32, (SUBLANES, LANES), 0) < nb

    def step(i, carry):
        r0 = pl.multiple_of(i * SUBLANES, SUBLANES)
        for j in range(N_STATE // LANES):
            ls = slice(j * LANES, (j + 1) * LANES)
            x_r = xr_ref[pl.ds(r0, SUBLANES), ls]
            x_i = xi_ref[pl.ds(r0, SUBLANES), ls]
            s_r = pltpu.roll(x_r, nb, 0)
            s_i = pltpu.roll(x_i, nb, 0)
            a1r, a1i = a1r_ref[:, ls], a1i_ref[:, ls]
            c_r, c_i = cr_ref[:, ls], ci_ref[:, ls]
            par, pai = par_ref[:, ls], pai_ref[:, ls]
            h_r = x_r + (a1r * s_r - a1i * s_i) + (par * c_r - pai * c_i)
            h_i = x_i + (a1r * s_i + a1i * s_r) + (par * c_i + pai * c_r)
            xr_ref[pl.ds(r0, SUBLANES), ls] = h_r
            xi_ref[pl.ds(r0, SUBLANES), ls] = h_i
            cr_ref[:, ls] = jnp.where(half, pltpu.roll(h_r, nb, 0), h_r)
            ci_ref[:, ls] = jnp.where(half, pltpu.roll(h_i, nb, 0), h_i)
        return carry

    lax.fori_loop(0, rows // SUBLANES, step, 0)

    for kt in range(N_KT):
        y = _cproj(xr_ref, xi_ref, cre_ref, cimn_ref, kt)
        il_ref[2 * kt] = y[:, :LANES]
        il_ref[2 * kt + 1] = y[:, LANES:]
    for b in range(nb):
        for j in range(N_SLAB):
            ls = slice(j * LANES, (j + 1) * LANES)
            y_ref[b, :, ls] = il_ref[j, pl.ds(b, tt, stride=nb), :] + d_ref[:, ls] * u_ref[b, :, ls]


def _s5_prompt(u, wts, tabs, d_skip, h0r, h0i):
    nb, l, _ = u.shape
    tt = S5_TT
    rows = nb * tt
    full = lambda a: pl.BlockSpec(a.shape, lambda i: (0,) * a.ndim)
    blk = pl.BlockSpec((nb, tt, D_SSM), lambda i: (0, i, 0))
    carry = jax.ShapeDtypeStruct((SUBLANES, N_STATE), F32)
    return pl.pallas_call(
        functools.partial(_s5_prompt_kernel, nb=nb, tt=tt),
        out_shape=(jax.ShapeDtypeStruct((nb, l, D_SSM), F32), carry, carry),
        grid=(l // tt,),
        in_specs=[blk] + [full(a) for a in (*wts, *tabs, d_skip, h0r, h0i)],
        out_specs=(blk, pl.BlockSpec((SUBLANES, N_STATE), lambda i: (0, 0)),
                   pl.BlockSpec((SUBLANES, N_STATE), lambda i: (0, 0))),
        scratch_shapes=[pltpu.VMEM((N_SLAB, rows, LANES), F32), pltpu.VMEM((rows, N_STATE), F32),
                        pltpu.VMEM((rows, N_STATE), F32)],
        compiler_params=_cparams(("arbitrary",)),
        name="s5_prompt",
    )(u, *wts, *tabs, d_skip, h0r, h0i)


def _s5_sample_kernel(u_ref, bre_ref, bim_ref, cre_ref, cimn_ref, ar_ref, ai_ref, d_ref, h0r_ref, h0i_ref,
                      y_ref, sr_ref, si_ref, xr_ref, xi_ref, *, nt):
    sr_ref[...] = h0r_ref[...]
    si_ref[...] = h0i_ref[...]
    for t in range(nt):
        for kt in range(N_KT):
            ub = u_ref[t, :, kt * MXU_DIM:(kt + 1) * MXU_DIM].astype(BF16)
            cs = slice(kt * ST_PER_KT, (kt + 1) * ST_PER_KT)
            xr_ref[:, cs] = jnp.dot(ub, bre_ref[kt], preferred_element_type=F32)
            xi_ref[:, cs] = jnp.dot(ub, bim_ref[kt], preferred_element_type=F32)
        ar, ai = ar_ref[...], ai_ref[...]
        s_r, s_i = sr_ref[...], si_ref[...]
        sr_ref[...] = xr_ref[...] + (ar * s_r - ai * s_i)
        si_ref[...] = xi_ref[...] + (ar * s_i + ai * s_r)
        for kt in range(N_KT):
            ys = slice(kt * MXU_DIM, (kt + 1) * MXU_DIM)
            y_ref[t, :, ys] = _cproj(sr_ref, si_ref, cre_ref, cimn_ref, kt) + d_ref[:, ys] * u_ref[t, :, ys]


def _s5_sample(u_tb, wts, ar, ai, d_skip, h0r, h0i):
    nt, nb, _ = u_tb.shape
    st = jax.ShapeDtypeStruct((nb, N_STATE), F32)
    args = (u_tb, *wts, ar, ai, d_skip, h0r, h0i)
    full = lambda a: pl.BlockSpec(a.shape, lambda i: (0,) * a.ndim)
    return pl.pallas_call(
        functools.partial(_s5_sample_kernel, nt=nt),
        out_shape=(jax.ShapeDtypeStruct((nt, nb, D_SSM), F32), st, st),
        grid=(1,),
        in_specs=[full(a) for a in args],
        out_specs=(pl.BlockSpec((nt, nb, D_SSM), lambda i: (0, 0, 0)), pl.BlockSpec((nb, N_STATE), lambda i: (0, 0)),
                   pl.BlockSpec((nb, N_STATE), lambda i: (0, 0))),
        scratch_shapes=[pltpu.VMEM((nb, N_STATE), F32), pltpu.VMEM((nb, N_STATE), F32)],
        compiler_params=_cparams(("arbitrary",)),
        name="s5_sample",
    )(*args)


def _gather_group(src_ref, ids_ref, id_stride, id_offset, dst_ref, sem, priorities, g):
    for j in range(SUBLANES):
        t = ids_ref[0, id_stride * (g * SUBLANES + j) + id_offset]
        pltpu.make_async_copy(src_ref.at[t >> 3, pl.ds(t & (SUBLANES - 1), 1)], dst_ref.at[g, pl.ds(j, 1)],
                              sem).start(priority=priorities[j % len(priorities)])


def _gather_rows(src_ref, ids_ref, id_stride, id_offset, dst_ref, sem, priorities):
    def body(g, c):
        _gather_group(src_ref, ids_ref, id_stride, id_offset, dst_ref, sem, priorities, g)
        return c
    lax.fori_loop(0, dst_ref.shape[0], body, 0)


def _index_specs(n_cols, first_tile, n_steps, depth=1):
    def spec(fn):
        return pl.BlockSpec((None, 1, n_cols), fn, memory_space=pltpu.SMEM)

    head = [spec(lambda s, *_, k=k: (first_tile + min(k, n_steps - 1), 0, 0)) for k in range(depth)]
    return (*head, spec(lambda s, *_: (first_tile + jnp.minimum(s + depth, n_steps - 1), 0, 0)))


N_POST_INPUTS = 17
POST_ROW_PARTS = 1


def _post_kernel(*refs, has_tail):
    ins, outs = refs[:N_POST_INPUTS], refs[N_POST_INPUTS + int(has_tail):]
    if not has_tail:
        _post_body(*ins, *outs)
        return
    tail_ref, h2_ref = refs[N_POST_INPUTS], outs[1]
    last = pl.num_programs(0) - 1

    @pl.when(pl.program_id(0) == last)
    def _():
        h2_ref[...] = tail_ref[...]

    @pl.when(pl.program_id(0) < last)
    def _():
        _post_body(*ins, *outs)


def _post_body(attn_ref, yssm_ref, x_ref, g1_ref, sh2_ref, sc2_ref, gluw_ref, glub_ref, ga_ref, gs_ref, wout_ref,
               l1g_ref, l1b_ref, wr_ref, br_ref, tri_ref, cnt0_ref, x1_ref, h2_ref, route_ref, cnt_ref):
    tm = x_ref.shape[0]

    @pl.when(pl.program_id(0) == 0)
    def _():
        cnt_ref[...] = cnt0_ref[...]

    hm = tm // POST_ROW_PARTS
    lane = lax.broadcasted_iota(jnp.int32, (hm, LANES), 1).astype(F32)
    big = float(4 * LANES)
    neg = -jnp.inf
    cnt = cnt_ref[0:1, :]
    for part in range(POST_ROW_PARTS):
        rs = slice(part * hm, (part + 1) * hm)
        rows = lambda ref: ref[rs, :] if ref.shape[0] == tm else ref[...]
        z = jax.nn.gelu(yssm_ref[rs, :])
        ssm = z * jax.nn.sigmoid(jnp.dot(z.astype(BF16), gluw_ref[...], preferred_element_type=F32) + glub_ref[...])
        mixed_a = (_rms(attn_ref[rs, :]) * ga_ref[...]).astype(BF16)
        mixed_s = (_rms(ssm) * gs_ref[...]).astype(BF16)
        o = (jnp.dot(mixed_a, wout_ref[:D_ATTN, :], preferred_element_type=F32)
             + jnp.dot(mixed_s, wout_ref[D_ATTN:, :], preferred_element_type=F32))
        x1 = _ln(DEEPNORM_ALPHA * x_ref[rs, :] + rows(g1_ref) * o) * l1g_ref[...] + l1b_ref[...]
        x1_ref[rs, :] = x1
        h2 = _ln(x1) * (1.0 + rows(sc2_ref)) + rows(sh2_ref)
        h2_ref[rs, :] = h2
        logits = jnp.dot(h2.astype(BF16), wr_ref[...], preferred_element_type=F32) + br_ref[...]

        gl = jnp.where(lane < N_EXPERT_GROUPS, logits, neg)
        gp = jnp.exp(gl - jnp.max(gl, axis=-1, keepdims=True))
        gp = gp / jnp.sum(gp, axis=-1, keepdims=True)
        g_val = jnp.max(gp, axis=-1, keepdims=True)
        g_idx = jnp.min(jnp.where(gp == g_val, lane, big), axis=-1, keepdims=True)
        lo = ROUTE_LANE0 + EXPERTS_PER_GROUP * g_idx
        emask = (lane >= lo) & (lane < lo + EXPERTS_PER_GROUP)
        el = jnp.where(emask, logits, neg)
        ep = jnp.exp(el - jnp.max(el, axis=-1, keepdims=True))
        ep = jnp.where(emask, ep / jnp.sum(ep, axis=-1, keepdims=True), -1.0)
        v1 = jnp.max(ep, axis=-1, keepdims=True)
        i1 = jnp.min(jnp.where(ep == v1, lane, big), axis=-1, keepdims=True)
        ep2 = jnp.where(lane == i1, -1.0, ep)
        v2 = jnp.max(ep2, axis=-1, keepdims=True)
        i2 = jnp.min(jnp.where(ep2 == v2, lane, big), axis=-1, keepdims=True)
        vs = v1 + v2
        w1 = g_val * (v1 / vs)
        w2 = g_val * (v2 / vs)
        hit = jnp.where((lane == i1) | (lane == i2), 1.0, 0.0)
        before = jnp.dot(tri_ref[:hm, :hm], hit.astype(BF16), preferred_element_type=F32) + cnt
        r1 = jnp.sum(jnp.where(lane == i1, before, 0.0), axis=-1, keepdims=True)
        r2 = jnp.sum(jnp.where(lane == i2, before, 0.0), axis=-1, keepdims=True)
        cnt = cnt + jnp.sum(hit, axis=0, keepdims=True)
        e1 = i1 - ROUTE_LANE0
        e2 = i2 - ROUTE_LANE0
        route = jnp.zeros((hm, LANES), F32)
        for n, val in enumerate((e1, e2, w1, w2, r1, r2)):
            route = jnp.where(lane == n, val, route)
        route_ref[rs, :] = route
    cnt_ref[...] = jnp.broadcast_to(cnt, cnt_ref.shape)


def _post(attn, yssm, x, mod, per_row, w, cnt0, tail=None):
    nb, l, _ = x.shape
    tm = min(TM, l)
    nt = l // tm
    n_body = nb * nt
    has_tail = tail is not None
    if has_tail:
        assert tail.shape == (tm, D_MODEL)

    def bi(s):
        s = jnp.minimum(s, n_body - 1)
        return s // nt, s % nt

    row = lambda wd: pl.BlockSpec((None, tm, wd), lambda s: (*bi(s), 0))
    if per_row:
        mspec = lambda chunk: pl.BlockSpec((None, tm, D_MODEL), lambda s: (*bi(s), chunk))
    else:
        mspec = lambda chunk: pl.BlockSpec((None, 1, D_MODEL), lambda s: (bi(s)[0], 0, chunk))
    full = lambda a: pl.BlockSpec(a.shape, lambda s: (0,) * a.ndim)
    tri = jnp.tril(jnp.ones((tm, tm), F32), -1).astype(BF16)
    consts = (w["glu_w"], w["glu_b"], w["attn_g"], w["ssm_g"], w["w_out"], w["ln1_g"], w["ln1_b"], w["wr"], w["br"],
              tri, cnt0) + ((tail,) if has_tail else ())
    assert 6 + len(consts) == N_POST_INPUTS + int(has_tail)
    n_steps = n_body + int(has_tail)
    return pl.pallas_call(
        functools.partial(_post_kernel, has_tail=has_tail),
        out_shape=(jax.ShapeDtypeStruct((nb, l, D_MODEL), F32),
                   jax.ShapeDtypeStruct((n_steps * tm, D_MODEL), F32),
                   jax.ShapeDtypeStruct((nb, l, LANES), F32), jax.ShapeDtypeStruct((SUBLANES, LANES), F32)),
        grid=(n_steps,),
        in_specs=[row(D_ATTN), row(D_SSM), row(D_MODEL), mspec(2), mspec(3), mspec(4)] + [full(a) for a in consts],
        out_specs=(row(D_MODEL), pl.BlockSpec((tm, D_MODEL), lambda s: (s, 0)),
                   row(LANES), pl.BlockSpec((SUBLANES, LANES), lambda s: (0, 0))),
        compiler_params=_cparams(("arbitrary",)),
        name="post",
    )(attn, yssm, x, mod, mod, mod, *consts)


MOE_LOOKAHEAD = 2
MOE_SLOTS = MOE_LOOKAHEAD + 1


def _moe_kernel(te_ref, nv_ref, first_ref, nxt_ref, par_ref, rt0_ref, rt1_ref, rtn_ref, h2_ref, wg_hbm, wu_hbm, wd_hbm,
                y_ref, buf_ref, wgf_ref, wuf_ref, wdf_ref, wgb_ref, wub_ref, wdb_ref, sem, wsem):
    i = pl.program_id(0)
    nv = nv_ref[0]
    slot = lax.rem(i, MOE_SLOTS)
    w_pairs = ((wg_hbm, wgf_ref), (wu_hbm, wuf_ref), (wd_hbm, wdf_ref))

    def gather(ids_ref, dst_slot):
        _gather_rows(h2_ref, ids_ref, 1, 0, buf_ref.at[dst_slot], sem.at[dst_slot], (0,))

    def fetch_weights(e, ws):
        for src, dst in w_pairs:
            pltpu.make_async_copy(src.at[e], dst.at[ws], wsem.at[ws]).start(priority=1)

    @pl.when(i == 0)
    def _():
        gather(rt0_ref, 0)
        gather(rt1_ref, 1)
        fetch_weights(te_ref[0], par_ref[0])

    @pl.when(i < nv + MOE_LOOKAHEAD)
    def _():
        pltpu.make_async_copy(h2_ref.at[pl.ds(0, TM_MOE // SUBLANES)], buf_ref.at[slot], sem.at[slot]).wait()

    @pl.when((i < nv) & (first_ref[i] == 1))
    def _():
        ws = par_ref[i]
        for src, dst in w_pairs:
            pltpu.make_async_copy(src.at[0], dst.at[ws], wsem.at[ws]).wait()
        wgb_ref[...] = wgf_ref[ws].astype(BF16)
        wub_ref[...] = wuf_ref[ws].astype(BF16)
        wdb_ref[...] = wdf_ref[ws].astype(BF16)

        @pl.when(nxt_ref[i] >= 0)
        def _():
            fetch_weights(nxt_ref[i], 1 - ws)

    @pl.when(i < nv)
    def _():
        nslot = lax.rem(i + MOE_LOOKAHEAD, MOE_SLOTS)
        n_kc = D_MODEL // MXU_DIM
        groups_per_kc = TM_MOE // SUBLANES // n_kc
        hg = hu = None
        for kc in range(n_kc):
            for gg in range(groups_per_kc):
                _gather_group(h2_ref, rtn_ref, 1, 0, buf_ref.at[nslot], sem.at[nslot], (0,), kc * groups_per_kc + gg)
            ks = slice(kc * MXU_DIM, (kc + 1) * MXU_DIM)
            xk = buf_ref[slot, :, :, ks].reshape(TM_MOE, MXU_DIM).astype(BF16)
            pg = jnp.dot(xk, wgb_ref[ks, :], preferred_element_type=F32)
            pu = jnp.dot(xk, wub_ref[ks, :], preferred_element_type=F32)
            hg = pg if hg is None else hg + pg
            hu = pu if hu is None else hu + pu
        act = (hg * jax.nn.sigmoid(hg)) * hu
        y_ref[...] = jnp.dot(act.astype(BF16), wdb_ref[...], preferred_element_type=F32)

    @pl.when(i >= nv)
    def _():
        y_ref[...] = jnp.zeros(y_ref.shape, y_ref.dtype)


def _moe(tile_meta, n_valid, row_token, h2, w_gate, w_up, w_down):
    n_steps = row_token.shape[0] // TM_MOE
    rt = row_token.reshape(n_steps, 1, TM_MOE)
    te, first, nxt, par = tile_meta
    any_spec = pl.BlockSpec(memory_space=pl.ANY)
    return pl.pallas_call(
        _moe_kernel,
        out_shape=jax.ShapeDtypeStruct((n_steps * TM_MOE, D_MODEL), F32),
        grid_spec=pltpu.PrefetchScalarGridSpec(
            num_scalar_prefetch=5, grid=(n_steps,),
            in_specs=[*_index_specs(TM_MOE, 0, n_steps, MOE_LOOKAHEAD), any_spec, any_spec, any_spec, any_spec],
            out_specs=pl.BlockSpec((TM_MOE, D_MODEL), lambda i, *_: (i, 0)),
            scratch_shapes=[pltpu.VMEM((MOE_SLOTS, TM_MOE // SUBLANES, SUBLANES, D_MODEL), F32),
                            pltpu.VMEM((2, D_MODEL, D_EXPERT), F32), pltpu.VMEM((2, D_MODEL, D_EXPERT), F32),
                            pltpu.VMEM((2, D_EXPERT, D_MODEL), F32),
                            pltpu.VMEM((D_MODEL, D_EXPERT), BF16), pltpu.VMEM((D_MODEL, D_EXPERT), BF16),
                            pltpu.VMEM((D_EXPERT, D_MODEL), BF16),
                            pltpu.SemaphoreType.DMA((MOE_SLOTS,)), pltpu.SemaphoreType.DMA((2,))]),
        compiler_params=_cparams(("arbitrary",)),
        name="moe",
    )(te, n_valid, first, nxt, par, rt, rt, rt, h2.reshape(-1, SUBLANES, D_MODEL), w_gate, w_up, w_down)


def _final_kernel(pos0_ref, posn_ref, x1_ref, g2_ref, route_ref, l2g_ref, l2b_ref, ys_ref, o_ref, buf_ref, sem, *, tm):
    step = pl.program_id(0)
    slot = step & 1

    def gather(ids_ref, dst_slot):
        for k in range(2):
            _gather_rows(ys_ref, ids_ref, 1, k * tm, buf_ref.at[dst_slot, k], sem.at[dst_slot], (0, 1))

    @pl.when(step == 0)
    def _():
        gather(pos0_ref, 0)

    for k in range(2):
        pltpu.make_async_copy(ys_ref.at[pl.ds(0, tm // SUBLANES)], buf_ref.at[slot, k], sem.at[slot]).wait()

    route = route_ref[...]
    w1, w2 = route[:, 2:3], route[:, 3:4]
    n_groups = tm // SUBLANES

    def combine(issue_next):
        n_chunks = D_MODEL // MXU_DIM
        per_chunk = 2 * n_groups // n_chunks
        for c in range(n_chunks):
            if issue_next:
                for q in range(c * per_chunk, (c + 1) * per_chunk):
                    k, g = q // n_groups, q % n_groups
                    _gather_group(ys_ref, posn_ref, 1, k * tm, buf_ref.at[1 - slot, k], sem.at[1 - slot], (0, 1), g)
            cs = slice(c * MXU_DIM, (c + 1) * MXU_DIM)
            f = (w1 * buf_ref[slot, 0, :, :, cs].reshape(tm, MXU_DIM)
                 + w2 * buf_ref[slot, 1, :, :, cs].reshape(tm, MXU_DIM))
            o_ref[:, cs] = DEEPNORM_ALPHA * x1_ref[:, cs] + g2_ref[:, cs] * f

    @pl.when(step + 1 < pl.num_programs(0))
    def _():
        combine(True)

    @pl.when(step + 1 == pl.num_programs(0))
    def _():
        combine(False)

    o_ref[...] = _ln(o_ref[...]) * l2g_ref[...] + l2b_ref[...]


def _final(pos_tiles, first_tile, x1, mod, per_row, route, ln2_g, ln2_b, ys):
    nb, l, _ = x1.shape
    tm = min(TM, l)
    nt = l // tm
    n_steps = nb * nt
    row = lambda wd: pl.BlockSpec((None, tm, wd), lambda s: (s // nt, s % nt, 0))
    if per_row:
        g2 = pl.BlockSpec((None, tm, D_MODEL), lambda s: (s // nt, s % nt, 5))
    else:
        g2 = pl.BlockSpec((None, 1, D_MODEL), lambda s: (s // nt, 0, 5))
    vec = pl.BlockSpec((1, D_MODEL), lambda s: (0, 0))
    return pl.pallas_call(
        functools.partial(_final_kernel, tm=tm),
        out_shape=jax.ShapeDtypeStruct((nb, l, D_MODEL), F32),
        grid=(n_steps,),
        in_specs=[*_index_specs(2 * tm, first_tile, n_steps), row(D_MODEL), g2, row(LANES), vec, vec,
                  pl.BlockSpec(memory_space=pl.ANY)],
        out_specs=row(D_MODEL),
        scratch_shapes=[pltpu.VMEM((2, 2, tm // SUBLANES, SUBLANES, D_MODEL), F32), pltpu.SemaphoreType.DMA((2,))],
        compiler_params=_cparams(("arbitrary",)),
        name="final",
    )(pos_tiles, pos_tiles, x1, mod, route, ln2_g, ln2_b, ys.reshape(-1, SUBLANES, D_MODEL))


def _rope_tables(pos):
    half = ROT_DIM // 2
    inv_freq = ROPE_THETA ** (-jnp.arange(half, dtype=jnp.float32) * 2.0 / ROT_DIM)
    ang = pos.astype(jnp.float32)[:, None] * inv_freq[None, :]
    cos, sin = jnp.cos(ang), jnp.sin(ang)
    n = pos.shape[0]
    one = jnp.ones((n, HEAD_DIM - ROT_DIM), F32)
    zero = jnp.zeros((n, HEAD_DIM - half), F32)
    c = jnp.concatenate([cos, cos, one], -1)
    a = jnp.concatenate([-sin, zero], -1)
    b = jnp.concatenate([jnp.zeros((n, half), F32), sin, jnp.zeros((n, HEAD_DIM - ROT_DIM), F32)], -1)
    return tuple(jnp.tile(t, (1, LANES // HEAD_DIM)) for t in (c, a, b))


def _s5_params(lam_re, lam_im, log_step, b_re, b_im, c_re, c_im):
    f32 = jnp.float32
    dt = jnp.exp(log_step.astype(f32))[:, None]
    lr, li = lam_re.astype(f32), lam_im.astype(f32)
    mag = jnp.exp(lr * dt)
    ar, ai = mag * jnp.cos(li * dt), mag * jnp.sin(li * dt)
    den = lr * lr + li * li
    cr = ((ar - 1.0) * lr + ai * li) / den
    ci = (ai * lr - (ar - 1.0) * li) / den
    br, bi = b_re.astype(f32), b_im.astype(f32)
    bbr = cr[..., None] * br - ci[..., None] * bi
    bbi = cr[..., None] * bi + ci[..., None] * br
    gpt = MXU_DIM // SSM_CH
    ch_group = jnp.arange(MXU_DIM) // SSM_CH
    st_group = jnp.arange(ST_PER_KT) // SSM_STATE

    def pack_b(m):
        rows = jnp.transpose(m.reshape(N_KT, gpt, SSM_STATE, SSM_CH), (0, 1, 3, 2)).reshape(N_KT, MXU_DIM, SSM_STATE)
        keep = ch_group[:, None] == st_group[None, :]
        return jnp.where(keep[None], jnp.tile(rows, (1, 1, gpt)), 0.0).astype(BF16)

    def pack_c(m):
        cols = jnp.transpose(m.astype(f32).reshape(N_KT, gpt, SSM_CH, SSM_STATE), (0, 3, 1, 2))
        cols = cols.reshape(N_KT, SSM_STATE, MXU_DIM)
        keep = st_group[:, None] == ch_group[None, :]
        return jnp.where(keep[None], jnp.tile(cols, (1, gpt, 1)), 0.0).astype(BF16)

    wts = (pack_b(bbr), pack_b(bbi), pack_c(c_re), pack_c(-c_im.astype(f32)))
    return wts, ar.reshape(1, N_STATE), ai.reshape(1, N_STATE)


def _scan_tables(ar, ai, nb):
    assert SUBLANES // nb == 2
    a2r, a2i = ar * ar - ai * ai, 2.0 * ar * ai
    z = jnp.zeros_like(ar)
    rep = lambda first, second: jnp.concatenate([jnp.tile(first, (nb, 1)), jnp.tile(second, (nb, 1))], 0)
    return rep(z, ar), rep(z, ai), rep(ar, a2r), rep(ai, a2i)


def kernel(x_prompt, x_sample, cache_k_win, cache_v_win, state_ssm_re, state_ssm_im, c_prompt, c_sample, ada_w, ada_b,
           w_in, attn_sinks, ssm_lambda_re, ssm_lambda_im, ssm_log_step, ssm_b_re, ssm_b_im, ssm_c_re, ssm_c_im, ssm_d,
           ssm_glu_w, ssm_glu_b, attn_norm_g, ssm_norm_g, w_out, ln1_g, ln1_b, router_group_w, router_group_b,
           router_expert_w, router_expert_b, exp_w_gate, exp_w_up, exp_w_down, ln2_g, ln2_b):
    assert ada_w.shape[0] == DEPTH
    bp, lp, _ = x_prompt.shape
    bs, ls, _ = x_sample.shape
    lsp = SAMPLE_PAD

    w_in_bf = w_in[0].astype(BF16)
    wr = jnp.concatenate([router_group_w[0], router_expert_w[0]], -1)
    wr = jnp.pad(wr, ((0, 0), (0, LANES - wr.shape[1]))).astype(BF16)
    br = jnp.pad(jnp.concatenate([router_group_b[0], router_expert_b[0]], -1), (0, LANES - N_EXPERT_GROUPS - N_EXPERTS))
    wpost = dict(glu_w=ssm_glu_w[0].astype(BF16), glu_b=ssm_glu_b[0][None], attn_g=attn_norm_g[0][None],
                 ssm_g=ssm_norm_g[0][None], w_out=w_out[0].astype(BF16), ln1_g=ln1_g[0][None], ln1_b=ln1_b[0][None],
                 wr=wr, br=br[None])
    s5w, ar, ai = _s5_params(ssm_lambda_re[0], ssm_lambda_im[0], ssm_log_step[0], ssm_b_re[0], ssm_b_im[0],
                             ssm_c_re[0], ssm_c_im[0])
    d_skip = ssm_d[0][None]
    sinks = attn_sinks[0]

    n_c = bp + bs
    n_cp = -(-n_c // SUBLANES) * SUBLANES
    c_all = jnp.pad(jnp.concatenate([c_prompt, c_sample], 0), ((0, n_cp - n_c), (0, 0)))
    mod = _ada(c_all, ada_w[0], ada_b[0][None])
    mod_p = mod[:bp][:, None, :]
    mod_s = jnp.repeat(mod[bp:n_c], lsp, axis=0)[None]

    xs_pad = jnp.pad(x_sample, ((0, 0), (0, lsp - ls), (0, 0))).reshape(1, bs * lsp, D_MODEL)

    rope_p = _rope_tables(jnp.arange(lp))
    pos_s = PAST_LEN + jnp.minimum(jnp.arange(lsp), ls - 1)
    rope_s = tuple(jnp.tile(t, (bs, 1)) for t in _rope_tables(pos_s))
    q_p, k_p, v_p, u_p = _inproj(x_prompt, mod_p, w_in_bf, rope_p, False)
    q_s, k_s, v_s, u_s = _inproj(xs_pad, mod_s, w_in_bf, rope_s, True)

    attn_p = _attn_prompt(sinks, q_p, k_p, v_p)
    k_s3 = k_s.reshape(bs, lsp, D_KV)
    v_s3 = v_s.reshape(bs, lsp, D_KV)
    ck = cache_k_win[0].reshape(bs, WINDOW, D_KV)
    cv = cache_v_win[0].reshape(bs, WINDOW, D_KV)
    padk = lambda new: jnp.pad(new, ((0, 0), (0, WINDOW - lsp), (0, 0)))
    attn_s = _attn_sample(sinks, q_s.reshape(bs, lsp, D_ATTN), ck, padk(k_s3), cv, padk(v_s3))

    tabs = _scan_tables(ar, ai, bp)
    zero_carry = jnp.zeros((SUBLANES, N_STATE), F32)
    y_p, hr_p, hi_p = _s5_prompt(u_p, s5w, tabs, d_skip, zero_carry, zero_carry)
    u_tb = jnp.transpose(u_s.reshape(bs, lsp, D_SSM)[:, :ls], (1, 0, 2))
    y_tb, hr_s, hi_s = _s5_sample(u_tb, s5w, ar, ai, d_skip, state_ssm_re[0].reshape(bs, N_STATE),
                                  state_ssm_im[0].reshape(bs, N_STATE))
    y_s = jnp.pad(jnp.transpose(y_tb, (1, 0, 2)), ((0, 0), (0, lsp - ls), (0, 0))).reshape(1, bs * lsp, D_SSM)

    cnt0 = jnp.zeros((SUBLANES, LANES), F32)
    n_p, n_s = bp * lp, bs * lsp
    n_tok = n_p + n_s
    x1_s, h2_s, route_s, cnt_s = _post(attn_s.reshape(1, bs * lsp, D_ATTN), y_s, xs_pad, mod_s, True, wpost, cnt0)
    x1_p, h2, route_p, cnt = _post(attn_p, y_p, x_prompt, mod_p, False, wpost, cnt_s, tail=h2_s)

    n_tiles = -(-(2 * n_tok + N_EXPERTS * (TM_MOE - 1)) // TM_MOE) + MOE_LOOKAHEAD
    route = jnp.concatenate([route_p.reshape(n_p, LANES), route_s.reshape(n_s, LANES)], 0)
    counts = cnt[0, ROUTE_LANE0:ROUTE_LANE0 + N_EXPERTS].astype(jnp.int32)
    padded = ((counts + TM_MOE - 1) // TM_MOE) * TM_MOE
    ends = jnp.cumsum(padded)
    offs = ends - padded
    col = lambda c: route[:, c].astype(jnp.int32)
    expert_ids = jnp.arange(N_EXPERTS, dtype=jnp.int32)
    seg_start = lambda e: jnp.sum(jnp.where(e[:, None] == expert_ids[None, :], offs[None, :], 0), axis=1)
    pos_k = [seg_start(col(k)) + col(4 + k) for k in range(2)]
    n_valid = ends[-1] // TM_MOE
    tile_row = jnp.minimum(jnp.arange(n_tiles), n_valid - 1) * TM_MOE
    tile_expert = jnp.sum((ends[None, :] <= tile_row[:, None]).astype(jnp.int32), axis=1)
    first_tile = jnp.concatenate([jnp.ones((1,), jnp.int32), (tile_expert[1:] != tile_expert[:-1]).astype(jnp.int32)])
    first_tile = first_tile * (jnp.arange(n_tiles) < n_valid)
    nonempty = padded > 0
    later =jnp.where(nonempty[None, :] & (expert_ids[None, :] > expert_ids[:, None]), expert_ids[None, :], N_EXPERTS)
    next_expert = jnp.min(later, axis=1)
    next_expert = jnp.where(next_expert < N_EXPERTS, next_expert, -1).astype(jnp.int32)
    weight_slot = ((jnp.cumsum(nonempty.astype(jnp.int32)) - 1) & 1).astype(jnp.int32)
    tile_meta = (tile_expert, first_tile.astype(jnp.int32), next_expert[tile_expert], weight_slot[tile_expert])
    token_ids = jnp.arange(n_tok, dtype=jnp.int32)
    row_token = (jnp.arange(n_tiles * TM_MOE, dtype=jnp.int32) % n_tok).at[jnp.concatenate(pos_k)].set(
        jnp.concatenate([token_ids, token_ids]))

    ys = _moe(tile_meta, n_valid.reshape(1).astype(jnp.int32), row_token, h2, exp_w_gate[0], exp_w_up[0],
              exp_w_down[0])
    pos_tiles = jnp.concatenate([p.reshape(n_tok // TM, 1, TM) for p in pos_k], axis=2)
    y_prompt = _final(pos_tiles, 0, x1_p, mod_p, False, route_p, ln2_g[0][None], ln2_b[0][None], ys)
    y_samp = _final(pos_tiles, n_p // TM, x1_s, mod_s, True, route_s, ln2_g[0][None], ln2_b[0][None], ys)
    y_sample = y_samp.reshape(bs, lsp, D_MODEL)[:, :ls]

    kv5 = lambda a, n: a.reshape(1, a.shape[0], n, N_KV_HEADS, HEAD_DIM)
    k_win_p = kv5(k_p[:, lp - WINDOW:], WINDOW)
    v_win_p = kv5(v_p[:, lp - WINDOW:], WINDOW)
    k_win_s = kv5(jnp.concatenate([ck[:, ls:], k_s3[:, :ls]], 1), WINDOW)
    v_win_s = kv5(jnp.concatenate([cv[:, ls:], v_s3[:, :ls]], 1), WINDOW)
    st = lambda a: a.reshape(1, a.shape[0], N_SSM_GROUPS, SSM_STATE)
    return (y_prompt, y_sample, k_win_p, v_win_p, st(hr_p[bp:2 * bp]), st(hi_p[bp:2 * bp]),
            k_win_s, v_win_s, st(hr_s), st(hi_s))
```

```python
import functools
import math

import jax
import jax.numpy as jnp
from jax import lax
from jax.experimental import pallas as pl
from jax.experimental.pallas import tpu as pltpu

F32 = jnp.float32
BF16 = jnp.bfloat16

D_MODEL = 2048
D_ATTN = 1024
D_SSM = 1024
HEAD_DIM = 64
N_HEADS = 16
N_KV_HEADS = 4
REP = 4
D_KV = 256
ROT_DIM = 16
ROPE_THETA = 500000.0
WINDOW = 128
SSM_CH = 16
N_SSM_GROUPS = 64
SSM_STATE = 64
N_STATE = N_SSM_GROUPS * SSM_STATE
PROJ_COLS = D_ATTN + 2 * D_KV + D_SSM
N_EXPERT_GROUPS = 4
EXPERTS_PER_GROUP = 8
N_EXPERTS = 32
D_EXPERT = 512
DEPTH = 1
DEEPNORM_ALPHA = (2.0 * DEPTH) ** 0.25
LN_EPS = 1e-5
PAST_LEN = 16384

LANES = 128
SUBLANES = 8
MXU_DIM = 256
TM = 256
TM_MOE = 256
TM_INPROJ = 512
S5_TT = 128
SAMPLE_PAD = 8
ROUTE_LANE0 = N_EXPERT_GROUPS
VMEM_LIMIT = 56 * 1024 * 1024


def _cparams(sem):
    return pltpu.CompilerParams(dimension_semantics=sem, vmem_limit_bytes=VMEM_LIMIT)


def _ln(x):
    mu = jnp.mean(x, axis=-1, keepdims=True)
    xc = x - mu
    var = jnp.mean(xc * xc, axis=-1, keepdims=True)
    return xc * lax.rsqrt(var + LN_EPS)


def _rms(x):
    return x * lax.rsqrt(jnp.mean(x * x, axis=-1, keepdims=True) + LN_EPS)


def _ada_kernel(c_ref, w_ref, b_ref, o_ref):
    c = c_ref[...]
    s = c * jax.nn.sigmoid(c)
    o_ref[...] = jnp.dot(s.astype(BF16), w_ref[...].astype(BF16), preferred_element_type=F32) + b_ref[...]


def _ada(c_all, ada_w, ada_b):
    n, tn = c_all.shape[0], 1024
    return pl.pallas_call(
        _ada_kernel,
        out_shape=jax.ShapeDtypeStruct((n, 6 * D_MODEL), F32),
        grid=(6 * D_MODEL // tn,),
        in_specs=[pl.BlockSpec((n, D_MODEL), lambda j: (0, 0)),
                  pl.BlockSpec((D_MODEL, tn), lambda j: (0, j)),
                  pl.BlockSpec((1, tn), lambda j: (0, j))],
        out_specs=pl.BlockSpec((n, tn), lambda j: (0, j)),
        compiler_params=_cparams(("arbitrary",)),
        name="ada",
    )(c_all, ada_w, ada_b)


def _mod_spec(per_row, tm, chunk):
    if per_row:
        return pl.BlockSpec((None, tm, D_MODEL), lambda b, i: (b, i, chunk))
    return pl.BlockSpec((None, 1, D_MODEL), lambda b, i: (b, 0, chunk))


def _inproj_kernel(x_ref, sh_ref, sc_ref, w_ref, rc_ref, ra_ref, rb_ref, q_ref, k_ref, v_ref, u_ref):
    h = _ln(x_ref[...]) * (1.0 + sc_ref[...]) + sh_ref[...]
    proj = jnp.dot(h.astype(BF16), w_ref[...], preferred_element_type=F32)
    rc, ra, rb = rc_ref[...], ra_ref[...], rb_ref[...]

    def rope(t):
        return t * rc + pltpu.roll(t, LANES - ROT_DIM // 2, 1) * ra + pltpu.roll(t, ROT_DIM // 2, 1) * rb

    for j in range(D_ATTN // LANES):
        q_ref[:, j * LANES:(j + 1) * LANES] = (rope(proj[:, j * LANES:(j + 1) * LANES]) * HEAD_DIM ** -0.5).astype(BF16)
    for j in range(D_KV // LANES):
        c0 = D_ATTN + j * LANES
        k_ref[:, j * LANES:(j + 1) * LANES] = rope(proj[:, c0:c0 + LANES])
    v_ref[...] = proj[:, D_ATTN + D_KV:D_ATTN + 2 * D_KV]
    u_ref[...] = proj[:, D_ATTN + 2 * D_KV:]


def _inproj(x, mod, w_in_bf, rope_tabs, per_row):
    nb, l, _ = x.shape
    tm = min(TM_INPROJ, l)
    row = lambda w: pl.BlockSpec((None, tm, w), lambda b, i: (b, i, 0))
    tab = pl.BlockSpec((tm, LANES), lambda b, i: (i, 0))
    return pl.pallas_call(
        _inproj_kernel,
        out_shape=(jax.ShapeDtypeStruct((nb, l, D_ATTN), BF16), jax.ShapeDtypeStruct((nb, l, D_KV), F32),
                   jax.ShapeDtypeStruct((nb, l, D_KV), F32), jax.ShapeDtypeStruct((nb, l, D_SSM), F32)),
        grid=(nb, l // tm),
        in_specs=[row(D_MODEL), _mod_spec(per_row, tm, 0), _mod_spec(per_row, tm, 1),
                  pl.BlockSpec((D_MODEL, PROJ_COLS), lambda b, i: (0, 0)), tab, tab, tab],
        out_specs=(row(D_ATTN), row(D_KV), row(D_KV), row(D_SSM)),
        compiler_params=_cparams(("arbitrary", "arbitrary")),
        name="inproj",
    )(x, mod, mod, w_in_bf, *rope_tabs)


ATTN_SAMPLE_BATCH = 8


def _attn_kernel(sink_ref, q_ref, kp_ref, kc_ref, vp_ref, vc_ref, o_ref, *, lq, prev_from_block):
    m_rows = REP * lq
    ii = lax.broadcasted_iota(jnp.int32, (m_rows, WINDOW), 0) & (lq - 1)
    jj = lax.broadcasted_iota(jnp.int32, (m_rows, WINDOW), 1)
    from_prev = jj > ii
    if prev_from_block:
        dead = jj > ii + jnp.where(pl.program_id(1) > 0, WINDOW, 0)
    rr = lax.broadcasted_iota(jnp.int32, (m_rows, 1), 0)
    dn = (((1,), (1,)), ((), ()))
    for bi in range(q_ref.shape[0]):
        for g in range(N_KV_HEADS):
            qg = q_ref[bi, :, g * REP * HEAD_DIM:(g + 1) * REP * HEAD_DIM].astype(F32)
            qs = jnp.concatenate([qg[:, r * HEAD_DIM:(r + 1) * HEAD_DIM] for r in range(REP)], axis=0).astype(BF16)
            ks = slice(g * HEAD_DIM, (g + 1) * HEAD_DIM)
            s_p = lax.dot_general(qs, kp_ref[bi, :, ks].astype(BF16), dn, preferred_element_type=F32)
            s_c = lax.dot_general(qs, kc_ref[bi, :, ks].astype(BF16), dn, preferred_element_type=F32)
            s = jnp.where(from_prev, s_p, s_c)
            if prev_from_block:
                s = jnp.where(dead, -jnp.inf, s)
            sink = jnp.zeros((m_rows, 1), F32)
            for r in range(REP):
                sink = jnp.where((rr >= r * lq) & (rr < (r + 1) * lq), sink_ref[g * REP + r], sink)
            m = jnp.maximum(jnp.max(s, axis=-1, keepdims=True), sink)
            p = jnp.exp(s - m)
            p = p / (jnp.sum(p, axis=-1, keepdims=True) + jnp.exp(sink - m))
            o = (jnp.dot(jnp.where(from_prev, p, 0.0).astype(BF16), vp_ref[bi, :, ks].astype(BF16),
                         preferred_element_type=F32)
                 + jnp.dot(jnp.where(from_prev, 0.0, p).astype(BF16), vc_ref[bi, :, ks].astype(BF16),
                           preferred_element_type=F32))
            for r in range(REP):
                h = g * REP + r
                o_ref[bi, :, h * HEAD_DIM:(h + 1) * HEAD_DIM] = o[r * lq:(r + 1) * lq]


def _attn_prompt(sinks, q, k, v):
    nb, l, _ = q.shape
    nblk = l // WINDOW
    cur = lambda w: pl.BlockSpec((1, WINDOW, w), lambda b, n: (b, n, 0))
    prev = lambda w: pl.BlockSpec((1, WINDOW, w), lambda b, n: (b, jnp.maximum(n - 1, 0), 0))
    return pl.pallas_call(
        functools.partial(_attn_kernel, lq=WINDOW, prev_from_block=True),
        out_shape=jax.ShapeDtypeStruct((nb, l, D_ATTN), F32),
        grid=(nb, nblk),
        in_specs=[pl.BlockSpec(memory_space=pltpu.SMEM), cur(D_ATTN), prev(D_KV), cur(D_KV), prev(D_KV), cur(D_KV)],
        out_specs=cur(D_ATTN),
        compiler_params=_cparams(("arbitrary", "arbitrary")),
        name="attn_prompt",
    )(sinks, q, k, k, v, v)


ATTN_SAMPLE_JOINT = 4


def _attn_sample_kernel(sink_ref, q_ref, kp_ref, kc_ref, vp_ref, vc_ref, o_ref, *, lq):
    gb = ATTN_SAMPLE_JOINT
    m_b = REP * lq
    m, n = gb * m_b, gb * WINDOW
    row = lax.broadcasted_iota(jnp.int32, (m, n), 0)
    col = lax.broadcasted_iota(jnp.int32, (m, n), 1)
    same = (row // m_b) == (col // WINDOW)
    from_prev = (col & (WINDOW - 1)) > (row & (lq - 1))
    rep_of_row = (lax.broadcasted_iota(jnp.int32, (m, 1), 0) // lq) % REP
    dn = (((1,), (1,)), ((), ()))
    for blk in range(q_ref.shape[0] // gb):
        bs = range(blk * gb, (blk + 1) * gb)
        for g in range(N_KV_HEADS):
            qg = [q_ref[b, :, g * REP * HEAD_DIM:(g + 1) * REP * HEAD_DIM].astype(F32) for b in bs]
            qs = jnp.concatenate([qb[:, r * HEAD_DIM:(r + 1) * HEAD_DIM] for qb in qg for r in range(REP)],
                                 axis=0).astype(BF16)
            ks = slice(g * HEAD_DIM, (g + 1) * HEAD_DIM)
            stack = lambda ref: jnp.concatenate([ref[b, :, ks] for b in bs], axis=0).astype(BF16)
            s_p = lax.dot_general(qs, stack(kp_ref), dn, preferred_element_type=F32)
            s_c = lax.dot_general(qs, stack(kc_ref), dn, preferred_element_type=F32)
            s = jnp.where(same, jnp.where(from_prev, s_p, s_c), -jnp.inf)
            sink = jnp.zeros((m, 1), F32)
            for r in range(REP):
                sink = jnp.where(rep_of_row == r, sink_ref[g * REP + r], sink)
            mx = jnp.maximum(jnp.max(s, axis=-1, keepdims=True), sink)
            p = jnp.exp(s - mx)
            p = p / (jnp.sum(p, axis=-1, keepdims=True) + jnp.exp(sink - mx))
            o = (jnp.dot(jnp.where(from_prev, p, 0.0).astype(BF16), stack(vp_ref), preferred_element_type=F32)
                 + jnp.dot(jnp.where(from_prev, 0.0, p).astype(BF16), stack(vc_ref), preferred_element_type=F32))
            for bi, b in enumerate(bs):
                for r in range(REP):
                    h = g * REP + r
                    o_ref[b, :, h * HEAD_DIM:(h + 1) * HEAD_DIM] = o[bi * m_b + r * lq:bi * m_b + (r + 1) * lq]


def _attn_sample(sinks, q, k_cache, k_new, v_cache, v_new):
    nb, lq, _ = q.shape
    nbb = ATTN_SAMPLE_BATCH
    kblk = pl.BlockSpec((nbb, WINDOW, D_KV), lambda b: (b, 0, 0))
    return pl.pallas_call(
        functools.partial(_attn_sample_kernel, lq=lq),
        out_shape=jax.ShapeDtypeStruct((nb, lq, D_ATTN), F32),
        grid=(nb // nbb,),
        in_specs=[pl.BlockSpec(memory_space=pltpu.SMEM), pl.BlockSpec((nbb, lq, D_ATTN), lambda b: (b, 0, 0)),
                  kblk, kblk, kblk, kblk],
        out_specs=pl.BlockSpec((nbb, lq, D_ATTN), lambda b: (b, 0, 0)),
        compiler_params=_cparams(("arbitrary",)),
        name="attn_sample",
    )(sinks, q, k_cache, k_new, v_cache, v_new)


N_KT = D_SSM // MXU_DIM
ST_PER_KT = N_STATE // N_KT
N_SLAB = D_SSM // LANES


def _cproj(hr_ref, hi_ref, cre_ref, cimn_ref, kt):
    cs = slice(kt * ST_PER_KT, (kt + 1) * ST_PER_KT)
    return (jnp.dot(hr_ref[:, cs].astype(BF16), cre_ref[kt], preferred_element_type=F32)
            + jnp.dot(hi_ref[:, cs].astype(BF16), cimn_ref[kt], preferred_element_type=F32))


def _s5_prompt_kernel(u_ref, bre_ref, bim_ref, cre_ref, cimn_ref, a1r_ref, a1i_ref, par_ref, pai_ref, d_ref,
                      h0r_ref, h0i_ref, y_ref, cr_ref, ci_ref, il_ref, xr_ref, xi_ref, *, nb, tt):
    rows = nb * tt

    @pl.when(pl.program_id(0) == 0)
    def _():
        cr_ref[...] = h0r_ref[...]
        ci_ref[...] = h0i_ref[...]

    for b in range(nb):
        for j in range(N_SLAB):
            il_ref[j, pl.ds(b, tt, stride=nb), :] = u_ref[b, :, j * LANES:(j + 1) * LANES]
    for kt in range(N_KT):
        ub = jnp.concatenate([il_ref[2 * kt], il_ref[2 * kt + 1]], axis=1).astype(BF16)
        cs = slice(kt * ST_PER_KT, (kt + 1) * ST_PER_KT)
        xr_ref[:, cs] = jnp.dot(ub, bre_ref[kt], preferred_element_type=F32)
        xi_ref[:, cs] = jnp.dot(ub, bim_ref[kt], preferred_element_type=F32)

    half = lax.broadcasted_iota(jnp.int32, (SUBLANES, LANES), 0) < nb

    def step(i, carry):
        r0 = pl.multiple_of(i * SUBLANES, SUBLANES)
        for j in range(N_STATE // LANES):
            ls = slice(j * LANES, (j + 1) * LANES)
            x_r = xr_ref[pl.ds(r0, SUBLANES), ls]
            x_i = xi_ref[pl.ds(r0, SUBLANES), ls]
            s_r = pltpu.roll(x_r, nb, 0)
            s_i = pltpu.roll(x_i, nb, 0)
            a1r, a1i = a1r_ref[:, ls], a1i_ref[:, ls]
            c_r, c_i = cr_ref[:, ls], ci_ref[:, ls]
            par, pai = par_ref[:, ls], pai_ref[:, ls]
            h_r = x_r + (a1r * s_r - a1i * s_i) + (par * c_r - pai * c_i)
            h_i = x_i + (a1r * s_i + a1i * s_r) + (par * c_i + pai * c_r)
            xr_ref[pl.ds(r0, SUBLANES), ls] = h_r
            xi_ref[pl.ds(r0, SUBLANES), ls] = h_i
            cr_ref[:, ls] = jnp.where(half, pltpu.roll(h_r, nb, 0), h_r)
            ci_ref[:, ls] = jnp.where(half, pltpu.roll(h_i, nb, 0), h_i)
        return carry

    lax.fori_loop(0, rows // SUBLANES, step, 0)

    for kt in range(N_KT):
        y = _cproj(xr_ref, xi_ref, cre_ref, cimn_ref, kt)
        il_ref[2 * kt] = y[:, :LANES]
        il_ref[2 * kt + 1] = y[:, LANES:]
    for b in range(nb):
        for j in range(N_SLAB):
            ls = slice(j * LANES, (j + 1) * LANES)
            y_ref[b, :, ls] = il_ref[j, pl.ds(b, tt, stride=nb), :] + d_ref[:, ls] * u_ref[b, :, ls]


def _s5_prompt(u, wts, tabs, d_skip, h0r, h0i):
    nb, l, _ = u.shape
    tt = S5_TT
    rows = nb * tt
    full = lambda a: pl.BlockSpec(a.shape, lambda i: (0,) * a.ndim)
    blk = pl.BlockSpec((nb, tt, D_SSM), lambda i: (0, i, 0))
    carry = jax.ShapeDtypeStruct((SUBLANES, N_STATE), F32)
    return pl.pallas_call(
        functools.partial(_s5_prompt_kernel, nb=nb, tt=tt),
        out_shape=(jax.ShapeDtypeStruct((nb, l, D_SSM), F32), carry, carry),
        grid=(l // tt,),
        in_specs=[blk] + [full(a) for a in (*wts, *tabs, d_skip, h0r, h0i)],
        out_specs=(blk, pl.BlockSpec((SUBLANES, N_STATE), lambda i: (0, 0)),
                   pl.BlockSpec((SUBLANES, N_STATE), lambda i: (0, 0))),
        scratch_shapes=[pltpu.VMEM((N_SLAB, rows, LANES), F32), pltpu.VMEM((rows, N_STATE), F32),
                        pltpu.VMEM((rows, N_STATE), F32)],
        compiler_params=_cparams(("arbitrary",)),
        name="s5_prompt",
    )(u, *wts, *tabs, d_skip, h0r, h0i)


def _s5_sample_kernel(u_ref, bre_ref, bim_ref, cre_ref, cimn_ref, ar_ref, ai_ref, d_ref, h0r_ref, h0i_ref,
                      y_ref, sr_ref, si_ref, xr_ref, xi_ref, *, nt):
    sr_ref[...] = h0r_ref[...]
    si_ref[...] = h0i_ref[...]
    for t in range(nt):
        for kt in range(N_KT):
            ub = u_ref[t, :, kt * MXU_DIM:(kt + 1) * MXU_DIM].astype(BF16)
            cs = slice(kt * ST_PER_KT, (kt + 1) * ST_PER_KT)
            xr_ref[:, cs] = jnp.dot(ub, bre_ref[kt], preferred_element_type=F32)
            xi_ref[:, cs] = jnp.dot(ub, bim_ref[kt], preferred_element_type=F32)
        ar, ai = ar_ref[...], ai_ref[...]
        s_r, s_i = sr_ref[...], si_ref[...]
        sr_ref[...] = xr_ref[...] + (ar * s_r - ai * s_i)
        si_ref[...] = xi_ref[...] + (ar * s_i + ai * s_r)
        for kt in range(N_KT):
            ys = slice(kt * MXU_DIM, (kt + 1) * MXU_DIM)
            y_ref[t, :, ys] = _cproj(sr_ref, si_ref, cre_ref, cimn_ref, kt) + d_ref[:, ys] * u_ref[t, :, ys]


def _s5_sample(u_tb, wts, ar, ai, d_skip, h0r, h0i):
    nt, nb, _ = u_tb.shape
    st = jax.ShapeDtypeStruct((nb, N_STATE), F32)
    args = (u_tb, *wts, ar, ai, d_skip, h0r, h0i)
    full = lambda a: pl.BlockSpec(a.shape, lambda i: (0,) * a.ndim)
    return pl.pallas_call(
        functools.partial(_s5_sample_kernel, nt=nt),
        out_shape=(jax.ShapeDtypeStruct((nt, nb, D_SSM), F32), st, st),
        grid=(1,),
        in_specs=[full(a) for a in args],
        out_specs=(pl.BlockSpec((nt, nb, D_SSM), lambda i: (0, 0, 0)), pl.BlockSpec((nb, N_STATE), lambda i: (0, 0)),
                   pl.BlockSpec((nb, N_STATE), lambda i: (0, 0))),
        scratch_shapes=[pltpu.VMEM((nb, N_STATE), F32), pltpu.VMEM((nb, N_STATE), F32)],
        compiler_params=_cparams(("arbitrary",)),
        name="s5_sample",
    )(*args)


def _gather_group(src_ref, ids_ref, id_stride, id_offset, dst_ref, sem, priorities, g):
    for j in range(SUBLANES):
        t = ids_ref[0, id_stride * (g * SUBLANES + j) + id_offset]
        pltpu.make_async_copy(src_ref.at[t >> 3, pl.ds(t & (SUBLANES - 1), 1)], dst_ref.at[g, pl.ds(j, 1)],
                              sem).start(priority=priorities[j % len(priorities)])


def _gather_rows(src_ref, ids_ref, id_stride, id_offset, dst_ref, sem, priorities):
    def body(g, c):
        _gather_group(src_ref, ids_ref, id_stride, id_offset, dst_ref, sem, priorities, g)
        return c
    lax.fori_loop(0, dst_ref.shape[0], body, 0)


def _index_specs(n_cols, first_tile, n_steps, depth=1):
    def spec(fn):
        return pl.BlockSpec((None, 1, n_cols), fn, memory_space=pltpu.SMEM)

    head = [spec(lambda s, *_, k=k: (first_tile + min(k, n_steps - 1), 0, 0)) for k in range(depth)]
    return (*head, spec(lambda s, *_: (first_tile + jnp.minimum(s + depth, n_steps - 1), 0, 0)))


N_POST_INPUTS = 17
POST_ROW_PARTS = 1


def _post_kernel(*refs, has_tail):
    ins, outs = refs[:N_POST_INPUTS], refs[N_POST_INPUTS + int(has_tail):]
    if not has_tail:
        _post_body(*ins, *outs)
        return
    tail_ref, h2_ref = refs[N_POST_INPUTS], outs[1]
    last = pl.num_programs(0) - 1

    @pl.when(pl.program_id(0) == last)
    def _():
        h2_ref[...] = tail_ref[...]

    @pl.when(pl.program_id(0) < last)
    def _():
        _post_body(*ins, *outs)


def _post_body(attn_ref, yssm_ref, x_ref, g1_ref, sh2_ref, sc2_ref, gluw_ref, glub_ref, ga_ref, gs_ref, wout_ref,
               l1g_ref, l1b_ref, wr_ref, br_ref, tri_ref, cnt0_ref, x1_ref, h2_ref, route_ref, cnt_ref):
    tm = x_ref.shape[0]

    @pl.when(pl.program_id(0) == 0)
    def _():
        cnt_ref[...] = cnt0_ref[...]

    hm = tm // POST_ROW_PARTS
    lane = lax.broadcasted_iota(jnp.int32, (hm, LANES), 1).astype(F32)
    big = float(4 * LANES)
    neg = -jnp.inf
    cnt = cnt_ref[0:1, :]
    for part in range(POST_ROW_PARTS):
        rs = slice(part * hm, (part + 1) * hm)
        rows = lambda ref: ref[rs, :] if ref.shape[0] == tm else ref[...]
        z = jax.nn.gelu(yssm_ref[rs, :])
        ssm = z * jax.nn.sigmoid(jnp.dot(z.astype(BF16), gluw_ref[...], preferred_element_type=F32) + glub_ref[...])
        mixed_a = (_rms(attn_ref[rs, :]) * ga_ref[...]).astype(BF16)
        mixed_s = (_rms(ssm) * gs_ref[...]).astype(BF16)
        o = (jnp.dot(mixed_a, wout_ref[:D_ATTN, :], preferred_element_type=F32)
             + jnp.dot(mixed_s, wout_ref[D_ATTN:, :], preferred_element_type=F32))
        x1 = _ln(DEEPNORM_ALPHA * x_ref[rs, :] + rows(g1_ref) * o) * l1g_ref[...] + l1b_ref[...]
        x1_ref[rs, :] = x1
        h2 = _ln(x1) * (1.0 + rows(sc2_ref)) + rows(sh2_ref)
        h2_ref[rs, :] = h2
        logits = jnp.dot(h2.astype(BF16), wr_ref[...], preferred_element_type=F32) + br_ref[...]

        gl = jnp.where(lane < N_EXPERT_GROUPS, logits, neg)
        gp = jnp.exp(gl - jnp.max(gl, axis=-1, keepdims=True))
        gp = gp / jnp.sum(gp, axis=-1, keepdims=True)
        g_val = jnp.max(gp, axis=-1, keepdims=True)
        g_idx = jnp.min(jnp.where(gp == g_val, lane, big), axis=-1, keepdims=True)
        lo = ROUTE_LANE0 + EXPERTS_PER_GROUP * g_idx
        emask = (lane >= lo) & (lane < lo + EXPERTS_PER_GROUP)
        el = jnp.where(emask, logits, neg)
        ep = jnp.exp(el - jnp.max(el, axis=-1, keepdims=True))
        ep = jnp.where(emask, ep / jnp.sum(ep, axis=-1, keepdims=True), -1.0)
        v1 = jnp.max(ep, axis=-1, keepdims=True)
        i1 = jnp.min(jnp.where(ep == v1, lane, big), axis=-1, keepdims=True)
        ep2 = jnp.where(lane == i1, -1.0, ep)
        v2 = jnp.max(ep2, axis=-1, keepdims=True)
        i2 = jnp.min(jnp.where(ep2 == v2, lane, big), axis=-1, keepdims=True)
        vs = v1 + v2
        w1 = g_val * (v1 / vs)
        w2 = g_val * (v2 / vs)
        hit = jnp.where((lane == i1) | (lane == i2), 1.0, 0.0)
        before = jnp.dot(tri_ref[:hm, :hm], hit.astype(BF16), preferred_element_type=F32) + cnt
        r1 = jnp.sum(jnp.where(lane == i1, before, 0.0), axis=-1, keepdims=True)
        r2 = jnp.sum(jnp.where(lane == i2, before, 0.0), axis=-1, keepdims=True)
        cnt = cnt + jnp.sum(hit, axis=0, keepdims=True)
        e1 = i1 - ROUTE_LANE0
        e2 = i2 - ROUTE_LANE0
        route = jnp.zeros((hm, LANES), F32)
        for n, val in enumerate((e1, e2, w1, w2, r1, r2)):
            route = jnp.where(lane == n, val, route)
        route_ref[rs, :] = route
    cnt_ref[...] = jnp.broadcast_to(cnt, cnt_ref.shape)


def _post(attn, yssm, x, mod, per_row, w, cnt0, tail=None):
    nb, l, _ = x.shape
    tm = min(TM, l)
    nt = l // tm
    n_body = nb * nt
    has_tail = tail is not None
    if has_tail:
        assert tail.shape == (tm, D_MODEL)

    def bi(s):
        s = jnp.minimum(s, n_body - 1)
        return s // nt, s % nt

    row = lambda wd: pl.BlockSpec((None, tm, wd), lambda s: (*bi(s), 0))
    if per_row:
        mspec = lambda chunk: pl.BlockSpec((None, tm, D_MODEL), lambda s: (*bi(s), chunk))
    else:
        mspec = lambda chunk: pl.BlockSpec((None, 1, D_MODEL), lambda s: (bi(s)[0], 0, chunk))
    full = lambda a: pl.BlockSpec(a.shape, lambda s: (0,) * a.ndim)
    tri = jnp.tril(jnp.ones((tm, tm), F32), -1).astype(BF16)
    consts = (w["glu_w"], w["glu_b"], w["attn_g"], w["ssm_g"], w["w_out"], w["ln1_g"], w["ln1_b"], w["wr"], w["br"],
              tri, cnt0) + ((tail,) if has_tail else ())
    assert 6 + len(consts) == N_POST_INPUTS + int(has_tail)
    n_steps = n_body + int(has_tail)
    return pl.pallas_call(
        functools.partial(_post_kernel, has_tail=has_tail),
        out_shape=(jax.ShapeDtypeStruct((nb, l, D_MODEL), F32),
                   jax.ShapeDtypeStruct((n_steps * tm, D_MODEL), F32),
                   jax.ShapeDtypeStruct((nb, l, LANES), F32), jax.ShapeDtypeStruct((SUBLANES, LANES), F32)),
        grid=(n_steps,),
        in_specs=[row(D_ATTN), row(D_SSM), row(D_MODEL), mspec(2), mspec(3), mspec(4)] + [full(a) for a in consts],
        out_specs=(row(D_MODEL), pl.BlockSpec((tm, D_MODEL), lambda s: (s, 0)),
                   row(LANES), pl.BlockSpec((SUBLANES, LANES), lambda s: (0, 0))),
        compiler_params=_cparams(("arbitrary",)),
        name="post",
    )(attn, yssm, x, mod, mod, mod, *consts)


MOE_LOOKAHEAD = 2
MOE_SLOTS = MOE_LOOKAHEAD + 1


def _moe_kernel(te_ref, nv_ref, first_ref, nxt_ref, par_ref, rt0_ref, rt1_ref, rtn_ref, h2_ref, wg_hbm, wu_hbm, wd_hbm,
                y_ref, buf_ref, wgf_ref, wuf_ref, wdf_ref, sem, wsem):
    i = pl.program_id(0)
    nv = nv_ref[0]
    slot = lax.rem(i, MOE_SLOTS)
    w_pairs = ((wg_hbm, wgf_ref), (wu_hbm, wuf_ref), (wd_hbm, wdf_ref))

    def gather(ids_ref, dst_slot):
        _gather_rows(h2_ref, ids_ref, 1, 0, buf_ref.at[dst_slot], sem.at[dst_slot], (0,))

    def fetch_weights(e, ws):
        for src, dst in w_pairs:
            pltpu.make_async_copy(src.at[e], dst.at[ws], wsem.at[ws]).start(priority=1)

    @pl.when(i == 0)
    def _():
        gather(rt0_ref, 0)
        gather(rt1_ref, 1)
        fetch_weights(te_ref[0], par_ref[0])

    @pl.when(i < nv + MOE_LOOKAHEAD)
    def _():
        pltpu.make_async_copy(h2_ref.at[pl.ds(0, TM_MOE // SUBLANES)], buf_ref.at[slot], sem.at[slot]).wait()

    @pl.when((i < nv) & (first_ref[i] == 1))
    def _():
        ws = par_ref[i]
        for src, dst in w_pairs:
            pltpu.make_async_copy(src.at[0], dst.at[ws], wsem.at[ws]).wait()

        @pl.when(nxt_ref[i] >= 0)
        def _():
            fetch_weights(nxt_ref[i], 1 - ws)

    @pl.when(i < nv)
    def _():
        nslot = lax.rem(i + MOE_LOOKAHEAD, MOE_SLOTS)
        ws = par_ref[i]
        n_kc = D_MODEL // MXU_DIM
        groups_per_kc = TM_MOE // SUBLANES // n_kc
        hg = hu = None
        for kc in range(n_kc):
            for gg in range(groups_per_kc):
                _gather_group(h2_ref, rtn_ref, 1, 0, buf_ref.at[nslot], sem.at[nslot], (0,), kc * groups_per_kc + gg)
            ks = slice(kc * MXU_DIM, (kc + 1) * MXU_DIM)
            xk = buf_ref[slot, :, :, ks].reshape(TM_MOE, MXU_DIM).astype(BF16)
            pg = jnp.dot(xk, wgf_ref[ws, ks, :].astype(BF16), preferred_element_type=F32)
            pu = jnp.dot(xk, wuf_ref[ws, ks, :].astype(BF16), preferred_element_type=F32)
            hg = pg if hg is None else hg + pg
            hu = pu if hu is None else hu + pu
        act = (hg * jax.nn.sigmoid(hg)) * hu
        y_ref[...] = jnp.dot(act.astype(BF16), wdf_ref[ws].astype(BF16), preferred_element_type=F32)

    @pl.when(i >= nv)
    def _():
        y_ref[...] = jnp.zeros(y_ref.shape, y_ref.dtype)


def _moe(tile_meta, n_valid, row_token, h2, w_gate, w_up, w_down):
    n_steps = row_token.shape[0] // TM_MOE
    rt = row_token.reshape(n_steps, 1, TM_MOE)
    te, first, nxt, par = tile_meta
    any_spec = pl.BlockSpec(memory_space=pl.ANY)
    return pl.pallas_call(
        _moe_kernel,
        out_shape=jax.ShapeDtypeStruct((n_steps * TM_MOE, D_MODEL), F32),
        grid_spec=pltpu.PrefetchScalarGridSpec(
            num_scalar_prefetch=5, grid=(n_steps,),
            in_specs=[*_index_specs(TM_MOE, 0, n_steps, MOE_LOOKAHEAD), any_spec, any_spec, any_spec, any_spec],
            out_specs=pl.BlockSpec((TM_MOE, D_MODEL), lambda i, *_: (i, 0)),
            scratch_shapes=[pltpu.VMEM((MOE_SLOTS, TM_MOE // SUBLANES, SUBLANES, D_MODEL), F32),
                            pltpu.VMEM((2, D_MODEL, D_EXPERT), F32), pltpu.VMEM((2, D_MODEL, D_EXPERT), F32),
                            pltpu.VMEM((2, D_EXPERT, D_MODEL), F32),
                            pltpu.SemaphoreType.DMA((MOE_SLOTS,)), pltpu.SemaphoreType.DMA((2,))]),
        compiler_params=_cparams(("arbitrary",)),
        name="moe",
    )(te, n_valid, first, nxt, par, rt, rt, rt, h2.reshape(-1, SUBLANES, D_MODEL), w_gate, w_up, w_down)


def _final_kernel(pos0_ref, posn_ref, x1_ref, g2_ref, route_ref, l2g_ref, l2b_ref, ys_ref, o_ref, buf_ref, sem, *, tm):
    step = pl.program_id(0)
    slot = step & 1

    def gather(ids_ref, dst_slot):
        for k in range(2):
            _gather_rows(ys_ref, ids_ref, 1, k * tm, buf_ref.at[dst_slot, k], sem.at[dst_slot], (0, 1))

    @pl.when(step == 0)
    def _():
        gather(pos0_ref, 0)

    for k in range(2):
        pltpu.make_async_copy(ys_ref.at[pl.ds(0, tm // SUBLANES)], buf_ref.at[slot, k], sem.at[slot]).wait()

    route = route_ref[...]
    w1, w2 = route[:, 2:3], route[:, 3:4]
    n_groups = tm // SUBLANES

    def combine(issue_next):
        n_chunks = D_MODEL // MXU_DIM
        per_chunk = 2 * n_groups // n_chunks
        for c in range(n_chunks):
            if issue_next:
                for q in range(c * per_chunk, (c + 1) * per_chunk):
                    k, g = q // n_groups, q % n_groups
                    _gather_group(ys_ref, posn_ref, 1, k * tm, buf_ref.at[1 - slot, k], sem.at[1 - slot], (0, 1), g)
            cs = slice(c * MXU_DIM, (c + 1) * MXU_DIM)
            f = (w1 * buf_ref[slot, 0, :, :, cs].reshape(tm, MXU_DIM)
                 + w2 * buf_ref[slot, 1, :, :, cs].reshape(tm, MXU_DIM))
            o_ref[:, cs] = DEEPNORM_ALPHA * x1_ref[:, cs] + g2_ref[:, cs] * f

    @pl.when(step + 1 < pl.num_programs(0))
    def _():
        combine(True)

    @pl.when(step + 1 == pl.num_programs(0))
    def _():
        combine(False)

    o_ref[...] = _ln(o_ref[...]) * l2g_ref[...] + l2b_ref[...]


def _final(pos_tiles, first_tile, x1, mod, per_row, route, ln2_g, ln2_b, ys):
    nb, l, _ = x1.shape
    tm = min(TM, l)
    nt = l // tm
    n_steps = nb * nt
    row = lambda wd: pl.BlockSpec((None, tm, wd), lambda s: (s // nt, s % nt, 0))
    if per_row:
        g2 = pl.BlockSpec((None, tm, D_MODEL), lambda s: (s // nt, s % nt, 5))
    else:
        g2 = pl.BlockSpec((None, 1, D_MODEL), lambda s: (s // nt, 0, 5))
    vec = pl.BlockSpec((1, D_MODEL), lambda s: (0, 0))
    return pl.pallas_call(
        functools.partial(_final_kernel, tm=tm),
        out_shape=jax.ShapeDtypeStruct((nb, l, D_MODEL), F32),
        grid=(n_steps,),
        in_specs=[*_index_specs(2 * tm, first_tile, n_steps), row(D_MODEL), g2, row(LANES), vec, vec,
                  pl.BlockSpec(memory_space=pl.ANY)],
        out_specs=row(D_MODEL),
        scratch_shapes=[pltpu.VMEM((2, 2, tm // SUBLANES, SUBLANES, D_MODEL), F32), pltpu.SemaphoreType.DMA((2,))],
        compiler_params=_cparams(("arbitrary",)),
        name="final",
    )(pos_tiles, pos_tiles, x1, mod, route, ln2_g, ln2_b, ys.reshape(-1, SUBLANES, D_MODEL))


def _rope_tables(pos):
    half = ROT_DIM // 2
    inv_freq = ROPE_THETA ** (-jnp.arange(half, dtype=jnp.float32) * 2.0 / ROT_DIM)
    ang = pos.astype(jnp.float32)[:, None] * inv_freq[None, :]
    cos, sin = jnp.cos(ang), jnp.sin(ang)
    n = pos.shape[0]
    one = jnp.ones((n, HEAD_DIM - ROT_DIM), F32)
    zero = jnp.zeros((n, HEAD_DIM - half), F32)
    c = jnp.concatenate([cos, cos, one], -1)
    a = jnp.concatenate([-sin, zero], -1)
    b = jnp.concatenate([jnp.zeros((n, half), F32), sin, jnp.zeros((n, HEAD_DIM - ROT_DIM), F32)], -1)
    return tuple(jnp.tile(t, (1, LANES // HEAD_DIM)) for t in (c, a, b))


def _s5_params(lam_re, lam_im, log_step, b_re, b_im, c_re, c_im):
    f32 = jnp.float32
    dt = jnp.exp(log_step.astype(f32))[:, None]
    lr, li = lam_re.astype(f32), lam_im.astype(f32)
    mag = jnp.exp(lr * dt)
    ar, ai = mag * jnp.cos(li * dt), mag * jnp.sin(li * dt)
    den = lr * lr + li * li
    cr = ((ar - 1.0) * lr + ai * li) / den
    ci = (ai * lr - (ar - 1.0) * li) / den
    br, bi = b_re.astype(f32), b_im.astype(f32)
    bbr = cr[..., None] * br - ci[..., None] * bi
    bbi = cr[..., None] * bi + ci[..., None] * br
    gpt = MXU_DIM // SSM_CH
    ch_group = jnp.arange(MXU_DIM) // SSM_CH
    st_group = jnp.arange(ST_PER_KT) // SSM_STATE

    def pack_b(m):
        rows = jnp.transpose(m.reshape(N_KT, gpt, SSM_STATE, SSM_CH), (0, 1, 3, 2)).reshape(N_KT, MXU_DIM, SSM_STATE)
        keep = ch_group[:, None] == st_group[None, :]
        return jnp.where(keep[None], jnp.tile(rows, (1, 1, gpt)), 0.0).astype(BF16)

    def pack_c(m):
        cols = jnp.transpose(m.astype(f32).reshape(N_KT, gpt, SSM_CH, SSM_STATE), (0, 3, 1, 2))
        cols = cols.reshape(N_KT, SSM_STATE, MXU_DIM)
        keep = st_group[:, None] == ch_group[None, :]
        return jnp.where(keep[None], jnp.tile(cols, (1, gpt, 1)), 0.0).astype(BF16)

    wts = (pack_b(bbr), pack_b(bbi), pack_c(c_re), pack_c(-c_im.astype(f32)))
    return wts, ar.reshape(1, N_STATE), ai.reshape(1, N_STATE)


def _scan_tables(ar, ai, nb):
    assert SUBLANES // nb == 2
    a2r, a2i = ar * ar - ai * ai, 2.0 * ar * ai
    z = jnp.zeros_like(ar)
    rep = lambda first, second: jnp.concatenate([jnp.tile(first, (nb, 1)), jnp.tile(second, (nb, 1))], 0)
    return rep(z, ar), rep(z, ai), rep(ar, a2r), rep(ai, a2i)


def kernel(x_prompt, x_sample, cache_k_win, cache_v_win, state_ssm_re, state_ssm_im, c_prompt, c_sample, ada_w, ada_b,
           w_in, attn_sinks, ssm_lambda_re, ssm_lambda_im, ssm_log_step, ssm_b_re, ssm_b_im, ssm_c_re, ssm_c_im, ssm_d,
           ssm_glu_w, ssm_glu_b, attn_norm_g, ssm_norm_g, w_out, ln1_g, ln1_b, router_group_w, router_group_b,
           router_expert_w, router_expert_b, exp_w_gate, exp_w_up, exp_w_down, ln2_g, ln2_b):
    assert ada_w.shape[0] == DEPTH
    bp, lp, _ = x_prompt.shape
    bs, ls, _ = x_sample.shape
    lsp = SAMPLE_PAD

    w_in_bf = w_in[0].astype(BF16)
    wr = jnp.concatenate([router_group_w[0], router_expert_w[0]], -1)
    wr = jnp.pad(wr, ((0, 0), (0, LANES - wr.shape[1]))).astype(BF16)
    br = jnp.pad(jnp.concatenate([router_group_b[0], router_expert_b[0]], -1), (0, LANES - N_EXPERT_GROUPS - N_EXPERTS))
    wpost = dict(glu_w=ssm_glu_w[0].astype(BF16), glu_b=ssm_glu_b[0][None], attn_g=attn_norm_g[0][None],
                 ssm_g=ssm_norm_g[0][None], w_out=w_out[0].astype(BF16), ln1_g=ln1_g[0][None], ln1_b=ln1_b[0][None],
                 wr=wr, br=br[None])
    s5w, ar, ai = _s5_params(ssm_lambda_re[0], ssm_lambda_im[0], ssm_log_step[0], ssm_b_re[0], ssm_b_im[0],
                             ssm_c_re[0], ssm_c_im[0])
    d_skip = ssm_d[0][None]
    sinks = attn_sinks[0]

    n_c = bp + bs
    n_cp = -(-n_c // SUBLANES) * SUBLANES
    c_all = jnp.pad(jnp.concatenate([c_prompt, c_sample], 0), ((0, n_cp - n_c), (0, 0)))
    mod = _ada(c_all, ada_w[0], ada_b[0][None])
    mod_p = mod[:bp][:, None, :]
    mod_s = jnp.repeat(mod[bp:n_c], lsp, axis=0)[None]

    xs_pad = jnp.pad(x_sample, ((0, 0), (0, lsp - ls), (0, 0))).reshape(1, bs * lsp, D_MODEL)

    rope_p = _rope_tables(jnp.arange(lp))
    pos_s = PAST_LEN + jnp.minimum(jnp.arange(lsp), ls - 1)
    rope_s = tuple(jnp.tile(t, (bs, 1)) for t in _rope_tables(pos_s))
    q_p, k_p, v_p, u_p = _inproj(x_prompt, mod_p, w_in_bf, rope_p, False)
    q_s, k_s, v_s, u_s = _inproj(xs_pad, mod_s, w_in_bf, rope_s, True)

    attn_p = _attn_prompt(sinks, q_p, k_p, v_p)
    k_s3 = k_s.reshape(bs, lsp, D_KV)
    v_s3 = v_s.reshape(bs, lsp, D_KV)
    ck = cache_k_win[0].reshape(bs, WINDOW, D_KV)
    cv = cache_v_win[0].reshape(bs, WINDOW, D_KV)
    padk = lambda new: jnp.pad(new, ((0, 0), (0, WINDOW - lsp), (0, 0)))
    attn_s = _attn_sample(sinks, q_s.reshape(bs, lsp, D_ATTN), ck, padk(k_s3), cv, padk(v_s3))

    tabs = _scan_tables(ar, ai, bp)
    zero_carry = jnp.zeros((SUBLANES, N_STATE), F32)
    y_p, hr_p, hi_p = _s5_prompt(u_p, s5w, tabs, d_skip, zero_carry, zero_carry)
    u_tb = jnp.transpose(u_s.reshape(bs, lsp, D_SSM)[:, :ls], (1, 0, 2))
    y_tb, hr_s, hi_s = _s5_sample(u_tb, s5w, ar, ai, d_skip, state_ssm_re[0].reshape(bs, N_STATE),
                                  state_ssm_im[0].reshape(bs, N_STATE))
    y_s = jnp.pad(jnp.transpose(y_tb, (1, 0, 2)), ((0, 0), (0, lsp - ls), (0, 0))).reshape(1, bs * lsp, D_SSM)

    cnt0 = jnp.zeros((SUBLANES, LANES), F32)
    n_p, n_s = bp * lp, bs * lsp
    n_tok = n_p + n_s
    x1_s, h2_s, route_s, cnt_s = _post(attn_s.reshape(1, bs * lsp, D_ATTN), y_s, xs_pad, mod_s, True, wpost, cnt0)
    x1_p, h2, route_p, cnt = _post(attn_p, y_p, x_prompt, mod_p, False, wpost, cnt_s, tail=h2_s)

    n_tiles = -(-(2 * n_tok + N_EXPERTS * (TM_MOE - 1)) // TM_MOE) + MOE_LOOKAHEAD
    route = jnp.concatenate([route_p.reshape(n_p, LANES), route_s.reshape(n_s, LANES)], 0)
    counts = cnt[0, ROUTE_LANE0:ROUTE_LANE0 + N_EXPERTS].astype(jnp.int32)
    padded = ((counts + TM_MOE - 1) // TM_MOE) * TM_MOE
    ends = jnp.cumsum(padded)
    offs = ends - padded
    col = lambda c: route[:, c].astype(jnp.int32)
    expert_ids = jnp.arange(N_EXPERTS, dtype=jnp.int32)
    seg_start = lambda e: jnp.sum(jnp.where(e[:, None] == expert_ids[None, :], offs[None, :], 0), axis=1)
    pos_k = [seg_start(col(k)) + col(4 + k) for k in range(2)]
    n_valid = ends[-1] // TM_MOE
    tile_row = jnp.minimum(jnp.arange(n_tiles), n_valid - 1) * TM_MOE
    tile_expert = jnp.sum((ends[None, :] <= tile_row[:, None]).astype(jnp.int32), axis=1)
    first_tile = jnp.concatenate([jnp.ones((1,), jnp.int32), (tile_expert[1:] != tile_expert[:-1]).astype(jnp.int32)])
    first_tile = first_tile * (jnp.arange(n_tiles) < n_valid)
    nonempty = padded > 0
    later =jnp.where(nonempty[None, :] & (expert_ids[None, :] > expert_ids[:, None]), expert_ids[None, :], N_EXPERTS)
    next_expert = jnp.min(later, axis=1)
    next_expert = jnp.where(next_expert < N_EXPERTS, next_expert, -1).astype(jnp.int32)
    weight_slot = ((jnp.cumsum(nonempty.astype(jnp.int32)) - 1) & 1).astype(jnp.int32)
    tile_meta = (tile_expert, first_tile.astype(jnp.int32), next_expert[tile_expert], weight_slot[tile_expert])
    token_ids = jnp.arange(n_tok, dtype=jnp.int32)
    row_token = (jnp.arange(n_tiles * TM_MOE, dtype=jnp.int32) % n_tok).at[jnp.concatenate(pos_k)].set(
        jnp.concatenate([token_ids, token_ids]))

    ys = _moe(tile_meta, n_valid.reshape(1).astype(jnp.int32), row_token, h2, exp_w_gate[0], exp_w_up[0],
              exp_w_down[0])
    pos_tiles = jnp.concatenate([p.reshape(n_tok // TM, 1, TM) for p in pos_k], axis=2)
    y_prompt = _final(pos_tiles, 0, x1_p, mod_p, False, route_p, ln2_g[0][None], ln2_b[0][None], ys)
    y_samp = _final(pos_tiles, n_p // TM, x1_s, mod_s, True, route_s, ln2_g[0][None], ln2_b[0][None], ys)
    y_sample = y_samp.reshape(bs, lsp, D_MODEL)[:, :ls]

    kv5 = lambda a, n: a.reshape(1, a.shape[0], n, N_KV_HEADS, HEAD_DIM)
    k_win_p = kv5(k_p[:, lp - WINDOW:], WINDOW)
    v_win_p = kv5(v_p[:, lp - WINDOW:], WINDOW)
    k_win_s = kv5(jnp.concatenate([ck[:, ls:], k_s3[:, :ls]], 1), WINDOW)
    v_win_s = kv5(jnp.concatenate([cv[:, ls:], v_s3[:, :ls]], 1), WINDOW)
    st = lambda a: a.reshape(1, a.shape[0], N_SSM_GROUPS, SSM_STATE)
    return (y_prompt, y_sample, k_win_p, v_win_p, st(hr_p[bp:2 * bp]), st(hi_p[bp:2 * bp]),
            k_win_s, v_win_s, st(hr_s), st(hi_s))
```

```python
import functools

import jax
import jax.numpy as jnp
from jax import lax
from jax.experimental import pallas as pl
from jax.experimental.pallas import tpu as pltpu

F32 = jnp.float32
BF16 = jnp.bfloat16

D_MODEL = 2048
D_ATTN = 1024
D_SSM = 1024
HEAD_DIM = 64
N_HEADS = 16
N_KV_HEADS = 4
REP = 4
D_KV = 256
ROT_DIM = 16
ROPE_THETA = 500000.0
WINDOW = 128
SSM_CH = 16
N_SSM_GROUPS = 64
SSM_STATE = 64
N_STATE = N_SSM_GROUPS * SSM_STATE
PROJ_COLS = D_ATTN + 2 * D_KV + D_SSM
N_EXPERT_GROUPS = 4
EXPERTS_PER_GROUP = 8
N_EXPERTS = 32
D_EXPERT = 512
DEPTH = 1
DEEPNORM_ALPHA = (2.0 * DEPTH) ** 0.25
LN_EPS = 1e-5
PAST_LEN = 16384

LANES = 128
SUBLANES = 8
MXU_DIM = 256
TM = 256
TM_MOE = 256
TM_INPROJ = 512
S5_TT = 128
SAMPLE_PAD = 8
ROUTE_LANE0 = N_EXPERT_GROUPS
VMEM_LIMIT = 56 * 1024 * 1024


def _cparams(sem):
    return pltpu.CompilerParams(dimension_semantics=sem, vmem_limit_bytes=VMEM_LIMIT)


def _ln(x):
    mu = jnp.mean(x, axis=-1, keepdims=True)
    xc = x - mu
    var = jnp.mean(xc * xc, axis=-1, keepdims=True)
    return xc * lax.rsqrt(var + LN_EPS)


def _rms(x):
    return x * lax.rsqrt(jnp.mean(x * x, axis=-1, keepdims=True) + LN_EPS)


def _ada_kernel(c_ref, w_ref, b_ref, o_ref):
    c = c_ref[...]
    s = c * jax.nn.sigmoid(c)
    o_ref[...] = jnp.dot(s.astype(BF16), w_ref[...].astype(BF16), preferred_element_type=F32) + b_ref[...]


def _ada(c_all, ada_w, ada_b):
    n, tn = c_all.shape[0], 1024
    return pl.pallas_call(
        _ada_kernel,
        out_shape=jax.ShapeDtypeStruct((n, 6 * D_MODEL), F32),
        grid=(6 * D_MODEL // tn,),
        in_specs=[pl.BlockSpec((n, D_MODEL), lambda j: (0, 0)),
                  pl.BlockSpec((D_MODEL, tn), lambda j: (0, j)),
                  pl.BlockSpec((1, tn), lambda j: (0, j))],
        out_specs=pl.BlockSpec((n, tn), lambda j: (0, j)),
        compiler_params=_cparams(("arbitrary",)),
        name="ada",
    )(c_all, ada_w, ada_b)


def _mod_spec(per_row, tm, chunk):
    if per_row:
        return pl.BlockSpec((None, tm, D_MODEL), lambda b, i: (b, i, chunk))
    return pl.BlockSpec((None, 1, D_MODEL), lambda b, i: (b, 0, chunk))


def _inproj_kernel(x_ref, sh_ref, sc_ref, w_ref, rc_ref, ra_ref, rb_ref, q_ref, k_ref, v_ref, u_ref):
    h = _ln(x_ref[...]) * (1.0 + sc_ref[...]) + sh_ref[...]
    proj = jnp.dot(h.astype(BF16), w_ref[...], preferred_element_type=F32)
    rc, ra, rb = rc_ref[...], ra_ref[...], rb_ref[...]

    def rope(t):
        return t * rc + pltpu.roll(t, LANES - ROT_DIM // 2, 1) * ra + pltpu.roll(t, ROT_DIM // 2, 1) * rb

    for j in range(D_ATTN // LANES):
        q_ref[:, j * LANES:(j + 1) * LANES] = (rope(proj[:, j * LANES:(j + 1) * LANES]) * HEAD_DIM ** -0.5).astype(BF16)
    for j in range(D_KV // LANES):
        c0 = D_ATTN + j * LANES
        k_ref[:, j * LANES:(j + 1) * LANES] = rope(proj[:, c0:c0 + LANES])
    v_ref[...] = proj[:, D_ATTN + D_KV:D_ATTN + 2 * D_KV]
    u_ref[...] = proj[:, D_ATTN + 2 * D_KV:]


def _inproj(x, mod, w_in_bf, rope_tabs, per_row):
    nb, l, _ = x.shape
    tm = min(TM_INPROJ, l)
    row = lambda w: pl.BlockSpec((None, tm, w), lambda b, i: (b, i, 0))
    tab = pl.BlockSpec((tm, LANES), lambda b, i: (i, 0))
    return pl.pallas_call(
        _inproj_kernel,
        out_shape=(jax.ShapeDtypeStruct((nb, l, D_ATTN), BF16), jax.ShapeDtypeStruct((nb, l, D_KV), F32),
                   jax.ShapeDtypeStruct((nb, l, D_KV), F32), jax.ShapeDtypeStruct((nb, l, D_SSM), F32)),
        grid=(nb, l // tm),
        in_specs=[row(D_MODEL), _mod_spec(per_row, tm, 0), _mod_spec(per_row, tm, 1),
                  pl.BlockSpec((D_MODEL, PROJ_COLS), lambda b, i: (0, 0)), tab, tab, tab],
        out_specs=(row(D_ATTN), row(D_KV), row(D_KV), row(D_SSM)),
        compiler_params=_cparams(("arbitrary", "arbitrary")),
        name="inproj",
    )(x, mod, mod, w_in_bf, *rope_tabs)


ATTN_SAMPLE_BATCH = 8


def _attn_kernel(sink_ref, q_ref, kp_ref, kc_ref, vp_ref, vc_ref, o_ref, *, lq):
    m_rows = REP * lq
    ii = lax.broadcasted_iota(jnp.int32, (m_rows, WINDOW), 0) & (lq - 1)
    jj = lax.broadcasted_iota(jnp.int32, (m_rows, WINDOW), 1)
    from_prev = jj > ii
    dead = jj > ii + jnp.where(pl.program_id(1) > 0, WINDOW, 0)
    rr = lax.broadcasted_iota(jnp.int32, (m_rows, 1), 0)
    dn = (((1,), (1,)), ((), ()))
    for bi in range(q_ref.shape[0]):
        for g in range(N_KV_HEADS):
            qg = q_ref[bi, :, g * REP * HEAD_DIM:(g + 1) * REP * HEAD_DIM].astype(F32)
            qs = jnp.concatenate([qg[:, r * HEAD_DIM:(r + 1) * HEAD_DIM] for r in range(REP)], axis=0).astype(BF16)
            ks = slice(g * HEAD_DIM, (g + 1) * HEAD_DIM)
            s_p = lax.dot_general(qs, kp_ref[bi, :, ks].astype(BF16), dn, preferred_element_type=F32)
            s_c = lax.dot_general(qs, kc_ref[bi, :, ks].astype(BF16), dn, preferred_element_type=F32)
            s = jnp.where(dead, -jnp.inf, jnp.where(from_prev, s_p, s_c))
            sink = jnp.zeros((m_rows, 1), F32)
            for r in range(REP):
                sink = jnp.where((rr >= r * lq) & (rr < (r + 1) * lq), sink_ref[g * REP + r], sink)
            m = jnp.maximum(jnp.max(s, axis=-1, keepdims=True), sink)
            p = jnp.exp(s - m)
            p = p / (jnp.sum(p, axis=-1, keepdims=True) + jnp.exp(sink - m))
            o = (jnp.dot(jnp.where(from_prev, p, 0.0).astype(BF16), vp_ref[bi, :, ks].astype(BF16),
                         preferred_element_type=F32)
                 + jnp.dot(jnp.where(from_prev, 0.0, p).astype(BF16), vc_ref[bi, :, ks].astype(BF16),
                           preferred_element_type=F32))
            for r in range(REP):
                h = g * REP + r
                o_ref[bi, :, h * HEAD_DIM:(h + 1) * HEAD_DIM] = o[r * lq:(r + 1) * lq]


def _attn_prompt(sinks, q, k, v):
    nb, l, _ = q.shape
    nblk = l // WINDOW
    cur = lambda w: pl.BlockSpec((1, WINDOW, w), lambda b, n: (b, n, 0))
    prev = lambda w: pl.BlockSpec((1, WINDOW, w), lambda b, n: (b, jnp.maximum(n - 1, 0), 0))
    return pl.pallas_call(
        functools.partial(_attn_kernel, lq=WINDOW),
        out_shape=jax.ShapeDtypeStruct((nb, l, D_ATTN), F32),
        grid=(nb, nblk),
        in_specs=[pl.BlockSpec(memory_space=pltpu.SMEM), cur(D_ATTN), prev(D_KV), cur(D_KV), prev(D_KV), cur(D_KV)],
        out_specs=cur(D_ATTN),
        compiler_params=_cparams(("arbitrary", "arbitrary")),
        name="attn_prompt",
    )(sinks, q, k, k, v, v)


ATTN_SAMPLE_JOINT = 4


def _attn_sample_kernel(sink_ref, q_ref, kp_ref, kc_ref, vp_ref, vc_ref, o_ref, *, lq):
    gb = ATTN_SAMPLE_JOINT
    m_b = REP * lq
    m, n = gb * m_b, gb * WINDOW
    row = lax.broadcasted_iota(jnp.int32, (m, n), 0)
    col = lax.broadcasted_iota(jnp.int32, (m, n), 1)
    same = (row // m_b) == (col // WINDOW)
    from_prev = (col & (WINDOW - 1)) > (row & (lq - 1))
    rep_of_row = (lax.broadcasted_iota(jnp.int32, (m, 1), 0) // lq) % REP
    dn = (((1,), (1,)), ((), ()))
    for blk in range(q_ref.shape[0] // gb):
        bs = range(blk * gb, (blk + 1) * gb)
        for g in range(N_KV_HEADS):
            qg = [q_ref[b, :, g * REP * HEAD_DIM:(g + 1) * REP * HEAD_DIM].astype(F32) for b in bs]
            qs = jnp.concatenate([qb[:, r * HEAD_DIM:(r + 1) * HEAD_DIM] for qb in qg for r in range(REP)],
                                 axis=0).astype(BF16)
            ks = slice(g * HEAD_DIM, (g + 1) * HEAD_DIM)
            stack = lambda ref: jnp.concatenate([ref[b, :, ks] for b in bs], axis=0).astype(BF16)
            s_p = lax.dot_general(qs, stack(kp_ref), dn, preferred_element_type=F32)
            s_c = lax.dot_general(qs, stack(kc_ref), dn, preferred_element_type=F32)
            s = jnp.where(same, jnp.where(from_prev, s_p, s_c), -jnp.inf)
            sink = jnp.zeros((m, 1), F32)
            for r in range(REP):
                sink = jnp.where(rep_of_row == r, sink_ref[g * REP + r], sink)
            mx = jnp.maximum(jnp.max(s, axis=-1, keepdims=True), sink)
            p = jnp.exp(s - mx)
            p = p / (jnp.sum(p, axis=-1, keepdims=True) + jnp.exp(sink - mx))
            o = (jnp.dot(jnp.where(from_prev, p, 0.0).astype(BF16), stack(vp_ref), preferred_element_type=F32)
                 + jnp.dot(jnp.where(from_prev, 0.0, p).astype(BF16), stack(vc_ref), preferred_element_type=F32))
            for bi, b in enumerate(bs):
                for r in range(REP):
                    h = g * REP + r
                    o_ref[b, :, h * HEAD_DIM:(h + 1) * HEAD_DIM] = o[bi * m_b + r * lq:bi * m_b + (r + 1) * lq]


def _attn_sample(sinks, q, k_cache, k_new, v_cache, v_new):
    nb, lq, _ = q.shape
    nbb = ATTN_SAMPLE_BATCH
    kblk = pl.BlockSpec((nbb, WINDOW, D_KV), lambda b: (b, 0, 0))
    return pl.pallas_call(
        functools.partial(_attn_sample_kernel, lq=lq),
        out_shape=jax.ShapeDtypeStruct((nb, lq, D_ATTN), F32),
        grid=(nb // nbb,),
        in_specs=[pl.BlockSpec(memory_space=pltpu.SMEM), pl.BlockSpec((nbb, lq, D_ATTN), lambda b: (b, 0, 0)),
                  kblk, kblk, kblk, kblk],
        out_specs=pl.BlockSpec((nbb, lq, D_ATTN), lambda b: (b, 0, 0)),
        compiler_params=_cparams(("arbitrary",)),
        name="attn_sample",
    )(sinks, q, k_cache, k_new, v_cache, v_new)


N_KT = D_SSM // MXU_DIM
ST_PER_KT = N_STATE // N_KT
N_SLAB = D_SSM // LANES


def _cproj(hr_ref, hi_ref, cre_ref, cimn_ref, kt):
    cs = slice(kt * ST_PER_KT, (kt + 1) * ST_PER_KT)
    return (jnp.dot(hr_ref[:, cs].astype(BF16), cre_ref[kt], preferred_element_type=F32)
            + jnp.dot(hi_ref[:, cs].astype(BF16), cimn_ref[kt], preferred_element_type=F32))


def _s5_prompt_kernel(u_ref, bre_ref, bim_ref, cre_ref, cimn_ref, a1r_ref, a1i_ref, par_ref, pai_ref, d_ref,
                      h0r_ref, h0i_ref, y_ref, cr_ref, ci_ref, il_ref, xr_ref, xi_ref, *, nb, tt):
    rows = nb * tt

    @pl.when(pl.program_id(0) == 0)
    def _():
        cr_ref[...] = h0r_ref[...]
        ci_ref[...] = h0i_ref[...]

    for b in range(nb):
        for j in range(N_SLAB):
            il_ref[j, pl.ds(b, tt, stride=nb), :] = u_ref[b, :, j * LANES:(j + 1) * LANES]
    for kt in range(N_KT):
        ub = jnp.concatenate([il_ref[2 * kt], il_ref[2 * kt + 1]], axis=1).astype(BF16)
        cs = slice(kt * ST_PER_KT, (kt + 1) * ST_PER_KT)
        xr_ref[:, cs] = jnp.dot(ub, bre_ref[kt], preferred_element_type=F32)
        xi_ref[:, cs] = jnp.dot(ub, bim_ref[kt], preferred_element_type=F32)

    half = lax.broadcasted_iota(jnp.int32, (SUBLANES, LANES), 0) < nb

    def step(i, carry):
        r0 = pl.multiple_of(i * SUBLANES, SUBLANES)
        for j in range(N_STATE // LANES):
            ls = slice(j * LANES, (j + 1) * LANES)
            x_r = xr_ref[pl.ds(r0, SUBLANES), ls]
            x_i = xi_ref[pl.ds(r0, SUBLANES), ls]
            s_r = pltpu.roll(x_r, nb, 0)
            s_i = pltpu.roll(x_i, nb, 0)
            a1r, a1i = a1r_ref[:, ls], a1i_ref[:, ls]
            c_r, c_i = cr_ref[:, ls], ci_ref[:, ls]
            par, pai = par_ref[:, ls], pai_ref[:, ls]
            h_r = x_r + (a1r * s_r - a1i * s_i) + (par * c_r - pai * c_i)
            h_i = x_i + (a1r * s_i + a1i * s_r) + (par * c_i + pai * c_r)
            xr_ref[pl.ds(r0, SUBLANES), ls] = h_r
            xi_ref[pl.ds(r0, SUBLANES), ls] = h_i
            cr_ref[:, ls] = jnp.where(half, pltpu.roll(h_r, nb, 0), h_r)
            ci_ref[:, ls] = jnp.where(half, pltpu.roll(h_i, nb, 0), h_i)
        return carry

    lax.fori_loop(0, rows // SUBLANES, step, 0)

    for kt in range(N_KT):
        y = _cproj(xr_ref, xi_ref, cre_ref, cimn_ref, kt)
        il_ref[2 * kt] = y[:, :LANES]
        il_ref[2 * kt + 1] = y[:, LANES:]
    for b in range(nb):
        for j in range(N_SLAB):
            ls = slice(j * LANES, (j + 1) * LANES)
            y_ref[b, :, ls] = il_ref[j, pl.ds(b, tt, stride=nb), :] + d_ref[:, ls] * u_ref[b, :, ls]


def _s5_prompt(u, wts, tabs, d_skip, h0r, h0i):
    nb, l, _ = u.shape
    tt = S5_TT
    rows = nb * tt
    full = lambda a: pl.BlockSpec(a.shape, lambda i: (0,) * a.ndim)
    blk = pl.BlockSpec((nb, tt, D_SSM), lambda i: (0, i, 0))
    carry = jax.ShapeDtypeStruct((SUBLANES, N_STATE), F32)
    return pl.pallas_call(
        functools.partial(_s5_prompt_kernel, nb=nb, tt=tt),
        out_shape=(jax.ShapeDtypeStruct((nb, l, D_SSM), F32), carry, carry),
        grid=(l // tt,),
        in_specs=[blk] + [full(a) for a in (*wts, *tabs, d_skip, h0r, h0i)],
        out_specs=(blk, pl.BlockSpec((SUBLANES, N_STATE), lambda i: (0, 0)),
                   pl.BlockSpec((SUBLANES, N_STATE), lambda i: (0, 0))),
        scratch_shapes=[pltpu.VMEM((N_SLAB, rows, LANES), F32), pltpu.VMEM((rows, N_STATE), F32),
                        pltpu.VMEM((rows, N_STATE), F32)],
        compiler_params=_cparams(("arbitrary",)),
        name="s5_prompt",
    )(u, *wts, *tabs, d_skip, h0r, h0i)


def _s5_sample_kernel(u_ref, bre_ref, bim_ref, cre_ref, cimn_ref, ar_ref, ai_ref, d_ref, h0r_ref, h0i_ref,
                      y_ref, sr_ref, si_ref, xr_ref, xi_ref, *, nt):
    sr_ref[...] = h0r_ref[...]
    si_ref[...] = h0i_ref[...]
    for t in range(nt):
        for kt in range(N_KT):
            ub = u_ref[t, :, kt * MXU_DIM:(kt + 1) * MXU_DIM].astype(BF16)
            cs = slice(kt * ST_PER_KT, (kt + 1) * ST_PER_KT)
            xr_ref[:, cs] = jnp.dot(ub, bre_ref[kt], preferred_element_type=F32)
            xi_ref[:, cs] = jnp.dot(ub, bim_ref[kt], preferred_element_type=F32)
        ar, ai = ar_ref[...], ai_ref[...]
        s_r, s_i = sr_ref[...], si_ref[...]
        sr_ref[...] = xr_ref[...] + (ar * s_r - ai * s_i)
        si_ref[...] = xi_ref[...] + (ar * s_i + ai * s_r)
        for kt in range(N_KT):
            ys = slice(kt * MXU_DIM, (kt + 1) * MXU_DIM)
            y_ref[t, :, ys] = _cproj(sr_ref, si_ref, cre_ref, cimn_ref, kt) + d_ref[:, ys] * u_ref[t, :, ys]


def _s5_sample(u_tb, wts, ar, ai, d_skip, h0r, h0i):
    nt, nb, _ = u_tb.shape
    st = jax.ShapeDtypeStruct((nb, N_STATE), F32)
    args = (u_tb, *wts, ar, ai, d_skip, h0r, h0i)
    full = lambda a: pl.BlockSpec(a.shape, lambda i: (0,) * a.ndim)
    return pl.pallas_call(
        functools.partial(_s5_sample_kernel, nt=nt),
        out_shape=(jax.ShapeDtypeStruct((nt, nb, D_SSM), F32), st, st),
        grid=(1,),
        in_specs=[full(a) for a in args],
        out_specs=(pl.BlockSpec((nt, nb, D_SSM), lambda i: (0, 0, 0)), pl.BlockSpec((nb, N_STATE), lambda i: (0, 0)),
                   pl.BlockSpec((nb, N_STATE), lambda i: (0, 0))),
        scratch_shapes=[pltpu.VMEM((nb, N_STATE), F32), pltpu.VMEM((nb, N_STATE), F32)],
        compiler_params=_cparams(("arbitrary",)),
        name="s5_sample",
    )(*args)


def _gather_group(src_ref, ids_ref, id_stride, id_offset, dst_ref, sem, priorities, g):
    for j in range(SUBLANES):
        t = ids_ref[0, id_stride * (g * SUBLANES + j) + id_offset]
        pltpu.make_async_copy(src_ref.at[t >> 3, pl.ds(t & (SUBLANES - 1), 1)], dst_ref.at[g, pl.ds(j, 1)],
                              sem).start(priority=priorities[j % len(priorities)])


def _gather_rows(src_ref, ids_ref, id_stride, id_offset, dst_ref, sem, priorities):
    def body(g, c):
        _gather_group(src_ref, ids_ref, id_stride, id_offset, dst_ref, sem, priorities, g)
        return c
    lax.fori_loop(0, dst_ref.shape[0], body, 0)


def _index_specs(n_cols, first_tile, n_steps, depth=1):
    def spec(fn):
        return pl.BlockSpec((None, 1, n_cols), fn, memory_space=pltpu.SMEM)

    head = [spec(lambda s, *_, k=k: (first_tile + min(k, n_steps - 1), 0, 0)) for k in range(depth)]
    return (*head, spec(lambda s, *_: (first_tile + jnp.minimum(s + depth, n_steps - 1), 0, 0)))


N_POST_INPUTS = 17


def _post_kernel(*refs, has_tail):
    ins, outs = refs[:N_POST_INPUTS], refs[N_POST_INPUTS + int(has_tail):]
    if not has_tail:
        _post_body(*ins, *outs)
        return
    tail_ref, h2_ref = refs[N_POST_INPUTS], outs[1]
    last = pl.num_programs(0) - 1

    @pl.when(pl.program_id(0) == last)
    def _():
        h2_ref[...] = tail_ref[...]

    @pl.when(pl.program_id(0) < last)
    def _():
        _post_body(*ins, *outs)


def _post_body(attn_ref, yssm_ref, x_ref, g1_ref, sh2_ref, sc2_ref, gluw_ref, glub_ref, ga_ref, gs_ref, wout_ref,
               l1g_ref, l1b_ref, wr_ref, br_ref, tri_ref, cnt0_ref, x1_ref, h2_ref, route_ref, cnt_ref):
    tm = x_ref.shape[0]

    @pl.when(pl.program_id(0) == 0)
    def _():
        cnt_ref[...] = cnt0_ref[...]

    z = jax.nn.gelu(yssm_ref[...])
    ssm = z * jax.nn.sigmoid(jnp.dot(z.astype(BF16), gluw_ref[...], preferred_element_type=F32) + glub_ref[...])
    mixed_a = (_rms(attn_ref[...]) * ga_ref[...]).astype(BF16)
    mixed_s = (_rms(ssm) * gs_ref[...]).astype(BF16)
    o = (jnp.dot(mixed_a, wout_ref[:D_ATTN, :], preferred_element_type=F32)
         + jnp.dot(mixed_s, wout_ref[D_ATTN:, :], preferred_element_type=F32))
    x1 = _ln(DEEPNORM_ALPHA * x_ref[...] + g1_ref[...] * o) * l1g_ref[...] + l1b_ref[...]
    x1_ref[...] = x1
    h2 = _ln(x1) * (1.0 + sc2_ref[...]) + sh2_ref[...]
    h2_ref[...] = h2
    logits = jnp.dot(h2.astype(BF16), wr_ref[...], preferred_element_type=F32) + br_ref[...]

    lane = lax.broadcasted_iota(jnp.int32, (tm, LANES), 1).astype(F32)
    big = float(4 * LANES)
    neg = -jnp.inf
    gl = jnp.where(lane < N_EXPERT_GROUPS, logits, neg)
    gp = jnp.exp(gl - jnp.max(gl, axis=-1, keepdims=True))
    gp = gp / jnp.sum(gp, axis=-1, keepdims=True)
    g_val = jnp.max(gp, axis=-1, keepdims=True)
    g_idx = jnp.min(jnp.where(gp == g_val, lane, big), axis=-1, keepdims=True)
    lo = ROUTE_LANE0 + EXPERTS_PER_GROUP * g_idx
    emask = (lane >= lo) & (lane < lo + EXPERTS_PER_GROUP)
    el = jnp.where(emask, logits, neg)
    ep = jnp.exp(el - jnp.max(el, axis=-1, keepdims=True))
    ep = jnp.where(emask, ep / jnp.sum(ep, axis=-1, keepdims=True), -1.0)
    v1 = jnp.max(ep, axis=-1, keepdims=True)
    i1 = jnp.min(jnp.where(ep == v1, lane, big), axis=-1, keepdims=True)
    ep2 = jnp.where(lane == i1, -1.0, ep)
    v2 = jnp.max(ep2, axis=-1, keepdims=True)
    i2 = jnp.min(jnp.where(ep2 == v2, lane, big), axis=-1, keepdims=True)
    vs = v1 + v2
    w1 = g_val * (v1 / vs)
    w2 = g_val * (v2 / vs)
    hit = jnp.where((lane == i1) | (lane == i2), 1.0, 0.0)
    before = jnp.dot(tri_ref[...], hit.astype(BF16), preferred_element_type=F32) + cnt_ref[0:1, :]
    r1 = jnp.sum(jnp.where(lane == i1, before, 0.0), axis=-1, keepdims=True)
    r2 = jnp.sum(jnp.where(lane == i2, before, 0.0), axis=-1, keepdims=True)
    cnt_ref[...] = cnt_ref[...] + jnp.sum(hit, axis=0, keepdims=True)
    e1 = i1 - ROUTE_LANE0
    e2 = i2 - ROUTE_LANE0
    route = jnp.zeros((tm, LANES), F32)
    for n, val in enumerate((e1, e2, w1, w2, r1, r2)):
        route = jnp.where(lane == n, val, route)
    route_ref[...] = route


def _post(attn, yssm, x, mod, per_row, w, cnt0, tail=None):
    nb, l, _ = x.shape
    tm = min(TM, l)
    nt = l // tm
    n_body = nb * nt
    has_tail = tail is not None
    if has_tail:
        assert tail.shape == (tm, D_MODEL)

    def bi(s):
        s = jnp.minimum(s, n_body - 1)
        return s // nt, s % nt

    row = lambda wd: pl.BlockSpec((None, tm, wd), lambda s: (*bi(s), 0))
    if per_row:
        mspec = lambda chunk: pl.BlockSpec((None, tm, D_MODEL), lambda s: (*bi(s), chunk))
    else:
        mspec = lambda chunk: pl.BlockSpec((None, 1, D_MODEL), lambda s: (bi(s)[0], 0, chunk))
    full = lambda a: pl.BlockSpec(a.shape, lambda s: (0,) * a.ndim)
    tri = jnp.tril(jnp.ones((tm, tm), F32), -1).astype(BF16)
    consts = (w["glu_w"], w["glu_b"], w["attn_g"], w["ssm_g"], w["w_out"], w["ln1_g"], w["ln1_b"], w["wr"], w["br"],
              tri, cnt0) + ((tail,) if has_tail else ())
    assert 6 + len(consts) == N_POST_INPUTS + int(has_tail)
    n_steps = n_body + int(has_tail)
    return pl.pallas_call(
        functools.partial(_post_kernel, has_tail=has_tail),
        out_shape=(jax.ShapeDtypeStruct((nb, l, D_MODEL), F32),
                   jax.ShapeDtypeStruct((n_steps * tm, D_MODEL), F32),
                   jax.ShapeDtypeStruct((nb, l, LANES), F32), jax.ShapeDtypeStruct((SUBLANES, LANES), F32)),
        grid=(n_steps,),
        in_specs=[row(D_ATTN), row(D_SSM), row(D_MODEL), mspec(2), mspec(3), mspec(4)] + [full(a) for a in consts],
        out_specs=(row(D_MODEL), pl.BlockSpec((tm, D_MODEL), lambda s: (s, 0)),
                   row(LANES), pl.BlockSpec((SUBLANES, LANES), lambda s: (0, 0))),
        compiler_params=_cparams(("arbitrary",)),
        name="post",
    )(attn, yssm, x, mod, mod, mod, *consts)


MOE_LOOKAHEAD = 2
MOE_SLOTS = MOE_LOOKAHEAD + 1


def _moe_kernel(te_ref, nv_ref, first_ref, nxt_ref, par_ref, rt0_ref, rt1_ref, rtn_ref, h2_ref, wg_hbm, wu_hbm, wd_hbm,
                y_ref, buf_ref, wgf_ref, wuf_ref, wdf_ref, sem, wsem):
    i = pl.program_id(0)
    nv = nv_ref[0]
    slot = lax.rem(i, MOE_SLOTS)
    w_pairs = ((wg_hbm, wgf_ref), (wu_hbm, wuf_ref), (wd_hbm, wdf_ref))

    def gather(ids_ref, dst_slot):
        _gather_rows(h2_ref, ids_ref, 1, 0, buf_ref.at[dst_slot], sem.at[dst_slot], (0,))

    def fetch_weights(e, ws):
        for src, dst in w_pairs:
            pltpu.make_async_copy(src.at[e], dst.at[ws], wsem.at[ws]).start(priority=1)

    @pl.when(i == 0)
    def _():
        gather(rt0_ref, 0)
        gather(rt1_ref, 1)
        fetch_weights(te_ref[0], par_ref[0])

    @pl.when(i < nv + MOE_LOOKAHEAD)
    def _():
        pltpu.make_async_copy(h2_ref.at[pl.ds(0, TM_MOE // SUBLANES)], buf_ref.at[slot], sem.at[slot]).wait()

    @pl.when((i < nv) & (first_ref[i] == 1))
    def _():
        ws = par_ref[i]
        for src, dst in w_pairs:
            pltpu.make_async_copy(src.at[0], dst.at[ws], wsem.at[ws]).wait()

        @pl.when(nxt_ref[i] >= 0)
        def _():
            fetch_weights(nxt_ref[i], 1 - ws)

    @pl.when(i < nv)
    def _():
        nslot = lax.rem(i + MOE_LOOKAHEAD, MOE_SLOTS)
        ws = par_ref[i]
        n_kc = D_MODEL // MXU_DIM
        groups_per_kc = TM_MOE // SUBLANES // n_kc
        hg = hu = None
        for kc in range(n_kc):
            for gg in range(groups_per_kc):
                _gather_group(h2_ref, rtn_ref, 1, 0, buf_ref.at[nslot], sem.at[nslot], (0,), kc * groups_per_kc + gg)
            ks = slice(kc * MXU_DIM, (kc + 1) * MXU_DIM)
            xk = buf_ref[slot, :, :, ks].reshape(TM_MOE, MXU_DIM).astype(BF16)
            pg = jnp.dot(xk, wgf_ref[ws, ks, :].astype(BF16), preferred_element_type=F32)
            pu = jnp.dot(xk, wuf_ref[ws, ks, :].astype(BF16), preferred_element_type=F32)
            hg = pg if hg is None else hg + pg
            hu = pu if hu is None else hu + pu
        act = (hg * jax.nn.sigmoid(hg)) * hu
        y_ref[...] = jnp.dot(act.astype(BF16), wdf_ref[ws].astype(BF16), preferred_element_type=F32)

    @pl.when(i >= nv)
    def _():
        y_ref[...] = jnp.zeros(y_ref.shape, y_ref.dtype)


def _moe(tile_meta, n_valid, row_token, h2, w_gate, w_up, w_down):
    n_steps = row_token.shape[0] // TM_MOE
    rt = row_token.reshape(n_steps, 1, TM_MOE)
    te, first, nxt, par = tile_meta
    any_spec = pl.BlockSpec(memory_space=pl.ANY)
    return pl.pallas_call(
        _moe_kernel,
        out_shape=jax.ShapeDtypeStruct((n_steps * TM_MOE, D_MODEL), F32),
        grid_spec=pltpu.PrefetchScalarGridSpec(
            num_scalar_prefetch=5, grid=(n_steps,),
            in_specs=[*_index_specs(TM_MOE, 0, n_steps, MOE_LOOKAHEAD), any_spec, any_spec, any_spec, any_spec],
            out_specs=pl.BlockSpec((TM_MOE, D_MODEL), lambda i, *_: (i, 0)),
            scratch_shapes=[pltpu.VMEM((MOE_SLOTS, TM_MOE // SUBLANES, SUBLANES, D_MODEL), F32),
                            pltpu.VMEM((2, D_MODEL, D_EXPERT), F32), pltpu.VMEM((2, D_MODEL, D_EXPERT), F32),
                            pltpu.VMEM((2, D_EXPERT, D_MODEL), F32),
                            pltpu.SemaphoreType.DMA((MOE_SLOTS,)), pltpu.SemaphoreType.DMA((2,))]),
        compiler_params=_cparams(("arbitrary",)),
        name="moe",
    )(te, n_valid, first, nxt, par, rt, rt, rt, h2.reshape(-1, SUBLANES, D_MODEL), w_gate, w_up, w_down)


def _final_kernel(pos0_ref, posn_ref, x1_ref, g2_ref, route_ref, l2g_ref, l2b_ref, ys_ref, o_ref, buf_ref, sem, *, tm):
    step = pl.program_id(0)
    slot = step & 1

    def gather(ids_ref, dst_slot):
        for k in range(2):
            _gather_rows(ys_ref, ids_ref, 1, k * tm, buf_ref.at[dst_slot, k], sem.at[dst_slot], (0, 1))

    @pl.when(step == 0)
    def _():
        gather(pos0_ref, 0)

    for k in range(2):
        pltpu.make_async_copy(ys_ref.at[pl.ds(0, tm // SUBLANES)], buf_ref.at[slot, k], sem.at[slot]).wait()

    route = route_ref[...]
    w1, w2 = route[:, 2:3], route[:, 3:4]
    n_groups = tm // SUBLANES

    def combine(issue_next):
        n_chunks = D_MODEL // MXU_DIM
        per_chunk = 2 * n_groups // n_chunks
        for c in range(n_chunks):
            if issue_next:
                for q in range(c * per_chunk, (c + 1) * per_chunk):
                    k, g = q // n_groups, q % n_groups
                    _gather_group(ys_ref, posn_ref, 1, k * tm, buf_ref.at[1 - slot, k], sem.at[1 - slot], (0, 1), g)
            cs = slice(c * MXU_DIM, (c + 1) * MXU_DIM)
            f = (w1 * buf_ref[slot, 0, :, :, cs].reshape(tm, MXU_DIM)
                 + w2 * buf_ref[slot, 1, :, :, cs].reshape(tm, MXU_DIM))
            o_ref[:, cs] = DEEPNORM_ALPHA * x1_ref[:, cs] + g2_ref[:, cs] * f

    @pl.when(step + 1 < pl.num_programs(0))
    def _():
        combine(True)

    @pl.when(step + 1 == pl.num_programs(0))
    def _():
        combine(False)

    o_ref[...] = _ln(o_ref[...]) * l2g_ref[...] + l2b_ref[...]


def _final(pos_tiles, first_tile, x1, mod, per_row, route, ln2_g, ln2_b, ys):
    nb, l, _ = x1.shape
    tm = min(TM, l)
    nt = l // tm
    n_steps = nb * nt
    row = lambda wd: pl.BlockSpec((None, tm, wd), lambda s: (s // nt, s % nt, 0))
    if per_row:
        g2 = pl.BlockSpec((None, tm, D_MODEL), lambda s: (s // nt, s % nt, 5))
    else:
        g2 = pl.BlockSpec((None, 1, D_MODEL), lambda s: (s // nt, 0, 5))
    vec = pl.BlockSpec((1, D_MODEL), lambda s: (0, 0))
    return pl.pallas_call(
        functools.partial(_final_kernel, tm=tm),
        out_shape=jax.ShapeDtypeStruct((nb, l, D_MODEL), F32),
        grid=(n_steps,),
        in_specs=[*_index_specs(2 * tm, first_tile, n_steps), row(D_MODEL), g2, row(LANES), vec, vec,
                  pl.BlockSpec(memory_space=pl.ANY)],
        out_specs=row(D_MODEL),
        scratch_shapes=[pltpu.VMEM((2, 2, tm // SUBLANES, SUBLANES, D_MODEL), F32), pltpu.SemaphoreType.DMA((2,))],
        compiler_params=_cparams(("arbitrary",)),
        name="final",
    )(pos_tiles, pos_tiles, x1, mod, route, ln2_g, ln2_b, ys.reshape(-1, SUBLANES, D_MODEL))


def _rope_tables(pos):
    half = ROT_DIM // 2
    inv_freq = ROPE_THETA ** (-jnp.arange(half, dtype=jnp.float32) * 2.0 / ROT_DIM)
    ang = pos.astype(jnp.float32)[:, None] * inv_freq[None, :]
    cos, sin = jnp.cos(ang), jnp.sin(ang)
    n = pos.shape[0]
    one = jnp.ones((n, HEAD_DIM - ROT_DIM), F32)
    zero = jnp.zeros((n, HEAD_DIM - half), F32)
    c = jnp.concatenate([cos, cos, one], -1)
    a = jnp.concatenate([-sin, zero], -1)
    b = jnp.concatenate([jnp.zeros((n, half), F32), sin, jnp.zeros((n, HEAD_DIM - ROT_DIM), F32)], -1)
    return tuple(jnp.tile(t, (1, LANES // HEAD_DIM)) for t in (c, a, b))


def _s5_params(lam_re, lam_im, log_step, b_re, b_im, c_re, c_im):
    f32 = jnp.float32
    dt = jnp.exp(log_step.astype(f32))[:, None]
    lr, li = lam_re.astype(f32), lam_im.astype(f32)
    mag = jnp.exp(lr * dt)
    ar, ai = mag * jnp.cos(li * dt), mag * jnp.sin(li * dt)
    den = lr * lr + li * li
    cr = ((ar - 1.0) * lr + ai * li) / den
    ci = (ai * lr - (ar - 1.0) * li) / den
    br, bi = b_re.astype(f32), b_im.astype(f32)
    bbr = cr[..., None] * br - ci[..., None] * bi
    bbi = cr[..., None] * bi + ci[..., None] * br
    gpt = MXU_DIM // SSM_CH
    ch_group = jnp.arange(MXU_DIM) // SSM_CH
    st_group = jnp.arange(ST_PER_KT) // SSM_STATE

    def pack_b(m):
        rows = jnp.transpose(m.reshape(N_KT, gpt, SSM_STATE, SSM_CH), (0, 1, 3, 2)).reshape(N_KT, MXU_DIM, SSM_STATE)
        keep = ch_group[:, None] == st_group[None, :]
        return jnp.where(keep[None], jnp.tile(rows, (1, 1, gpt)), 0.0).astype(BF16)

    def pack_c(m):
        cols = jnp.transpose(m.astype(f32).reshape(N_KT, gpt, SSM_CH, SSM_STATE), (0, 3, 1, 2))
        cols = cols.reshape(N_KT, SSM_STATE, MXU_DIM)
        keep = st_group[:, None] == ch_group[None, :]
        return jnp.where(keep[None], jnp.tile(cols, (1, gpt, 1)), 0.0).astype(BF16)

    wts = (pack_b(bbr), pack_b(bbi), pack_c(c_re), pack_c(-c_im.astype(f32)))
    return wts, ar.reshape(1, N_STATE), ai.reshape(1, N_STATE)


def _scan_tables(ar, ai, nb):
    assert SUBLANES // nb == 2
    a2r, a2i = ar * ar - ai * ai, 2.0 * ar * ai
    z = jnp.zeros_like(ar)
    rep = lambda first, second: jnp.concatenate([jnp.tile(first, (nb, 1)), jnp.tile(second, (nb, 1))], 0)
    return rep(z, ar), rep(z, ai), rep(ar, a2r), rep(ai, a2i)


def kernel(x_prompt, x_sample, cache_k_win, cache_v_win, state_ssm_re, state_ssm_im, c_prompt, c_sample, ada_w, ada_b,
           w_in, attn_sinks, ssm_lambda_re, ssm_lambda_im, ssm_log_step, ssm_b_re, ssm_b_im, ssm_c_re, ssm_c_im, ssm_d,
           ssm_glu_w, ssm_glu_b, attn_norm_g, ssm_norm_g, w_out, ln1_g, ln1_b, router_group_w, router_group_b,
           router_expert_w, router_expert_b, exp_w_gate, exp_w_up, exp_w_down, ln2_g, ln2_b):
    assert ada_w.shape[0] == DEPTH
    bp, lp, _ = x_prompt.shape
    bs, ls, _ = x_sample.shape
    lsp = SAMPLE_PAD

    w_in_bf = w_in[0].astype(BF16)
    wr = jnp.concatenate([router_group_w[0], router_expert_w[0]], -1)
    wr = jnp.pad(wr, ((0, 0), (0, LANES - wr.shape[1]))).astype(BF16)
    br = jnp.pad(jnp.concatenate([router_group_b[0], router_expert_b[0]], -1), (0, LANES - N_EXPERT_GROUPS - N_EXPERTS))
    wpost = dict(glu_w=ssm_glu_w[0].astype(BF16), glu_b=ssm_glu_b[0][None], attn_g=attn_norm_g[0][None],
                 ssm_g=ssm_norm_g[0][None], w_out=w_out[0].astype(BF16), ln1_g=ln1_g[0][None], ln1_b=ln1_b[0][None],
                 wr=wr, br=br[None])
    s5w, ar, ai = _s5_params(ssm_lambda_re[0], ssm_lambda_im[0], ssm_log_step[0], ssm_b_re[0], ssm_b_im[0],
                             ssm_c_re[0], ssm_c_im[0])
    d_skip = ssm_d[0][None]
    sinks = attn_sinks[0]

    n_c = bp + bs
    n_cp = -(-n_c // SUBLANES) * SUBLANES
    c_all = jnp.pad(jnp.concatenate([c_prompt, c_sample], 0), ((0, n_cp - n_c), (0, 0)))
    mod = _ada(c_all, ada_w[0], ada_b[0][None])
    mod_p = mod[:bp][:, None, :]
    mod_s = jnp.repeat(mod[bp:n_c], lsp, axis=0)[None]

    xs_pad = jnp.pad(x_sample, ((0, 0), (0, lsp - ls), (0, 0))).reshape(1, bs * lsp, D_MODEL)

    rope_p = _rope_tables(jnp.arange(lp))
    pos_s = PAST_LEN + jnp.minimum(jnp.arange(lsp), ls - 1)
    rope_s = tuple(jnp.tile(t, (bs, 1)) for t in _rope_tables(pos_s))
    q_p, k_p, v_p, u_p = _inproj(x_prompt, mod_p, w_in_bf, rope_p, False)
    q_s, k_s, v_s, u_s = _inproj(xs_pad, mod_s, w_in_bf, rope_s, True)

    attn_p = _attn_prompt(sinks, q_p, k_p, v_p)
    k_s3 = k_s.reshape(bs, lsp, D_KV)
    v_s3 = v_s.reshape(bs, lsp, D_KV)
    ck = cache_k_win[0].reshape(bs, WINDOW, D_KV)
    cv = cache_v_win[0].reshape(bs, WINDOW, D_KV)
    padk = lambda new: jnp.pad(new, ((0, 0), (0, WINDOW - lsp), (0, 0)))
    attn_s = _attn_sample(sinks, q_s.reshape(bs, lsp, D_ATTN), ck, padk(k_s3), cv, padk(v_s3))

    tabs = _scan_tables(ar, ai, bp)
    zero_carry = jnp.zeros((SUBLANES, N_STATE), F32)
    y_p, hr_p, hi_p = _s5_prompt(u_p, s5w, tabs, d_skip, zero_carry, zero_carry)
    u_tb = jnp.transpose(u_s.reshape(bs, lsp, D_SSM)[:, :ls], (1, 0, 2))
    y_tb, hr_s, hi_s = _s5_sample(u_tb, s5w, ar, ai, d_skip, state_ssm_re[0].reshape(bs, N_STATE),
                                  state_ssm_im[0].reshape(bs, N_STATE))
    y_s = jnp.pad(jnp.transpose(y_tb, (1, 0, 2)), ((0, 0), (0, lsp - ls), (0, 0))).reshape(1, bs * lsp, D_SSM)

    cnt0 = jnp.zeros((SUBLANES, LANES), F32)
    n_p, n_s = bp * lp, bs * lsp
    n_tok = n_p + n_s
    x1_s, h2_s, route_s, cnt_s = _post(attn_s.reshape(1, bs * lsp, D_ATTN), y_s, xs_pad, mod_s, True, wpost, cnt0)
    x1_p, h2, route_p, cnt = _post(attn_p, y_p, x_prompt, mod_p, False, wpost, cnt_s, tail=h2_s)

    n_tiles = -(-(2 * n_tok + N_EXPERTS * (TM_MOE - 1)) // TM_MOE) + MOE_LOOKAHEAD
    route = jnp.concatenate([route_p.reshape(n_p, LANES), route_s.reshape(n_s, LANES)], 0)
    counts = cnt[0, ROUTE_LANE0:ROUTE_LANE0 + N_EXPERTS].astype(jnp.int32)
    padded = ((counts + TM_MOE - 1) // TM_MOE) * TM_MOE
    ends = jnp.cumsum(padded)
    offs = ends - padded
    col = lambda c: route[:, c].astype(jnp.int32)
    expert_ids = jnp.arange(N_EXPERTS, dtype=jnp.int32)
    seg_start = lambda e: jnp.sum(jnp.where(e[:, None] == expert_ids[None, :], offs[None, :], 0), axis=1)
    pos_k = [seg_start(col(k)) + col(4 + k) for k in range(2)]
    n_valid = ends[-1] // TM_MOE
    tile_row = jnp.minimum(jnp.arange(n_tiles), n_valid - 1) * TM_MOE
    tile_expert = jnp.sum((ends[None, :] <= tile_row[:, None]).astype(jnp.int32), axis=1)
    first_tile = jnp.concatenate([jnp.ones((1,), jnp.int32), (tile_expert[1:] != tile_expert[:-1]).astype(jnp.int32)])
    first_tile = first_tile * (jnp.arange(n_tiles) < n_valid)
    nonempty = padded > 0
    later =jnp.where(nonempty[None, :] & (expert_ids[None, :] > expert_ids[:, None]), expert_ids[None, :], N_EXPERTS)
    next_expert = jnp.min(later, axis=1)
    next_expert = jnp.where(next_expert < N_EXPERTS, next_expert, -1).astype(jnp.int32)
    weight_slot = ((jnp.cumsum(nonempty.astype(jnp.int32)) - 1) & 1).astype(jnp.int32)
    tile_meta = (tile_expert, first_tile.astype(jnp.int32), next_expert[tile_expert], weight_slot[tile_expert])
    token_ids = jnp.arange(n_tok, dtype=jnp.int32)
    row_token = (jnp.arange(n_tiles * TM_MOE, dtype=jnp.int32) % n_tok).at[jnp.concatenate(pos_k)].set(
        jnp.concatenate([token_ids, token_ids]), unique_indices=True, mode="promise_in_bounds")

    ys = _moe(tile_meta, n_valid.reshape(1).astype(jnp.int32), row_token, h2, exp_w_gate[0], exp_w_up[0],
              exp_w_down[0])
    pos_tiles = jnp.concatenate([p.reshape(n_tok // TM, 1, TM) for p in pos_k], axis=2)
    y_prompt = _final(pos_tiles, 0, x1_p, mod_p, False, route_p, ln2_g[0][None], ln2_b[0][None], ys)
    y_samp = _final(pos_tiles, n_p // TM, x1_s, mod_s, True, route_s, ln2_g[0][None], ln2_b[0][None], ys)
    y_sample = y_samp.reshape(bs, lsp, D_MODEL)[:, :ls]

    kv5 = lambda a, n: a.reshape(1, a.shape[0], n, N_KV_HEADS, HEAD_DIM)
    k_win_p = kv5(k_p[:, lp - WINDOW:], WINDOW)
    v_win_p = kv5(v_p[:, lp - WINDOW:], WINDOW)
    k_win_s = kv5(jnp.concatenate([ck[:, ls:], k_s3[:, :ls]], 1), WINDOW)
    v_win_s = kv5(jnp.concatenate([cv[:, ls:], v_s3[:, :ls]], 1), WINDOW)
    st = lambda a: a.reshape(1, a.shape[0], N_SSM_GROUPS, SSM_STATE)
    return (y_prompt, y_sample, k_win_p, v_win_p, st(hr_p[bp:2 * bp]), st(hi_p[bp:2 * bp]),
            k_win_s, v_win_s, st(hr_s), st(hi_s))
```

```python
import functools

import jax
import jax.numpy as jnp
from jax import lax
from jax.experimental import pallas as pl
from jax.experimental.pallas import tpu as pltpu

F32 = jnp.float32
BF16 = jnp.bfloat16

D_MODEL = 2048
D_ATTN = 1024
D_SSM = 1024
HEAD_DIM = 64
N_HEADS = 16
N_KV_HEADS = 4
REP = 4
D_KV = 256
ROT_DIM = 16
ROPE_THETA = 500000.0
WINDOW = 128
SSM_CH = 16
N_SSM_GROUPS = 64
SSM_STATE = 64
N_STATE = N_SSM_GROUPS * SSM_STATE
PROJ_COLS = D_ATTN + 2 * D_KV + D_SSM
N_EXPERT_GROUPS = 4
EXPERTS_PER_GROUP = 8
N_EXPERTS = 32
D_EXPERT = 512
DEPTH = 1
DEEPNORM_ALPHA = (2.0 * DEPTH) ** 0.25
LN_EPS = 1e-5
PAST_LEN = 16384

LANES = 128
SUBLANES = 8
MXU_DIM = 256
TM = 256
TM_MOE = 256
TM_INPROJ = 512
S5_TT = 128
SAMPLE_PAD = 8
ROUTE_LANE0 = N_EXPERT_GROUPS
VMEM_LIMIT = 56 * 1024 * 1024


def _cparams(sem):
    return pltpu.CompilerParams(dimension_semantics=sem, vmem_limit_bytes=VMEM_LIMIT)


def _ln(x):
    mu = jnp.mean(x, axis=-1, keepdims=True)
    xc = x - mu
    var = jnp.mean(xc * xc, axis=-1, keepdims=True)
    return xc * lax.rsqrt(var + LN_EPS)


def _rms(x):
    return x * lax.rsqrt(jnp.mean(x * x, axis=-1, keepdims=True) + LN_EPS)


def _ada_kernel(c_ref, w_ref, b_ref, o_ref):
    c = c_ref[...]
    s = c * jax.nn.sigmoid(c)
    o_ref[...] = jnp.dot(s.astype(BF16), w_ref[...].astype(BF16), preferred_element_type=F32) + b_ref[...]


def _ada(c_all, ada_w, ada_b):
    n, tn = c_all.shape[0], 1024
    return pl.pallas_call(
        _ada_kernel,
        out_shape=jax.ShapeDtypeStruct((n, 6 * D_MODEL), F32),
        grid=(6 * D_MODEL // tn,),
        in_specs=[pl.BlockSpec((n, D_MODEL), lambda j: (0, 0)),
                  pl.BlockSpec((D_MODEL, tn), lambda j: (0, j)),
                  pl.BlockSpec((1, tn), lambda j: (0, j))],
        out_specs=pl.BlockSpec((n, tn), lambda j: (0, j)),
        compiler_params=_cparams(("arbitrary",)),
        name="ada",
    )(c_all, ada_w, ada_b)


def _mod_spec(per_row, tm, chunk):
    if per_row:
        return pl.BlockSpec((None, tm, D_MODEL), lambda b, i: (b, i, chunk))
    return pl.BlockSpec((None, 1, D_MODEL), lambda b, i: (b, 0, chunk))


def _inproj_kernel(x_ref, sh_ref, sc_ref, w_ref, rc_ref, ra_ref, rb_ref, q_ref, k_ref, v_ref, u_ref):
    h = _ln(x_ref[...]) * (1.0 + sc_ref[...]) + sh_ref[...]
    proj = jnp.dot(h.astype(BF16), w_ref[...], preferred_element_type=F32)
    rc, ra, rb = rc_ref[...], ra_ref[...], rb_ref[...]

    def rope(t):
        return t * rc + pltpu.roll(t, LANES - ROT_DIM // 2, 1) * ra + pltpu.roll(t, ROT_DIM // 2, 1) * rb

    for j in range(D_ATTN // LANES):
        q_ref[:, j * LANES:(j + 1) * LANES] = (rope(proj[:, j * LANES:(j + 1) * LANES]) * HEAD_DIM ** -0.5).astype(BF16)
    for j in range(D_KV // LANES):
        c0 = D_ATTN + j * LANES
        k_ref[:, j * LANES:(j + 1) * LANES] = rope(proj[:, c0:c0 + LANES])
    v_ref[...] = proj[:, D_ATTN + D_KV:D_ATTN + 2 * D_KV]
    u_ref[...] = proj[:, D_ATTN + 2 * D_KV:]


def _inproj(x, mod, w_in_bf, rope_tabs, per_row):
    nb, l, _ = x.shape
    tm = min(TM_INPROJ, l)
    row = lambda w: pl.BlockSpec((None, tm, w), lambda b, i: (b, i, 0))
    tab = pl.BlockSpec((tm, LANES), lambda b, i: (i, 0))
    return pl.pallas_call(
        _inproj_kernel,
        out_shape=(jax.ShapeDtypeStruct((nb, l, D_ATTN), BF16), jax.ShapeDtypeStruct((nb, l, D_KV), F32),
                   jax.ShapeDtypeStruct((nb, l, D_KV), F32), jax.ShapeDtypeStruct((nb, l, D_SSM), F32)),
        grid=(nb, l // tm),
        in_specs=[row(D_MODEL), _mod_spec(per_row, tm, 0), _mod_spec(per_row, tm, 1),
                  pl.BlockSpec((D_MODEL, PROJ_COLS), lambda b, i: (0, 0)), tab, tab, tab],
        out_specs=(row(D_ATTN), row(D_KV), row(D_KV), row(D_SSM)),
        compiler_params=_cparams(("arbitrary", "arbitrary")),
        name="inproj",
    )(x, mod, mod, w_in_bf, *rope_tabs)


ATTN_SAMPLE_BATCH = 8


def _attn_kernel(sink_ref, q_ref, kp_ref, kc_ref, vp_ref, vc_ref, o_ref, *, lq):
    m_rows = REP * lq
    ii = lax.broadcasted_iota(jnp.int32, (m_rows, WINDOW), 0) & (lq - 1)
    jj = lax.broadcasted_iota(jnp.int32, (m_rows, WINDOW), 1)
    from_prev = jj > ii
    dead = jj > ii + jnp.where(pl.program_id(1) > 0, WINDOW, 0)
    rr = lax.broadcasted_iota(jnp.int32, (m_rows, 1), 0)
    dn = (((1,), (1,)), ((), ()))
    for bi in range(q_ref.shape[0]):
        for g in range(N_KV_HEADS):
            qg = q_ref[bi, :, g * REP * HEAD_DIM:(g + 1) * REP * HEAD_DIM].astype(F32)
            qs = jnp.concatenate([qg[:, r * HEAD_DIM:(r + 1) * HEAD_DIM] for r in range(REP)], axis=0).astype(BF16)
            ks = slice(g * HEAD_DIM, (g + 1) * HEAD_DIM)
            s_p = lax.dot_general(qs, kp_ref[bi, :, ks].astype(BF16), dn, preferred_element_type=F32)
            s_c = lax.dot_general(qs, kc_ref[bi, :, ks].astype(BF16), dn, preferred_element_type=F32)
            s = jnp.where(dead, -jnp.inf, jnp.where(from_prev, s_p, s_c))
            sink = jnp.zeros((m_rows, 1), F32)
            for r in range(REP):
                sink = jnp.where((rr >= r * lq) & (rr < (r + 1) * lq), sink_ref[g * REP + r], sink)
            m = jnp.maximum(jnp.max(s, axis=-1, keepdims=True), sink)
            p = jnp.exp(s - m)
            p = p / (jnp.sum(p, axis=-1, keepdims=True) + jnp.exp(sink - m))
            o = (jnp.dot(jnp.where(from_prev, p, 0.0).astype(BF16), vp_ref[bi, :, ks].astype(BF16),
                         preferred_element_type=F32)
                 + jnp.dot(jnp.where(from_prev, 0.0, p).astype(BF16), vc_ref[bi, :, ks].astype(BF16),
                           preferred_element_type=F32))
            for r in range(REP):
                h = g * REP + r
                o_ref[bi, :, h * HEAD_DIM:(h + 1) * HEAD_DIM] = o[r * lq:(r + 1) * lq]


def _attn_prompt(sinks, q, k, v):
    nb, l, _ = q.shape
    nblk = l // WINDOW
    cur = lambda w: pl.BlockSpec((1, WINDOW, w), lambda b, n: (b, n, 0))
    prev = lambda w: pl.BlockSpec((1, WINDOW, w), lambda b, n: (b, jnp.maximum(n - 1, 0), 0))
    return pl.pallas_call(
        functools.partial(_attn_kernel, lq=WINDOW),
        out_shape=jax.ShapeDtypeStruct((nb, l, D_ATTN), F32),
        grid=(nb, nblk),
        in_specs=[pl.BlockSpec(memory_space=pltpu.SMEM), cur(D_ATTN), prev(D_KV), cur(D_KV), prev(D_KV), cur(D_KV)],
        out_specs=cur(D_ATTN),
        compiler_params=_cparams(("arbitrary", "arbitrary")),
        name="attn_prompt",
    )(sinks, q, k, k, v, v)


ATTN_SAMPLE_JOINT = 4


def _attn_sample_kernel(sink_ref, q_ref, kp_ref, kc_ref, vp_ref, vc_ref, o_ref, *, lq):
    gb = ATTN_SAMPLE_JOINT
    m_b = REP * lq
    m, n = gb * m_b, gb * WINDOW
    row = lax.broadcasted_iota(jnp.int32, (m, n), 0)
    col = lax.broadcasted_iota(jnp.int32, (m, n), 1)
    same = (row // m_b) == (col // WINDOW)
    from_prev = (col & (WINDOW - 1)) > (row & (lq - 1))
    rep_of_row = (lax.broadcasted_iota(jnp.int32, (m, 1), 0) // lq) % REP
    dn = (((1,), (1,)), ((), ()))
    for blk in range(q_ref.shape[0] // gb):
        bs = range(blk * gb, (blk + 1) * gb)
        for g in range(N_KV_HEADS):
            qg = [q_ref[b, :, g * REP * HEAD_DIM:(g + 1) * REP * HEAD_DIM].astype(F32) for b in bs]
            qs = jnp.concatenate([qb[:, r * HEAD_DIM:(r + 1) * HEAD_DIM] for qb in qg for r in range(REP)],
                                 axis=0).astype(BF16)
            ks = slice(g * HEAD_DIM, (g + 1) * HEAD_DIM)
            stack = lambda ref: jnp.concatenate([ref[b, :, ks] for b in bs], axis=0).astype(BF16)
            s_p = lax.dot_general(qs, stack(kp_ref), dn, preferred_element_type=F32)
            s_c = lax.dot_general(qs, stack(kc_ref), dn, preferred_element_type=F32)
            s = jnp.where(same, jnp.where(from_prev, s_p, s_c), -jnp.inf)
            sink = jnp.zeros((m, 1), F32)
            for r in range(REP):
                sink = jnp.where(rep_of_row == r, sink_ref[g * REP + r], sink)
            mx = jnp.maximum(jnp.max(s, axis=-1, keepdims=True), sink)
            p = jnp.exp(s - mx)
            p = p / (jnp.sum(p, axis=-1, keepdims=True) + jnp.exp(sink - mx))
            o = (jnp.dot(jnp.where(from_prev, p, 0.0).astype(BF16), stack(vp_ref), preferred_element_type=F32)
                 + jnp.dot(jnp.where(from_prev, 0.0, p).astype(BF16), stack(vc_ref), preferred_element_type=F32))
            for bi, b in enumerate(bs):
                for r in range(REP):
                    h = g * REP + r
                    o_ref[b, :, h * HEAD_DIM:(h + 1) * HEAD_DIM] = o[bi * m_b + r * lq:bi * m_b + (r + 1) * lq]


def _attn_sample(sinks, q, k_cache, k_new, v_cache, v_new):
    nb, lq, _ = q.shape
    nbb = ATTN_SAMPLE_BATCH
    kblk = pl.BlockSpec((nbb, WINDOW, D_KV), lambda b: (b, 0, 0))
    return pl.pallas_call(
        functools.partial(_attn_sample_kernel, lq=lq),
        out_shape=jax.ShapeDtypeStruct((nb, lq, D_ATTN), F32),
        grid=(nb // nbb,),
        in_specs=[pl.BlockSpec(memory_space=pltpu.SMEM), pl.BlockSpec((nbb, lq, D_ATTN), lambda b: (b, 0, 0)),
                  kblk, kblk, kblk, kblk],
        out_specs=pl.BlockSpec((nbb, lq, D_ATTN), lambda b: (b, 0, 0)),
        compiler_params=_cparams(("arbitrary",)),
        name="attn_sample",
    )(sinks, q, k_cache, k_new, v_cache, v_new)


N_KT = D_SSM // MXU_DIM
ST_PER_KT = N_STATE // N_KT
N_SLAB = D_SSM // LANES


def _cproj(hr_ref, hi_ref, cre_ref, cimn_ref, kt):
    cs = slice(kt * ST_PER_KT, (kt + 1) * ST_PER_KT)
    return (jnp.dot(hr_ref[:, cs].astype(BF16), cre_ref[kt], preferred_element_type=F32)
            + jnp.dot(hi_ref[:, cs].astype(BF16), cimn_ref[kt], preferred_element_type=F32))


def _s5_prompt_kernel(u_ref, bre_ref, bim_ref, cre_ref, cimn_ref, a1r_ref, a1i_ref, par_ref, pai_ref, d_ref,
                      h0r_ref, h0i_ref, y_ref, cr_ref, ci_ref, il_ref, xr_ref, xi_ref, *, nb, tt):
    rows = nb * tt

    @pl.when(pl.program_id(0) == 0)
    def _():
        cr_ref[...] = h0r_ref[...]
        ci_ref[...] = h0i_ref[...]

    for b in range(nb):
        for j in range(N_SLAB):
            il_ref[j, pl.ds(b, tt, stride=nb), :] = u_ref[b, :, j * LANES:(j + 1) * LANES]
    for kt in range(N_KT):
        ub = jnp.concatenate([il_ref[2 * kt], il_ref[2 * kt + 1]], axis=1).astype(BF16)
        cs = slice(kt * ST_PER_KT, (kt + 1) * ST_PER_KT)
        xr_ref[:, cs] = jnp.dot(ub, bre_ref[kt], preferred_element_type=F32)
        xi_ref[:, cs] = jnp.dot(ub, bim_ref[kt], preferred_element_type=F32)

    half = lax.broadcasted_iota(jnp.int32, (SUBLANES, LANES), 0) < nb

    def step(i, carry):
        r0 = pl.multiple_of(i * SUBLANES, SUBLANES)
        for j in range(N_STATE // LANES):
            ls = slice(j * LANES, (j + 1) * LANES)
            x_r = xr_ref[pl.ds(r0, SUBLANES), ls]
            x_i = xi_ref[pl.ds(r0, SUBLANES), ls]
            s_r = pltpu.roll(x_r, nb, 0)
            s_i = pltpu.roll(x_i, nb, 0)
            a1r, a1i = a1r_ref[:, ls], a1i_ref[:, ls]
            c_r, c_i = cr_ref[:, ls], ci_ref[:, ls]
            par, pai = par_ref[:, ls], pai_ref[:, ls]
            h_r = x_r + (a1r * s_r - a1i * s_i) + (par * c_r - pai * c_i)
            h_i = x_i + (a1r * s_i + a1i * s_r) + (par * c_i + pai * c_r)
            xr_ref[pl.ds(r0, SUBLANES), ls] = h_r
            xi_ref[pl.ds(r0, SUBLANES), ls] = h_i
            cr_ref[:, ls] = jnp.where(half, pltpu.roll(h_r, nb, 0), h_r)
            ci_ref[:, ls] = jnp.where(half, pltpu.roll(h_i, nb, 0), h_i)
        return carry

    lax.fori_loop(0, rows // SUBLANES, step, 0)

    for kt in range(N_KT):
        y = _cproj(xr_ref, xi_ref, cre_ref, cimn_ref, kt)
        il_ref[2 * kt] = y[:, :LANES]
        il_ref[2 * kt + 1] = y[:, LANES:]
    for b in range(nb):
        for j in range(N_SLAB):
            ls = slice(j * LANES, (j + 1) * LANES)
            y_ref[b, :, ls] = il_ref[j, pl.ds(b, tt, stride=nb), :] + d_ref[:, ls] * u_ref[b, :, ls]


def _s5_prompt(u, wts, tabs, d_skip, h0r, h0i):
    nb, l, _ = u.shape
    tt = S5_TT
    rows = nb * tt
    full = lambda a: pl.BlockSpec(a.shape, lambda i: (0,) * a.ndim)
    blk = pl.BlockSpec((nb, tt, D_SSM), lambda i: (0, i, 0))
    carry = jax.ShapeDtypeStruct((SUBLANES, N_STATE), F32)
    return pl.pallas_call(
        functools.partial(_s5_prompt_kernel, nb=nb, tt=tt),
        out_shape=(jax.ShapeDtypeStruct((nb, l, D_SSM), F32), carry, carry),
        grid=(l // tt,),
        in_specs=[blk] + [full(a) for a in (*wts, *tabs, d_skip, h0r, h0i)],
        out_specs=(blk, pl.BlockSpec((SUBLANES, N_STATE), lambda i: (0, 0)),
                   pl.BlockSpec((SUBLANES, N_STATE), lambda i: (0, 0))),
        scratch_shapes=[pltpu.VMEM((N_SLAB, rows, LANES), F32), pltpu.VMEM((rows, N_STATE), F32),
                        pltpu.VMEM((rows, N_STATE), F32)],
        compiler_params=_cparams(("arbitrary",)),
        name="s5_prompt",
    )(u, *wts, *tabs, d_skip, h0r, h0i)


def _s5_sample_kernel(u_ref, bre_ref, bim_ref, cre_ref, cimn_ref, ar_ref, ai_ref, d_ref, h0r_ref, h0i_ref,
                      y_ref, sr_ref, si_ref, xr_ref, xi_ref, *, nt):
    sr_ref[...] = h0r_ref[...]
    si_ref[...] = h0i_ref[...]
    for t in range(nt):
        for kt in range(N_KT):
            ub = u_ref[t, :, kt * MXU_DIM:(kt + 1) * MXU_DIM].astype(BF16)
            cs = slice(kt * ST_PER_KT, (kt + 1) * ST_PER_KT)
            xr_ref[:, cs] = jnp.dot(ub, bre_ref[kt], preferred_element_type=F32)
            xi_ref[:, cs] = jnp.dot(ub, bim_ref[kt], preferred_element_type=F32)
        ar, ai = ar_ref[...], ai_ref[...]
        s_r, s_i = sr_ref[...], si_ref[...]
        sr_ref[...] = xr_ref[...] + (ar * s_r - ai * s_i)
        si_ref[...] = xi_ref[...] + (ar * s_i + ai * s_r)
        for kt in range(N_KT):
            ys = slice(kt * MXU_DIM, (kt + 1) * MXU_DIM)
            y_ref[t, :, ys] = _cproj(sr_ref, si_ref, cre_ref, cimn_ref, kt) + d_ref[:, ys] * u_ref[t, :, ys]


def _s5_sample(u_tb, wts, ar, ai, d_skip, h0r, h0i):
    nt, nb, _ = u_tb.shape
    st = jax.ShapeDtypeStruct((nb, N_STATE), F32)
    args = (u_tb, *wts, ar, ai, d_skip, h0r, h0i)
    full = lambda a: pl.BlockSpec(a.shape, lambda i: (0,) * a.ndim)
    return pl.pallas_call(
        functools.partial(_s5_sample_kernel, nt=nt),
        out_shape=(jax.ShapeDtypeStruct((nt, nb, D_SSM), F32), st, st),
        grid=(1,),
        in_specs=[full(a) for a in args],
        out_specs=(pl.BlockSpec((nt, nb, D_SSM), lambda i: (0, 0, 0)), pl.BlockSpec((nb, N_STATE), lambda i: (0, 0)),
                   pl.BlockSpec((nb, N_STATE), lambda i: (0, 0))),
        scratch_shapes=[pltpu.VMEM((nb, N_STATE), F32), pltpu.VMEM((nb, N_STATE), F32)],
        compiler_params=_cparams(("arbitrary",)),
        name="s5_sample",
    )(*args)


def _gather_group(src_ref, ids_ref, id_stride, id_offset, dst_ref, sem, priorities, g):
    for j in range(SUBLANES):
        t = ids_ref[0, id_stride * (g * SUBLANES + j) + id_offset]
        pltpu.make_async_copy(src_ref.at[t >> 3, pl.ds(t & (SUBLANES - 1), 1)], dst_ref.at[g, pl.ds(j, 1)],
                              sem).start(priority=priorities[j % len(priorities)])


def _gather_rows(src_ref, ids_ref, id_stride, id_offset, dst_ref, sem, priorities):
    def body(g, c):
        _gather_group(src_ref, ids_ref, id_stride, id_offset, dst_ref, sem, priorities, g)
        return c
    lax.fori_loop(0, dst_ref.shape[0], body, 0)


def _index_specs(n_cols, first_tile, n_steps, depth=1):
    def spec(fn):
        return pl.BlockSpec((None, 1, n_cols), fn, memory_space=pltpu.SMEM)

    head = [spec(lambda s, *_, k=k: (first_tile + min(k, n_steps - 1), 0, 0)) for k in range(depth)]
    return (*head, spec(lambda s, *_: (first_tile + jnp.minimum(s + depth, n_steps - 1), 0, 0)))


N_POST_INPUTS = 17


def _post_kernel(*refs, has_tail):
    ins, outs = refs[:N_POST_INPUTS], refs[N_POST_INPUTS + int(has_tail):]
    if not has_tail:
        _post_body(*ins, *outs)
        return
    tail_ref, h2_ref = refs[N_POST_INPUTS], outs[1]
    last = pl.num_programs(0) - 1

    @pl.when(pl.program_id(0) == last)
    def _():
        h2_ref[...] = tail_ref[...]

    @pl.when(pl.program_id(0) < last)
    def _():
        _post_body(*ins, *outs)


def _post_body(attn_ref, yssm_ref, x_ref, g1_ref, sh2_ref, sc2_ref, gluw_ref, glub_ref, ga_ref, gs_ref, wout_ref,
               l1g_ref, l1b_ref, wr_ref, br_ref, tri_ref, cnt0_ref, x1_ref, h2_ref, route_ref, cnt_ref):
    tm = x_ref.shape[0]

    @pl.when(pl.program_id(0) == 0)
    def _():
        cnt_ref[...] = cnt0_ref[...]

    z = jax.nn.gelu(yssm_ref[...])
    ssm = z * jax.nn.sigmoid(jnp.dot(z.astype(BF16), gluw_ref[...], preferred_element_type=F32) + glub_ref[...])
    mixed_a = (_rms(attn_ref[...]) * ga_ref[...]).astype(BF16)
    mixed_s = (_rms(ssm) * gs_ref[...]).astype(BF16)
    o = (jnp.dot(mixed_a, wout_ref[:D_ATTN, :], preferred_element_type=F32)
         + jnp.dot(mixed_s, wout_ref[D_ATTN:, :], preferred_element_type=F32))
    x1 = _ln(DEEPNORM_ALPHA * x_ref[...] + g1_ref[...] * o) * l1g_ref[...] + l1b_ref[...]
    x1_ref[...] = x1
    h2 = _ln(x1) * (1.0 + sc2_ref[...]) + sh2_ref[...]
    h2_ref[...] = h2
    logits = jnp.dot(h2.astype(BF16), wr_ref[...], preferred_element_type=F32) + br_ref[...]

    lane = lax.broadcasted_iota(jnp.int32, (tm, LANES), 1).astype(F32)
    big = float(4 * LANES)
    neg = -jnp.inf
    gl = jnp.where(lane < N_EXPERT_GROUPS, logits, neg)
    gp = jnp.exp(gl - jnp.max(gl, axis=-1, keepdims=True))
    gp = gp / jnp.sum(gp, axis=-1, keepdims=True)
    g_val = jnp.max(gp, axis=-1, keepdims=True)
    g_idx = jnp.min(jnp.where(gp == g_val, lane, big), axis=-1, keepdims=True)
    lo = ROUTE_LANE0 + EXPERTS_PER_GROUP * g_idx
    emask = (lane >= lo) & (lane < lo + EXPERTS_PER_GROUP)
    el = jnp.where(emask, logits, neg)
    ep = jnp.exp(el - jnp.max(el, axis=-1, keepdims=True))
    ep = jnp.where(emask, ep / jnp.sum(ep, axis=-1, keepdims=True), -1.0)
    v1 = jnp.max(ep, axis=-1, keepdims=True)
    i1 = jnp.min(jnp.where(ep == v1, lane, big), axis=-1, keepdims=True)
    ep2 = jnp.where(lane == i1, -1.0, ep)
    v2 = jnp.max(ep2, axis=-1, keepdims=True)
    i2 = jnp.min(jnp.where(ep2 == v2, lane, big), axis=-1, keepdims=True)
    vs = v1 + v2
    w1 = g_val * (v1 / vs)
    w2 = g_val * (v2 / vs)
    hit = jnp.where((lane == i1) | (lane == i2), 1.0, 0.0)
    before = jnp.dot(tri_ref[...], hit.astype(BF16), preferred_element_type=F32) + cnt_ref[0:1, :]
    r1 = jnp.sum(jnp.where(lane == i1, before, 0.0), axis=-1, keepdims=True)
    r2 = jnp.sum(jnp.where(lane == i2, before, 0.0), axis=-1, keepdims=True)
    cnt_ref[...] = cnt_ref[...] + jnp.sum(hit, axis=0, keepdims=True)
    e1 = i1 - ROUTE_LANE0
    e2 = i2 - ROUTE_LANE0
    route = jnp.zeros((tm, LANES), F32)
    for n, val in enumerate((e1, e2, w1, w2, r1, r2)):
        route = jnp.where(lane == n, val, route)
    route_ref[...] = route


def _post(attn, yssm, x, mod, per_row, w, cnt0, tail=None):
    nb, l, _ = x.shape
    tm = min(TM, l)
    nt = l // tm
    n_body = nb * nt
    has_tail = tail is not None
    if has_tail:
        assert tail.shape == (tm, D_MODEL)

    def bi(s):
        s = jnp.minimum(s, n_body - 1)
        return s // nt, s % nt

    row = lambda wd: pl.BlockSpec((None, tm, wd), lambda s: (*bi(s), 0))
    if per_row:
        mspec = lambda chunk: pl.BlockSpec((None, tm, D_MODEL), lambda s: (*bi(s), chunk))
    else:
        mspec = lambda chunk: pl.BlockSpec((None, 1, D_MODEL), lambda s: (bi(s)[0], 0, chunk))
    full = lambda a: pl.BlockSpec(a.shape, lambda s: (0,) * a.ndim)
    tri = jnp.tril(jnp.ones((tm, tm), F32), -1).astype(BF16)
    consts = (w["glu_w"], w["glu_b"], w["attn_g"], w["ssm_g"], w["w_out"], w["ln1_g"], w["ln1_b"], w["wr"], w["br"],
              tri, cnt0) + ((tail,) if has_tail else ())
    assert 6 + len(consts) == N_POST_INPUTS + int(has_tail)
    n_steps = n_body + int(has_tail)
    return pl.pallas_call(
        functools.partial(_post_kernel, has_tail=has_tail),
        out_shape=(jax.ShapeDtypeStruct((nb, l, D_MODEL), F32),
                   jax.ShapeDtypeStruct((n_steps * tm, D_MODEL), F32),
                   jax.ShapeDtypeStruct((nb, l, LANES), F32), jax.ShapeDtypeStruct((SUBLANES, LANES), F32)),
        grid=(n_steps,),
        in_specs=[row(D_ATTN), row(D_SSM), row(D_MODEL), mspec(2), mspec(3), mspec(4)] + [full(a) for a in consts],
        out_specs=(row(D_MODEL), pl.BlockSpec((tm, D_MODEL), lambda s: (s, 0)),
                   row(LANES), pl.BlockSpec((SUBLANES, LANES), lambda s: (0, 0))),
        compiler_params=_cparams(("arbitrary",)),
        name="post",
    )(attn, yssm, x, mod, mod, mod, *consts)


MOE_LOOKAHEAD = 2
MOE_SLOTS = MOE_LOOKAHEAD + 1


def _moe_kernel(te_ref, nv_ref, first_ref, nxt_ref, par_ref, rt0_ref, rt1_ref, rtn_ref, h2_ref, wg_hbm, wu_hbm, wd_hbm,
                y_ref, buf_ref, wgf_ref, wuf_ref, wdf_ref, sem, wsem):
    i = pl.program_id(0)
    nv = nv_ref[0]
    slot = lax.rem(i, MOE_SLOTS)
    w_pairs = ((wg_hbm, wgf_ref), (wu_hbm, wuf_ref), (wd_hbm, wdf_ref))

    def gather(ids_ref, dst_slot):
        _gather_rows(h2_ref, ids_ref, 1, 0, buf_ref.at[dst_slot], sem.at[dst_slot], (0,))

    def fetch_weights(e, ws):
        for src, dst in w_pairs:
            pltpu.make_async_copy(src.at[e], dst.at[ws], wsem.at[ws]).start(priority=1)

    @pl.when(i == 0)
    def _():
        gather(rt0_ref, 0)
        gather(rt1_ref, 1)
        fetch_weights(te_ref[0], par_ref[0])

    @pl.when(i < nv + MOE_LOOKAHEAD)
    def _():
        pltpu.make_async_copy(h2_ref.at[pl.ds(0, TM_MOE // SUBLANES)], buf_ref.at[slot], sem.at[slot]).wait()

    @pl.when((i < nv) & (first_ref[i] == 1))
    def _():
        ws = par_ref[i]
        for src, dst in w_pairs:
            pltpu.make_async_copy(src.at[0], dst.at[ws], wsem.at[ws]).wait()

        @pl.when(nxt_ref[i] >= 0)
        def _():
            fetch_weights(nxt_ref[i], 1 - ws)

    @pl.when(i < nv)
    def _():
        nslot = lax.rem(i + MOE_LOOKAHEAD, MOE_SLOTS)
        ws = par_ref[i]
        n_kc = D_MODEL // MXU_DIM
        groups_per_kc = TM_MOE // SUBLANES // n_kc
        hg = hu = None
        for kc in range(n_kc):
            for gg in range(groups_per_kc):
                _gather_group(h2_ref, rtn_ref, 1, 0, buf_ref.at[nslot], sem.at[nslot], (0,), kc * groups_per_kc + gg)
            ks = slice(kc * MXU_DIM, (kc + 1) * MXU_DIM)
            xk = buf_ref[slot, :, :, ks].reshape(TM_MOE, MXU_DIM).astype(BF16)
            pg = jnp.dot(xk, wgf_ref[ws, ks, :].astype(BF16), preferred_element_type=F32)
            pu = jnp.dot(xk, wuf_ref[ws, ks, :].astype(BF16), preferred_element_type=F32)
            hg = pg if hg is None else hg + pg
            hu = pu if hu is None else hu + pu
        act = (hg * jax.nn.sigmoid(hg)) * hu
        y_ref[...] = jnp.dot(act.astype(BF16), wdf_ref[ws].astype(BF16), preferred_element_type=F32)

    @pl.when(i >= nv)
    def _():
        y_ref[...] = jnp.zeros(y_ref.shape, y_ref.dtype)


def _moe(tile_meta, n_valid, row_token, h2, w_gate, w_up, w_down):
    n_steps = row_token.shape[0] // TM_MOE
    rt = row_token.reshape(n_steps, 1, TM_MOE)
    te, first, nxt, par = tile_meta
    any_spec = pl.BlockSpec(memory_space=pl.ANY)
    return pl.pallas_call(
        _moe_kernel,
        out_shape=jax.ShapeDtypeStruct((n_steps * TM_MOE, D_MODEL), F32),
        grid_spec=pltpu.PrefetchScalarGridSpec(
            num_scalar_prefetch=5, grid=(n_steps,),
            in_specs=[*_index_specs(TM_MOE, 0, n_steps, MOE_LOOKAHEAD), any_spec, any_spec, any_spec, any_spec],
            out_specs=pl.BlockSpec((TM_MOE, D_MODEL), lambda i, *_: (i, 0)),
            scratch_shapes=[pltpu.VMEM((MOE_SLOTS, TM_MOE // SUBLANES, SUBLANES, D_MODEL), F32),
                            pltpu.VMEM((2, D_MODEL, D_EXPERT), F32), pltpu.VMEM((2, D_MODEL, D_EXPERT), F32),
                            pltpu.VMEM((2, D_EXPERT, D_MODEL), F32),
                            pltpu.SemaphoreType.DMA((MOE_SLOTS,)), pltpu.SemaphoreType.DMA((2,))]),
        compiler_params=_cparams(("arbitrary",)),
        name="moe",
    )(te, n_valid, first, nxt, par, rt, rt, rt, h2.reshape(-1, SUBLANES, D_MODEL), w_gate, w_up, w_down)


def _final_kernel(pos0_ref, posn_ref, x1_ref, g2_ref, route_ref, l2g_ref, l2b_ref, ys_ref, o_ref, buf_ref, sem, *, tm):
    step = pl.program_id(0)
    slot = step & 1

    def gather(ids_ref, dst_slot):
        for k in range(2):
            _gather_rows(ys_ref, ids_ref, 1, k * tm, buf_ref.at[dst_slot, k], sem.at[dst_slot], (0, 1))

    @pl.when(step == 0)
    def _():
        gather(pos0_ref, 0)

    for k in range(2):
        pltpu.make_async_copy(ys_ref.at[pl.ds(0, tm // SUBLANES)], buf_ref.at[slot, k], sem.at[slot]).wait()

    route = route_ref[...]
    w1, w2 = route[:, 2:3], route[:, 3:4]
    n_groups = tm // SUBLANES

    def combine(issue_next):
        n_chunks = D_MODEL // MXU_DIM
        per_chunk = 2 * n_groups // n_chunks
        for c in range(n_chunks):
            if issue_next:
                for q in range(c * per_chunk, (c + 1) * per_chunk):
                    k, g = q // n_groups, q % n_groups
                    _gather_group(ys_ref, posn_ref, 1, k * tm, buf_ref.at[1 - slot, k], sem.at[1 - slot], (0, 1), g)
            cs = slice(c * MXU_DIM, (c + 1) * MXU_DIM)
            f = (w1 * buf_ref[slot, 0, :, :, cs].reshape(tm, MXU_DIM)
                 + w2 * buf_ref[slot, 1, :, :, cs].reshape(tm, MXU_DIM))
            o_ref[:, cs] = DEEPNORM_ALPHA * x1_ref[:, cs] + g2_ref[:, cs] * f

    @pl.when(step + 1 < pl.num_programs(0))
    def _():
        combine(True)

    @pl.when(step + 1 == pl.num_programs(0))
    def _():
        combine(False)

    o_ref[...] = _ln(o_ref[...]) * l2g_ref[...] + l2b_ref[...]


def _final(pos_tiles, first_tile, x1, mod, per_row, route, ln2_g, ln2_b, ys):
    nb, l, _ = x1.shape
    tm = min(TM, l)
    nt = l // tm
    n_steps = nb * nt
    row = lambda wd: pl.BlockSpec((None, tm, wd), lambda s: (s // nt, s % nt, 0))
    if per_row:
        g2 = pl.BlockSpec((None, tm, D_MODEL), lambda s: (s // nt, s % nt, 5))
    else:
        g2 = pl.BlockSpec((None, 1, D_MODEL), lambda s: (s // nt, 0, 5))
    vec = pl.BlockSpec((1, D_MODEL), lambda s: (0, 0))
    return pl.pallas_call(
        functools.partial(_final_kernel, tm=tm),
        out_shape=jax.ShapeDtypeStruct((nb, l, D_MODEL), F32),
        grid=(n_steps,),
        in_specs=[*_index_specs(2 * tm, first_tile, n_steps), row(D_MODEL), g2, row(LANES), vec, vec,
                  pl.BlockSpec(memory_space=pl.ANY)],
        out_specs=row(D_MODEL),
        scratch_shapes=[pltpu.VMEM((2, 2, tm // SUBLANES, SUBLANES, D_MODEL), F32), pltpu.SemaphoreType.DMA((2,))],
        compiler_params=_cparams(("arbitrary",)),
        name="final",
    )(pos_tiles, pos_tiles, x1, mod, route, ln2_g, ln2_b, ys.reshape(-1, SUBLANES, D_MODEL))


def _rope_tables(pos):
    half = ROT_DIM // 2
    inv_freq = ROPE_THETA ** (-jnp.arange(half, dtype=jnp.float32) * 2.0 / ROT_DIM)
    ang = pos.astype(jnp.float32)[:, None] * inv_freq[None, :]
    cos, sin = jnp.cos(ang), jnp.sin(ang)
    n = pos.shape[0]
    one = jnp.ones((n, HEAD_DIM - ROT_DIM), F32)
    zero = jnp.zeros((n, HEAD_DIM - half), F32)
    c = jnp.concatenate([cos, cos, one], -1)
    a = jnp.concatenate([-sin, zero], -1)
    b = jnp.concatenate([jnp.zeros((n, half), F32), sin, jnp.zeros((n, HEAD_DIM - ROT_DIM), F32)], -1)
    return tuple(jnp.tile(t, (1, LANES // HEAD_DIM)) for t in (c, a, b))


def _s5_params(lam_re, lam_im, log_step, b_re, b_im, c_re, c_im):
    f32 = jnp.float32
    dt = jnp.exp(log_step.astype(f32))[:, None]
    lr, li = lam_re.astype(f32), lam_im.astype(f32)
    mag = jnp.exp(lr * dt)
    ar, ai = mag * jnp.cos(li * dt), mag * jnp.sin(li * dt)
    den = lr * lr + li * li
    cr = ((ar - 1.0) * lr + ai * li) / den
    ci = (ai * lr - (ar - 1.0) * li) / den
    br, bi = b_re.astype(f32), b_im.astype(f32)
    bbr = cr[..., None] * br - ci[..., None] * bi
    bbi = cr[..., None] * bi + ci[..., None] * br
    gpt = MXU_DIM // SSM_CH
    ch_group = jnp.arange(MXU_DIM) // SSM_CH
    st_group = jnp.arange(ST_PER_KT) // SSM_STATE

    def pack_b(m):
        rows = jnp.transpose(m.reshape(N_KT, gpt, SSM_STATE, SSM_CH), (0, 1, 3, 2)).reshape(N_KT, MXU_DIM, SSM_STATE)
        keep = ch_group[:, None] == st_group[None, :]
        return jnp.where(keep[None], jnp.tile(rows, (1, 1, gpt)), 0.0).astype(BF16)

    def pack_c(m):
        cols = jnp.transpose(m.astype(f32).reshape(N_KT, gpt, SSM_CH, SSM_STATE), (0, 3, 1, 2))
        cols = cols.reshape(N_KT, SSM_STATE, MXU_DIM)
        keep = st_group[:, None] == ch_group[None, :]
        return jnp.where(keep[None], jnp.tile(cols, (1, gpt, 1)), 0.0).astype(BF16)

    wts = (pack_b(bbr), pack_b(bbi), pack_c(c_re), pack_c(-c_im.astype(f32)))
    return wts, ar.reshape(1, N_STATE), ai.reshape(1, N_STATE)


def _scan_tables(ar, ai, nb):
    assert SUBLANES // nb == 2
    a2r, a2i = ar * ar - ai * ai, 2.0 * ar * ai
    z = jnp.zeros_like(ar)
    rep = lambda first, second: jnp.concatenate([jnp.tile(first, (nb, 1)), jnp.tile(second, (nb, 1))], 0)
    return rep(z, ar), rep(z, ai), rep(ar, a2r), rep(ai, a2i)


def kernel(x_prompt, x_sample, cache_k_win, cache_v_win, state_ssm_re, state_ssm_im, c_prompt, c_sample, ada_w, ada_b,
           w_in, attn_sinks, ssm_lambda_re, ssm_lambda_im, ssm_log_step, ssm_b_re, ssm_b_im, ssm_c_re, ssm_c_im, ssm_d,
           ssm_glu_w, ssm_glu_b, attn_norm_g, ssm_norm_g, w_out, ln1_g, ln1_b, router_group_w, router_group_b,
           router_expert_w, router_expert_b, exp_w_gate, exp_w_up, exp_w_down, ln2_g, ln2_b):
    assert ada_w.shape[0] == DEPTH
    bp, lp, _ = x_prompt.shape
    bs, ls, _ = x_sample.shape
    lsp = SAMPLE_PAD

    w_in_bf = w_in[0].astype(BF16)
    wr = jnp.concatenate([router_group_w[0], router_expert_w[0]], -1)
    wr = jnp.pad(wr, ((0, 0), (0, LANES - wr.shape[1]))).astype(BF16)
    br = jnp.pad(jnp.concatenate([router_group_b[0], router_expert_b[0]], -1), (0, LANES - N_EXPERT_GROUPS - N_EXPERTS))
    wpost = dict(glu_w=ssm_glu_w[0].astype(BF16), glu_b=ssm_glu_b[0][None], attn_g=attn_norm_g[0][None],
                 ssm_g=ssm_norm_g[0][None], w_out=w_out[0].astype(BF16), ln1_g=ln1_g[0][None], ln1_b=ln1_b[0][None],
                 wr=wr, br=br[None])
    s5w, ar, ai = _s5_params(ssm_lambda_re[0], ssm_lambda_im[0], ssm_log_step[0], ssm_b_re[0], ssm_b_im[0],
                             ssm_c_re[0], ssm_c_im[0])
    d_skip = ssm_d[0][None]
    sinks = attn_sinks[0]

    n_c = bp + bs
    n_cp = -(-n_c // SUBLANES) * SUBLANES
    c_all = jnp.pad(jnp.concatenate([c_prompt, c_sample], 0), ((0, n_cp - n_c), (0, 0)))
    mod = _ada(c_all, ada_w[0], ada_b[0][None])
    mod_p = mod[:bp][:, None, :]
    mod_s = jnp.repeat(mod[bp:n_c], lsp, axis=0)[None]

    xs_pad = jnp.pad(x_sample, ((0, 0), (0, lsp - ls), (0, 0))).reshape(1, bs * lsp, D_MODEL)

    rope_p = _rope_tables(jnp.arange(lp))
    pos_s = PAST_LEN + jnp.minimum(jnp.arange(lsp), ls - 1)
    rope_s = tuple(jnp.tile(t, (bs, 1)) for t in _rope_tables(pos_s))
    q_p, k_p, v_p, u_p = _inproj(x_prompt, mod_p, w_in_bf, rope_p, False)
    q_s, k_s, v_s, u_s = _inproj(xs_pad, mod_s, w_in_bf, rope_s, True)

    attn_p = _attn_prompt(sinks, q_p, k_p, v_p)
    k_s3 = k_s.reshape(bs, lsp, D_KV)
    v_s3 = v_s.reshape(bs, lsp, D_KV)
    ck = cache_k_win[0].reshape(bs, WINDOW, D_KV)
    cv = cache_v_win[0].reshape(bs, WINDOW, D_KV)
    padk = lambda new: jnp.pad(new, ((0, 0), (0, WINDOW - lsp), (0, 0)))
    attn_s = _attn_sample(sinks, q_s.reshape(bs, lsp, D_ATTN), ck, padk(k_s3), cv, padk(v_s3))

    tabs = _scan_tables(ar, ai, bp)
    zero_carry = jnp.zeros((SUBLANES, N_STATE), F32)
    y_p, hr_p, hi_p = _s5_prompt(u_p, s5w, tabs, d_skip, zero_carry, zero_carry)
    u_tb = jnp.transpose(u_s.reshape(bs, lsp, D_SSM)[:, :ls], (1, 0, 2))
    y_tb, hr_s, hi_s = _s5_sample(u_tb, s5w, ar, ai, d_skip, state_ssm_re[0].reshape(bs, N_STATE),
                                  state_ssm_im[0].reshape(bs, N_STATE))
    y_s = jnp.pad(jnp.transpose(y_tb, (1, 0, 2)), ((0, 0), (0, lsp - ls), (0, 0))).reshape(1, bs * lsp, D_SSM)

    cnt0 = jnp.zeros((SUBLANES, LANES), F32)
    n_p, n_s = bp * lp, bs * lsp
    n_tok = n_p + n_s
    x1_s, h2_s, route_s, cnt_s = _post(attn_s.reshape(1, bs * lsp, D_ATTN), y_s, xs_pad, mod_s, True, wpost, cnt0)
    x1_p, h2, route_p, cnt = _post(attn_p, y_p, x_prompt, mod_p, False, wpost, cnt_s, tail=h2_s)

    n_tiles = -(-(2 * n_tok + N_EXPERTS * (TM_MOE - 1)) // TM_MOE) + MOE_LOOKAHEAD
    route = jnp.concatenate([route_p.reshape(n_p, LANES), route_s.reshape(n_s, LANES)], 0)
    counts = cnt[0, ROUTE_LANE0:ROUTE_LANE0 + N_EXPERTS].astype(jnp.int32)
    padded = ((counts + TM_MOE - 1) // TM_MOE) * TM_MOE
    ends = jnp.cumsum(padded)
    offs = ends - padded
    col = lambda c: route[:, c].astype(jnp.int32)
    expert_ids = jnp.arange(N_EXPERTS, dtype=jnp.int32)
    seg_start = lambda e: jnp.sum(jnp.where(e[:, None] == expert_ids[None, :], offs[None, :], 0), axis=1)
    pos_k = [seg_start(col(k)) + col(4 + k) for k in range(2)]
    n_valid = ends[-1] // TM_MOE
    tile_row = jnp.minimum(jnp.arange(n_tiles), n_valid - 1) * TM_MOE
    tile_expert = jnp.sum((ends[None, :] <= tile_row[:, None]).astype(jnp.int32), axis=1)
    first_tile = jnp.concatenate([jnp.ones((1,), jnp.int32), (tile_expert[1:] != tile_expert[:-1]).astype(jnp.int32)])
    first_tile = first_tile * (jnp.arange(n_tiles) < n_valid)
    nonempty = padded > 0
    later = jnp.where(nonempty[None, :] & (expert_ids[None, :] > expert_ids[:, None]), expert_ids[None, :], N_EXPERTS)
    next_expert = jnp.min(later, axis=1)
    next_expert = jnp.where(next_expert < N_EXPERTS, next_expert, -1).astype(jnp.int32)
    weight_slot = ((jnp.cumsum(nonempty.astype(jnp.int32)) - 1) & 1).astype(jnp.int32)
    tile_meta = (tile_expert, first_tile.astype(jnp.int32), next_expert[tile_expert], weight_slot[tile_expert])
    token_ids = jnp.arange(n_tok, dtype=jnp.int32)
    row_token = (jnp.arange(n_tiles * TM_MOE, dtype=jnp.int32) % n_tok).at[jnp.concatenate(pos_k)].set(
        jnp.concatenate([token_ids, token_ids]), unique_indices=True, mode="promise_in_bounds")

    ys = _moe(tile_meta, n_valid.reshape(1).astype(jnp.int32), row_token, h2, exp_w_gate[0], exp_w_up[0],
              exp_w_down[0])
    pos_tiles = jnp.concatenate([p.reshape(n_tok // TM, 1, TM) for p in pos_k], axis=2)
    y_prompt = _final(pos_tiles, 0, x1_p, mod_p, False, route_p, ln2_g[0][None], ln2_b[0][None], ys)
    y_samp = _final(pos_tiles, n_p // TM, x1_s, mod_s, True, route_s, ln2_g[0][None], ln2_b[0][None], ys)
    y_sample = y_samp.reshape(bs, lsp, D_MODEL)[:, :ls]

    kv5 = lambda a, n: a.reshape(1, a.shape[0], n, N_KV_HEADS, HEAD_DIM)
    k_win_p = kv5(k_p[:, lp - WINDOW:], WINDOW)
    v_win_p = kv5(v_p[:, lp - WINDOW:], WINDOW)
    k_win_s = kv5(jnp.concatenate([ck[:, ls:], k_s3[:, :ls]], 1), WINDOW)
    v_win_s = kv5(jnp.concatenate([cv[:, ls:], v_s3[:, :ls]], 1), WINDOW)
    st = lambda a: a.reshape(1, a.shape[0], N_SSM_GROUPS, SSM_STATE)
    return (y_prompt, y_sample, k_win_p, v_win_p, st(hr_p[bp:2 * bp]), st(hi_p[bp:2 * bp]),
            k_win_s, v_win_s, st(hr_s), st(hi_s))
```

```python
import functools

import jax
import jax.numpy as jnp
from jax import lax
from jax.experimental import pallas as pl
from jax.experimental.pallas import tpu as pltpu

F32 = jnp.float32
BF16 = jnp.bfloat16

D_MODEL = 2048
D_ATTN = 1024
D_SSM = 1024
HEAD_DIM = 64
N_HEADS = 16
N_KV_HEADS = 4
REP = 4
D_KV = 256
ROT_DIM = 16
ROPE_THETA = 500000.0
WINDOW = 128
SSM_CH = 16
N_SSM_GROUPS = 64
SSM_STATE = 64
N_STATE = N_SSM_GROUPS * SSM_STATE
PROJ_COLS = D_ATTN + 2 * D_KV + D_SSM
N_EXPERT_GROUPS = 4
EXPERTS_PER_GROUP = 8
N_EXPERTS = 32
D_EXPERT = 512
DEPTH = 1
DEEPNORM_ALPHA = (2.0 * DEPTH) ** 0.25
LN_EPS = 1e-5
PAST_LEN = 16384

LANES = 128
SUBLANES = 8
MXU_DIM = 256
TM = 256
TM_MOE = 256
TM_INPROJ = 512
S5_TT = 128
SAMPLE_PAD = 8
ROUTE_LANE0 = N_EXPERT_GROUPS
VMEM_LIMIT = 56 * 1024 * 1024


def _cparams(sem):
    return pltpu.CompilerParams(dimension_semantics=sem, vmem_limit_bytes=VMEM_LIMIT)


def _ln(x):
    mu = jnp.mean(x, axis=-1, keepdims=True)
    xc = x - mu
    var = jnp.mean(xc * xc, axis=-1, keepdims=True)
    return xc * lax.rsqrt(var + LN_EPS)


def _rms(x):
    return x * lax.rsqrt(jnp.mean(x * x, axis=-1, keepdims=True) + LN_EPS)


def _ada_kernel(c_ref, w_ref, b_ref, o_ref):
    c = c_ref[...]
    s = c * jax.nn.sigmoid(c)
    o_ref[...] = jnp.dot(s.astype(BF16), w_ref[...].astype(BF16), preferred_element_type=F32) + b_ref[...]


def _ada(c_all, ada_w, ada_b):
    n, tn = c_all.shape[0], 1024
    return pl.pallas_call(
        _ada_kernel,
        out_shape=jax.ShapeDtypeStruct((n, 6 * D_MODEL), F32),
        grid=(6 * D_MODEL // tn,),
        in_specs=[pl.BlockSpec((n, D_MODEL), lambda j: (0, 0)),
                  pl.BlockSpec((D_MODEL, tn), lambda j: (0, j)),
                  pl.BlockSpec((1, tn), lambda j: (0, j))],
        out_specs=pl.BlockSpec((n, tn), lambda j: (0, j)),
        compiler_params=_cparams(("arbitrary",)),
        name="ada",
    )(c_all, ada_w, ada_b)


def _mod_spec(per_row, tm, chunk):
    if per_row:
        return pl.BlockSpec((None, tm, D_MODEL), lambda b, i: (b, i, chunk))
    return pl.BlockSpec((None, 1, D_MODEL), lambda b, i: (b, 0, chunk))


def _inproj_kernel(x_ref, sh_ref, sc_ref, w_ref, rc_ref, ra_ref, rb_ref, q_ref, k_ref, v_ref, u_ref):
    h = _ln(x_ref[...]) * (1.0 + sc_ref[...]) + sh_ref[...]
    proj = jnp.dot(h.astype(BF16), w_ref[...], preferred_element_type=F32)
    rc, ra, rb = rc_ref[...], ra_ref[...], rb_ref[...]

    def rope(t):
        return t * rc + pltpu.roll(t, LANES - ROT_DIM // 2, 1) * ra + pltpu.roll(t, ROT_DIM // 2, 1) * rb

    for j in range(D_ATTN // LANES):
        q_ref[:, j * LANES:(j + 1) * LANES] = (rope(proj[:, j * LANES:(j + 1) * LANES]) * HEAD_DIM ** -0.5).astype(BF16)
    for j in range(D_KV // LANES):
        c0 = D_ATTN + j * LANES
        k_ref[:, j * LANES:(j + 1) * LANES] = rope(proj[:, c0:c0 + LANES])
    v_ref[...] = proj[:, D_ATTN + D_KV:D_ATTN + 2 * D_KV]
    u_ref[...] = proj[:, D_ATTN + 2 * D_KV:]


def _inproj(x, mod, w_in_bf, rope_tabs, per_row):
    nb, l, _ = x.shape
    tm = min(TM_INPROJ, l)
    row = lambda w: pl.BlockSpec((None, tm, w), lambda b, i: (b, i, 0))
    tab = pl.BlockSpec((tm, LANES), lambda b, i: (i, 0))
    return pl.pallas_call(
        _inproj_kernel,
        out_shape=(jax.ShapeDtypeStruct((nb, l, D_ATTN), BF16), jax.ShapeDtypeStruct((nb, l, D_KV), F32),
                   jax.ShapeDtypeStruct((nb, l, D_KV), F32), jax.ShapeDtypeStruct((nb, l, D_SSM), F32)),
        grid=(nb, l // tm),
        in_specs=[row(D_MODEL), _mod_spec(per_row, tm, 0), _mod_spec(per_row, tm, 1),
                  pl.BlockSpec((D_MODEL, PROJ_COLS), lambda b, i: (0, 0)), tab, tab, tab],
        out_specs=(row(D_ATTN), row(D_KV), row(D_KV), row(D_SSM)),
        compiler_params=_cparams(("arbitrary", "arbitrary")),
        name="inproj",
    )(x, mod, mod, w_in_bf, *rope_tabs)


ATTN_SAMPLE_BATCH = 8


def _attn_kernel(sink_ref, q_ref, kp_ref, kc_ref, vp_ref, vc_ref, o_ref, *, lq):
    m_rows = REP * lq
    ii = lax.broadcasted_iota(jnp.int32, (m_rows, WINDOW), 0) & (lq - 1)
    jj = lax.broadcasted_iota(jnp.int32, (m_rows, WINDOW), 1)
    from_prev = jj > ii
    dead = jj > ii + jnp.where(pl.program_id(1) > 0, WINDOW, 0)
    rr = lax.broadcasted_iota(jnp.int32, (m_rows, 1), 0)
    dn = (((1,), (1,)), ((), ()))
    for bi in range(q_ref.shape[0]):
        for g in range(N_KV_HEADS):
            qg = q_ref[bi, :, g * REP * HEAD_DIM:(g + 1) * REP * HEAD_DIM].astype(F32)
            qs = jnp.concatenate([qg[:, r * HEAD_DIM:(r + 1) * HEAD_DIM] for r in range(REP)], axis=0).astype(BF16)
            ks = slice(g * HEAD_DIM, (g + 1) * HEAD_DIM)
            s_p = lax.dot_general(qs, kp_ref[bi, :, ks].astype(BF16), dn, preferred_element_type=F32)
            s_c = lax.dot_general(qs, kc_ref[bi, :, ks].astype(BF16), dn, preferred_element_type=F32)
            s = jnp.where(dead, -jnp.inf, jnp.where(from_prev, s_p, s_c))
            sink = jnp.zeros((m_rows, 1), F32)
            for r in range(REP):
                sink = jnp.where((rr >= r * lq) & (rr < (r + 1) * lq), sink_ref[g * REP + r], sink)
            m = jnp.maximum(jnp.max(s, axis=-1, keepdims=True), sink)
            p = jnp.exp(s - m)
            p = p / (jnp.sum(p, axis=-1, keepdims=True) + jnp.exp(sink - m))
            o = (jnp.dot(jnp.where(from_prev, p, 0.0).astype(BF16), vp_ref[bi, :, ks].astype(BF16),
                         preferred_element_type=F32)
                 + jnp.dot(jnp.where(from_prev, 0.0, p).astype(BF16), vc_ref[bi, :, ks].astype(BF16),
                           preferred_element_type=F32))
            for r in range(REP):
                h = g * REP + r
                o_ref[bi, :, h * HEAD_DIM:(h + 1) * HEAD_DIM] = o[r * lq:(r + 1) * lq]


def _attn_prompt(sinks, q, k, v):
    nb, l, _ = q.shape
    nblk = l // WINDOW
    cur = lambda w: pl.BlockSpec((1, WINDOW, w), lambda b, n: (b, n, 0))
    prev = lambda w: pl.BlockSpec((1, WINDOW, w), lambda b, n: (b, jnp.maximum(n - 1, 0), 0))
    return pl.pallas_call(
        functools.partial(_attn_kernel, lq=WINDOW),
        out_shape=jax.ShapeDtypeStruct((nb, l, D_ATTN), F32),
        grid=(nb, nblk),
        in_specs=[pl.BlockSpec(memory_space=pltpu.SMEM), cur(D_ATTN), prev(D_KV), cur(D_KV), prev(D_KV), cur(D_KV)],
        out_specs=cur(D_ATTN),
        compiler_params=_cparams(("arbitrary", "arbitrary")),
        name="attn_prompt",
    )(sinks, q, k, k, v, v)


ATTN_SAMPLE_JOINT = 4


def _attn_sample_kernel(sink_ref, q_ref, kp_ref, kc_ref, vp_ref, vc_ref, o_ref, *, lq):
    gb = ATTN_SAMPLE_JOINT
    m_b = REP * lq
    m, n = gb * m_b, gb * WINDOW
    row = lax.broadcasted_iota(jnp.int32, (m, n), 0)
    col = lax.broadcasted_iota(jnp.int32, (m, n), 1)
    same = (row // m_b) == (col // WINDOW)
    from_prev = (col & (WINDOW - 1)) > (row & (lq - 1))
    rep_of_row = (lax.broadcasted_iota(jnp.int32, (m, 1), 0) // lq) % REP
    dn = (((1,), (1,)), ((), ()))
    for blk in range(q_ref.shape[0] // gb):
        bs = range(blk * gb, (blk + 1) * gb)
        for g in range(N_KV_HEADS):
            qg = [q_ref[b, :, g * REP * HEAD_DIM:(g + 1) * REP * HEAD_DIM].astype(F32) for b in bs]
            qs = jnp.concatenate([qb[:, r * HEAD_DIM:(r + 1) * HEAD_DIM] for qb in qg for r in range(REP)],
                                 axis=0).astype(BF16)
            ks = slice(g * HEAD_DIM, (g + 1) * HEAD_DIM)
            stack = lambda ref: jnp.concatenate([ref[b, :, ks] for b in bs], axis=0).astype(BF16)
            s_p = lax.dot_general(qs, stack(kp_ref), dn, preferred_element_type=F32)
            s_c = lax.dot_general(qs, stack(kc_ref), dn, preferred_element_type=F32)
            s = jnp.where(same, jnp.where(from_prev, s_p, s_c), -jnp.inf)
            sink = jnp.zeros((m, 1), F32)
            for r in range(REP):
                sink = jnp.where(rep_of_row == r, sink_ref[g * REP + r], sink)
            mx = jnp.maximum(jnp.max(s, axis=-1, keepdims=True), sink)
            p = jnp.exp(s - mx)
            p = p / (jnp.sum(p, axis=-1, keepdims=True) + jnp.exp(sink - mx))
            o = (jnp.dot(jnp.where(from_prev, p, 0.0).astype(BF16), stack(vp_ref), preferred_element_type=F32)
                 + jnp.dot(jnp.where(from_prev, 0.0, p).astype(BF16), stack(vc_ref), preferred_element_type=F32))
            for bi, b in enumerate(bs):
                for r in range(REP):
                    h = g * REP + r
                    o_ref[b, :, h * HEAD_DIM:(h + 1) * HEAD_DIM] = o[bi * m_b + r * lq:bi * m_b + (r + 1) * lq]


def _attn_sample(sinks, q, k_cache, k_new, v_cache, v_new):
    nb, lq, _ = q.shape
    nbb = ATTN_SAMPLE_BATCH
    kblk = pl.BlockSpec((nbb, WINDOW, D_KV), lambda b: (b, 0, 0))
    return pl.pallas_call(
        functools.partial(_attn_sample_kernel, lq=lq),
        out_shape=jax.ShapeDtypeStruct((nb, lq, D_ATTN), F32),
        grid=(nb // nbb,),
        in_specs=[pl.BlockSpec(memory_space=pltpu.SMEM), pl.BlockSpec((nbb, lq, D_ATTN), lambda b: (b, 0, 0)),
                  kblk, kblk, kblk, kblk],
        out_specs=pl.BlockSpec((nbb, lq, D_ATTN), lambda b: (b, 0, 0)),
        compiler_params=_cparams(("arbitrary",)),
        name="attn_sample",
    )(sinks, q, k_cache, k_new, v_cache, v_new)


N_KT = D_SSM // MXU_DIM
ST_PER_KT = N_STATE // N_KT
N_SLAB = D_SSM // LANES


def _cproj(hr_ref, hi_ref, cre_ref, cimn_ref, kt):
    cs = slice(kt * ST_PER_KT, (kt + 1) * ST_PER_KT)
    return (jnp.dot(hr_ref[:, cs].astype(BF16), cre_ref[kt], preferred_element_type=F32)
            + jnp.dot(hi_ref[:, cs].astype(BF16), cimn_ref[kt], preferred_element_type=F32))


def _s5_prompt_kernel(u_ref, bre_ref, bim_ref, cre_ref, cimn_ref, a1r_ref, a1i_ref, par_ref, pai_ref, d_ref,
                      h0r_ref, h0i_ref, y_ref, cr_ref, ci_ref, il_ref, xr_ref, xi_ref, *, nb, tt):
    rows = nb * tt

    @pl.when(pl.program_id(0) == 0)
    def _():
        cr_ref[...] = h0r_ref[...]
        ci_ref[...] = h0i_ref[...]

    for b in range(nb):
        for j in range(N_SLAB):
            il_ref[j, pl.ds(b, tt, stride=nb), :] = u_ref[b, :, j * LANES:(j + 1) * LANES]
    for kt in range(N_KT):
        ub = jnp.concatenate([il_ref[2 * kt], il_ref[2 * kt + 1]], axis=1).astype(BF16)
        cs = slice(kt * ST_PER_KT, (kt + 1) * ST_PER_KT)
        xr_ref[:, cs] = jnp.dot(ub, bre_ref[kt], preferred_element_type=F32)
        xi_ref[:, cs] = jnp.dot(ub, bim_ref[kt], preferred_element_type=F32)

    half = lax.broadcasted_iota(jnp.int32, (SUBLANES, LANES), 0) < nb

    def step(i, carry):
        r0 = pl.multiple_of(i * SUBLANES, SUBLANES)
        for j in range(N_STATE // LANES):
            ls = slice(j * LANES, (j + 1) * LANES)
            x_r = xr_ref[pl.ds(r0, SUBLANES), ls]
            x_i = xi_ref[pl.ds(r0, SUBLANES), ls]
            s_r = pltpu.roll(x_r, nb, 0)
            s_i = pltpu.roll(x_i, nb, 0)
            a1r, a1i = a1r_ref[:, ls], a1i_ref[:, ls]
            c_r, c_i = cr_ref[:, ls], ci_ref[:, ls]
            par, pai = par_ref[:, ls], pai_ref[:, ls]
            h_r = x_r + (a1r * s_r - a1i * s_i) + (par * c_r - pai * c_i)
            h_i = x_i + (a1r * s_i + a1i * s_r) + (par * c_i + pai * c_r)
            xr_ref[pl.ds(r0, SUBLANES), ls] = h_r
            xi_ref[pl.ds(r0, SUBLANES), ls] = h_i
            cr_ref[:, ls] = jnp.where(half, pltpu.roll(h_r, nb, 0), h_r)
            ci_ref[:, ls] = jnp.where(half, pltpu.roll(h_i, nb, 0), h_i)
        return carry

    lax.fori_loop(0, rows // SUBLANES, step, 0)

    for kt in range(N_KT):
        y = _cproj(xr_ref, xi_ref, cre_ref, cimn_ref, kt)
        il_ref[2 * kt] = y[:, :LANES]
        il_ref[2 * kt + 1] = y[:, LANES:]
    for b in range(nb):
        for j in range(N_SLAB):
            ls = slice(j * LANES, (j + 1) * LANES)
            y_ref[b, :, ls] = il_ref[j, pl.ds(b, tt, stride=nb), :] + d_ref[:, ls] * u_ref[b, :, ls]


def _s5_prompt(u, wts, tabs, d_skip, h0r, h0i):
    nb, l, _ = u.shape
    tt = S5_TT
    rows = nb * tt
    full = lambda a: pl.BlockSpec(a.shape, lambda i: (0,) * a.ndim)
    blk = pl.BlockSpec((nb, tt, D_SSM), lambda i: (0, i, 0))
    carry = jax.ShapeDtypeStruct((SUBLANES, N_STATE), F32)
    return pl.pallas_call(
        functools.partial(_s5_prompt_kernel, nb=nb, tt=tt),
        out_shape=(jax.ShapeDtypeStruct((nb, l, D_SSM), F32), carry, carry),
        grid=(l // tt,),
        in_specs=[blk] + [full(a) for a in (*wts, *tabs, d_skip, h0r, h0i)],
        out_specs=(blk, pl.BlockSpec((SUBLANES, N_STATE), lambda i: (0, 0)),
                   pl.BlockSpec((SUBLANES, N_STATE), lambda i: (0, 0))),
        scratch_shapes=[pltpu.VMEM((N_SLAB, rows, LANES), F32), pltpu.VMEM((rows, N_STATE), F32),
                        pltpu.VMEM((rows, N_STATE), F32)],
        compiler_params=_cparams(("arbitrary",)),
        name="s5_prompt",
    )(u, *wts, *tabs, d_skip, h0r, h0i)


def _s5_sample_kernel(u_ref, bre_ref, bim_ref, cre_ref, cimn_ref, ar_ref, ai_ref, d_ref, h0r_ref, h0i_ref,
                      y_ref, sr_ref, si_ref, xr_ref, xi_ref, *, nt):
    sr_ref[...] = h0r_ref[...]
    si_ref[...] = h0i_ref[...]
    for t in range(nt):
        for kt in range(N_KT):
            ub = u_ref[t, :, kt * MXU_DIM:(kt + 1) * MXU_DIM].astype(BF16)
            cs = slice(kt * ST_PER_KT, (kt + 1) * ST_PER_KT)
            xr_ref[:, cs] = jnp.dot(ub, bre_ref[kt], preferred_element_type=F32)
            xi_ref[:, cs] = jnp.dot(ub, bim_ref[kt], preferred_element_type=F32)
        ar, ai = ar_ref[...], ai_ref[...]
        s_r, s_i = sr_ref[...], si_ref[...]
        sr_ref[...] = xr_ref[...] + (ar * s_r - ai * s_i)
        si_ref[...] = xi_ref[...] + (ar * s_i + ai * s_r)
        for kt in range(N_KT):
            ys = slice(kt * MXU_DIM, (kt + 1) * MXU_DIM)
            y_ref[t, :, ys] = _cproj(sr_ref, si_ref, cre_ref, cimn_ref, kt) + d_ref[:, ys] * u_ref[t, :, ys]


def _s5_sample(u_tb, wts, ar, ai, d_skip, h0r, h0i):
    nt, nb, _ = u_tb.shape
    st = jax.ShapeDtypeStruct((nb, N_STATE), F32)
    args = (u_tb, *wts, ar, ai, d_skip, h0r, h0i)
    full = lambda a: pl.BlockSpec(a.shape, lambda i: (0,) * a.ndim)
    return pl.pallas_call(
        functools.partial(_s5_sample_kernel, nt=nt),
        out_shape=(jax.ShapeDtypeStruct((nt, nb, D_SSM), F32), st, st),
        grid=(1,),
        in_specs=[full(a) for a in args],
        out_specs=(pl.BlockSpec((nt, nb, D_SSM), lambda i: (0, 0, 0)), pl.BlockSpec((nb, N_STATE), lambda i: (0, 0)),
                   pl.BlockSpec((nb, N_STATE), lambda i: (0, 0))),
        scratch_shapes=[pltpu.VMEM((nb, N_STATE), F32), pltpu.VMEM((nb, N_STATE), F32)],
        compiler_params=_cparams(("arbitrary",)),
        name="s5_sample",
    )(*args)


def _gather_group(src_ref, ids_ref, id_stride, id_offset, dst_ref, sem, priorities, g):
    for j in range(SUBLANES):
        t = ids_ref[0, id_stride * (g * SUBLANES + j) + id_offset]
        pltpu.make_async_copy(src_ref.at[t >> 3, pl.ds(t & (SUBLANES - 1), 1)], dst_ref.at[g, pl.ds(j, 1)],
                              sem).start(priority=priorities[j % len(priorities)])


def _gather_rows(src_ref, ids_ref, id_stride, id_offset, dst_ref, sem, priorities):
    def body(g, c):
        _gather_group(src_ref, ids_ref, id_stride, id_offset, dst_ref, sem, priorities, g)
        return c
    lax.fori_loop(0, dst_ref.shape[0], body, 0)


def _index_specs(n_cols, first_tile, n_steps, depth=1):
    def spec(fn):
        return pl.BlockSpec((None, 1, n_cols), fn, memory_space=pltpu.SMEM)

    head = [spec(lambda s, *_, k=k: (first_tile + min(k, n_steps - 1), 0, 0)) for k in range(depth)]
    return (*head, spec(lambda s, *_: (first_tile + jnp.minimum(s + depth, n_steps - 1), 0, 0)))


N_POST_INPUTS = 17


def _post_kernel(*refs, has_tail):
    ins, outs = refs[:N_POST_INPUTS], refs[N_POST_INPUTS + int(has_tail):]
    if not has_tail:
        _post_body(*ins, *outs)
        return
    tail_ref, h2_ref = refs[N_POST_INPUTS], outs[1]
    last = pl.num_programs(0) - 1

    @pl.when(pl.program_id(0) == last)
    def _():
        h2_ref[...] = tail_ref[...]

    @pl.when(pl.program_id(0) < last)
    def _():
        _post_body(*ins, *outs)


def _post_body(attn_ref, yssm_ref, x_ref, g1_ref, sh2_ref, sc2_ref, gluw_ref, glub_ref, ga_ref, gs_ref, wout_ref,
               l1g_ref, l1b_ref, wr_ref, br_ref, tri_ref, cnt0_ref, x1_ref, h2_ref, route_ref, cnt_ref):
    tm = x_ref.shape[0]

    @pl.when(pl.program_id(0) == 0)
    def _():
        cnt_ref[...] = cnt0_ref[...]

    z = jax.nn.gelu(yssm_ref[...])
    ssm = z * jax.nn.sigmoid(jnp.dot(z.astype(BF16), gluw_ref[...], preferred_element_type=F32) + glub_ref[...])
    mixed_a = (_rms(attn_ref[...]) * ga_ref[...]).astype(BF16)
    mixed_s = (_rms(ssm) * gs_ref[...]).astype(BF16)
    o = (jnp.dot(mixed_a, wout_ref[:D_ATTN, :], preferred_element_type=F32)
         + jnp.dot(mixed_s, wout_ref[D_ATTN:, :], preferred_element_type=F32))
    x1 = _ln(DEEPNORM_ALPHA * x_ref[...] + g1_ref[...] * o) * l1g_ref[...] + l1b_ref[...]
    x1_ref[...] = x1
    h2 = _ln(x1) * (1.0 + sc2_ref[...]) + sh2_ref[...]
    h2_ref[...] = h2
    logits = jnp.dot(h2.astype(BF16), wr_ref[...], preferred_element_type=F32) + br_ref[...]

    lane = lax.broadcasted_iota(jnp.int32, (tm, LANES), 1).astype(F32)
    big = float(4 * LANES)
    neg = -jnp.inf
    gl = jnp.where(lane < N_EXPERT_GROUPS, logits, neg)
    gp = jnp.exp(gl - jnp.max(gl, axis=-1, keepdims=True))
    gp = gp / jnp.sum(gp, axis=-1, keepdims=True)
    g_val = jnp.max(gp, axis=-1, keepdims=True)
    g_idx = jnp.min(jnp.where(gp == g_val, lane, big), axis=-1, keepdims=True)
    lo = ROUTE_LANE0 + EXPERTS_PER_GROUP * g_idx
    emask = (lane >= lo) & (lane < lo + EXPERTS_PER_GROUP)
    el = jnp.where(emask, logits, neg)
    ep = jnp.exp(el - jnp.max(el, axis=-1, keepdims=True))
    ep = jnp.where(emask, ep / jnp.sum(ep, axis=-1, keepdims=True), -1.0)
    v1 = jnp.max(ep, axis=-1, keepdims=True)
    i1 = jnp.min(jnp.where(ep == v1, lane, big), axis=-1, keepdims=True)
    ep2 = jnp.where(lane == i1, -1.0, ep)
    v2 = jnp.max(ep2, axis=-1, keepdims=True)
    i2 = jnp.min(jnp.where(ep2 == v2, lane, big), axis=-1, keepdims=True)
    vs = v1 + v2
    w1 = g_val * (v1 / vs)
    w2 = g_val * (v2 / vs)
    hit = jnp.where((lane == i1) | (lane == i2), 1.0, 0.0)
    before = jnp.dot(tri_ref[...], hit.astype(BF16), preferred_element_type=F32) + cnt_ref[0:1, :]
    r1 = jnp.sum(jnp.where(lane == i1, before, 0.0), axis=-1, keepdims=True)
    r2 = jnp.sum(jnp.where(lane == i2, before, 0.0), axis=-1, keepdims=True)
    cnt_ref[...] = cnt_ref[...] + jnp.sum(hit, axis=0, keepdims=True)
    e1 = i1 - ROUTE_LANE0
    e2 = i2 - ROUTE_LANE0
    route = jnp.zeros((tm, LANES), F32)
    for n, val in enumerate((e1, e2, w1, w2, r1, r2)):
        route = jnp.where(lane == n, val, route)
    route_ref[...] = route


def _post(attn, yssm, x, mod, per_row, w, cnt0, tail=None):
    nb, l, _ = x.shape
    tm = min(TM, l)
    nt = l // tm
    n_body = nb * nt
    has_tail = tail is not None
    if has_tail:
        assert tail.shape == (tm, D_MODEL)

    def bi(s):
        s = jnp.minimum(s, n_body - 1)
        return s // nt, s % nt

    row = lambda wd: pl.BlockSpec((None, tm, wd), lambda s: (*bi(s), 0))
    if per_row:
        mspec = lambda chunk: pl.BlockSpec((None, tm, D_MODEL), lambda s: (*bi(s), chunk))
    else:
        mspec = lambda chunk: pl.BlockSpec((None, 1, D_MODEL), lambda s: (bi(s)[0], 0, chunk))
    full = lambda a: pl.BlockSpec(a.shape, lambda s: (0,) * a.ndim)
    tri = jnp.tril(jnp.ones((tm, tm), F32), -1).astype(BF16)
    consts = (w["glu_w"], w["glu_b"], w["attn_g"], w["ssm_g"], w["w_out"], w["ln1_g"], w["ln1_b"], w["wr"], w["br"],
              tri, cnt0) + ((tail,) if has_tail else ())
    assert 6 + len(consts) == N_POST_INPUTS + int(has_tail)
    n_steps = n_body + int(has_tail)
    return pl.pallas_call(
        functools.partial(_post_kernel, has_tail=has_tail),
        out_shape=(jax.ShapeDtypeStruct((nb, l, D_MODEL), F32),
                   jax.ShapeDtypeStruct((n_steps * tm, D_MODEL), F32),
                   jax.ShapeDtypeStruct((nb, l, LANES), F32), jax.ShapeDtypeStruct((SUBLANES, LANES), F32)),
        grid=(n_steps,),
        in_specs=[row(D_ATTN), row(D_SSM), row(D_MODEL), mspec(2), mspec(3), mspec(4)] + [full(a) for a in consts],
        out_specs=(row(D_MODEL), pl.BlockSpec((tm, D_MODEL), lambda s: (s, 0)),
                   row(LANES), pl.BlockSpec((SUBLANES, LANES), lambda s: (0, 0))),
        compiler_params=_cparams(("arbitrary",)),
        name="post",
    )(attn, yssm, x, mod, mod, mod, *consts)


MOE_LOOKAHEAD = 2
MOE_SLOTS = MOE_LOOKAHEAD + 1


def _moe_kernel(te_ref, nv_ref, first_ref, nxt_ref, par_ref, rt0_ref, rt1_ref, rtn_ref, h2_ref, wg_hbm, wu_hbm, wd_hbm,
                y_ref, buf_ref, wgf_ref, wuf_ref, wdf_ref, sem, wsem):
    i = pl.program_id(0)
    nv = nv_ref[0]
    slot = lax.rem(i, MOE_SLOTS)
    w_pairs = ((wg_hbm, wgf_ref), (wu_hbm, wuf_ref), (wd_hbm, wdf_ref))

    def gather(ids_ref, dst_slot):
        _gather_rows(h2_ref, ids_ref, 1, 0, buf_ref.at[dst_slot], sem.at[dst_slot], (0,))

    def fetch_weights(e, ws):
        for src, dst in w_pairs:
            pltpu.make_async_copy(src.at[e], dst.at[ws], wsem.at[ws]).start(priority=1)

    @pl.when(i == 0)
    def _():
        gather(rt0_ref, 0)
        gather(rt1_ref, 1)
        fetch_weights(te_ref[0], par_ref[0])

    @pl.when(i < nv + MOE_LOOKAHEAD)
    def _():
        pltpu.make_async_copy(h2_ref.at[pl.ds(0, TM_MOE // SUBLANES)], buf_ref.at[slot], sem.at[slot]).wait()

    @pl.when((i < nv) & (first_ref[i] == 1))
    def _():
        ws = par_ref[i]
        for src, dst in w_pairs:
            pltpu.make_async_copy(src.at[0], dst.at[ws], wsem.at[ws]).wait()

        @pl.when(nxt_ref[i] >= 0)
        def _():
            fetch_weights(nxt_ref[i], 1 - ws)

    @pl.when(i < nv)
    def _():
        nslot = lax.rem(i + MOE_LOOKAHEAD, MOE_SLOTS)
        ws = par_ref[i]
        n_kc = D_MODEL // MXU_DIM
        groups_per_kc = TM_MOE // SUBLANES // n_kc
        hg = hu = None
        for kc in range(n_kc):
            for gg in range(groups_per_kc):
                _gather_group(h2_ref, rtn_ref, 1, 0, buf_ref.at[nslot], sem.at[nslot], (0,), kc * groups_per_kc + gg)
            ks = slice(kc * MXU_DIM, (kc + 1) * MXU_DIM)
            xk = buf_ref[slot, :, :, ks].reshape(TM_MOE, MXU_DIM).astype(BF16)
            pg = jnp.dot(xk, wgf_ref[ws, ks, :].astype(BF16), preferred_element_type=F32)
            pu = jnp.dot(xk, wuf_ref[ws, ks, :].astype(BF16), preferred_element_type=F32)
            hg = pg if hg is None else hg + pg
            hu = pu if hu is None else hu + pu
        act = (hg * jax.nn.sigmoid(hg)) * hu
        y_ref[...] = jnp.dot(act.astype(BF16), wdf_ref[ws].astype(BF16), preferred_element_type=F32)

    @pl.when(i >= nv)
    def _():
        y_ref[...] = jnp.zeros(y_ref.shape, y_ref.dtype)


def _moe(tile_meta, n_valid, row_token, h2, w_gate, w_up, w_down):
    n_steps = row_token.shape[0] // TM_MOE
    rt = row_token.reshape(n_steps, 1, TM_MOE)
    te, first, nxt, par = tile_meta
    any_spec = pl.BlockSpec(memory_space=pl.ANY)
    return pl.pallas_call(
        _moe_kernel,
        out_shape=jax.ShapeDtypeStruct((n_steps * TM_MOE, D_MODEL), F32),
        grid_spec=pltpu.PrefetchScalarGridSpec(
            num_scalar_prefetch=5, grid=(n_steps,),
            in_specs=[*_index_specs(TM_MOE, 0, n_steps, MOE_LOOKAHEAD), any_spec, any_spec, any_spec, any_spec],
            out_specs=pl.BlockSpec((TM_MOE, D_MODEL), lambda i, *_: (i, 0)),
            scratch_shapes=[pltpu.VMEM((MOE_SLOTS, TM_MOE // SUBLANES, SUBLANES, D_MODEL), F32),
                            pltpu.VMEM((2, D_MODEL, D_EXPERT), F32), pltpu.VMEM((2, D_MODEL, D_EXPERT), F32),
                            pltpu.VMEM((2, D_EXPERT, D_MODEL), F32),
                            pltpu.SemaphoreType.DMA((MOE_SLOTS,)), pltpu.SemaphoreType.DMA((2,))]),
        compiler_params=_cparams(("arbitrary",)),
        name="moe",
    )(te, n_valid, first, nxt, par, rt, rt, rt, h2.reshape(-1, SUBLANES, D_MODEL), w_gate, w_up, w_down)


FINAL_NORM_SHARE = 2


def _final_kernel(pos0_ref, posn_ref, x1_ref, g2_ref, route_ref, l2g_ref, l2b_ref, ys_ref, o_ref, buf_ref, sem, *, tm):
    step = pl.program_id(0)
    slot = step & 1

    def gather(ids_ref, dst_slot):
        for k in range(2):
            _gather_rows(ys_ref, ids_ref, 1, k * tm, buf_ref.at[dst_slot, k], sem.at[dst_slot], (0, 1))

    @pl.when(step == 0)
    def _():
        gather(pos0_ref, 0)

    for k in range(2):
        pltpu.make_async_copy(ys_ref.at[pl.ds(0, tm // SUBLANES)], buf_ref.at[slot, k], sem.at[slot]).wait()

    route = route_ref[...]
    w1, w2 = route[:, 2:3], route[:, 3:4]
    n_groups = tm // SUBLANES

    def combine(issue_next):
        n_chunks = D_MODEL // MXU_DIM
        per_part = 2 * n_groups // (n_chunks + FINAL_NORM_SHARE)

        def issue(lo, hi):
            if issue_next:
                for q in range(lo, hi):
                    k, g = q // n_groups, q % n_groups
                    _gather_group(ys_ref, posn_ref, 1, k * tm, buf_ref.at[1 - slot, k], sem.at[1 - slot], (0, 1), g)

        for c in range(n_chunks):
            issue(c * per_part, (c + 1) * per_part)
            cs = slice(c * MXU_DIM, (c + 1) * MXU_DIM)
            f = (w1 * buf_ref[slot, 0, :, :, cs].reshape(tm, MXU_DIM)
                 + w2 * buf_ref[slot, 1, :, :, cs].reshape(tm, MXU_DIM))
            o_ref[:, cs] = DEEPNORM_ALPHA * x1_ref[:, cs] + g2_ref[:, cs] * f
        issue(n_chunks * per_part, 2 * n_groups)
        o_ref[...] = _ln(o_ref[...]) * l2g_ref[...] + l2b_ref[...]

    @pl.when(step + 1 < pl.num_programs(0))
    def _():
        combine(True)

    @pl.when(step + 1 == pl.num_programs(0))
    def _():
        combine(False)


def _final(pos_tiles, first_tile, x1, mod, per_row, route, ln2_g, ln2_b, ys):
    nb, l, _ = x1.shape
    tm = min(TM, l)
    nt = l // tm
    n_steps = nb * nt
    row = lambda wd: pl.BlockSpec((None, tm, wd), lambda s: (s // nt, s % nt, 0))
    if per_row:
        g2 = pl.BlockSpec((None, tm, D_MODEL), lambda s: (s // nt, s % nt, 5))
    else:
        g2 = pl.BlockSpec((None, 1, D_MODEL), lambda s: (s // nt, 0, 5))
    vec = pl.BlockSpec((1, D_MODEL), lambda s: (0, 0))
    return pl.pallas_call(
        functools.partial(_final_kernel, tm=tm),
        out_shape=jax.ShapeDtypeStruct((nb, l, D_MODEL), F32),
        grid=(n_steps,),
        in_specs=[*_index_specs(2 * tm, first_tile, n_steps), row(D_MODEL), g2, row(LANES), vec, vec,
                  pl.BlockSpec(memory_space=pl.ANY)],
        out_specs=row(D_MODEL),
        scratch_shapes=[pltpu.VMEM((2, 2, tm // SUBLANES, SUBLANES, D_MODEL), F32), pltpu.SemaphoreType.DMA((2,))],
        compiler_params=_cparams(("arbitrary",)),
        name="final",
    )(pos_tiles, pos_tiles, x1, mod, route, ln2_g, ln2_b, ys.reshape(-1, SUBLANES, D_MODEL))


def _rope_tables(pos):
    half = ROT_DIM // 2
    inv_freq = ROPE_THETA ** (-jnp.arange(half, dtype=jnp.float32) * 2.0 / ROT_DIM)
    ang = pos.astype(jnp.float32)[:, None] * inv_freq[None, :]
    cos, sin = jnp.cos(ang), jnp.sin(ang)
    n = pos.shape[0]
    one = jnp.ones((n, HEAD_DIM - ROT_DIM), F32)
    zero = jnp.zeros((n, HEAD_DIM - half), F32)
    c = jnp.concatenate([cos, cos, one], -1)
    a = jnp.concatenate([-sin, zero], -1)
    b = jnp.concatenate([jnp.zeros((n, half), F32), sin, jnp.zeros((n, HEAD_DIM - ROT_DIM), F32)], -1)
    return tuple(jnp.tile(t, (1, LANES // HEAD_DIM)) for t in (c, a, b))


def _s5_params(lam_re, lam_im, log_step, b_re, b_im, c_re, c_im):
    f32 = jnp.float32
    dt = jnp.exp(log_step.astype(f32))[:, None]
    lr, li = lam_re.astype(f32), lam_im.astype(f32)
    mag = jnp.exp(lr * dt)
    ar, ai = mag * jnp.cos(li * dt), mag * jnp.sin(li * dt)
    den = lr * lr + li * li
    cr = ((ar - 1.0) * lr + ai * li) / den
    ci = (ai * lr - (ar - 1.0) * li) / den
    br, bi = b_re.astype(f32), b_im.astype(f32)
    bbr = cr[..., None] * br - ci[..., None] * bi
    bbi = cr[..., None] * bi + ci[..., None] * br
    gpt = MXU_DIM // SSM_CH
    ch_group = jnp.arange(MXU_DIM) // SSM_CH
    st_group = jnp.arange(ST_PER_KT) // SSM_STATE

    def pack_b(m):
        rows = jnp.transpose(m.reshape(N_KT, gpt, SSM_STATE, SSM_CH), (0, 1, 3, 2)).reshape(N_KT, MXU_DIM, SSM_STATE)
        keep = ch_group[:, None] == st_group[None, :]
        return jnp.where(keep[None], jnp.tile(rows, (1, 1, gpt)), 0.0).astype(BF16)

    def pack_c(m):
        cols = jnp.transpose(m.astype(f32).reshape(N_KT, gpt, SSM_CH, SSM_STATE), (0, 3, 1, 2))
        cols = cols.reshape(N_KT, SSM_STATE, MXU_DIM)
        keep = st_group[:, None] == ch_group[None, :]
        return jnp.where(keep[None], jnp.tile(cols, (1, gpt, 1)), 0.0).astype(BF16)

    wts = (pack_b(bbr), pack_b(bbi), pack_c(c_re), pack_c(-c_im.astype(f32)))
    return wts, ar.reshape(1, N_STATE), ai.reshape(1, N_STATE)


def _scan_tables(ar, ai, nb):
    assert SUBLANES // nb == 2
    a2r, a2i = ar * ar - ai * ai, 2.0 * ar * ai
    z = jnp.zeros_like(ar)
    rep = lambda first, second: jnp.concatenate([jnp.tile(first, (nb, 1)), jnp.tile(second, (nb, 1))], 0)
    return rep(z, ar), rep(z, ai), rep(ar, a2r), rep(ai, a2i)


def kernel(x_prompt, x_sample, cache_k_win, cache_v_win, state_ssm_re, state_ssm_im, c_prompt, c_sample, ada_w, ada_b,
           w_in, attn_sinks, ssm_lambda_re, ssm_lambda_im, ssm_log_step, ssm_b_re, ssm_b_im, ssm_c_re, ssm_c_im, ssm_d,
           ssm_glu_w, ssm_glu_b, attn_norm_g, ssm_norm_g, w_out, ln1_g, ln1_b, router_group_w, router_group_b,
           router_expert_w, router_expert_b, exp_w_gate, exp_w_up, exp_w_down, ln2_g, ln2_b):
    assert ada_w.shape[0] == DEPTH
    bp, lp, _ = x_prompt.shape
    bs, ls, _ = x_sample.shape
    lsp = SAMPLE_PAD

    w_in_bf = w_in[0].astype(BF16)
    wr = jnp.concatenate([router_group_w[0], router_expert_w[0]], -1)
    wr = jnp.pad(wr, ((0, 0), (0, LANES - wr.shape[1]))).astype(BF16)
    br = jnp.pad(jnp.concatenate([router_group_b[0], router_expert_b[0]], -1), (0, LANES - N_EXPERT_GROUPS - N_EXPERTS))
    wpost = dict(glu_w=ssm_glu_w[0].astype(BF16), glu_b=ssm_glu_b[0][None], attn_g=attn_norm_g[0][None],
                 ssm_g=ssm_norm_g[0][None], w_out=w_out[0].astype(BF16), ln1_g=ln1_g[0][None], ln1_b=ln1_b[0][None],
                 wr=wr, br=br[None])
    s5w, ar, ai = _s5_params(ssm_lambda_re[0], ssm_lambda_im[0], ssm_log_step[0], ssm_b_re[0], ssm_b_im[0],
                             ssm_c_re[0], ssm_c_im[0])
    d_skip = ssm_d[0][None]
    sinks = attn_sinks[0]

    n_c = bp + bs
    n_cp = -(-n_c // SUBLANES) * SUBLANES
    c_all = jnp.pad(jnp.concatenate([c_prompt, c_sample], 0), ((0, n_cp - n_c), (0, 0)))
    mod = _ada(c_all, ada_w[0], ada_b[0][None])
    mod_p = mod[:bp][:, None, :]
    mod_s = jnp.repeat(mod[bp:n_c], lsp, axis=0)[None]

    xs_pad = jnp.pad(x_sample, ((0, 0), (0, lsp - ls), (0, 0))).reshape(1, bs * lsp, D_MODEL)

    rope_p = _rope_tables(jnp.arange(lp))
    pos_s = PAST_LEN + jnp.minimum(jnp.arange(lsp), ls - 1)
    rope_s = tuple(jnp.tile(t, (bs, 1)) for t in _rope_tables(pos_s))
    q_p, k_p, v_p, u_p = _inproj(x_prompt, mod_p, w_in_bf, rope_p, False)
    q_s, k_s, v_s, u_s = _inproj(xs_pad, mod_s, w_in_bf, rope_s, True)

    attn_p = _attn_prompt(sinks, q_p, k_p, v_p)
    k_s3 = k_s.reshape(bs, lsp, D_KV)
    v_s3 = v_s.reshape(bs, lsp, D_KV)
    ck = cache_k_win[0].reshape(bs, WINDOW, D_KV)
    cv = cache_v_win[0].reshape(bs, WINDOW, D_KV)
    padk = lambda new: jnp.pad(new, ((0, 0), (0, WINDOW - lsp), (0, 0)))
    attn_s = _attn_sample(sinks, q_s.reshape(bs, lsp, D_ATTN), ck, padk(k_s3), cv, padk(v_s3))

    tabs = _scan_tables(ar, ai, bp)
    zero_carry = jnp.zeros((SUBLANES, N_STATE), F32)
    y_p, hr_p, hi_p = _s5_prompt(u_p, s5w, tabs, d_skip, zero_carry, zero_carry)
    u_tb = jnp.transpose(u_s.reshape(bs, lsp, D_SSM)[:, :ls], (1, 0, 2))
    y_tb, hr_s, hi_s = _s5_sample(u_tb, s5w, ar, ai, d_skip, state_ssm_re[0].reshape(bs, N_STATE),
                                  state_ssm_im[0].reshape(bs, N_STATE))
    y_s = jnp.pad(jnp.transpose(y_tb, (1, 0, 2)), ((0, 0), (0, lsp - ls), (0, 0))).reshape(1, bs * lsp, D_SSM)

    cnt0 = jnp.zeros((SUBLANES, LANES), F32)
    n_p, n_s = bp * lp, bs * lsp
    n_tok = n_p + n_s
    x1_s, h2_s, route_s, cnt_s = _post(attn_s.reshape(1, bs * lsp, D_ATTN), y_s, xs_pad, mod_s, True, wpost, cnt0)
    x1_p, h2, route_p, cnt = _post(attn_p, y_p, x_prompt, mod_p, False, wpost, cnt_s, tail=h2_s)

    n_tiles = -(-(2 * n_tok + N_EXPERTS * (TM_MOE - 1)) // TM_MOE) + MOE_LOOKAHEAD
    route = jnp.concatenate([route_p.reshape(n_p, LANES), route_s.reshape(n_s, LANES)], 0)
    counts = cnt[0, ROUTE_LANE0:ROUTE_LANE0 + N_EXPERTS].astype(jnp.int32)
    padded = ((counts + TM_MOE - 1) // TM_MOE) * TM_MOE
    ends = jnp.cumsum(padded)
    offs = ends - padded
    col = lambda c: route[:, c].astype(jnp.int32)
    expert_ids = jnp.arange(N_EXPERTS, dtype=jnp.int32)
    seg_start = lambda e: jnp.sum(jnp.where(e[:, None] == expert_ids[None, :], offs[None, :], 0), axis=1)
    pos_k = [seg_start(col(k)) + col(4 + k) for k in range(2)]
    n_valid = ends[-1] // TM_MOE
    tile_row = jnp.minimum(jnp.arange(n_tiles), n_valid - 1) * TM_MOE
    tile_expert = jnp.sum((ends[None, :] <= tile_row[:, None]).astype(jnp.int32), axis=1)
    first_tile = jnp.concatenate([jnp.ones((1,), jnp.int32), (tile_expert[1:] != tile_expert[:-1]).astype(jnp.int32)])
    first_tile = first_tile * (jnp.arange(n_tiles) < n_valid)
    nonempty = padded > 0
    later = jnp.where(nonempty[None, :] & (expert_ids[None, :] > expert_ids[:, None]), expert_ids[None, :], N_EXPERTS)
    next_expert = jnp.min(later, axis=1)
    next_expert = jnp.where(next_expert < N_EXPERTS, next_expert, -1).astype(jnp.int32)
    weight_slot = ((jnp.cumsum(nonempty.astype(jnp.int32)) - 1) & 1).astype(jnp.int32)
    tile_meta = (tile_expert, first_tile.astype(jnp.int32), next_expert[tile_expert], weight_slot[tile_expert])
    token_ids = jnp.arange(n_tok, dtype=jnp.int32)
    row_token = (jnp.arange(n_tiles * TM_MOE, dtype=jnp.int32) % n_tok).at[jnp.concatenate(pos_k)].set(
        jnp.concatenate([token_ids, token_ids]), unique_indices=True, mode="promise_in_bounds")

    ys = _moe(tile_meta, n_valid.reshape(1).astype(jnp.int32), row_token, h2, exp_w_gate[0], exp_w_up[0],
              exp_w_down[0])
    pos_tiles = jnp.concatenate([p.reshape(n_tok // TM, 1, TM) for p in pos_k], axis=2)
    y_prompt = _final(pos_tiles, 0, x1_p, mod_p, False, route_p, ln2_g[0][None], ln2_b[0][None], ys)
    y_samp = _final(pos_tiles, n_p // TM, x1_s, mod_s, True, route_s, ln2_g[0][None], ln2_b[0][None], ys)
    y_sample = y_samp.reshape(bs, lsp, D_MODEL)[:, :ls]

    kv5 = lambda a, n: a.reshape(1, a.shape[0], n, N_KV_HEADS, HEAD_DIM)
    k_win_p = kv5(k_p[:, lp - WINDOW:], WINDOW)
    v_win_p = kv5(v_p[:, lp - WINDOW:], WINDOW)
    k_win_s = kv5(jnp.concatenate([ck[:, ls:], k_s3[:, :ls]], 1), WINDOW)
    v_win_s = kv5(jnp.concatenate([cv[:, ls:], v_s3[:, :ls]], 1), WINDOW)
    st = lambda a: a.reshape(1, a.shape[0], N_SSM_GROUPS, SSM_STATE)
    return (y_prompt, y_sample, k_win_p, v_win_p, st(hr_p[bp:2 * bp]), st(hi_p[bp:2 * bp]),
            k_win_s, v_win_s, st(hr_s), st(hi_s))
```

```python
import functools

import jax
import jax.numpy as jnp
from jax import lax
from jax.experimental import pallas as pl
from jax.experimental.pallas import tpu as pltpu

F32 = jnp.float32
BF16 = jnp.bfloat16

D_MODEL = 2048
D_ATTN = 1024
D_SSM = 1024
HEAD_DIM = 64
N_HEADS = 16
N_KV_HEADS = 4
REP = 4
D_KV = 256
ROT_DIM = 16
ROPE_THETA = 500000.0
WINDOW = 128
SSM_CH = 16
N_SSM_GROUPS = 64
SSM_STATE = 64
N_STATE = N_SSM_GROUPS * SSM_STATE
PROJ_COLS = D_ATTN + 2 * D_KV + D_SSM
N_EXPERT_GROUPS = 4
EXPERTS_PER_GROUP = 8
N_EXPERTS = 32
D_EXPERT = 512
DEPTH = 1
DEEPNORM_ALPHA = (2.0 * DEPTH) ** 0.25
LN_EPS = 1e-5
PAST_LEN = 16384

LANES = 128
SUBLANES = 8
MXU_DIM = 256
TM = 256
TM_MOE = 256
TM_INPROJ = 512
S5_TT = 128
SAMPLE_PAD = 8
ROUTE_LANE0 = N_EXPERT_GROUPS
VMEM_LIMIT = 56 * 1024 * 1024


def _cparams(sem):
    return pltpu.CompilerParams(dimension_semantics=sem, vmem_limit_bytes=VMEM_LIMIT)


def _ln(x):
    mu = jnp.mean(x, axis=-1, keepdims=True)
    xc = x - mu
    var = jnp.mean(xc * xc, axis=-1, keepdims=True)
    return xc * lax.rsqrt(var + LN_EPS)


def _rms(x):
    return x * lax.rsqrt(jnp.mean(x * x, axis=-1, keepdims=True) + LN_EPS)


def _ada_kernel(c_ref, w_ref, b_ref, o_ref):
    c = c_ref[...]
    s = c * jax.nn.sigmoid(c)
    o_ref[...] = jnp.dot(s.astype(BF16), w_ref[...].astype(BF16), preferred_element_type=F32) + b_ref[...]


def _ada(c_all, ada_w, ada_b):
    n, tn = c_all.shape[0], 1024
    return pl.pallas_call(
        _ada_kernel,
        out_shape=jax.ShapeDtypeStruct((n, 6 * D_MODEL), F32),
        grid=(6 * D_MODEL // tn,),
        in_specs=[pl.BlockSpec((n, D_MODEL), lambda j: (0, 0)),
                  pl.BlockSpec((D_MODEL, tn), lambda j: (0, j)),
                  pl.BlockSpec((1, tn), lambda j: (0, j))],
        out_specs=pl.BlockSpec((n, tn), lambda j: (0, j)),
        compiler_params=_cparams(("arbitrary",)),
        name="ada",
    )(c_all, ada_w, ada_b)


def _mod_spec(per_row, tm, chunk):
    if per_row:
        return pl.BlockSpec((None, tm, D_MODEL), lambda b, i: (b, i, chunk))
    return pl.BlockSpec((None, 1, D_MODEL), lambda b, i: (b, 0, chunk))


def _inproj_kernel(x_ref, sh_ref, sc_ref, w_ref, rc_ref, ra_ref, rb_ref, q_ref, k_ref, v_ref, u_ref):
    h = _ln(x_ref[...]) * (1.0 + sc_ref[...]) + sh_ref[...]
    proj = jnp.dot(h.astype(BF16), w_ref[...], preferred_element_type=F32)
    rc, ra, rb = rc_ref[...], ra_ref[...], rb_ref[...]

    def rope(t):
        return t * rc + pltpu.roll(t, LANES - ROT_DIM // 2, 1) * ra + pltpu.roll(t, ROT_DIM // 2, 1) * rb

    for j in range(D_ATTN // LANES):
        q_ref[:, j * LANES:(j + 1) * LANES] = (rope(proj[:, j * LANES:(j + 1) * LANES]) * HEAD_DIM ** -0.5).astype(BF16)
    for j in range(D_KV // LANES):
        c0 = D_ATTN + j * LANES
        k_ref[:, j * LANES:(j + 1) * LANES] = rope(proj[:, c0:c0 + LANES])
    v_ref[...] = proj[:, D_ATTN + D_KV:D_ATTN + 2 * D_KV]
    u_ref[...] = proj[:, D_ATTN + 2 * D_KV:]


def _inproj(x, mod, w_in_bf, rope_tabs, per_row):
    nb, l, _ = x.shape
    tm = min(TM_INPROJ, l)
    row = lambda w: pl.BlockSpec((None, tm, w), lambda b, i: (b, i, 0))
    tab = pl.BlockSpec((tm, LANES), lambda b, i: (i, 0))
    return pl.pallas_call(
        _inproj_kernel,
        out_shape=(jax.ShapeDtypeStruct((nb, l, D_ATTN), BF16), jax.ShapeDtypeStruct((nb, l, D_KV), F32),
                   jax.ShapeDtypeStruct((nb, l, D_KV), F32), jax.ShapeDtypeStruct((nb, l, D_SSM), F32)),
        grid=(nb, l // tm),
        in_specs=[row(D_MODEL), _mod_spec(per_row, tm, 0), _mod_spec(per_row, tm, 1),
                  pl.BlockSpec((D_MODEL, PROJ_COLS), lambda b, i: (0, 0)), tab, tab, tab],
        out_specs=(row(D_ATTN), row(D_KV), row(D_KV), row(D_SSM)),
        compiler_params=_cparams(("arbitrary", "arbitrary")),
        name="inproj",
    )(x, mod, mod, w_in_bf, *rope_tabs)


ATTN_SAMPLE_BATCH = 8


def _attn_kernel(sink_ref, q_ref, kp_ref, kc_ref, vp_ref, vc_ref, o_ref, *, lq):
    m_rows = REP * lq
    ii = lax.broadcasted_iota(jnp.int32, (m_rows, WINDOW), 0) & (lq - 1)
    jj = lax.broadcasted_iota(jnp.int32, (m_rows, WINDOW), 1)
    from_prev = jj > ii
    dead = jj > ii + jnp.where(pl.program_id(1) > 0, WINDOW, 0)
    rr = lax.broadcasted_iota(jnp.int32, (m_rows, 1), 0)
    dn = (((1,), (1,)), ((), ()))
    for bi in range(q_ref.shape[0]):
        for g in range(N_KV_HEADS):
            qg = q_ref[bi, :, g * REP * HEAD_DIM:(g + 1) * REP * HEAD_DIM].astype(F32)
            qs = jnp.concatenate([qg[:, r * HEAD_DIM:(r + 1) * HEAD_DIM] for r in range(REP)], axis=0).astype(BF16)
            ks = slice(g * HEAD_DIM, (g + 1) * HEAD_DIM)
            s_p = lax.dot_general(qs, kp_ref[bi, :, ks].astype(BF16), dn, preferred_element_type=F32)
            s_c = lax.dot_general(qs, kc_ref[bi, :, ks].astype(BF16), dn, preferred_element_type=F32)
            s = jnp.where(dead, -jnp.inf, jnp.where(from_prev, s_p, s_c))
            sink = jnp.zeros((m_rows, 1), F32)
            for r in range(REP):
                sink = jnp.where((rr >= r * lq) & (rr < (r + 1) * lq), sink_ref[g * REP + r], sink)
            m = jnp.maximum(jnp.max(s, axis=-1, keepdims=True), sink)
            p = jnp.exp(s - m)
            p = p / (jnp.sum(p, axis=-1, keepdims=True) + jnp.exp(sink - m))
            o = (jnp.dot(jnp.where(from_prev, p, 0.0).astype(BF16), vp_ref[bi, :, ks].astype(BF16),
                         preferred_element_type=F32)
                 + jnp.dot(jnp.where(from_prev, 0.0, p).astype(BF16), vc_ref[bi, :, ks].astype(BF16),
                           preferred_element_type=F32))
            for r in range(REP):
                h = g * REP + r
                o_ref[bi, :, h * HEAD_DIM:(h + 1) * HEAD_DIM] = o[r * lq:(r + 1) * lq]


def _attn_prompt(sinks, q, k, v):
    nb, l, _ = q.shape
    nblk = l // WINDOW
    cur = lambda w: pl.BlockSpec((1, WINDOW, w), lambda b, n: (b, n, 0))
    prev = lambda w: pl.BlockSpec((1, WINDOW, w), lambda b, n: (b, jnp.maximum(n - 1, 0), 0))
    return pl.pallas_call(
        functools.partial(_attn_kernel, lq=WINDOW),
        out_shape=jax.ShapeDtypeStruct((nb, l, D_ATTN), F32),
        grid=(nb, nblk),
        in_specs=[pl.BlockSpec(memory_space=pltpu.SMEM), cur(D_ATTN), prev(D_KV), cur(D_KV), prev(D_KV), cur(D_KV)],
        out_specs=cur(D_ATTN),
        compiler_params=_cparams(("arbitrary", "arbitrary")),
        name="attn_prompt",
    )(sinks, q, k, k, v, v)


ATTN_SAMPLE_JOINT = 4


def _attn_sample_kernel(sink_ref, q_ref, kp_ref, kc_ref, vp_ref, vc_ref, o_ref, *, lq):
    gb = ATTN_SAMPLE_JOINT
    m_b = REP * lq
    m, n = gb * m_b, gb * WINDOW
    row = lax.broadcasted_iota(jnp.int32, (m, n), 0)
    col = lax.broadcasted_iota(jnp.int32, (m, n), 1)
    same = (row // m_b) == (col // WINDOW)
    from_prev = (col & (WINDOW - 1)) > (row & (lq - 1))
    rep_of_row = (lax.broadcasted_iota(jnp.int32, (m, 1), 0) // lq) % REP
    dn = (((1,), (1,)), ((), ()))
    for blk in range(q_ref.shape[0] // gb):
        bs = range(blk * gb, (blk + 1) * gb)
        for g in range(N_KV_HEADS):
            qg = [q_ref[b, :, g * REP * HEAD_DIM:(g + 1) * REP * HEAD_DIM].astype(F32) for b in bs]
            qs = jnp.concatenate([qb[:, r * HEAD_DIM:(r + 1) * HEAD_DIM] for qb in qg for r in range(REP)],
                                 axis=0).astype(BF16)
            ks = slice(g * HEAD_DIM, (g + 1) * HEAD_DIM)
            stack = lambda ref: jnp.concatenate([ref[b, :, ks] for b in bs], axis=0).astype(BF16)
            s_p = lax.dot_general(qs, stack(kp_ref), dn, preferred_element_type=F32)
            s_c = lax.dot_general(qs, stack(kc_ref), dn, preferred_element_type=F32)
            s = jnp.where(same, jnp.where(from_prev, s_p, s_c), -jnp.inf)
            sink = jnp.zeros((m, 1), F32)
            for r in range(REP):
                sink = jnp.where(rep_of_row == r, sink_ref[g * REP + r], sink)
            mx = jnp.maximum(jnp.max(s, axis=-1, keepdims=True), sink)
            p = jnp.exp(s - mx)
            p = p / (jnp.sum(p, axis=-1, keepdims=True) + jnp.exp(sink - mx))
            o = (jnp.dot(jnp.where(from_prev, p, 0.0).astype(BF16), stack(vp_ref), preferred_element_type=F32)
                 + jnp.dot(jnp.where(from_prev, 0.0, p).astype(BF16), stack(vc_ref), preferred_element_type=F32))
            for bi, b in enumerate(bs):
                for r in range(REP):
                    h = g * REP + r
                    o_ref[b, :, h * HEAD_DIM:(h + 1) * HEAD_DIM] = o[bi * m_b + r * lq:bi * m_b + (r + 1) * lq]


def _attn_sample(sinks, q, k_cache, k_new, v_cache, v_new):
    nb, lq, _ = q.shape
    nbb = ATTN_SAMPLE_BATCH
    kblk = pl.BlockSpec((nbb, WINDOW, D_KV), lambda b: (b, 0, 0))
    return pl.pallas_call(
        functools.partial(_attn_sample_kernel, lq=lq),
        out_shape=jax.ShapeDtypeStruct((nb, lq, D_ATTN), F32),
        grid=(nb // nbb,),
        in_specs=[pl.BlockSpec(memory_space=pltpu.SMEM), pl.BlockSpec((nbb, lq, D_ATTN), lambda b: (b, 0, 0)),
                  kblk, kblk, kblk, kblk],
        out_specs=pl.BlockSpec((nbb, lq, D_ATTN), lambda b: (b, 0, 0)),
        compiler_params=_cparams(("arbitrary",)),
        name="attn_sample",
    )(sinks, q, k_cache, k_new, v_cache, v_new)


N_KT = D_SSM // MXU_DIM
ST_PER_KT = N_STATE // N_KT
N_SLAB = D_SSM // LANES


def _cproj(hr_ref, hi_ref, cre_ref, cimn_ref, kt):
    cs = slice(kt * ST_PER_KT, (kt + 1) * ST_PER_KT)
    return (jnp.dot(hr_ref[:, cs].astype(BF16), cre_ref[kt], preferred_element_type=F32)
            + jnp.dot(hi_ref[:, cs].astype(BF16), cimn_ref[kt], preferred_element_type=F32))


def _s5_prompt_kernel(u_ref, bre_ref, bim_ref, cre_ref, cimn_ref, a1r_ref, a1i_ref, par_ref, pai_ref, d_ref,
                      h0r_ref, h0i_ref, y_ref, cr_ref, ci_ref, il_ref, xr_ref, xi_ref, *, nb, tt):
    rows = nb * tt

    @pl.when(pl.program_id(0) == 0)
    def _():
        cr_ref[...] = h0r_ref[...]
        ci_ref[...] = h0i_ref[...]

    for b in range(nb):
        for j in range(N_SLAB):
            il_ref[j, pl.ds(b, tt, stride=nb), :] = u_ref[b, :, j * LANES:(j + 1) * LANES]
    for kt in range(N_KT):
        ub = jnp.concatenate([il_ref[2 * kt], il_ref[2 * kt + 1]], axis=1).astype(BF16)
        cs = slice(kt * ST_PER_KT, (kt + 1) * ST_PER_KT)
        xr_ref[:, cs] = jnp.dot(ub, bre_ref[kt], preferred_element_type=F32)
        xi_ref[:, cs] = jnp.dot(ub, bim_ref[kt], preferred_element_type=F32)

    half = lax.broadcasted_iota(jnp.int32, (SUBLANES, LANES), 0) < nb

    def step(i, carry):
        r0 = pl.multiple_of(i * SUBLANES, SUBLANES)
        for j in range(N_STATE // LANES):
            ls = slice(j * LANES, (j + 1) * LANES)
            x_r = xr_ref[pl.ds(r0, SUBLANES), ls]
            x_i = xi_ref[pl.ds(r0, SUBLANES), ls]
            s_r = pltpu.roll(x_r, nb, 0)
            s_i = pltpu.roll(x_i, nb, 0)
            a1r, a1i = a1r_ref[:, ls], a1i_ref[:, ls]
            c_r, c_i = cr_ref[:, ls], ci_ref[:, ls]
            par, pai = par_ref[:, ls], pai_ref[:, ls]
            h_r = x_r + (a1r * s_r - a1i * s_i) + (par * c_r - pai * c_i)
            h_i = x_i + (a1r * s_i + a1i * s_r) + (par * c_i + pai * c_r)
            xr_ref[pl.ds(r0, SUBLANES), ls] = h_r
            xi_ref[pl.ds(r0, SUBLANES), ls] = h_i
            cr_ref[:, ls] = jnp.where(half, pltpu.roll(h_r, nb, 0), h_r)
            ci_ref[:, ls] = jnp.where(half, pltpu.roll(h_i, nb, 0), h_i)
        return carry

    lax.fori_loop(0, rows // SUBLANES, step, 0)

    for kt in range(N_KT):
        y = _cproj(xr_ref, xi_ref, cre_ref, cimn_ref, kt)
        il_ref[2 * kt] = y[:, :LANES]
        il_ref[2 * kt + 1] = y[:, LANES:]
    for b in range(nb):
        for j in range(N_SLAB):
            ls = slice(j * LANES, (j + 1) * LANES)
            y_ref[b, :, ls] = il_ref[j, pl.ds(b, tt, stride=nb), :] + d_ref[:, ls] * u_ref[b, :, ls]


def _s5_prompt(u, wts, tabs, d_skip, h0r, h0i):
    nb, l, _ = u.shape
    tt = S5_TT
    rows = nb * tt
    full = lambda a: pl.BlockSpec(a.shape, lambda i: (0,) * a.ndim)
    blk = pl.BlockSpec((nb, tt, D_SSM), lambda i: (0, i, 0))
    carry = jax.ShapeDtypeStruct((SUBLANES, N_STATE), F32)
    return pl.pallas_call(
        functools.partial(_s5_prompt_kernel, nb=nb, tt=tt),
        out_shape=(jax.ShapeDtypeStruct((nb, l, D_SSM), F32), carry, carry),
        grid=(l // tt,),
        in_specs=[blk] + [full(a) for a in (*wts, *tabs, d_skip, h0r, h0i)],
        out_specs=(blk, pl.BlockSpec((SUBLANES, N_STATE), lambda i: (0, 0)),
                   pl.BlockSpec((SUBLANES, N_STATE), lambda i: (0, 0))),
        scratch_shapes=[pltpu.VMEM((N_SLAB, rows, LANES), F32), pltpu.VMEM((rows, N_STATE), F32),
                        pltpu.VMEM((rows, N_STATE), F32)],
        compiler_params=_cparams(("arbitrary",)),
        name="s5_prompt",
    )(u, *wts, *tabs, d_skip, h0r, h0i)


def _s5_sample_kernel(u_ref, bre_ref, bim_ref, cre_ref, cimn_ref, ar_ref, ai_ref, d_ref, h0r_ref, h0i_ref,
                      y_ref, sr_ref, si_ref, xr_ref, xi_ref, *, nt):
    sr_ref[...] = h0r_ref[...]
    si_ref[...] = h0i_ref[...]
    for t in range(nt):
        for kt in range(N_KT):
            ub = u_ref[t, :, kt * MXU_DIM:(kt + 1) * MXU_DIM].astype(BF16)
            cs = slice(kt * ST_PER_KT, (kt + 1) * ST_PER_KT)
            xr_ref[:, cs] = jnp.dot(ub, bre_ref[kt], preferred_element_type=F32)
            xi_ref[:, cs] = jnp.dot(ub, bim_ref[kt], preferred_element_type=F32)
        ar, ai = ar_ref[...], ai_ref[...]
        s_r, s_i = sr_ref[...], si_ref[...]
        sr_ref[...] = xr_ref[...] + (ar * s_r - ai * s_i)
        si_ref[...] = xi_ref[...] + (ar * s_i + ai * s_r)
        for kt in range(N_KT):
            ys = slice(kt * MXU_DIM, (kt + 1) * MXU_DIM)
            y_ref[t, :, ys] = _cproj(sr_ref, si_ref, cre_ref, cimn_ref, kt) + d_ref[:, ys] * u_ref[t, :, ys]


def _s5_sample(u_tb, wts, ar, ai, d_skip, h0r, h0i):
    nt, nb, _ = u_tb.shape
    st = jax.ShapeDtypeStruct((nb, N_STATE), F32)
    args = (u_tb, *wts, ar, ai, d_skip, h0r, h0i)
    full = lambda a: pl.BlockSpec(a.shape, lambda i: (0,) * a.ndim)
    return pl.pallas_call(
        functools.partial(_s5_sample_kernel, nt=nt),
        out_shape=(jax.ShapeDtypeStruct((nt, nb, D_SSM), F32), st, st),
        grid=(1,),
        in_specs=[full(a) for a in args],
        out_specs=(pl.BlockSpec((nt, nb, D_SSM), lambda i: (0, 0, 0)), pl.BlockSpec((nb, N_STATE), lambda i: (0, 0)),
                   pl.BlockSpec((nb, N_STATE), lambda i: (0, 0))),
        scratch_shapes=[pltpu.VMEM((nb, N_STATE), F32), pltpu.VMEM((nb, N_STATE), F32)],
        compiler_params=_cparams(("arbitrary",)),
        name="s5_sample",
    )(*args)


def _gather_group(src_ref, ids_ref, id_stride, id_offset, dst_ref, sem, priorities, g):
    for j in range(SUBLANES):
        t = ids_ref[0, id_stride * (g * SUBLANES + j) + id_offset]
        pltpu.make_async_copy(src_ref.at[t >> 3, pl.ds(t & (SUBLANES - 1), 1)], dst_ref.at[g, pl.ds(j, 1)],
                              sem).start(priority=priorities[j % len(priorities)])


def _gather_rows(src_ref, ids_ref, id_stride, id_offset, dst_ref, sem, priorities):
    def body(g, c):
        _gather_group(src_ref, ids_ref, id_stride, id_offset, dst_ref, sem, priorities, g)
        return c
    lax.fori_loop(0, dst_ref.shape[0], body, 0)


def _index_specs(n_cols, first_tile, n_steps, depth=1):
    def spec(fn):
        return pl.BlockSpec((None, 1, n_cols), fn, memory_space=pltpu.SMEM)

    head = [spec(lambda s, *_, k=k: (first_tile + min(k, n_steps - 1), 0, 0)) for k in range(depth)]
    return (*head, spec(lambda s, *_: (first_tile + jnp.minimum(s + depth, n_steps - 1), 0, 0)))


N_POST_INPUTS = 17


def _post_kernel(*refs, has_tail):
    ins, outs = refs[:N_POST_INPUTS], refs[N_POST_INPUTS + int(has_tail):]
    if not has_tail:
        _post_body(*ins, *outs)
        return
    tail_ref, h2_ref = refs[N_POST_INPUTS], outs[1]
    last = pl.num_programs(0) - 1

    @pl.when(pl.program_id(0) == last)
    def _():
        h2_ref[...] = tail_ref[...]

    @pl.when(pl.program_id(0) < last)
    def _():
        _post_body(*ins, *outs)


def _post_body(attn_ref, yssm_ref, x_ref, g1_ref, sh2_ref, sc2_ref, gluw_ref, glub_ref, ga_ref, gs_ref, wout_ref,
               l1g_ref, l1b_ref, wr_ref, br_ref, tri_ref, cnt0_ref, x1_ref, h2_ref, route_ref, cnt_ref):
    tm = x_ref.shape[0]

    @pl.when(pl.program_id(0) == 0)
    def _():
        cnt_ref[...] = cnt0_ref[...]

    z = jax.nn.gelu(yssm_ref[...])
    ssm = z * jax.nn.sigmoid(jnp.dot(z.astype(BF16), gluw_ref[...], preferred_element_type=F32) + glub_ref[...])
    mixed_a = (_rms(attn_ref[...]) * ga_ref[...]).astype(BF16)
    mixed_s = (_rms(ssm) * gs_ref[...]).astype(BF16)
    o = (jnp.dot(mixed_a, wout_ref[:D_ATTN, :], preferred_element_type=F32)
         + jnp.dot(mixed_s, wout_ref[D_ATTN:, :], preferred_element_type=F32))
    x1 = _ln(DEEPNORM_ALPHA * x_ref[...] + g1_ref[...] * o) * l1g_ref[...] + l1b_ref[...]
    x1_ref[...] = x1
    h2 = _ln(x1) * (1.0 + sc2_ref[...]) + sh2_ref[...]
    h2_ref[...] = h2
    logits = jnp.dot(h2.astype(BF16), wr_ref[...], preferred_element_type=F32) + br_ref[...]

    lane = lax.broadcasted_iota(jnp.int32, (tm, LANES), 1).astype(F32)
    big = float(4 * LANES)
    neg = -jnp.inf
    gl = jnp.where(lane < N_EXPERT_GROUPS, logits, neg)
    gp = jnp.exp(gl - jnp.max(gl, axis=-1, keepdims=True))
    gp = gp / jnp.sum(gp, axis=-1, keepdims=True)
    g_val = jnp.max(gp, axis=-1, keepdims=True)
    g_idx = jnp.min(jnp.where(gp == g_val, lane, big), axis=-1, keepdims=True)
    lo = ROUTE_LANE0 + EXPERTS_PER_GROUP * g_idx
    emask = (lane >= lo) & (lane < lo + EXPERTS_PER_GROUP)
    el = jnp.where(emask, logits, neg)
    ep = jnp.exp(el - jnp.max(el, axis=-1, keepdims=True))
    ep = jnp.where(emask, ep / jnp.sum(ep, axis=-1, keepdims=True), -1.0)
    v1 = jnp.max(ep, axis=-1, keepdims=True)
    i1 = jnp.min(jnp.where(ep == v1, lane, big), axis=-1, keepdims=True)
    ep2 = jnp.where(lane == i1, -1.0, ep)
    v2 = jnp.max(ep2, axis=-1, keepdims=True)
    i2 = jnp.min(jnp.where(ep2 == v2, lane, big), axis=-1, keepdims=True)
    vs = v1 + v2
    w1 = g_val * (v1 / vs)
    w2 = g_val * (v2 / vs)
    hit = jnp.where((lane == i1) | (lane == i2), 1.0, 0.0)
    before = jnp.dot(tri_ref[...], hit.astype(BF16), preferred_element_type=F32) + cnt_ref[0:1, :]
    r1 = jnp.sum(jnp.where(lane == i1, before, 0.0), axis=-1, keepdims=True)
    r2 = jnp.sum(jnp.where(lane == i2, before, 0.0), axis=-1, keepdims=True)
    cnt_ref[...] = cnt_ref[...] + jnp.sum(hit, axis=0, keepdims=True)
    e1 = i1 - ROUTE_LANE0
    e2 = i2 - ROUTE_LANE0
    route = jnp.zeros((tm, LANES), F32)
    for n, val in enumerate((e1, e2, w1, w2, r1, r2)):
        route = jnp.where(lane == n, val, route)
    route_ref[...] = route


def _post(attn, yssm, x, mod, per_row, w, cnt0, tail=None):
    nb, l, _ = x.shape
    tm = min(TM, l)
    nt = l // tm
    n_body = nb * nt
    has_tail = tail is not None
    if has_tail:
        assert tail.shape == (tm, D_MODEL)

    def bi(s):
        s = jnp.minimum(s, n_body - 1)
        return s // nt, s % nt

    row = lambda wd: pl.BlockSpec((None, tm, wd), lambda s: (*bi(s), 0))
    if per_row:
        mspec = lambda chunk: pl.BlockSpec((None, tm, D_MODEL), lambda s: (*bi(s), chunk))
    else:
        mspec = lambda chunk: pl.BlockSpec((None, 1, D_MODEL), lambda s: (bi(s)[0], 0, chunk))
    full = lambda a: pl.BlockSpec(a.shape, lambda s: (0,) * a.ndim)
    tri = jnp.tril(jnp.ones((tm, tm), F32), -1).astype(BF16)
    consts = (w["glu_w"], w["glu_b"], w["attn_g"], w["ssm_g"], w["w_out"], w["ln1_g"], w["ln1_b"], w["wr"], w["br"],
              tri, cnt0) + ((tail,) if has_tail else ())
    assert 6 + len(consts) == N_POST_INPUTS + int(has_tail)
    n_steps = n_body + int(has_tail)
    return pl.pallas_call(
        functools.partial(_post_kernel, has_tail=has_tail),
        out_shape=(jax.ShapeDtypeStruct((nb, l, D_MODEL), F32),
                   jax.ShapeDtypeStruct((n_steps * tm, D_MODEL), F32),
                   jax.ShapeDtypeStruct((nb, l, LANES), F32), jax.ShapeDtypeStruct((SUBLANES, LANES), F32)),
        grid=(n_steps,),
        in_specs=[row(D_ATTN), row(D_SSM), row(D_MODEL), mspec(2), mspec(3), mspec(4)] + [full(a) for a in consts],
        out_specs=(row(D_MODEL), pl.BlockSpec((tm, D_MODEL), lambda s: (s, 0)),
                   row(LANES), pl.BlockSpec((SUBLANES, LANES), lambda s: (0, 0))),
        compiler_params=_cparams(("arbitrary",)),
        name="post",
    )(attn, yssm, x, mod, mod, mod, *consts)


MOE_LOOKAHEAD = 2
MOE_SLOTS = MOE_LOOKAHEAD + 1


def _moe_kernel(te_ref, nv_ref, first_ref, nxt_ref, par_ref, rt0_ref, rt1_ref, rtn_ref, h2_ref, wg_hbm, wu_hbm, wd_hbm,
                y_ref, buf_ref, wgf_ref, wuf_ref, wdf_ref, sem, wsem):
    i = pl.program_id(0)
    nv = nv_ref[0]
    slot = lax.rem(i, MOE_SLOTS)
    w_pairs = ((wg_hbm, wgf_ref), (wu_hbm, wuf_ref), (wd_hbm, wdf_ref))

    def gather(ids_ref, dst_slot):
        _gather_rows(h2_ref, ids_ref, 1, 0, buf_ref.at[dst_slot], sem.at[dst_slot], (0,))

    def fetch_weights(e, ws):
        for src, dst in w_pairs:
            pltpu.make_async_copy(src.at[e], dst.at[ws], wsem.at[ws]).start(priority=1)

    @pl.when(i == 0)
    def _():
        gather(rt0_ref, 0)
        gather(rt1_ref, 1)
        fetch_weights(te_ref[0], par_ref[0])

    @pl.when(i < nv + MOE_LOOKAHEAD)
    def _():
        pltpu.make_async_copy(h2_ref.at[pl.ds(0, TM_MOE // SUBLANES)], buf_ref.at[slot], sem.at[slot]).wait()

    @pl.when((i < nv) & (first_ref[i] == 1))
    def _():
        ws = par_ref[i]
        for src, dst in w_pairs:
            pltpu.make_async_copy(src.at[0], dst.at[ws], wsem.at[ws]).wait()

        @pl.when(nxt_ref[i] >= 0)
        def _():
            fetch_weights(nxt_ref[i], 1 - ws)

    @pl.when(i < nv)
    def _():
        nslot = lax.rem(i + MOE_LOOKAHEAD, MOE_SLOTS)
        ws = par_ref[i]
        n_kc = D_MODEL // MXU_DIM
        groups_per_kc = TM_MOE // SUBLANES // n_kc
        hg = hu = None
        for kc in range(n_kc):
            for gg in range(groups_per_kc):
                _gather_group(h2_ref, rtn_ref, 1, 0, buf_ref.at[nslot], sem.at[nslot], (0,), kc * groups_per_kc + gg)
            ks = slice(kc * MXU_DIM, (kc + 1) * MXU_DIM)
            xk = buf_ref[slot, :, :, ks].reshape(TM_MOE, MXU_DIM).astype(BF16)
            pg = jnp.dot(xk, wgf_ref[ws, ks, :].astype(BF16), preferred_element_type=F32)
            pu = jnp.dot(xk, wuf_ref[ws, ks, :].astype(BF16), preferred_element_type=F32)
            hg = pg if hg is None else hg + pg
            hu = pu if hu is None else hu + pu
        act = (hg * jax.nn.sigmoid(hg)) * hu
        y_ref[...] = jnp.dot(act.astype(BF16), wdf_ref[ws].astype(BF16), preferred_element_type=F32)

    @pl.when(i >= nv)
    def _():
        y_ref[...] = jnp.zeros(y_ref.shape, y_ref.dtype)


def _moe(tile_meta, n_valid, row_token, h2, w_gate, w_up, w_down):
    n_steps = row_token.shape[0] // TM_MOE
    rt = row_token.reshape(n_steps, 1, TM_MOE)
    te, first, nxt, par = tile_meta
    any_spec = pl.BlockSpec(memory_space=pl.ANY)
    return pl.pallas_call(
        _moe_kernel,
        out_shape=jax.ShapeDtypeStruct((n_steps * TM_MOE, D_MODEL), F32),
        grid_spec=pltpu.PrefetchScalarGridSpec(
            num_scalar_prefetch=5, grid=(n_steps,),
            in_specs=[*_index_specs(TM_MOE, 0, n_steps, MOE_LOOKAHEAD), any_spec, any_spec, any_spec, any_spec],
            out_specs=pl.BlockSpec((TM_MOE, D_MODEL), lambda i, *_: (i, 0)),
            scratch_shapes=[pltpu.VMEM((MOE_SLOTS, TM_MOE // SUBLANES, SUBLANES, D_MODEL), F32),
                            pltpu.VMEM((2, D_MODEL, D_EXPERT), F32), pltpu.VMEM((2, D_MODEL, D_EXPERT), F32),
                            pltpu.VMEM((2, D_EXPERT, D_MODEL), F32),
                            pltpu.SemaphoreType.DMA((MOE_SLOTS,)), pltpu.SemaphoreType.DMA((2,))]),
        compiler_params=_cparams(("arbitrary",)),
        name="moe",
    )(te, n_valid, first, nxt, par, rt, rt, rt, h2.reshape(-1, SUBLANES, D_MODEL), w_gate, w_up, w_down)


def _final_kernel(pos0_ref, posn_ref, x1_ref, g2_ref, route_ref, l2g_ref, l2b_ref, ys_ref, o_ref, buf_ref, sem, *, tm):
    step = pl.program_id(0)
    slot = step & 1

    def gather(ids_ref, dst_slot):
        for k in range(2):
            _gather_rows(ys_ref, ids_ref, 1, k * tm, buf_ref.at[dst_slot, k], sem.at[dst_slot], (0, 1))

    @pl.when(step == 0)
    def _():
        gather(pos0_ref, 0)

    for k in range(2):
        pltpu.make_async_copy(ys_ref.at[pl.ds(0, tm // SUBLANES)], buf_ref.at[slot, k], sem.at[slot]).wait()

    route = route_ref[...]
    w1, w2 = route[:, 2:3], route[:, 3:4]
    n_groups = tm // SUBLANES

    def combine(issue_next):
        n_chunks = D_MODEL // MXU_DIM
        issue_chunks = n_chunks // 2
        per_chunk = 2 * n_groups // issue_chunks
        for c in range(n_chunks):
            if issue_next and c < issue_chunks:
                for q in range(c * per_chunk, (c + 1) * per_chunk):
                    k, g = q // n_groups, q % n_groups
                    _gather_group(ys_ref, posn_ref, 1, k * tm, buf_ref.at[1 - slot, k], sem.at[1 - slot], (0, 1), g)
            cs = slice(c * MXU_DIM, (c + 1) * MXU_DIM)
            f = (w1 * buf_ref[slot, 0, :, :, cs].reshape(tm, MXU_DIM)
                 + w2 * buf_ref[slot, 1, :, :, cs].reshape(tm, MXU_DIM))
            o_ref[:, cs] = DEEPNORM_ALPHA * x1_ref[:, cs] + g2_ref[:, cs] * f

    @pl.when(step + 1 < pl.num_programs(0))
    def _():
        combine(True)

    @pl.when(step + 1 == pl.num_programs(0))
    def _():
        combine(False)

    o_ref[...] = _ln(o_ref[...]) * l2g_ref[...] + l2b_ref[...]


def _final(pos_tiles, first_tile, x1, mod, per_row, route, ln2_g, ln2_b, ys):
    nb, l, _ = x1.shape
    tm = min(TM, l)
    nt = l // tm
    n_steps = nb * nt
    row = lambda wd: pl.BlockSpec((None, tm, wd), lambda s: (s // nt, s % nt, 0))
    if per_row:
        g2 = pl.BlockSpec((None, tm, D_MODEL), lambda s: (s // nt, s % nt, 5))
    else:
        g2 = pl.BlockSpec((None, 1, D_MODEL), lambda s: (s // nt, 0, 5))
    vec = pl.BlockSpec((1, D_MODEL), lambda s: (0, 0))
    return pl.pallas_call(
        functools.partial(_final_kernel, tm=tm),
        out_shape=jax.ShapeDtypeStruct((nb, l, D_MODEL), F32),
        grid=(n_steps,),
        in_specs=[*_index_specs(2 * tm, first_tile, n_steps), row(D_MODEL), g2, row(LANES), vec, vec,
                  pl.BlockSpec(memory_space=pl.ANY)],
        out_specs=row(D_MODEL),
        scratch_shapes=[pltpu.VMEM((2, 2, tm // SUBLANES, SUBLANES, D_MODEL), F32), pltpu.SemaphoreType.DMA((2,))],
        compiler_params=_cparams(("arbitrary",)),
        name="final",
    )(pos_tiles, pos_tiles, x1, mod, route, ln2_g, ln2_b, ys.reshape(-1, SUBLANES, D_MODEL))


def _rope_tables(pos):
    half = ROT_DIM // 2
    inv_freq = ROPE_THETA ** (-jnp.arange(half, dtype=jnp.float32) * 2.0 / ROT_DIM)
    ang = pos.astype(jnp.float32)[:, None] * inv_freq[None, :]
    cos, sin = jnp.cos(ang), jnp.sin(ang)
    n = pos.shape[0]
    one = jnp.ones((n, HEAD_DIM - ROT_DIM), F32)
    zero = jnp.zeros((n, HEAD_DIM - half), F32)
    c = jnp.concatenate([cos, cos, one], -1)
    a = jnp.concatenate([-sin, zero], -1)
    b = jnp.concatenate([jnp.zeros((n, half), F32), sin, jnp.zeros((n, HEAD_DIM - ROT_DIM), F32)], -1)
    return tuple(jnp.tile(t, (1, LANES // HEAD_DIM)) for t in (c, a, b))


def _s5_params(lam_re, lam_im, log_step, b_re, b_im, c_re, c_im):
    f32 = jnp.float32
    dt = jnp.exp(log_step.astype(f32))[:, None]
    lr, li = lam_re.astype(f32), lam_im.astype(f32)
    mag = jnp.exp(lr * dt)
    ar, ai = mag * jnp.cos(li * dt), mag * jnp.sin(li * dt)
    den = lr * lr + li * li
    cr = ((ar - 1.0) * lr + ai * li) / den
    ci = (ai * lr - (ar - 1.0) * li) / den
    br, bi = b_re.astype(f32), b_im.astype(f32)
    bbr = cr[..., None] * br - ci[..., None] * bi
    bbi = cr[..., None] * bi + ci[..., None] * br
    gpt = MXU_DIM // SSM_CH
    ch_group = jnp.arange(MXU_DIM) // SSM_CH
    st_group = jnp.arange(ST_PER_KT) // SSM_STATE

    def pack_b(m):
        rows = jnp.transpose(m.reshape(N_KT, gpt, SSM_STATE, SSM_CH), (0, 1, 3, 2)).reshape(N_KT, MXU_DIM, SSM_STATE)
        keep = ch_group[:, None] == st_group[None, :]
        return jnp.where(keep[None], jnp.tile(rows, (1, 1, gpt)), 0.0).astype(BF16)

    def pack_c(m):
        cols = jnp.transpose(m.astype(f32).reshape(N_KT, gpt, SSM_CH, SSM_STATE), (0, 3, 1, 2))
        cols = cols.reshape(N_KT, SSM_STATE, MXU_DIM)
        keep = st_group[:, None] == ch_group[None, :]
        return jnp.where(keep[None], jnp.tile(cols, (1, gpt, 1)), 0.0).astype(BF16)

    wts = (pack_b(bbr), pack_b(bbi), pack_c(c_re), pack_c(-c_im.astype(f32)))
    return wts, ar.reshape(1, N_STATE), ai.reshape(1, N_STATE)


def _scan_tables(ar, ai, nb):
    assert SUBLANES // nb == 2
    a2r, a2i = ar * ar - ai * ai, 2.0 * ar * ai
    z = jnp.zeros_like(ar)
    rep = lambda first, second: jnp.concatenate([jnp.tile(first, (nb, 1)), jnp.tile(second, (nb, 1))], 0)
    return rep(z, ar), rep(z, ai), rep(ar, a2r), rep(ai, a2i)


def kernel(x_prompt, x_sample, cache_k_win, cache_v_win, state_ssm_re, state_ssm_im, c_prompt, c_sample, ada_w, ada_b,
           w_in, attn_sinks, ssm_lambda_re, ssm_lambda_im, ssm_log_step, ssm_b_re, ssm_b_im, ssm_c_re, ssm_c_im, ssm_d,
           ssm_glu_w, ssm_glu_b, attn_norm_g, ssm_norm_g, w_out, ln1_g, ln1_b, router_group_w, router_group_b,
           router_expert_w, router_expert_b, exp_w_gate, exp_w_up, exp_w_down, ln2_g, ln2_b):
    assert ada_w.shape[0] == DEPTH
    bp, lp, _ = x_prompt.shape
    bs, ls, _ = x_sample.shape
    lsp = SAMPLE_PAD

    w_in_bf = w_in[0].astype(BF16)
    wr = jnp.concatenate([router_group_w[0], router_expert_w[0]], -1)
    wr = jnp.pad(wr, ((0, 0), (0, LANES - wr.shape[1]))).astype(BF16)
    br = jnp.pad(jnp.concatenate([router_group_b[0], router_expert_b[0]], -1), (0, LANES - N_EXPERT_GROUPS - N_EXPERTS))
    wpost = dict(glu_w=ssm_glu_w[0].astype(BF16), glu_b=ssm_glu_b[0][None], attn_g=attn_norm_g[0][None],
                 ssm_g=ssm_norm_g[0][None], w_out=w_out[0].astype(BF16), ln1_g=ln1_g[0][None], ln1_b=ln1_b[0][None],
                 wr=wr, br=br[None])
    s5w, ar, ai = _s5_params(ssm_lambda_re[0], ssm_lambda_im[0], ssm_log_step[0], ssm_b_re[0], ssm_b_im[0],
                             ssm_c_re[0], ssm_c_im[0])
    d_skip = ssm_d[0][None]
    sinks = attn_sinks[0]

    n_c = bp + bs
    n_cp = -(-n_c // SUBLANES) * SUBLANES
    c_all = jnp.pad(jnp.concatenate([c_prompt, c_sample], 0), ((0, n_cp - n_c), (0, 0)))
    mod = _ada(c_all, ada_w[0], ada_b[0][None])
    mod_p = mod[:bp][:, None, :]
    mod_s = jnp.repeat(mod[bp:n_c], lsp, axis=0)[None]

    xs_pad = jnp.pad(x_sample, ((0, 0), (0, lsp - ls), (0, 0))).reshape(1, bs * lsp, D_MODEL)

    rope_p = _rope_tables(jnp.arange(lp))
    pos_s = PAST_LEN + jnp.minimum(jnp.arange(lsp), ls - 1)
    rope_s = tuple(jnp.tile(t, (bs, 1)) for t in _rope_tables(pos_s))
    q_p, k_p, v_p, u_p = _inproj(x_prompt, mod_p, w_in_bf, rope_p, False)
    q_s, k_s, v_s, u_s = _inproj(xs_pad, mod_s, w_in_bf, rope_s, True)

    attn_p = _attn_prompt(sinks, q_p, k_p, v_p)
    k_s3 = k_s.reshape(bs, lsp, D_KV)
    v_s3 = v_s.reshape(bs, lsp, D_KV)
    ck = cache_k_win[0].reshape(bs, WINDOW, D_KV)
    cv = cache_v_win[0].reshape(bs, WINDOW, D_KV)
    padk = lambda new: jnp.pad(new, ((0, 0), (0, WINDOW - lsp), (0, 0)))
    attn_s = _attn_sample(sinks, q_s.reshape(bs, lsp, D_ATTN), ck, padk(k_s3), cv, padk(v_s3))

    tabs = _scan_tables(ar, ai, bp)
    zero_carry = jnp.zeros((SUBLANES, N_STATE), F32)
    y_p, hr_p, hi_p = _s5_prompt(u_p, s5w, tabs, d_skip, zero_carry, zero_carry)
    u_tb = jnp.transpose(u_s.reshape(bs, lsp, D_SSM)[:, :ls], (1, 0, 2))
    y_tb, hr_s, hi_s = _s5_sample(u_tb, s5w, ar, ai, d_skip, state_ssm_re[0].reshape(bs, N_STATE),
                                  state_ssm_im[0].reshape(bs, N_STATE))
    y_s = jnp.pad(jnp.transpose(y_tb, (1, 0, 2)), ((0, 0), (0, lsp - ls), (0, 0))).reshape(1, bs * lsp, D_SSM)

    cnt0 = jnp.zeros((SUBLANES, LANES), F32)
    n_p, n_s = bp * lp, bs * lsp
    n_tok = n_p + n_s
    x1_s, h2_s, route_s, cnt_s = _post(attn_s.reshape(1, bs * lsp, D_ATTN), y_s, xs_pad, mod_s, True, wpost, cnt0)
    x1_p, h2, route_p, cnt = _post(attn_p, y_p, x_prompt, mod_p, False, wpost, cnt_s, tail=h2_s)

    n_tiles = -(-(2 * n_tok + N_EXPERTS * (TM_MOE - 1)) // TM_MOE) + MOE_LOOKAHEAD
    route = jnp.concatenate([route_p.reshape(n_p, LANES), route_s.reshape(n_s, LANES)], 0)
    counts = cnt[0, ROUTE_LANE0:ROUTE_LANE0 + N_EXPERTS].astype(jnp.int32)
    padded = ((counts + TM_MOE - 1) // TM_MOE) * TM_MOE
    ends = jnp.cumsum(padded)
    offs = ends - padded
    col = lambda c: route[:, c].astype(jnp.int32)
    expert_ids = jnp.arange(N_EXPERTS, dtype=jnp.int32)
    seg_start = lambda e: jnp.sum(jnp.where(e[:, None] == expert_ids[None, :], offs[None, :], 0), axis=1)
    pos_k = [seg_start(col(k)) + col(4 + k) for k in range(2)]
    n_valid = ends[-1] // TM_MOE
    tile_row = jnp.minimum(jnp.arange(n_tiles), n_valid - 1) * TM_MOE
    tile_expert = jnp.sum((ends[None, :] <= tile_row[:, None]).astype(jnp.int32), axis=1)
    first_tile = jnp.concatenate([jnp.ones((1,), jnp.int32), (tile_expert[1:] != tile_expert[:-1]).astype(jnp.int32)])
    first_tile = first_tile * (jnp.arange(n_tiles) < n_valid)
    nonempty = padded > 0
    later = jnp.where(nonempty[None, :] & (expert_ids[None, :] > expert_ids[:, None]), expert_ids[None, :], N_EXPERTS)
    next_expert = jnp.min(later, axis=1)
    next_expert = jnp.where(next_expert < N_EXPERTS, next_expert, -1).astype(jnp.int32)
    weight_slot = ((jnp.cumsum(nonempty.astype(jnp.int32)) - 1) & 1).astype(jnp.int32)
    tile_meta = (tile_expert, first_tile.astype(jnp.int32), next_expert[tile_expert], weight_slot[tile_expert])
    token_ids = jnp.arange(n_tok, dtype=jnp.int32)
    row_token = (jnp.arange(n_tiles * TM_MOE, dtype=jnp.int32) % n_tok).at[jnp.concatenate(pos_k)].set(
        jnp.concatenate([token_ids, token_ids]), unique_indices=True, mode="promise_in_bounds")

    ys = _moe(tile_meta, n_valid.reshape(1).astype(jnp.int32), row_token, h2, exp_w_gate[0], exp_w_up[0],
              exp_w_down[0])
    pos_tiles = jnp.concatenate([p.reshape(n_tok // TM, 1, TM) for p in pos_k], axis=2)
    y_prompt = _final(pos_tiles, 0, x1_p, mod_p, False, route_p, ln2_g[0][None], ln2_b[0][None], ys)
    y_samp = _final(pos_tiles, n_p // TM, x1_s, mod_s, True, route_s, ln2_g[0][None], ln2_b[0][None], ys)
    y_sample = y_samp.reshape(bs, lsp, D_MODEL)[:, :ls]

    kv5 = lambda a, n: a.reshape(1, a.shape[0], n, N_KV_HEADS, HEAD_DIM)
    k_win_p = kv5(k_p[:, lp - WINDOW:], WINDOW)
    v_win_p = kv5(v_p[:, lp - WINDOW:], WINDOW)
    k_win_s = kv5(jnp.concatenate([ck[:, ls:], k_s3[:, :ls]], 1), WINDOW)
    v_win_s = kv5(jnp.concatenate([cv[:, ls:], v_s3[:, :ls]], 1), WINDOW)
    st = lambda a: a.reshape(1, a.shape[0], N_SSM_GROUPS, SSM_STATE)
    return (y_prompt, y_sample, k_win_p, v_win_p, st(hr_p[bp:2 * bp]), st(hi_p[bp:2 * bp]),
            k_win_s, v_win_s, st(hr_s), st(hi_s))
```

```python
import functools

import jax
import jax.numpy as jnp
from jax import lax
from jax.experimental import pallas as pl
from jax.experimental.pallas import tpu as pltpu

F32 = jnp.float32
BF16 = jnp.bfloat16

D_MODEL = 2048
D_ATTN = 1024
D_SSM = 1024
HEAD_DIM = 64
N_HEADS = 16
N_KV_HEADS = 4
REP = 4
D_KV = 256
ROT_DIM = 16
ROPE_THETA = 500000.0
WINDOW = 128
SSM_CH = 16
N_SSM_GROUPS = 64
SSM_STATE = 64
N_STATE = N_SSM_GROUPS * SSM_STATE
PROJ_COLS = D_ATTN + 2 * D_KV + D_SSM
N_EXPERT_GROUPS = 4
EXPERTS_PER_GROUP = 8
N_EXPERTS = 32
D_EXPERT = 512
DEPTH = 1
DEEPNORM_ALPHA = (2.0 * DEPTH) ** 0.25
LN_EPS = 1e-5
PAST_LEN = 16384

LANES = 128
SUBLANES = 8
MXU_DIM = 256
TM = 256
TM_MOE = 128
TM_INPROJ = 512
S5_TT = 128
SAMPLE_PAD = 8
ROUTE_LANE0 = N_EXPERT_GROUPS
VMEM_LIMIT = 56 * 1024 * 1024


def _cparams(sem):
    return pltpu.CompilerParams(dimension_semantics=sem, vmem_limit_bytes=VMEM_LIMIT)


def _ln(x):
    mu = jnp.mean(x, axis=-1, keepdims=True)
    xc = x - mu
    var = jnp.mean(xc * xc, axis=-1, keepdims=True)
    return xc * lax.rsqrt(var + LN_EPS)


def _rms(x):
    return x * lax.rsqrt(jnp.mean(x * x, axis=-1, keepdims=True) + LN_EPS)


def _ada_kernel(c_ref, w_ref, b_ref, o_ref):
    c = c_ref[...]
    s = c * jax.nn.sigmoid(c)
    o_ref[...] = jnp.dot(s.astype(BF16), w_ref[...].astype(BF16), preferred_element_type=F32) + b_ref[...]


def _ada(c_all, ada_w, ada_b):
    n, tn = c_all.shape[0], 1024
    return pl.pallas_call(
        _ada_kernel,
        out_shape=jax.ShapeDtypeStruct((n, 6 * D_MODEL), F32),
        grid=(6 * D_MODEL // tn,),
        in_specs=[pl.BlockSpec((n, D_MODEL), lambda j: (0, 0)),
                  pl.BlockSpec((D_MODEL, tn), lambda j: (0, j)),
                  pl.BlockSpec((1, tn), lambda j: (0, j))],
        out_specs=pl.BlockSpec((n, tn), lambda j: (0, j)),
        compiler_params=_cparams(("arbitrary",)),
        name="ada",
    )(c_all, ada_w, ada_b)


def _mod_spec(per_row, tm, chunk):
    if per_row:
        return pl.BlockSpec((None, tm, D_MODEL), lambda b, i: (b, i, chunk))
    return pl.BlockSpec((None, 1, D_MODEL), lambda b, i: (b, 0, chunk))


def _inproj_kernel(x_ref, sh_ref, sc_ref, w_ref, rc_ref, ra_ref, rb_ref, q_ref, k_ref, v_ref, u_ref):
    h = _ln(x_ref[...]) * (1.0 + sc_ref[...]) + sh_ref[...]
    proj = jnp.dot(h.astype(BF16), w_ref[...], preferred_element_type=F32)
    rc, ra, rb = rc_ref[...], ra_ref[...], rb_ref[...]

    def rope(t):
        return t * rc + pltpu.roll(t, LANES - ROT_DIM // 2, 1) * ra + pltpu.roll(t, ROT_DIM // 2, 1) * rb

    for j in range(D_ATTN // LANES):
        q_ref[:, j * LANES:(j + 1) * LANES] = (rope(proj[:, j * LANES:(j + 1) * LANES]) * HEAD_DIM ** -0.5).astype(BF16)
    for j in range(D_KV // LANES):
        c0 = D_ATTN + j * LANES
        k_ref[:, j * LANES:(j + 1) * LANES] = rope(proj[:, c0:c0 + LANES])
    v_ref[...] = proj[:, D_ATTN + D_KV:D_ATTN + 2 * D_KV]
    u_ref[...] = proj[:, D_ATTN + 2 * D_KV:]


def _inproj(x, mod, w_in_bf, rope_tabs, per_row):
    nb, l, _ = x.shape
    tm = min(TM_INPROJ, l)
    row = lambda w: pl.BlockSpec((None, tm, w), lambda b, i: (b, i, 0))
    tab = pl.BlockSpec((tm, LANES), lambda b, i: (i, 0))
    return pl.pallas_call(
        _inproj_kernel,
        out_shape=(jax.ShapeDtypeStruct((nb, l, D_ATTN), BF16), jax.ShapeDtypeStruct((nb, l, D_KV), F32),
                   jax.ShapeDtypeStruct((nb, l, D_KV), F32), jax.ShapeDtypeStruct((nb, l, D_SSM), F32)),
        grid=(nb, l // tm),
        in_specs=[row(D_MODEL), _mod_spec(per_row, tm, 0), _mod_spec(per_row, tm, 1),
                  pl.BlockSpec((D_MODEL, PROJ_COLS), lambda b, i: (0, 0)), tab, tab, tab],
        out_specs=(row(D_ATTN), row(D_KV), row(D_KV), row(D_SSM)),
        compiler_params=_cparams(("arbitrary", "arbitrary")),
        name="inproj",
    )(x, mod, mod, w_in_bf, *rope_tabs)


ATTN_SAMPLE_BATCH = 8


def _attn_kernel(sink_ref, q_ref, kp_ref, kc_ref, vp_ref, vc_ref, o_ref, *, lq):
    m_rows = REP * lq
    ii = lax.broadcasted_iota(jnp.int32, (m_rows, WINDOW), 0) & (lq - 1)
    jj = lax.broadcasted_iota(jnp.int32, (m_rows, WINDOW), 1)
    from_prev = jj > ii
    dead = jj > ii + jnp.where(pl.program_id(1) > 0, WINDOW, 0)
    rr = lax.broadcasted_iota(jnp.int32, (m_rows, 1), 0)
    dn = (((1,), (1,)), ((), ()))
    for bi in range(q_ref.shape[0]):
        for g in range(N_KV_HEADS):
            qg = q_ref[bi, :, g * REP * HEAD_DIM:(g + 1) * REP * HEAD_DIM].astype(F32)
            qs = jnp.concatenate([qg[:, r * HEAD_DIM:(r + 1) * HEAD_DIM] for r in range(REP)], axis=0).astype(BF16)
            ks = slice(g * HEAD_DIM, (g + 1) * HEAD_DIM)
            s_p = lax.dot_general(qs, kp_ref[bi, :, ks].astype(BF16), dn, preferred_element_type=F32)
            s_c = lax.dot_general(qs, kc_ref[bi, :, ks].astype(BF16), dn, preferred_element_type=F32)
            s = jnp.where(dead, -jnp.inf, jnp.where(from_prev, s_p, s_c))
            sink = jnp.zeros((m_rows, 1), F32)
            for r in range(REP):
                sink = jnp.where((rr >= r * lq) & (rr < (r + 1) * lq), sink_ref[g * REP + r], sink)
            m = jnp.maximum(jnp.max(s, axis=-1, keepdims=True), sink)
            p = jnp.exp(s - m)
            p = p / (jnp.sum(p, axis=-1, keepdims=True) + jnp.exp(sink - m))
            o = (jnp.dot(jnp.where(from_prev, p, 0.0).astype(BF16), vp_ref[bi, :, ks].astype(BF16),
                         preferred_element_type=F32)
                 + jnp.dot(jnp.where(from_prev, 0.0, p).astype(BF16), vc_ref[bi, :, ks].astype(BF16),
                           preferred_element_type=F32))
            for r in range(REP):
                h = g * REP + r
                o_ref[bi, :, h * HEAD_DIM:(h + 1) * HEAD_DIM] = o[r * lq:(r + 1) * lq]


def _attn_prompt(sinks, q, k, v):
    nb, l, _ = q.shape
    nblk = l // WINDOW
    cur = lambda w: pl.BlockSpec((1, WINDOW, w), lambda b, n: (b, n, 0))
    prev = lambda w: pl.BlockSpec((1, WINDOW, w), lambda b, n: (b, jnp.maximum(n - 1, 0), 0))
    return pl.pallas_call(
        functools.partial(_attn_kernel, lq=WINDOW),
        out_shape=jax.ShapeDtypeStruct((nb, l, D_ATTN), F32),
        grid=(nb, nblk),
        in_specs=[pl.BlockSpec(memory_space=pltpu.SMEM), cur(D_ATTN), prev(D_KV), cur(D_KV), prev(D_KV), cur(D_KV)],
        out_specs=cur(D_ATTN),
        compiler_params=_cparams(("arbitrary", "arbitrary")),
        name="attn_prompt",
    )(sinks, q, k, k, v, v)


ATTN_SAMPLE_JOINT = 4


def _attn_sample_kernel(sink_ref, q_ref, kp_ref, kc_ref, vp_ref, vc_ref, o_ref, *, lq):
    gb = ATTN_SAMPLE_JOINT
    m_b = REP * lq
    m, n = gb * m_b, gb * WINDOW
    row = lax.broadcasted_iota(jnp.int32, (m, n), 0)
    col = lax.broadcasted_iota(jnp.int32, (m, n), 1)
    same = (row // m_b) == (col // WINDOW)
    from_prev = (col & (WINDOW - 1)) > (row & (lq - 1))
    rep_of_row = (lax.broadcasted_iota(jnp.int32, (m, 1), 0) // lq) % REP
    dn = (((1,), (1,)), ((), ()))
    for blk in range(q_ref.shape[0] // gb):
        bs = range(blk * gb, (blk + 1) * gb)
        for g in range(N_KV_HEADS):
            qg = [q_ref[b, :, g * REP * HEAD_DIM:(g + 1) * REP * HEAD_DIM].astype(F32) for b in bs]
            qs = jnp.concatenate([qb[:, r * HEAD_DIM:(r + 1) * HEAD_DIM] for qb in qg for r in range(REP)],
                                 axis=0).astype(BF16)
            ks = slice(g * HEAD_DIM, (g + 1) * HEAD_DIM)
            stack = lambda ref: jnp.concatenate([ref[b, :, ks] for b in bs], axis=0).astype(BF16)
            s_p = lax.dot_general(qs, stack(kp_ref), dn, preferred_element_type=F32)
            s_c = lax.dot_general(qs, stack(kc_ref), dn, preferred_element_type=F32)
            s = jnp.where(same, jnp.where(from_prev, s_p, s_c), -jnp.inf)
            sink = jnp.zeros((m, 1), F32)
            for r in range(REP):
                sink = jnp.where(rep_of_row == r, sink_ref[g * REP + r], sink)
            mx = jnp.maximum(jnp.max(s, axis=-1, keepdims=True), sink)
            p = jnp.exp(s - mx)
            p = p / (jnp.sum(p, axis=-1, keepdims=True) + jnp.exp(sink - mx))
            o = (jnp.dot(jnp.where(from_prev, p, 0.0).astype(BF16), stack(vp_ref), preferred_element_type=F32)
                 + jnp.dot(jnp.where(from_prev, 0.0, p).astype(BF16), stack(vc_ref), preferred_element_type=F32))
            for bi, b in enumerate(bs):
                for r in range(REP):
                    h = g * REP + r
                    o_ref[b, :, h * HEAD_DIM:(h + 1) * HEAD_DIM] = o[bi * m_b + r * lq:bi * m_b + (r + 1) * lq]


def _attn_sample(sinks, q, k_cache, k_new, v_cache, v_new):
    nb, lq, _ = q.shape
    nbb = ATTN_SAMPLE_BATCH
    kblk = pl.BlockSpec((nbb, WINDOW, D_KV), lambda b: (b, 0, 0))
    return pl.pallas_call(
        functools.partial(_attn_sample_kernel, lq=lq),
        out_shape=jax.ShapeDtypeStruct((nb, lq, D_ATTN), F32),
        grid=(nb // nbb,),
        in_specs=[pl.BlockSpec(memory_space=pltpu.SMEM), pl.BlockSpec((nbb, lq, D_ATTN), lambda b: (b, 0, 0)),
                  kblk, kblk, kblk, kblk],
        out_specs=pl.BlockSpec((nbb, lq, D_ATTN), lambda b: (b, 0, 0)),
        compiler_params=_cparams(("arbitrary",)),
        name="attn_sample",
    )(sinks, q, k_cache, k_new, v_cache, v_new)


N_KT = D_SSM // MXU_DIM
ST_PER_KT = N_STATE // N_KT
N_SLAB = D_SSM // LANES


def _cproj(hr_ref, hi_ref, cre_ref, cimn_ref, kt):
    cs = slice(kt * ST_PER_KT, (kt + 1) * ST_PER_KT)
    return (jnp.dot(hr_ref[:, cs].astype(BF16), cre_ref[kt], preferred_element_type=F32)
            + jnp.dot(hi_ref[:, cs].astype(BF16), cimn_ref[kt], preferred_element_type=F32))


def _s5_prompt_kernel(u_ref, bre_ref, bim_ref, cre_ref, cimn_ref, a1r_ref, a1i_ref, par_ref, pai_ref, d_ref,
                      h0r_ref, h0i_ref, y_ref, cr_ref, ci_ref, il_ref, xr_ref, xi_ref, *, nb, tt):
    rows = nb * tt

    @pl.when(pl.program_id(0) == 0)
    def _():
        cr_ref[...] = h0r_ref[...]
        ci_ref[...] = h0i_ref[...]

    for b in range(nb):
        for j in range(N_SLAB):
            il_ref[j, pl.ds(b, tt, stride=nb), :] = u_ref[b, :, j * LANES:(j + 1) * LANES]
    for kt in range(N_KT):
        ub = jnp.concatenate([il_ref[2 * kt], il_ref[2 * kt + 1]], axis=1).astype(BF16)
        cs = slice(kt * ST_PER_KT, (kt + 1) * ST_PER_KT)
        xr_ref[:, cs] = jnp.dot(ub, bre_ref[kt], preferred_element_type=F32)
        xi_ref[:, cs] = jnp.dot(ub, bim_ref[kt], preferred_element_type=F32)

    half = lax.broadcasted_iota(jnp.int32, (SUBLANES, LANES), 0) < nb

    def step(i, carry):
        r0 = pl.multiple_of(i * SUBLANES, SUBLANES)
        for j in range(N_STATE // LANES):
            ls = slice(j * LANES, (j + 1) * LANES)
            x_r = xr_ref[pl.ds(r0, SUBLANES), ls]
            x_i = xi_ref[pl.ds(r0, SUBLANES), ls]
            s_r = pltpu.roll(x_r, nb, 0)
            s_i = pltpu.roll(x_i, nb, 0)
            a1r, a1i = a1r_ref[:, ls], a1i_ref[:, ls]
            c_r, c_i = cr_ref[:, ls], ci_ref[:, ls]
            par, pai = par_ref[:, ls], pai_ref[:, ls]
            h_r = x_r + (a1r * s_r - a1i * s_i) + (par * c_r - pai * c_i)
            h_i = x_i + (a1r * s_i + a1i * s_r) + (par * c_i + pai * c_r)
            xr_ref[pl.ds(r0, SUBLANES), ls] = h_r
            xi_ref[pl.ds(r0, SUBLANES), ls] = h_i
            cr_ref[:, ls] = jnp.where(half, pltpu.roll(h_r, nb, 0), h_r)
            ci_ref[:, ls] = jnp.where(half, pltpu.roll(h_i, nb, 0), h_i)
        return carry

    lax.fori_loop(0, rows // SUBLANES, step, 0)

    for kt in range(N_KT):
        y = _cproj(xr_ref, xi_ref, cre_ref, cimn_ref, kt)
        il_ref[2 * kt] = y[:, :LANES]
        il_ref[2 * kt + 1] = y[:, LANES:]
    for b in range(nb):
        for j in range(N_SLAB):
            ls = slice(j * LANES, (j + 1) * LANES)
            y_ref[b, :, ls] = il_ref[j, pl.ds(b, tt, stride=nb), :] + d_ref[:, ls] * u_ref[b, :, ls]


def _s5_prompt(u, wts, tabs, d_skip, h0r, h0i):
    nb, l, _ = u.shape
    tt = S5_TT
    rows = nb * tt
    full = lambda a: pl.BlockSpec(a.shape, lambda i: (0,) * a.ndim)
    blk = pl.BlockSpec((nb, tt, D_SSM), lambda i: (0, i, 0))
    carry = jax.ShapeDtypeStruct((SUBLANES, N_STATE), F32)
    return pl.pallas_call(
        functools.partial(_s5_prompt_kernel, nb=nb, tt=tt),
        out_shape=(jax.ShapeDtypeStruct((nb, l, D_SSM), F32), carry, carry),
        grid=(l // tt,),
        in_specs=[blk] + [full(a) for a in (*wts, *tabs, d_skip, h0r, h0i)],
        out_specs=(blk, pl.BlockSpec((SUBLANES, N_STATE), lambda i: (0, 0)),
                   pl.BlockSpec((SUBLANES, N_STATE), lambda i: (0, 0))),
        scratch_shapes=[pltpu.VMEM((N_SLAB, rows, LANES), F32), pltpu.VMEM((rows, N_STATE), F32),
                        pltpu.VMEM((rows, N_STATE), F32)],
        compiler_params=_cparams(("arbitrary",)),
        name="s5_prompt",
    )(u, *wts, *tabs, d_skip, h0r, h0i)


def _s5_sample_kernel(u_ref, bre_ref, bim_ref, cre_ref, cimn_ref, ar_ref, ai_ref, d_ref, h0r_ref, h0i_ref,
                      y_ref, sr_ref, si_ref, xr_ref, xi_ref, *, nt):
    sr_ref[...] = h0r_ref[...]
    si_ref[...] = h0i_ref[...]
    for t in range(nt):
        for kt in range(N_KT):
            ub = u_ref[t, :, kt * MXU_DIM:(kt + 1) * MXU_DIM].astype(BF16)
            cs = slice(kt * ST_PER_KT, (kt + 1) * ST_PER_KT)
            xr_ref[:, cs] = jnp.dot(ub, bre_ref[kt], preferred_element_type=F32)
            xi_ref[:, cs] = jnp.dot(ub, bim_ref[kt], preferred_element_type=F32)
        ar, ai = ar_ref[...], ai_ref[...]
        s_r, s_i = sr_ref[...], si_ref[...]
        sr_ref[...] = xr_ref[...] + (ar * s_r - ai * s_i)
        si_ref[...] = xi_ref[...] + (ar * s_i + ai * s_r)
        for kt in range(N_KT):
            ys = slice(kt * MXU_DIM, (kt + 1) * MXU_DIM)
            y_ref[t, :, ys] = _cproj(sr_ref, si_ref, cre_ref, cimn_ref, kt) + d_ref[:, ys] * u_ref[t, :, ys]


def _s5_sample(u_tb, wts, ar, ai, d_skip, h0r, h0i):
    nt, nb, _ = u_tb.shape
    st = jax.ShapeDtypeStruct((nb, N_STATE), F32)
    args = (u_tb, *wts, ar, ai, d_skip, h0r, h0i)
    full = lambda a: pl.BlockSpec(a.shape, lambda i: (0,) * a.ndim)
    return pl.pallas_call(
        functools.partial(_s5_sample_kernel, nt=nt),
        out_shape=(jax.ShapeDtypeStruct((nt, nb, D_SSM), F32), st, st),
        grid=(1,),
        in_specs=[full(a) for a in args],
        out_specs=(pl.BlockSpec((nt, nb, D_SSM), lambda i: (0, 0, 0)), pl.BlockSpec((nb, N_STATE), lambda i: (0, 0)),
                   pl.BlockSpec((nb, N_STATE), lambda i: (0, 0))),
        scratch_shapes=[pltpu.VMEM((nb, N_STATE), F32), pltpu.VMEM((nb, N_STATE), F32)],
        compiler_params=_cparams(("arbitrary",)),
        name="s5_sample",
    )(*args)


def _gather_group(src_ref, ids_ref, id_stride, id_offset, dst_ref, sem, priorities, g):
    for j in range(SUBLANES):
        t = ids_ref[0, id_stride * (g * SUBLANES + j) + id_offset]
        pltpu.make_async_copy(src_ref.at[t >> 3, pl.ds(t & (SUBLANES - 1), 1)], dst_ref.at[g, pl.ds(j, 1)],
                              sem).start(priority=priorities[j % len(priorities)])


def _gather_rows(src_ref, ids_ref, id_stride, id_offset, dst_ref, sem, priorities):
    def body(g, c):
        _gather_group(src_ref, ids_ref, id_stride, id_offset, dst_ref, sem, priorities, g)
        return c
    lax.fori_loop(0, dst_ref.shape[0], body, 0)


def _index_specs(n_cols, first_tile, n_steps, depth=1):
    def spec(fn):
        return pl.BlockSpec((None, 1, n_cols), fn, memory_space=pltpu.SMEM)

    head = [spec(lambda s, *_, k=k: (first_tile + min(k, n_steps - 1), 0, 0)) for k in range(depth)]
    return (*head, spec(lambda s, *_: (first_tile + jnp.minimum(s + depth, n_steps - 1), 0, 0)))


N_POST_INPUTS = 17


def _post_kernel(*refs, has_tail):
    ins, outs = refs[:N_POST_INPUTS], refs[N_POST_INPUTS + int(has_tail):]
    if not has_tail:
        _post_body(*ins, *outs)
        return
    tail_ref, h2_ref = refs[N_POST_INPUTS], outs[1]
    last = pl.num_programs(0) - 1

    @pl.when(pl.program_id(0) == last)
    def _():
        h2_ref[...] = tail_ref[...]

    @pl.when(pl.program_id(0) < last)
    def _():
        _post_body(*ins, *outs)


def _post_body(attn_ref, yssm_ref, x_ref, g1_ref, sh2_ref, sc2_ref, gluw_ref, glub_ref, ga_ref, gs_ref, wout_ref,
               l1g_ref, l1b_ref, wr_ref, br_ref, tri_ref, cnt0_ref, x1_ref, h2_ref, route_ref, cnt_ref):
    tm = x_ref.shape[0]

    @pl.when(pl.program_id(0) == 0)
    def _():
        cnt_ref[...] = cnt0_ref[...]

    z = jax.nn.gelu(yssm_ref[...])
    ssm = z * jax.nn.sigmoid(jnp.dot(z.astype(BF16), gluw_ref[...], preferred_element_type=F32) + glub_ref[...])
    mixed_a = (_rms(attn_ref[...]) * ga_ref[...]).astype(BF16)
    mixed_s = (_rms(ssm) * gs_ref[...]).astype(BF16)
    o = (jnp.dot(mixed_a, wout_ref[:D_ATTN, :], preferred_element_type=F32)
         + jnp.dot(mixed_s, wout_ref[D_ATTN:, :], preferred_element_type=F32))
    x1 = _ln(DEEPNORM_ALPHA * x_ref[...] + g1_ref[...] * o) * l1g_ref[...] + l1b_ref[...]
    x1_ref[...] = x1
    h2 = _ln(x1) * (1.0 + sc2_ref[...]) + sh2_ref[...]
    h2_ref[...] = h2
    logits = jnp.dot(h2.astype(BF16), wr_ref[...], preferred_element_type=F32) + br_ref[...]

    lane = lax.broadcasted_iota(jnp.int32, (tm, LANES), 1).astype(F32)
    big = float(4 * LANES)
    neg = -jnp.inf
    gl = jnp.where(lane < N_EXPERT_GROUPS, logits, neg)
    gp = jnp.exp(gl - jnp.max(gl, axis=-1, keepdims=True))
    gp = gp / jnp.sum(gp, axis=-1, keepdims=True)
    g_val = jnp.max(gp, axis=-1, keepdims=True)
    g_idx = jnp.min(jnp.where(gp == g_val, lane, big), axis=-1, keepdims=True)
    lo = ROUTE_LANE0 + EXPERTS_PER_GROUP * g_idx
    emask = (lane >= lo) & (lane < lo + EXPERTS_PER_GROUP)
    el = jnp.where(emask, logits, neg)
    ep = jnp.exp(el - jnp.max(el, axis=-1, keepdims=True))
    ep = jnp.where(emask, ep / jnp.sum(ep, axis=-1, keepdims=True), -1.0)
    v1 = jnp.max(ep, axis=-1, keepdims=True)
    i1 = jnp.min(jnp.where(ep == v1, lane, big), axis=-1, keepdims=True)
    ep2 = jnp.where(lane == i1, -1.0, ep)
    v2 = jnp.max(ep2, axis=-1, keepdims=True)
    i2 = jnp.min(jnp.where(ep2 == v2, lane, big), axis=-1, keepdims=True)
    vs = v1 + v2
    w1 = g_val * (v1 / vs)
    w2 = g_val * (v2 / vs)
    hit = jnp.where((lane == i1) | (lane == i2), 1.0, 0.0)
    before = jnp.dot(tri_ref[...], hit.astype(BF16), preferred_element_type=F32) + cnt_ref[0:1, :]
    r1 = jnp.sum(jnp.where(lane == i1, before, 0.0), axis=-1, keepdims=True)
    r2 = jnp.sum(jnp.where(lane == i2, before, 0.0), axis=-1, keepdims=True)
    cnt_ref[...] = cnt_ref[...] + jnp.sum(hit, axis=0, keepdims=True)
    e1 = i1 - ROUTE_LANE0
    e2 = i2 - ROUTE_LANE0
    route = jnp.zeros((tm, LANES), F32)
    for n, val in enumerate((e1, e2, w1, w2, r1, r2)):
        route = jnp.where(lane == n, val, route)
    route_ref[...] = route


def _post(attn, yssm, x, mod, per_row, w, cnt0, tail=None):
    nb, l, _ = x.shape
    tm = min(TM, l)
    nt = l // tm
    n_body = nb * nt
    has_tail = tail is not None
    if has_tail:
        assert tail.shape == (tm, D_MODEL)

    def bi(s):
        s = jnp.minimum(s, n_body - 1)
        return s // nt, s % nt

    row = lambda wd: pl.BlockSpec((None, tm, wd), lambda s: (*bi(s), 0))
    if per_row:
        mspec = lambda chunk: pl.BlockSpec((None, tm, D_MODEL), lambda s: (*bi(s), chunk))
    else:
        mspec = lambda chunk: pl.BlockSpec((None, 1, D_MODEL), lambda s: (bi(s)[0], 0, chunk))
    full = lambda a: pl.BlockSpec(a.shape, lambda s: (0,) * a.ndim)
    tri = jnp.tril(jnp.ones((tm, tm), F32), -1).astype(BF16)
    consts = (w["glu_w"], w["glu_b"], w["attn_g"], w["ssm_g"], w["w_out"], w["ln1_g"], w["ln1_b"], w["wr"], w["br"],
              tri, cnt0) + ((tail,) if has_tail else ())
    assert 6 + len(consts) == N_POST_INPUTS + int(has_tail)
    n_steps = n_body + int(has_tail)
    return pl.pallas_call(
        functools.partial(_post_kernel, has_tail=has_tail),
        out_shape=(jax.ShapeDtypeStruct((nb, l, D_MODEL), F32),
                   jax.ShapeDtypeStruct((n_steps * tm, D_MODEL), F32),
                   jax.ShapeDtypeStruct((nb, l, LANES), F32), jax.ShapeDtypeStruct((SUBLANES, LANES), F32)),
        grid=(n_steps,),
        in_specs=[row(D_ATTN), row(D_SSM), row(D_MODEL), mspec(2), mspec(3), mspec(4)] + [full(a) for a in consts],
        out_specs=(row(D_MODEL), pl.BlockSpec((tm, D_MODEL), lambda s: (s, 0)),
                   row(LANES), pl.BlockSpec((SUBLANES, LANES), lambda s: (0, 0))),
        compiler_params=_cparams(("arbitrary",)),
        name="post",
    )(attn, yssm, x, mod, mod, mod, *consts)


MOE_LOOKAHEAD = 2
MOE_SLOTS = MOE_LOOKAHEAD + 1


def _moe_kernel(te_ref, nv_ref, first_ref, nxt_ref, par_ref, rt0_ref, rt1_ref, rtn_ref, h2_ref, wg_hbm, wu_hbm, wd_hbm,
                y_ref, buf_ref, wgf_ref, wuf_ref, wdf_ref, sem, wsem):
    i = pl.program_id(0)
    nv = nv_ref[0]
    slot = lax.rem(i, MOE_SLOTS)
    w_pairs = ((wg_hbm, wgf_ref), (wu_hbm, wuf_ref), (wd_hbm, wdf_ref))

    def gather(ids_ref, dst_slot):
        _gather_rows(h2_ref, ids_ref, 1, 0, buf_ref.at[dst_slot], sem.at[dst_slot], (0,))

    def fetch_weights(e, ws):
        for src, dst in w_pairs:
            pltpu.make_async_copy(src.at[e], dst.at[ws], wsem.at[ws]).start(priority=1)

    @pl.when(i == 0)
    def _():
        gather(rt0_ref, 0)
        gather(rt1_ref, 1)
        fetch_weights(te_ref[0], par_ref[0])

    @pl.when(i < nv + MOE_LOOKAHEAD)
    def _():
        pltpu.make_async_copy(h2_ref.at[pl.ds(0, TM_MOE // SUBLANES)], buf_ref.at[slot], sem.at[slot]).wait()

    @pl.when((i < nv) & (first_ref[i] == 1))
    def _():
        ws = par_ref[i]
        for src, dst in w_pairs:
            pltpu.make_async_copy(src.at[0], dst.at[ws], wsem.at[ws]).wait()

        @pl.when(nxt_ref[i] >= 0)
        def _():
            fetch_weights(nxt_ref[i], 1 - ws)

    @pl.when(i < nv)
    def _():
        nslot = lax.rem(i + MOE_LOOKAHEAD, MOE_SLOTS)
        ws = par_ref[i]
        n_kc = D_MODEL // MXU_DIM
        groups_per_kc = TM_MOE // SUBLANES // n_kc
        hg = hu = None
        for kc in range(n_kc):
            for gg in range(groups_per_kc):
                _gather_group(h2_ref, rtn_ref, 1, 0, buf_ref.at[nslot], sem.at[nslot], (0,), kc * groups_per_kc + gg)
            ks = slice(kc * MXU_DIM, (kc + 1) * MXU_DIM)
            xk = buf_ref[slot, :, :, ks].reshape(TM_MOE, MXU_DIM).astype(BF16)
            pg = jnp.dot(xk, wgf_ref[ws, ks, :].astype(BF16), preferred_element_type=F32)
            pu = jnp.dot(xk, wuf_ref[ws, ks, :].astype(BF16), preferred_element_type=F32)
            hg = pg if hg is None else hg + pg
            hu = pu if hu is None else hu + pu
        act = (hg * jax.nn.sigmoid(hg)) * hu
        y_ref[...] = jnp.dot(act.astype(BF16), wdf_ref[ws].astype(BF16), preferred_element_type=F32)

    @pl.when(i >= nv)
    def _():
        y_ref[...] = jnp.zeros(y_ref.shape, y_ref.dtype)


def _moe(tile_meta, n_valid, row_token, h2, w_gate, w_up, w_down):
    n_steps = row_token.shape[0] // TM_MOE
    rt = row_token.reshape(n_steps, 1, TM_MOE)
    te, first, nxt, par = tile_meta
    any_spec = pl.BlockSpec(memory_space=pl.ANY)
    return pl.pallas_call(
        _moe_kernel,
        out_shape=jax.ShapeDtypeStruct((n_steps * TM_MOE, D_MODEL), F32),
        grid_spec=pltpu.PrefetchScalarGridSpec(
            num_scalar_prefetch=5, grid=(n_steps,),
            in_specs=[*_index_specs(TM_MOE, 0, n_steps, MOE_LOOKAHEAD), any_spec, any_spec, any_spec, any_spec],
            out_specs=pl.BlockSpec((TM_MOE, D_MODEL), lambda i, *_: (i, 0)),
            scratch_shapes=[pltpu.VMEM((MOE_SLOTS, TM_MOE // SUBLANES, SUBLANES, D_MODEL), F32),
                            pltpu.VMEM((2, D_MODEL, D_EXPERT), F32), pltpu.VMEM((2, D_MODEL, D_EXPERT), F32),
                            pltpu.VMEM((2, D_EXPERT, D_MODEL), F32),
                            pltpu.SemaphoreType.DMA((MOE_SLOTS,)), pltpu.SemaphoreType.DMA((2,))]),
        compiler_params=_cparams(("arbitrary",)),
        name="moe",
    )(te, n_valid, first, nxt, par, rt, rt, rt, h2.reshape(-1, SUBLANES, D_MODEL), w_gate, w_up, w_down)


def _final_kernel(pos0_ref, posn_ref, x1_ref, g2_ref, route_ref, l2g_ref, l2b_ref, ys_ref, o_ref, buf_ref, sem, *, tm):
    step = pl.program_id(0)
    slot = step & 1

    def gather(ids_ref, dst_slot):
        for k in range(2):
            _gather_rows(ys_ref, ids_ref, 1, k * tm, buf_ref.at[dst_slot, k], sem.at[dst_slot], (0, 1))

    @pl.when(step == 0)
    def _():
        gather(pos0_ref, 0)

    for k in range(2):
        pltpu.make_async_copy(ys_ref.at[pl.ds(0, tm // SUBLANES)], buf_ref.at[slot, k], sem.at[slot]).wait()

    route = route_ref[...]
    w1, w2 = route[:, 2:3], route[:, 3:4]
    n_groups = tm // SUBLANES

    def combine(issue_next):
        n_chunks = D_MODEL // MXU_DIM
        per_chunk = 2 * n_groups // n_chunks
        for c in range(n_chunks):
            if issue_next:
                for q in range(c * per_chunk, (c + 1) * per_chunk):
                    k, g = q // n_groups, q % n_groups
                    _gather_group(ys_ref, posn_ref, 1, k * tm, buf_ref.at[1 - slot, k], sem.at[1 - slot], (0, 1), g)
            cs = slice(c * MXU_DIM, (c + 1) * MXU_DIM)
            f = (w1 * buf_ref[slot, 0, :, :, cs].reshape(tm, MXU_DIM)
                 + w2 * buf_ref[slot, 1, :, :, cs].reshape(tm, MXU_DIM))
            o_ref[:, cs] = DEEPNORM_ALPHA * x1_ref[:, cs] + g2_ref[:, cs] * f

    @pl.when(step + 1 < pl.num_programs(0))
    def _():
        combine(True)

    @pl.when(step + 1 == pl.num_programs(0))
    def _():
        combine(False)

    o_ref[...] = _ln(o_ref[...]) * l2g_ref[...] + l2b_ref[...]


def _final(pos_tiles, first_tile, x1, mod, per_row, route, ln2_g, ln2_b, ys):
    nb, l, _ = x1.shape
    tm = min(TM, l)
    nt = l // tm
    n_steps = nb * nt
    row = lambda wd: pl.BlockSpec((None, tm, wd), lambda s: (s // nt, s % nt, 0))
    if per_row:
        g2 = pl.BlockSpec((None, tm, D_MODEL), lambda s: (s // nt, s % nt, 5))
    else:
        g2 = pl.BlockSpec((None, 1, D_MODEL), lambda s: (s // nt, 0, 5))
    vec = pl.BlockSpec((1, D_MODEL), lambda s: (0, 0))
    return pl.pallas_call(
        functools.partial(_final_kernel, tm=tm),
        out_shape=jax.ShapeDtypeStruct((nb, l, D_MODEL), F32),
        grid=(n_steps,),
        in_specs=[*_index_specs(2 * tm, first_tile, n_steps), row(D_MODEL), g2, row(LANES), vec, vec,
                  pl.BlockSpec(memory_space=pl.ANY)],
        out_specs=row(D_MODEL),
        scratch_shapes=[pltpu.VMEM((2, 2, tm // SUBLANES, SUBLANES, D_MODEL), F32), pltpu.SemaphoreType.DMA((2,))],
        compiler_params=_cparams(("arbitrary",)),
        name="final",
    )(pos_tiles, pos_tiles, x1, mod, route, ln2_g, ln2_b, ys.reshape(-1, SUBLANES, D_MODEL))


def _rope_tables(pos):
    half = ROT_DIM // 2
    inv_freq = ROPE_THETA ** (-jnp.arange(half, dtype=jnp.float32) * 2.0 / ROT_DIM)
    ang = pos.astype(jnp.float32)[:, None] * inv_freq[None, :]
    cos, sin = jnp.cos(ang), jnp.sin(ang)
    n = pos.shape[0]
    one = jnp.ones((n, HEAD_DIM - ROT_DIM), F32)
    zero = jnp.zeros((n, HEAD_DIM - half), F32)
    c = jnp.concatenate([cos, cos, one], -1)
    a = jnp.concatenate([-sin, zero], -1)
    b = jnp.concatenate([jnp.zeros((n, half), F32), sin, jnp.zeros((n, HEAD_DIM - ROT_DIM), F32)], -1)
    return tuple(jnp.tile(t, (1, LANES // HEAD_DIM)) for t in (c, a, b))


def _s5_params(lam_re, lam_im, log_step, b_re, b_im, c_re, c_im):
    f32 = jnp.float32
    dt = jnp.exp(log_step.astype(f32))[:, None]
    lr, li = lam_re.astype(f32), lam_im.astype(f32)
    mag = jnp.exp(lr * dt)
    ar, ai = mag * jnp.cos(li * dt), mag * jnp.sin(li * dt)
    den = lr * lr + li * li
    cr = ((ar - 1.0) * lr + ai * li) / den
    ci = (ai * lr - (ar - 1.0) * li) / den
    br, bi = b_re.astype(f32), b_im.astype(f32)
    bbr = cr[..., None] * br - ci[..., None] * bi
    bbi = cr[..., None] * bi + ci[..., None] * br
    gpt = MXU_DIM // SSM_CH
    ch_group = jnp.arange(MXU_DIM) // SSM_CH
    st_group = jnp.arange(ST_PER_KT) // SSM_STATE

    def pack_b(m):
        rows = jnp.transpose(m.reshape(N_KT, gpt, SSM_STATE, SSM_CH), (0, 1, 3, 2)).reshape(N_KT, MXU_DIM, SSM_STATE)
        keep = ch_group[:, None] == st_group[None, :]
        return jnp.where(keep[None], jnp.tile(rows, (1, 1, gpt)), 0.0).astype(BF16)

    def pack_c(m):
        cols = jnp.transpose(m.astype(f32).reshape(N_KT, gpt, SSM_CH, SSM_STATE), (0, 3, 1, 2))
        cols = cols.reshape(N_KT, SSM_STATE, MXU_DIM)
        keep = st_group[:, None] == ch_group[None, :]
        return jnp.where(keep[None], jnp.tile(cols, (1, gpt, 1)), 0.0).astype(BF16)

    wts = (pack_b(bbr), pack_b(bbi), pack_c(c_re), pack_c(-c_im.astype(f32)))
    return wts, ar.reshape(1, N_STATE), ai.reshape(1, N_STATE)


def _scan_tables(ar, ai, nb):
    assert SUBLANES // nb == 2
    a2r, a2i = ar * ar - ai * ai, 2.0 * ar * ai
    z = jnp.zeros_like(ar)
    rep = lambda first, second: jnp.concatenate([jnp.tile(first, (nb, 1)), jnp.tile(second, (nb, 1))], 0)
    return rep(z, ar), rep(z, ai), rep(ar, a2r), rep(ai, a2i)


def kernel(x_prompt, x_sample, cache_k_win, cache_v_win, state_ssm_re, state_ssm_im, c_prompt, c_sample, ada_w, ada_b,
           w_in, attn_sinks, ssm_lambda_re, ssm_lambda_im, ssm_log_step, ssm_b_re, ssm_b_im, ssm_c_re, ssm_c_im, ssm_d,
           ssm_glu_w, ssm_glu_b, attn_norm_g, ssm_norm_g, w_out, ln1_g, ln1_b, router_group_w, router_group_b,
           router_expert_w, router_expert_b, exp_w_gate, exp_w_up, exp_w_down, ln2_g, ln2_b):
    assert ada_w.shape[0] == DEPTH
    bp, lp, _ = x_prompt.shape
    bs, ls, _ = x_sample.shape
    lsp = SAMPLE_PAD

    w_in_bf = w_in[0].astype(BF16)
    wr = jnp.concatenate([router_group_w[0], router_expert_w[0]], -1)
    wr = jnp.pad(wr, ((0, 0), (0, LANES - wr.shape[1]))).astype(BF16)
    br = jnp.pad(jnp.concatenate([router_group_b[0], router_expert_b[0]], -1), (0, LANES - N_EXPERT_GROUPS - N_EXPERTS))
    wpost = dict(glu_w=ssm_glu_w[0].astype(BF16), glu_b=ssm_glu_b[0][None], attn_g=attn_norm_g[0][None],
                 ssm_g=ssm_norm_g[0][None], w_out=w_out[0].astype(BF16), ln1_g=ln1_g[0][None], ln1_b=ln1_b[0][None],
                 wr=wr, br=br[None])
    s5w, ar, ai = _s5_params(ssm_lambda_re[0], ssm_lambda_im[0], ssm_log_step[0], ssm_b_re[0], ssm_b_im[0],
                             ssm_c_re[0], ssm_c_im[0])
    d_skip = ssm_d[0][None]
    sinks = attn_sinks[0]

    n_c = bp + bs
    n_cp = -(-n_c // SUBLANES) * SUBLANES
    c_all = jnp.pad(jnp.concatenate([c_prompt, c_sample], 0), ((0, n_cp - n_c), (0, 0)))
    mod = _ada(c_all, ada_w[0], ada_b[0][None])
    mod_p = mod[:bp][:, None, :]
    mod_s = jnp.repeat(mod[bp:n_c], lsp, axis=0)[None]

    xs_pad = jnp.pad(x_sample, ((0, 0), (0, lsp - ls), (0, 0))).reshape(1, bs * lsp, D_MODEL)

    rope_p = _rope_tables(jnp.arange(lp))
    pos_s = PAST_LEN + jnp.minimum(jnp.arange(lsp), ls - 1)
    rope_s = tuple(jnp.tile(t, (bs, 1)) for t in _rope_tables(pos_s))
    q_p, k_p, v_p, u_p = _inproj(x_prompt, mod_p, w_in_bf, rope_p, False)
    q_s, k_s, v_s, u_s = _inproj(xs_pad, mod_s, w_in_bf, rope_s, True)

    attn_p = _attn_prompt(sinks, q_p, k_p, v_p)
    k_s3 = k_s.reshape(bs, lsp, D_KV)
    v_s3 = v_s.reshape(bs, lsp, D_KV)
    ck = cache_k_win[0].reshape(bs, WINDOW, D_KV)
    cv = cache_v_win[0].reshape(bs, WINDOW, D_KV)
    padk = lambda new: jnp.pad(new, ((0, 0), (0, WINDOW - lsp), (0, 0)))
    attn_s = _attn_sample(sinks, q_s.reshape(bs, lsp, D_ATTN), ck, padk(k_s3), cv, padk(v_s3))

    tabs = _scan_tables(ar, ai, bp)
    zero_carry = jnp.zeros((SUBLANES, N_STATE), F32)
    y_p, hr_p, hi_p = _s5_prompt(u_p, s5w, tabs, d_skip, zero_carry, zero_carry)
    u_tb = jnp.transpose(u_s.reshape(bs, lsp, D_SSM)[:, :ls], (1, 0, 2))
    y_tb, hr_s, hi_s = _s5_sample(u_tb, s5w, ar, ai, d_skip, state_ssm_re[0].reshape(bs, N_STATE),
                                  state_ssm_im[0].reshape(bs, N_STATE))
    y_s = jnp.pad(jnp.transpose(y_tb, (1, 0, 2)), ((0, 0), (0, lsp - ls), (0, 0))).reshape(1, bs * lsp, D_SSM)

    cnt0 = jnp.zeros((SUBLANES, LANES), F32)
    n_p, n_s = bp * lp, bs * lsp
    n_tok = n_p + n_s
    x1_s, h2_s, route_s, cnt_s = _post(attn_s.reshape(1, bs * lsp, D_ATTN), y_s, xs_pad, mod_s, True, wpost, cnt0)
    x1_p, h2, route_p, cnt = _post(attn_p, y_p, x_prompt, mod_p, False, wpost, cnt_s, tail=h2_s)

    n_tiles = -(-(2 * n_tok + N_EXPERTS * (TM_MOE - 1)) // TM_MOE) + MOE_LOOKAHEAD
    route = jnp.concatenate([route_p.reshape(n_p, LANES), route_s.reshape(n_s, LANES)], 0)
    counts = cnt[0, ROUTE_LANE0:ROUTE_LANE0 + N_EXPERTS].astype(jnp.int32)
    padded = ((counts + TM_MOE - 1) // TM_MOE) * TM_MOE
    ends = jnp.cumsum(padded)
    offs = ends - padded
    col = lambda c: route[:, c].astype(jnp.int32)
    expert_ids = jnp.arange(N_EXPERTS, dtype=jnp.int32)
    seg_start = lambda e: jnp.sum(jnp.where(e[:, None] == expert_ids[None, :], offs[None, :], 0), axis=1)
    pos_k = [seg_start(col(k)) + col(4 + k) for k in range(2)]
    n_valid = ends[-1] // TM_MOE
    tile_row = jnp.minimum(jnp.arange(n_tiles), n_valid - 1) * TM_MOE
    tile_expert = jnp.sum((ends[None, :] <= tile_row[:, None]).astype(jnp.int32), axis=1)
    first_tile = jnp.concatenate([jnp.ones((1,), jnp.int32), (tile_expert[1:] != tile_expert[:-1]).astype(jnp.int32)])
    first_tile = first_tile * (jnp.arange(n_tiles) < n_valid)
    nonempty = padded > 0
    later = jnp.where(nonempty[None, :] & (expert_ids[None, :] > expert_ids[:, None]), expert_ids[None, :], N_EXPERTS)
    next_expert = jnp.min(later, axis=1)
    next_expert = jnp.where(next_expert < N_EXPERTS, next_expert, -1).astype(jnp.int32)
    weight_slot = ((jnp.cumsum(nonempty.astype(jnp.int32)) - 1) & 1).astype(jnp.int32)
    tile_meta = (tile_expert, first_tile.astype(jnp.int32), next_expert[tile_expert], weight_slot[tile_expert])
    token_ids = jnp.arange(n_tok, dtype=jnp.int32)
    row_token = (jnp.arange(n_tiles * TM_MOE, dtype=jnp.int32) % n_tok).at[jnp.concatenate(pos_k)].set(
        jnp.concatenate([token_ids, token_ids]), unique_indices=True, mode="promise_in_bounds")

    ys = _moe(tile_meta, n_valid.reshape(1).astype(jnp.int32), row_token, h2, exp_w_gate[0], exp_w_up[0],
              exp_w_down[0])
    pos_tiles = jnp.concatenate([p.reshape(n_tok // TM, 1, TM) for p in pos_k], axis=2)
    y_prompt = _final(pos_tiles, 0, x1_p, mod_p, False, route_p, ln2_g[0][None], ln2_b[0][None], ys)
    y_samp = _final(pos_tiles, n_p // TM, x1_s, mod_s, True, route_s, ln2_g[0][None], ln2_b[0][None], ys)
    y_sample = y_samp.reshape(bs, lsp, D_MODEL)[:, :ls]

    kv5 = lambda a, n: a.reshape(1, a.shape[0], n, N_KV_HEADS, HEAD_DIM)
    k_win_p = kv5(k_p[:, lp - WINDOW:], WINDOW)
    v_win_p = kv5(v_p[:, lp - WINDOW:], WINDOW)
    k_win_s = kv5(jnp.concatenate([ck[:, ls:], k_s3[:, :ls]], 1), WINDOW)
    v_win_s = kv5(jnp.concatenate([cv[:, ls:], v_s3[:, :ls]], 1), WINDOW)
    st = lambda a: a.reshape(1, a.shape[0], N_SSM_GROUPS, SSM_STATE)
    return (y_prompt, y_sample, k_win_p, v_win_p, st(hr_p[bp:2 * bp]), st(hi_p[bp:2 * bp]),
            k_win_s, v_win_s, st(hr_s), st(hi_s))
```

```python
import functools

import jax
import jax.numpy as jnp
from jax import lax
from jax.experimental import pallas as pl
from jax.experimental.pallas import tpu as pltpu

F32 = jnp.float32
BF16 = jnp.bfloat16

D_MODEL = 2048
D_ATTN = 1024
D_SSM = 1024
HEAD_DIM = 64
N_HEADS = 16
N_KV_HEADS = 4
REP = 4
D_KV = 256
ROT_DIM = 16
ROPE_THETA = 500000.0
WINDOW = 128
SSM_CH = 16
N_SSM_GROUPS = 64
SSM_STATE = 64
N_STATE = N_SSM_GROUPS * SSM_STATE
PROJ_COLS = D_ATTN + 2 * D_KV + D_SSM
N_EXPERT_GROUPS = 4
EXPERTS_PER_GROUP = 8
N_EXPERTS = 32
D_EXPERT = 512
DEPTH = 1
DEEPNORM_ALPHA = (2.0 * DEPTH) ** 0.25
LN_EPS = 1e-5
PAST_LEN = 16384

LANES = 128
SUBLANES = 8
MXU_DIM = 256
TM = 256
TM_MOE = 256
TM_INPROJ = 512
S5_TT = 128
SAMPLE_PAD = 8
ROUTE_LANE0 = N_EXPERT_GROUPS
VMEM_LIMIT = 56 * 1024 * 1024


def _cparams(sem):
    return pltpu.CompilerParams(dimension_semantics=sem, vmem_limit_bytes=VMEM_LIMIT)


def _ln(x):
    mu = jnp.mean(x, axis=-1, keepdims=True)
    xc = x - mu
    var = jnp.mean(xc * xc, axis=-1, keepdims=True)
    return xc * lax.rsqrt(var + LN_EPS)


def _rms(x):
    return x * lax.rsqrt(jnp.mean(x * x, axis=-1, keepdims=True) + LN_EPS)


def _ada_kernel(c_ref, w_ref, b_ref, o_ref):
    c = c_ref[...]
    s = c * jax.nn.sigmoid(c)
    o_ref[...] = jnp.dot(s.astype(BF16), w_ref[...].astype(BF16), preferred_element_type=F32) + b_ref[...]


def _ada(c_all, ada_w, ada_b):
    n, tn = c_all.shape[0], 1024
    return pl.pallas_call(
        _ada_kernel,
        out_shape=jax.ShapeDtypeStruct((n, 6 * D_MODEL), F32),
        grid=(6 * D_MODEL // tn,),
        in_specs=[pl.BlockSpec((n, D_MODEL), lambda j: (0, 0)),
                  pl.BlockSpec((D_MODEL, tn), lambda j: (0, j)),
                  pl.BlockSpec((1, tn), lambda j: (0, j))],
        out_specs=pl.BlockSpec((n, tn), lambda j: (0, j)),
        compiler_params=_cparams(("arbitrary",)),
        name="ada",
    )(c_all, ada_w, ada_b)


def _mod_spec(per_row, tm, chunk):
    if per_row:
        return pl.BlockSpec((None, tm, D_MODEL), lambda b, i: (b, i, chunk))
    return pl.BlockSpec((None, 1, D_MODEL), lambda b, i: (b, 0, chunk))


def _inproj_kernel(x_ref, sh_ref, sc_ref, w_ref, rc_ref, ra_ref, rb_ref, q_ref, k_ref, v_ref, u_ref):
    h = _ln(x_ref[...]) * (1.0 + sc_ref[...]) + sh_ref[...]
    proj = jnp.dot(h.astype(BF16), w_ref[...], preferred_element_type=F32)
    rc, ra, rb = rc_ref[...], ra_ref[...], rb_ref[...]

    def rope(t):
        return t * rc + pltpu.roll(t, LANES - ROT_DIM // 2, 1) * ra + pltpu.roll(t, ROT_DIM // 2, 1) * rb

    for j in range(D_ATTN // LANES):
        q_ref[:, j * LANES:(j + 1) * LANES] = (rope(proj[:, j * LANES:(j + 1) * LANES]) * HEAD_DIM ** -0.5).astype(BF16)
    for j in range(D_KV // LANES):
        c0 = D_ATTN + j * LANES
        k_ref[:, j * LANES:(j + 1) * LANES] = rope(proj[:, c0:c0 + LANES])
    v_ref[...] = proj[:, D_ATTN + D_KV:D_ATTN + 2 * D_KV]
    u_ref[...] = proj[:, D_ATTN + 2 * D_KV:]


def _inproj(x, mod, w_in_bf, rope_tabs, per_row):
    nb, l, _ = x.shape
    tm = min(TM_INPROJ, l)
    row = lambda w: pl.BlockSpec((None, tm, w), lambda b, i: (b, i, 0))
    tab = pl.BlockSpec((tm, LANES), lambda b, i: (i, 0))
    return pl.pallas_call(
        _inproj_kernel,
        out_shape=(jax.ShapeDtypeStruct((nb, l, D_ATTN), BF16), jax.ShapeDtypeStruct((nb, l, D_KV), F32),
                   jax.ShapeDtypeStruct((nb, l, D_KV), F32), jax.ShapeDtypeStruct((nb, l, D_SSM), F32)),
        grid=(nb, l // tm),
        in_specs=[row(D_MODEL), _mod_spec(per_row, tm, 0), _mod_spec(per_row, tm, 1),
                  pl.BlockSpec((D_MODEL, PROJ_COLS), lambda b, i: (0, 0)), tab, tab, tab],
        out_specs=(row(D_ATTN), row(D_KV), row(D_KV), row(D_SSM)),
        compiler_params=_cparams(("arbitrary", "arbitrary")),
        name="inproj",
    )(x, mod, mod, w_in_bf, *rope_tabs)


ATTN_SAMPLE_BATCH = 8


def _attn_kernel(sink_ref, q_ref, kp_ref, kc_ref, vp_ref, vc_ref, o_ref, *, lq):
    m_rows = REP * lq
    ii = lax.broadcasted_iota(jnp.int32, (m_rows, WINDOW), 0) & (lq - 1)
    jj = lax.broadcasted_iota(jnp.int32, (m_rows, WINDOW), 1)
    from_prev = jj > ii
    dead = jj > ii + jnp.where(pl.program_id(1) > 0, WINDOW, 0)
    rr = lax.broadcasted_iota(jnp.int32, (m_rows, 1), 0)
    dn = (((1,), (1,)), ((), ()))
    for bi in range(q_ref.shape[0]):
        for g in range(N_KV_HEADS):
            qg = q_ref[bi, :, g * REP * HEAD_DIM:(g + 1) * REP * HEAD_DIM].astype(F32)
            qs = jnp.concatenate([qg[:, r * HEAD_DIM:(r + 1) * HEAD_DIM] for r in range(REP)], axis=0).astype(BF16)
            ks = slice(g * HEAD_DIM, (g + 1) * HEAD_DIM)
            s_p = lax.dot_general(qs, kp_ref[bi, :, ks].astype(BF16), dn, preferred_element_type=F32)
            s_c = lax.dot_general(qs, kc_ref[bi, :, ks].astype(BF16), dn, preferred_element_type=F32)
            s = jnp.where(dead, -jnp.inf, jnp.where(from_prev, s_p, s_c))
            sink = jnp.zeros((m_rows, 1), F32)
            for r in range(REP):
                sink = jnp.where((rr >= r * lq) & (rr < (r + 1) * lq), sink_ref[g * REP + r], sink)
            m = jnp.maximum(jnp.max(s, axis=-1, keepdims=True), sink)
            p = jnp.exp(s - m)
            p = p / (jnp.sum(p, axis=-1, keepdims=True) + jnp.exp(sink - m))
            o = (jnp.dot(jnp.where(from_prev, p, 0.0).astype(BF16), vp_ref[bi, :, ks].astype(BF16),
                         preferred_element_type=F32)
                 + jnp.dot(jnp.where(from_prev, 0.0, p).astype(BF16), vc_ref[bi, :, ks].astype(BF16),
                           preferred_element_type=F32))
            for r in range(REP):
                h = g * REP + r
                o_ref[bi, :, h * HEAD_DIM:(h + 1) * HEAD_DIM] = o[r * lq:(r + 1) * lq]


def _attn_prompt(sinks, q, k, v):
    nb, l, _ = q.shape
    nblk = l // WINDOW
    cur = lambda w: pl.BlockSpec((1, WINDOW, w), lambda b, n: (b, n, 0))
    prev = lambda w: pl.BlockSpec((1, WINDOW, w), lambda b, n: (b, jnp.maximum(n - 1, 0), 0))
    return pl.pallas_call(
        functools.partial(_attn_kernel, lq=WINDOW),
        out_shape=jax.ShapeDtypeStruct((nb, l, D_ATTN), F32),
        grid=(nb, nblk),
        in_specs=[pl.BlockSpec(memory_space=pltpu.SMEM), cur(D_ATTN), prev(D_KV), cur(D_KV), prev(D_KV), cur(D_KV)],
        out_specs=cur(D_ATTN),
        compiler_params=_cparams(("arbitrary", "arbitrary")),
        name="attn_prompt",
    )(sinks, q, k, k, v, v)


ATTN_SAMPLE_JOINT = 4


def _attn_sample_kernel(sink_ref, q_ref, kp_ref, kc_ref, vp_ref, vc_ref, o_ref, *, lq):
    gb = ATTN_SAMPLE_JOINT
    m_b = REP * lq
    m, n = gb * m_b, gb * WINDOW
    row = lax.broadcasted_iota(jnp.int32, (m, n), 0)
    col = lax.broadcasted_iota(jnp.int32, (m, n), 1)
    same = (row // m_b) == (col // WINDOW)
    from_prev = (col & (WINDOW - 1)) > (row & (lq - 1))
    rep_of_row = (lax.broadcasted_iota(jnp.int32, (m, 1), 0) // lq) % REP
    dn = (((1,), (1,)), ((), ()))
    for blk in range(q_ref.shape[0] // gb):
        bs = range(blk * gb, (blk + 1) * gb)
        for g in range(N_KV_HEADS):
            qg = [q_ref[b, :, g * REP * HEAD_DIM:(g + 1) * REP * HEAD_DIM].astype(F32) for b in bs]
            qs = jnp.concatenate([qb[:, r * HEAD_DIM:(r + 1) * HEAD_DIM] for qb in qg for r in range(REP)],
                                 axis=0).astype(BF16)
            ks = slice(g * HEAD_DIM, (g + 1) * HEAD_DIM)
            stack = lambda ref: jnp.concatenate([ref[b, :, ks] for b in bs], axis=0).astype(BF16)
            s_p = lax.dot_general(qs, stack(kp_ref), dn, preferred_element_type=F32)
            s_c = lax.dot_general(qs, stack(kc_ref), dn, preferred_element_type=F32)
            s = jnp.where(same, jnp.where(from_prev, s_p, s_c), -jnp.inf)
            sink = jnp.zeros((m, 1), F32)
            for r in range(REP):
                sink = jnp.where(rep_of_row == r, sink_ref[g * REP + r], sink)
            mx = jnp.maximum(jnp.max(s, axis=-1, keepdims=True), sink)
            p = jnp.exp(s - mx)
            p = p / (jnp.sum(p, axis=-1, keepdims=True) + jnp.exp(sink - mx))
            o = (jnp.dot(jnp.where(from_prev, p, 0.0).astype(BF16), stack(vp_ref), preferred_element_type=F32)
                 + jnp.dot(jnp.where(from_prev, 0.0, p).astype(BF16), stack(vc_ref), preferred_element_type=F32))
            for bi, b in enumerate(bs):
                for r in range(REP):
                    h = g * REP + r
                    o_ref[b, :, h * HEAD_DIM:(h + 1) * HEAD_DIM] = o[bi * m_b + r * lq:bi * m_b + (r + 1) * lq]


def _attn_sample(sinks, q, k_cache, k_new, v_cache, v_new):
    nb, lq, _ = q.shape
    nbb = ATTN_SAMPLE_BATCH
    kblk = pl.BlockSpec((nbb, WINDOW, D_KV), lambda b: (b, 0, 0))
    return pl.pallas_call(
        functools.partial(_attn_sample_kernel, lq=lq),
        out_shape=jax.ShapeDtypeStruct((nb, lq, D_ATTN), F32),
        grid=(nb // nbb,),
        in_specs=[pl.BlockSpec(memory_space=pltpu.SMEM), pl.BlockSpec((nbb, lq, D_ATTN), lambda b: (b, 0, 0)),
                  kblk, kblk, kblk, kblk],
        out_specs=pl.BlockSpec((nbb, lq, D_ATTN), lambda b: (b, 0, 0)),
        compiler_params=_cparams(("arbitrary",)),
        name="attn_sample",
    )(sinks, q, k_cache, k_new, v_cache, v_new)


N_KT = D_SSM // MXU_DIM
ST_PER_KT = N_STATE // N_KT
N_SLAB = D_SSM // LANES


def _cproj(hr_ref, hi_ref, cre_ref, cimn_ref, kt):
    cs = slice(kt * ST_PER_KT, (kt + 1) * ST_PER_KT)
    return (jnp.dot(hr_ref[:, cs].astype(BF16), cre_ref[kt], preferred_element_type=F32)
            + jnp.dot(hi_ref[:, cs].astype(BF16), cimn_ref[kt], preferred_element_type=F32))


def _s5_prompt_kernel(u_ref, bre_ref, bim_ref, cre_ref, cimn_ref, a1r_ref, a1i_ref, par_ref, pai_ref, d_ref,
                      h0r_ref, h0i_ref, y_ref, cr_ref, ci_ref, il_ref, xr_ref, xi_ref, *, nb, tt):
    rows = nb * tt

    @pl.when(pl.program_id(0) == 0)
    def _():
        cr_ref[...] = h0r_ref[...]
        ci_ref[...] = h0i_ref[...]

    for b in range(nb):
        for j in range(N_SLAB):
            il_ref[j, pl.ds(b, tt, stride=nb), :] = u_ref[b, :, j * LANES:(j + 1) * LANES]
    for kt in range(N_KT):
        ub = jnp.concatenate([il_ref[2 * kt], il_ref[2 * kt + 1]], axis=1).astype(BF16)
        cs = slice(kt * ST_PER_KT, (kt + 1) * ST_PER_KT)
        xr_ref[:, cs] = jnp.dot(ub, bre_ref[kt], preferred_element_type=F32)
        xi_ref[:, cs] = jnp.dot(ub, bim_ref[kt], preferred_element_type=F32)

    half = lax.broadcasted_iota(jnp.int32, (SUBLANES, LANES), 0) < nb

    def step(i, carry):
        r0 = pl.multiple_of(i * SUBLANES, SUBLANES)
        for j in range(N_STATE // LANES):
            ls = slice(j * LANES, (j + 1) * LANES)
            x_r = xr_ref[pl.ds(r0, SUBLANES), ls]
            x_i = xi_ref[pl.ds(r0, SUBLANES), ls]
            s_r = pltpu.roll(x_r, nb, 0)
            s_i = pltpu.roll(x_i, nb, 0)
            a1r, a1i = a1r_ref[:, ls], a1i_ref[:, ls]
            c_r, c_i = cr_ref[:, ls], ci_ref[:, ls]
            par, pai = par_ref[:, ls], pai_ref[:, ls]
            h_r = x_r + (a1r * s_r - a1i * s_i) + (par * c_r - pai * c_i)
            h_i = x_i + (a1r * s_i + a1i * s_r) + (par * c_i + pai * c_r)
            xr_ref[pl.ds(r0, SUBLANES), ls] = h_r
            xi_ref[pl.ds(r0, SUBLANES), ls] = h_i
            cr_ref[:, ls] = jnp.where(half, pltpu.roll(h_r, nb, 0), h_r)
            ci_ref[:, ls] = jnp.where(half, pltpu.roll(h_i, nb, 0), h_i)
        return carry

    lax.fori_loop(0, rows // SUBLANES, step, 0)

    for kt in range(N_KT):
        y = _cproj(xr_ref, xi_ref, cre_ref, cimn_ref, kt)
        il_ref[2 * kt] = y[:, :LANES]
        il_ref[2 * kt + 1] = y[:, LANES:]
    for b in range(nb):
        for j in range(N_SLAB):
            ls = slice(j * LANES, (j + 1) * LANES)
            y_ref[b, :, ls] = il_ref[j, pl.ds(b, tt, stride=nb), :] + d_ref[:, ls] * u_ref[b, :, ls]


def _s5_prompt(u, wts, tabs, d_skip, h0r, h0i):
    nb, l, _ = u.shape
    tt = S5_TT
    rows = nb * tt
    full = lambda a: pl.BlockSpec(a.shape, lambda i: (0,) * a.ndim)
    blk = pl.BlockSpec((nb, tt, D_SSM), lambda i: (0, i, 0))
    carry = jax.ShapeDtypeStruct((SUBLANES, N_STATE), F32)
    return pl.pallas_call(
        functools.partial(_s5_prompt_kernel, nb=nb, tt=tt),
        out_shape=(jax.ShapeDtypeStruct((nb, l, D_SSM), F32), carry, carry),
        grid=(l // tt,),
        in_specs=[blk] + [full(a) for a in (*wts, *tabs, d_skip, h0r, h0i)],
        out_specs=(blk, pl.BlockSpec((SUBLANES, N_STATE), lambda i: (0, 0)),
                   pl.BlockSpec((SUBLANES, N_STATE), lambda i: (0, 0))),
        scratch_shapes=[pltpu.VMEM((N_SLAB, rows, LANES), F32), pltpu.VMEM((rows, N_STATE), F32),
                        pltpu.VMEM((rows, N_STATE), F32)],
        compiler_params=_cparams(("arbitrary",)),
        name="s5_prompt",
    )(u, *wts, *tabs, d_skip, h0r, h0i)


def _s5_sample_kernel(u_ref, bre_ref, bim_ref, cre_ref, cimn_ref, ar_ref, ai_ref, d_ref, h0r_ref, h0i_ref,
                      y_ref, sr_ref, si_ref, xr_ref, xi_ref, *, nt):
    sr_ref[...] = h0r_ref[...]
    si_ref[...] = h0i_ref[...]
    for t in range(nt):
        for kt in range(N_KT):
            ub = u_ref[t, :, kt * MXU_DIM:(kt + 1) * MXU_DIM].astype(BF16)
            cs = slice(kt * ST_PER_KT, (kt + 1) * ST_PER_KT)
            xr_ref[:, cs] = jnp.dot(ub, bre_ref[kt], preferred_element_type=F32)
            xi_ref[:, cs] = jnp.dot(ub, bim_ref[kt], preferred_element_type=F32)
        ar, ai = ar_ref[...], ai_ref[...]
        s_r, s_i = sr_ref[...], si_ref[...]
        sr_ref[...] = xr_ref[...] + (ar * s_r - ai * s_i)
        si_ref[...] = xi_ref[...] + (ar * s_i + ai * s_r)
        for kt in range(N_KT):
            ys = slice(kt * MXU_DIM, (kt + 1) * MXU_DIM)
            y_ref[t, :, ys] = _cproj(sr_ref, si_ref, cre_ref, cimn_ref, kt) + d_ref[:, ys] * u_ref[t, :, ys]


def _s5_sample(u_tb, wts, ar, ai, d_skip, h0r, h0i):
    nt, nb, _ = u_tb.shape
    st = jax.ShapeDtypeStruct((nb, N_STATE), F32)
    args = (u_tb, *wts, ar, ai, d_skip, h0r, h0i)
    full = lambda a: pl.BlockSpec(a.shape, lambda i: (0,) * a.ndim)
    return pl.pallas_call(
        functools.partial(_s5_sample_kernel, nt=nt),
        out_shape=(jax.ShapeDtypeStruct((nt, nb, D_SSM), F32), st, st),
        grid=(1,),
        in_specs=[full(a) for a in args],
        out_specs=(pl.BlockSpec((nt, nb, D_SSM), lambda i: (0, 0, 0)), pl.BlockSpec((nb, N_STATE), lambda i: (0, 0)),
                   pl.BlockSpec((nb, N_STATE), lambda i: (0, 0))),
        scratch_shapes=[pltpu.VMEM((nb, N_STATE), F32), pltpu.VMEM((nb, N_STATE), F32)],
        compiler_params=_cparams(("arbitrary",)),
        name="s5_sample",
    )(*args)


def _gather_group(src_ref, ids_ref, id_stride, id_offset, dst_ref, sem, priorities, g):
    for j in range(SUBLANES):
        t = ids_ref[0, id_stride * (g * SUBLANES + j) + id_offset]
        pltpu.make_async_copy(src_ref.at[t >> 3, pl.ds(t & (SUBLANES - 1), 1)], dst_ref.at[g, pl.ds(j, 1)],
                              sem).start(priority=priorities[j % len(priorities)])


def _gather_rows(src_ref, ids_ref, id_stride, id_offset, dst_ref, sem, priorities):
    def body(g, c):
        _gather_group(src_ref, ids_ref, id_stride, id_offset, dst_ref, sem, priorities, g)
        return c
    lax.fori_loop(0, dst_ref.shape[0], body, 0)


def _index_specs(n_cols, first_tile, n_steps, depth=1):
    def spec(fn):
        return pl.BlockSpec((None, 1, n_cols), fn, memory_space=pltpu.SMEM)

    head = [spec(lambda s, *_, k=k: (first_tile + min(k, n_steps - 1), 0, 0)) for k in range(depth)]
    return (*head, spec(lambda s, *_: (first_tile + jnp.minimum(s + depth, n_steps - 1), 0, 0)))


N_POST_INPUTS = 17


def _post_kernel(*refs, has_tail):
    ins, outs = refs[:N_POST_INPUTS], refs[N_POST_INPUTS + int(has_tail):]
    if not has_tail:
        _post_body(*ins, *outs)
        return
    tail_ref, h2_ref = refs[N_POST_INPUTS], outs[1]
    last = pl.num_programs(0) - 1

    @pl.when(pl.program_id(0) == last)
    def _():
        h2_ref[...] = tail_ref[...]

    @pl.when(pl.program_id(0) < last)
    def _():
        _post_body(*ins, *outs)


def _post_body(attn_ref, yssm_ref, x_ref, g1_ref, sh2_ref, sc2_ref, gluw_ref, glub_ref, ga_ref, gs_ref, wout_ref,
               l1g_ref, l1b_ref, wr_ref, br_ref, tri_ref, cnt0_ref, x1_ref, h2_ref, route_ref, cnt_ref):
    tm = x_ref.shape[0]

    @pl.when(pl.program_id(0) == 0)
    def _():
        cnt_ref[...] = cnt0_ref[...]

    z = jax.nn.gelu(yssm_ref[...])
    ssm = z * jax.nn.sigmoid(jnp.dot(z.astype(BF16), gluw_ref[...], preferred_element_type=F32) + glub_ref[...])
    mixed_a = (_rms(attn_ref[...]) * ga_ref[...]).astype(BF16)
    mixed_s = (_rms(ssm) * gs_ref[...]).astype(BF16)
    o = (jnp.dot(mixed_a, wout_ref[:D_ATTN, :], preferred_element_type=F32)
         + jnp.dot(mixed_s, wout_ref[D_ATTN:, :], preferred_element_type=F32))
    x1 = _ln(DEEPNORM_ALPHA * x_ref[...] + g1_ref[...] * o) * l1g_ref[...] + l1b_ref[...]
    x1_ref[...] = x1
    h2 = _ln(x1) * (1.0 + sc2_ref[...]) + sh2_ref[...]
    h2_ref[...] = h2
    logits = jnp.dot(h2.astype(BF16), wr_ref[...], preferred_element_type=F32) + br_ref[...]

    lane = lax.broadcasted_iota(jnp.int32, (tm, LANES), 1).astype(F32)
    big = float(4 * LANES)
    neg = -jnp.inf
    gl = jnp.where(lane < N_EXPERT_GROUPS, logits, neg)
    gp = jnp.exp(gl - jnp.max(gl, axis=-1, keepdims=True))
    gp = gp / jnp.sum(gp, axis=-1, keepdims=True)
    g_val = jnp.max(gp, axis=-1, keepdims=True)
    g_idx = jnp.min(jnp.where(gp == g_val, lane, big), axis=-1, keepdims=True)
    lo = ROUTE_LANE0 + EXPERTS_PER_GROUP * g_idx
    emask = (lane >= lo) & (lane < lo + EXPERTS_PER_GROUP)
    el = jnp.where(emask, logits, neg)
    ep = jnp.exp(el - jnp.max(el, axis=-1, keepdims=True))
    ep = jnp.where(emask, ep / jnp.sum(ep, axis=-1, keepdims=True), -1.0)
    v1 = jnp.max(ep, axis=-1, keepdims=True)
    i1 = jnp.min(jnp.where(ep == v1, lane, big), axis=-1, keepdims=True)
    ep2 = jnp.where(lane == i1, -1.0, ep)
    v2 = jnp.max(ep2, axis=-1, keepdims=True)
    i2 = jnp.min(jnp.where(ep2 == v2, lane, big), axis=-1, keepdims=True)
    vs = v1 + v2
    w1 = g_val * (v1 / vs)
    w2 = g_val * (v2 / vs)
    hit = jnp.where((lane == i1) | (lane == i2), 1.0, 0.0)
    before = jnp.dot(tri_ref[...], hit.astype(BF16), preferred_element_type=F32) + cnt_ref[0:1, :]
    r1 = jnp.sum(jnp.where(lane == i1, before, 0.0), axis=-1, keepdims=True)
    r2 = jnp.sum(jnp.where(lane == i2, before, 0.0), axis=-1, keepdims=True)
    cnt_ref[...] = cnt_ref[...] + jnp.sum(hit, axis=0, keepdims=True)
    e1 = i1 - ROUTE_LANE0
    e2 = i2 - ROUTE_LANE0
    route = jnp.zeros((tm, LANES), F32)
    for n, val in enumerate((e1, e2, w1, w2, r1, r2)):
        route = jnp.where(lane == n, val, route)
    route_ref[...] = route


def _post(attn, yssm, x, mod, per_row, w, cnt0, tail=None):
    nb, l, _ = x.shape
    tm = min(TM, l)
    nt = l // tm
    n_body = nb * nt
    has_tail = tail is not None
    if has_tail:
        assert tail.shape == (tm, D_MODEL)

    def bi(s):
        s = jnp.minimum(s, n_body - 1)
        return s // nt, s % nt

    row = lambda wd: pl.BlockSpec((None, tm, wd), lambda s: (*bi(s), 0))
    if per_row:
        mspec = lambda chunk: pl.BlockSpec((None, tm, D_MODEL), lambda s: (*bi(s), chunk))
    else:
        mspec = lambda chunk: pl.BlockSpec((None, 1, D_MODEL), lambda s: (bi(s)[0], 0, chunk))
    full = lambda a: pl.BlockSpec(a.shape, lambda s: (0,) * a.ndim)
    tri = jnp.tril(jnp.ones((tm, tm), F32), -1).astype(BF16)
    consts = (w["glu_w"], w["glu_b"], w["attn_g"], w["ssm_g"], w["w_out"], w["ln1_g"], w["ln1_b"], w["wr"], w["br"],
              tri, cnt0) + ((tail,) if has_tail else ())
    assert 6 + len(consts) == N_POST_INPUTS + int(has_tail)
    n_steps = n_body + int(has_tail)
    return pl.pallas_call(
        functools.partial(_post_kernel, has_tail=has_tail),
        out_shape=(jax.ShapeDtypeStruct((nb, l, D_MODEL), F32),
                   jax.ShapeDtypeStruct((n_steps * tm, D_MODEL), F32),
                   jax.ShapeDtypeStruct((nb, l, LANES), F32), jax.ShapeDtypeStruct((SUBLANES, LANES), F32)),
        grid=(n_steps,),
        in_specs=[row(D_ATTN), row(D_SSM), row(D_MODEL), mspec(2), mspec(3), mspec(4)] + [full(a) for a in consts],
        out_specs=(row(D_MODEL), pl.BlockSpec((tm, D_MODEL), lambda s: (s, 0)),
                   row(LANES), pl.BlockSpec((SUBLANES, LANES), lambda s: (0, 0))),
        compiler_params=_cparams(("arbitrary",)),
        name="post",
    )(attn, yssm, x, mod, mod, mod, *consts)


MOE_LOOKAHEAD = 2
MOE_SLOTS = MOE_LOOKAHEAD + 1


def _moe_kernel(te_ref, nv_ref, first_ref, nxt_ref, par_ref, rt0_ref, rt1_ref, rtn_ref, h2_ref, wg_hbm, wu_hbm, wd_hbm,
                y_ref, buf_ref, wgf_ref, wuf_ref, wdf_ref, sem, wsem):
    i = pl.program_id(0)
    nv = nv_ref[0]
    slot = lax.rem(i, MOE_SLOTS)
    w_pairs = ((wg_hbm, wgf_ref), (wu_hbm, wuf_ref), (wd_hbm, wdf_ref))

    def gather(ids_ref, dst_slot):
        _gather_rows(h2_ref, ids_ref, 1, 0, buf_ref.at[dst_slot], sem.at[dst_slot], (0, 1))

    def fetch_weights(e, ws):
        for src, dst in w_pairs:
            pltpu.make_async_copy(src.at[e], dst.at[ws], wsem.at[ws]).start(priority=1)

    @pl.when(i == 0)
    def _():
        gather(rt0_ref, 0)
        gather(rt1_ref, 1)
        fetch_weights(te_ref[0], par_ref[0])

    @pl.when(i < nv + MOE_LOOKAHEAD)
    def _():
        pltpu.make_async_copy(h2_ref.at[pl.ds(0, TM_MOE // SUBLANES)], buf_ref.at[slot], sem.at[slot]).wait()

    @pl.when((i < nv) & (first_ref[i] == 1))
    def _():
        ws = par_ref[i]
        for src, dst in w_pairs:
            pltpu.make_async_copy(src.at[0], dst.at[ws], wsem.at[ws]).wait()

        @pl.when(nxt_ref[i] >= 0)
        def _():
            fetch_weights(nxt_ref[i], 1 - ws)

    @pl.when(i < nv)
    def _():
        nslot = lax.rem(i + MOE_LOOKAHEAD, MOE_SLOTS)
        ws = par_ref[i]
        n_kc = D_MODEL // MXU_DIM
        groups_per_kc = TM_MOE // SUBLANES // n_kc
        hg = hu = None
        for kc in range(n_kc):
            for gg in range(groups_per_kc):
                _gather_group(h2_ref, rtn_ref, 1, 0, buf_ref.at[nslot], sem.at[nslot], (0, 1), kc * groups_per_kc + gg)
            ks = slice(kc * MXU_DIM, (kc + 1) * MXU_DIM)
            xk = buf_ref[slot, :, :, ks].reshape(TM_MOE, MXU_DIM).astype(BF16)
            pg = jnp.dot(xk, wgf_ref[ws, ks, :].astype(BF16), preferred_element_type=F32)
            pu = jnp.dot(xk, wuf_ref[ws, ks, :].astype(BF16), preferred_element_type=F32)
            hg = pg if hg is None else hg + pg
            hu = pu if hu is None else hu + pu
        act = (hg * jax.nn.sigmoid(hg)) * hu
        y_ref[...] = jnp.dot(act.astype(BF16), wdf_ref[ws].astype(BF16), preferred_element_type=F32)

    @pl.when(i >= nv)
    def _():
        y_ref[...] = jnp.zeros(y_ref.shape, y_ref.dtype)


def _moe(tile_meta, n_valid, row_token, h2, w_gate, w_up, w_down):
    n_steps = row_token.shape[0] // TM_MOE
    rt = row_token.reshape(n_steps, 1, TM_MOE)
    te, first, nxt, par = tile_meta
    any_spec = pl.BlockSpec(memory_space=pl.ANY)
    return pl.pallas_call(
        _moe_kernel,
        out_shape=jax.ShapeDtypeStruct((n_steps * TM_MOE, D_MODEL), F32),
        grid_spec=pltpu.PrefetchScalarGridSpec(
            num_scalar_prefetch=5, grid=(n_steps,),
            in_specs=[*_index_specs(TM_MOE, 0, n_steps, MOE_LOOKAHEAD), any_spec, any_spec, any_spec, any_spec],
            out_specs=pl.BlockSpec((TM_MOE, D_MODEL), lambda i, *_: (i, 0)),
            scratch_shapes=[pltpu.VMEM((MOE_SLOTS, TM_MOE // SUBLANES, SUBLANES, D_MODEL), F32),
                            pltpu.VMEM((2, D_MODEL, D_EXPERT), F32), pltpu.VMEM((2, D_MODEL, D_EXPERT), F32),
                            pltpu.VMEM((2, D_EXPERT, D_MODEL), F32),
                            pltpu.SemaphoreType.DMA((MOE_SLOTS,)), pltpu.SemaphoreType.DMA((2,))]),
        compiler_params=_cparams(("arbitrary",)),
        name="moe",
    )(te, n_valid, first, nxt, par, rt, rt, rt, h2.reshape(-1, SUBLANES, D_MODEL), w_gate, w_up, w_down)


def _final_kernel(pos0_ref, posn_ref, x1_ref, g2_ref, route_ref, l2g_ref, l2b_ref, ys_ref, o_ref, buf_ref, sem, *, tm):
    step = pl.program_id(0)
    slot = step & 1

    def gather(ids_ref, dst_slot):
        for k in range(2):
            _gather_rows(ys_ref, ids_ref, 1, k * tm, buf_ref.at[dst_slot, k], sem.at[dst_slot], (0, 1))

    @pl.when(step == 0)
    def _():
        gather(pos0_ref, 0)

    for k in range(2):
        pltpu.make_async_copy(ys_ref.at[pl.ds(0, tm // SUBLANES)], buf_ref.at[slot, k], sem.at[slot]).wait()

    route = route_ref[...]
    w1, w2 = route[:, 2:3], route[:, 3:4]
    n_groups = tm // SUBLANES

    def combine(issue_next):
        n_chunks = D_MODEL // MXU_DIM
        per_chunk = 2 * n_groups // n_chunks
        for c in range(n_chunks):
            if issue_next:
                for q in range(c * per_chunk, (c + 1) * per_chunk):
                    k, g = q // n_groups, q % n_groups
                    _gather_group(ys_ref, posn_ref, 1, k * tm, buf_ref.at[1 - slot, k], sem.at[1 - slot], (0, 1), g)
            cs = slice(c * MXU_DIM, (c + 1) * MXU_DIM)
            f = (w1 * buf_ref[slot, 0, :, :, cs].reshape(tm, MXU_DIM)
                 + w2 * buf_ref[slot, 1, :, :, cs].reshape(tm, MXU_DIM))
            o_ref[:, cs] = DEEPNORM_ALPHA * x1_ref[:, cs] + g2_ref[:, cs] * f

    @pl.when(step + 1 < pl.num_programs(0))
    def _():
        combine(True)

    @pl.when(step + 1 == pl.num_programs(0))
    def _():
        combine(False)

    o_ref[...] = _ln(o_ref[...]) * l2g_ref[...] + l2b_ref[...]


def _final(pos_tiles, first_tile, x1, mod, per_row, route, ln2_g, ln2_b, ys):
    nb, l, _ = x1.shape
    tm = min(TM, l)
    nt = l // tm
    n_steps = nb * nt
    row = lambda wd: pl.BlockSpec((None, tm, wd), lambda s: (s // nt, s % nt, 0))
    if per_row:
        g2 = pl.BlockSpec((None, tm, D_MODEL), lambda s: (s // nt, s % nt, 5))
    else:
        g2 = pl.BlockSpec((None, 1, D_MODEL), lambda s: (s // nt, 0, 5))
    vec = pl.BlockSpec((1, D_MODEL), lambda s: (0, 0))
    return pl.pallas_call(
        functools.partial(_final_kernel, tm=tm),
        out_shape=jax.ShapeDtypeStruct((nb, l, D_MODEL), F32),
        grid=(n_steps,),
        in_specs=[*_index_specs(2 * tm, first_tile, n_steps), row(D_MODEL), g2, row(LANES), vec, vec,
                  pl.BlockSpec(memory_space=pl.ANY)],
        out_specs=row(D_MODEL),
        scratch_shapes=[pltpu.VMEM((2, 2, tm // SUBLANES, SUBLANES, D_MODEL), F32), pltpu.SemaphoreType.DMA((2,))],
        compiler_params=_cparams(("arbitrary",)),
        name="final",
    )(pos_tiles, pos_tiles, x1, mod, route, ln2_g, ln2_b, ys.reshape(-1, SUBLANES, D_MODEL))


def _rope_tables(pos):
    half = ROT_DIM // 2
    inv_freq = ROPE_THETA ** (-jnp.arange(half, dtype=jnp.float32) * 2.0 / ROT_DIM)
    ang = pos.astype(jnp.float32)[:, None] * inv_freq[None, :]
    cos, sin = jnp.cos(ang), jnp.sin(ang)
    n = pos.shape[0]
    one = jnp.ones((n, HEAD_DIM - ROT_DIM), F32)
    zero = jnp.zeros((n, HEAD_DIM - half), F32)
    c = jnp.concatenate([cos, cos, one], -1)
    a = jnp.concatenate([-sin, zero], -1)
    b = jnp.concatenate([jnp.zeros((n, half), F32), sin, jnp.zeros((n, HEAD_DIM - ROT_DIM), F32)], -1)
    return tuple(jnp.tile(t, (1, LANES // HEAD_DIM)) for t in (c, a, b))


def _s5_params(lam_re, lam_im, log_step, b_re, b_im, c_re, c_im):
    f32 = jnp.float32
    dt = jnp.exp(log_step.astype(f32))[:, None]
    lr, li = lam_re.astype(f32), lam_im.astype(f32)
    mag = jnp.exp(lr * dt)
    ar, ai = mag * jnp.cos(li * dt), mag * jnp.sin(li * dt)
    den = lr * lr + li * li
    cr = ((ar - 1.0) * lr + ai * li) / den
    ci = (ai * lr - (ar - 1.0) * li) / den
    br, bi = b_re.astype(f32), b_im.astype(f32)
    bbr = cr[..., None] * br - ci[..., None] * bi
    bbi = cr[..., None] * bi + ci[..., None] * br
    gpt = MXU_DIM // SSM_CH
    ch_group = jnp.arange(MXU_DIM) // SSM_CH
    st_group = jnp.arange(ST_PER_KT) // SSM_STATE

    def pack_b(m):
        rows = jnp.transpose(m.reshape(N_KT, gpt, SSM_STATE, SSM_CH), (0, 1, 3, 2)).reshape(N_KT, MXU_DIM, SSM_STATE)
        keep = ch_group[:, None] == st_group[None, :]
        return jnp.where(keep[None], jnp.tile(rows, (1, 1, gpt)), 0.0).astype(BF16)

    def pack_c(m):
        cols = jnp.transpose(m.astype(f32).reshape(N_KT, gpt, SSM_CH, SSM_STATE), (0, 3, 1, 2))
        cols = cols.reshape(N_KT, SSM_STATE, MXU_DIM)
        keep = st_group[:, None] == ch_group[None, :]
        return jnp.where(keep[None], jnp.tile(cols, (1, gpt, 1)), 0.0).astype(BF16)

    wts = (pack_b(bbr), pack_b(bbi), pack_c(c_re), pack_c(-c_im.astype(f32)))
    return wts, ar.reshape(1, N_STATE), ai.reshape(1, N_STATE)


def _scan_tables(ar, ai, nb):
    assert SUBLANES // nb == 2
    a2r, a2i = ar * ar - ai * ai, 2.0 * ar * ai
    z = jnp.zeros_like(ar)
    rep = lambda first, second: jnp.concatenate([jnp.tile(first, (nb, 1)), jnp.tile(second, (nb, 1))], 0)
    return rep(z, ar), rep(z, ai), rep(ar, a2r), rep(ai, a2i)


def kernel(x_prompt, x_sample, cache_k_win, cache_v_win, state_ssm_re, state_ssm_im, c_prompt, c_sample, ada_w, ada_b,
           w_in, attn_sinks, ssm_lambda_re, ssm_lambda_im, ssm_log_step, ssm_b_re, ssm_b_im, ssm_c_re, ssm_c_im, ssm_d,
           ssm_glu_w, ssm_glu_b, attn_norm_g, ssm_norm_g, w_out, ln1_g, ln1_b, router_group_w, router_group_b,
           router_expert_w, router_expert_b, exp_w_gate, exp_w_up, exp_w_down, ln2_g, ln2_b):
    assert ada_w.shape[0] == DEPTH
    bp, lp, _ = x_prompt.shape
    bs, ls, _ = x_sample.shape
    lsp = SAMPLE_PAD

    w_in_bf = w_in[0].astype(BF16)
    wr = jnp.concatenate([router_group_w[0], router_expert_w[0]], -1)
    wr = jnp.pad(wr, ((0, 0), (0, LANES - wr.shape[1]))).astype(BF16)
    br = jnp.pad(jnp.concatenate([router_group_b[0], router_expert_b[0]], -1), (0, LANES - N_EXPERT_GROUPS - N_EXPERTS))
    wpost = dict(glu_w=ssm_glu_w[0].astype(BF16), glu_b=ssm_glu_b[0][None], attn_g=attn_norm_g[0][None],
                 ssm_g=ssm_norm_g[0][None], w_out=w_out[0].astype(BF16), ln1_g=ln1_g[0][None], ln1_b=ln1_b[0][None],
                 wr=wr, br=br[None])
    s5w, ar, ai = _s5_params(ssm_lambda_re[0], ssm_lambda_im[0], ssm_log_step[0], ssm_b_re[0], ssm_b_im[0],
                             ssm_c_re[0], ssm_c_im[0])
    d_skip = ssm_d[0][None]
    sinks = attn_sinks[0]

    n_c = bp + bs
    n_cp = -(-n_c // SUBLANES) * SUBLANES
    c_all = jnp.pad(jnp.concatenate([c_prompt, c_sample], 0), ((0, n_cp - n_c), (0, 0)))
    mod = _ada(c_all, ada_w[0], ada_b[0][None])
    mod_p = mod[:bp][:, None, :]
    mod_s = jnp.repeat(mod[bp:n_c], lsp, axis=0)[None]

    xs_pad = jnp.pad(x_sample, ((0, 0), (0, lsp - ls), (0, 0))).reshape(1, bs * lsp, D_MODEL)

    rope_p = _rope_tables(jnp.arange(lp))
    pos_s = PAST_LEN + jnp.minimum(jnp.arange(lsp), ls - 1)
    rope_s = tuple(jnp.tile(t, (bs, 1)) for t in _rope_tables(pos_s))
    q_p, k_p, v_p, u_p = _inproj(x_prompt, mod_p, w_in_bf, rope_p, False)
    q_s, k_s, v_s, u_s = _inproj(xs_pad, mod_s, w_in_bf, rope_s, True)

    attn_p = _attn_prompt(sinks, q_p, k_p, v_p)
    k_s3 = k_s.reshape(bs, lsp, D_KV)
    v_s3 = v_s.reshape(bs, lsp, D_KV)
    ck = cache_k_win[0].reshape(bs, WINDOW, D_KV)
    cv = cache_v_win[0].reshape(bs, WINDOW, D_KV)
    padk = lambda new: jnp.pad(new, ((0, 0), (0, WINDOW - lsp), (0, 0)))
    attn_s = _attn_sample(sinks, q_s.reshape(bs, lsp, D_ATTN), ck, padk(k_s3), cv, padk(v_s3))

    tabs = _scan_tables(ar, ai, bp)
    zero_carry = jnp.zeros((SUBLANES, N_STATE), F32)
    y_p, hr_p, hi_p = _s5_prompt(u_p, s5w, tabs, d_skip, zero_carry, zero_carry)
    u_tb = jnp.transpose(u_s.reshape(bs, lsp, D_SSM)[:, :ls], (1, 0, 2))
    y_tb, hr_s, hi_s = _s5_sample(u_tb, s5w, ar, ai, d_skip, state_ssm_re[0].reshape(bs, N_STATE),
                                  state_ssm_im[0].reshape(bs, N_STATE))
    y_s = jnp.pad(jnp.transpose(y_tb, (1, 0, 2)), ((0, 0), (0, lsp - ls), (0, 0))).reshape(1, bs * lsp, D_SSM)

    cnt0 = jnp.zeros((SUBLANES, LANES), F32)
    n_p, n_s = bp * lp, bs * lsp
    n_tok = n_p + n_s
    x1_s, h2_s, route_s, cnt_s = _post(attn_s.reshape(1, bs * lsp, D_ATTN), y_s, xs_pad, mod_s, True, wpost, cnt0)
    x1_p, h2, route_p, cnt = _post(attn_p, y_p, x_prompt, mod_p, False, wpost, cnt_s, tail=h2_s)

    n_tiles = -(-(2 * n_tok + N_EXPERTS * (TM_MOE - 1)) // TM_MOE) + MOE_LOOKAHEAD
    route = jnp.concatenate([route_p.reshape(n_p, LANES), route_s.reshape(n_s, LANES)], 0)
    counts = cnt[0, ROUTE_LANE0:ROUTE_LANE0 + N_EXPERTS].astype(jnp.int32)
    padded = ((counts + TM_MOE - 1) // TM_MOE) * TM_MOE
    ends = jnp.cumsum(padded)
    offs = ends - padded
    col = lambda c: route[:, c].astype(jnp.int32)
    expert_ids = jnp.arange(N_EXPERTS, dtype=jnp.int32)
    seg_start = lambda e: jnp.sum(jnp.where(e[:, None] == expert_ids[None, :], offs[None, :], 0), axis=1)
    pos_k = [seg_start(col(k)) + col(4 + k) for k in range(2)]
    n_valid = ends[-1] // TM_MOE
    tile_row = jnp.minimum(jnp.arange(n_tiles), n_valid - 1) * TM_MOE
    tile_expert = jnp.sum((ends[None, :] <= tile_row[:, None]).astype(jnp.int32), axis=1)
    first_tile = jnp.concatenate([jnp.ones((1,), jnp.int32), (tile_expert[1:] != tile_expert[:-1]).astype(jnp.int32)])
    first_tile = first_tile * (jnp.arange(n_tiles) < n_valid)
    nonempty = padded > 0
    later = jnp.where(nonempty[None, :] & (expert_ids[None, :] > expert_ids[:, None]), expert_ids[None, :], N_EXPERTS)
    next_expert = jnp.min(later, axis=1)
    next_expert = jnp.where(next_expert < N_EXPERTS, next_expert, -1).astype(jnp.int32)
    weight_slot = ((jnp.cumsum(nonempty.astype(jnp.int32)) - 1) & 1).astype(jnp.int32)
    tile_meta = (tile_expert, first_tile.astype(jnp.int32), next_expert[tile_expert], weight_slot[tile_expert])
    token_ids = jnp.arange(n_tok, dtype=jnp.int32)
    row_token = (jnp.arange(n_tiles * TM_MOE, dtype=jnp.int32) % n_tok).at[jnp.concatenate(pos_k)].set(
        jnp.concatenate([token_ids, token_ids]), unique_indices=True, mode="promise_in_bounds")

    ys = _moe(tile_meta, n_valid.reshape(1).astype(jnp.int32), row_token, h2, exp_w_gate[0], exp_w_up[0],
              exp_w_down[0])
    pos_tiles = jnp.concatenate([p.reshape(n_tok // TM, 1, TM) for p in pos_k], axis=2)
    y_prompt = _final(pos_tiles, 0, x1_p, mod_p, False, route_p, ln2_g[0][None], ln2_b[0][None], ys)
    y_samp = _final(pos_tiles, n_p // TM, x1_s, mod_s, True, route_s, ln2_g[0][None], ln2_b[0][None], ys)
    y_sample = y_samp.reshape(bs, lsp, D_MODEL)[:, :ls]

    kv5 = lambda a, n: a.reshape(1, a.shape[0], n, N_KV_HEADS, HEAD_DIM)
    k_win_p = kv5(k_p[:, lp - WINDOW:], WINDOW)
    v_win_p = kv5(v_p[:, lp - WINDOW:], WINDOW)
    k_win_s = kv5(jnp.concatenate([ck[:, ls:], k_s3[:, :ls]], 1), WINDOW)
    v_win_s = kv5(jnp.concatenate([cv[:, ls:], v_s3[:, :ls]], 1), WINDOW)
    st = lambda a: a.reshape(1, a.shape[0], N_SSM_GROUPS, SSM_STATE)
    return (y_prompt, y_sample, k_win_p, v_win_p, st(hr_p[bp:2 * bp]), st(hi_p[bp:2 * bp]),
            k_win_s, v_win_s, st(hr_s), st(hi_s))
```
